```python
import jax
import jax.numpy as jnp
from jax import lax
import numpy as np

D_MODEL = 1024
BATCH = 2
SEQ = 8192
DEPTH = 4
DEC_BATCH = 128
DEC_SEQ = 8
PAST_LEN = 8192
PAGE_SIZE = 128

HEAD_DIM = 64
N_HEADS = 8
N_KV_HEADS = 2
Q_PER_KV = N_HEADS // N_KV_HEADS
ATTN_WIDTH = N_HEADS * HEAD_DIM
KV_WIDTH = N_KV_HEADS * HEAD_DIM
WINDOW = 128
ROT_DIM = HEAD_DIM // 4
ROPE_THETA = 500000.0
ATTN_SCALE = HEAD_DIM ** -0.5
SSD_D_INNER = D_MODEL // 2
SSD_HEAD_DIM = 64
SSD_HEADS = SSD_D_INNER // SSD_HEAD_DIM
SSD_GROUPS = 2
SSD_HEADS_PER_GROUP = SSD_HEADS // SSD_GROUPS
SSD_STATE = 128
SSD_CONV = 4
SSD_CONV_DIM = SSD_D_INNER + 2 * SSD_GROUPS * SSD_STATE
SSD_CHUNK = 128
SC_DIM = D_MODEL // 2
SC_WIDTH = 3
IN_SIZES = (ATTN_WIDTH, KV_WIDTH, KV_WIDTH,
            SSD_D_INNER, SSD_CONV_DIM, SSD_HEADS,
            SC_DIM, SC_DIM, SC_DIM,
            3 * D_MODEL)
IN_DIM = sum(IN_SIZES)
MOE_GROUPS = 4
EXPERTS_PER_GROUP = 8
N_EXPERTS = MOE_GROUPS * EXPERTS_PER_GROUP
TOP_K = 2
EXPERT_FF = 512
DISPATCH_BLOCK = 128
ALPHA = (2 * DEPTH) ** 0.25
BETA = (8 * DEPTH) ** -0.25
LN_EPS = 1e-5
RMS_EPS = 1e-5

kernel_name = 'hybrid_swa_ssd_shortconv_hmoe_step'


def split_cols(a, sizes):
    out, start = [], 0
    for s in sizes:
        out.append(a[..., start:start + s])
        start += s
    return out


def layer_norm(x, g, b):
    xf = x.astype(jnp.float32)
    mu = jnp.mean(xf, axis=-1, keepdims=True)
    var = jnp.mean(jnp.square(xf - mu), axis=-1, keepdims=True)
    return ((xf - mu) * lax.rsqrt(var + LN_EPS) * g.astype(jnp.float32) + b.astype(jnp.float32)).astype(x.dtype)


def group_rmsnorm(y, w):
    b, L, _ = y.shape
    yf = y.astype(jnp.float32).reshape(b, L, SSD_GROUPS, SSD_D_INNER // SSD_GROUPS)
    yf = yf * lax.rsqrt(jnp.mean(yf * yf, axis=-1, keepdims=True) + RMS_EPS)
    return (yf.reshape(b, L, SSD_D_INNER) * w.astype(jnp.float32)).astype(y.dtype)


def partial_rope(x, pos):
    half = ROT_DIM // 2
    inv = jnp.power(ROPE_THETA, -jnp.arange(half, dtype=jnp.float32) * (2.0 / ROT_DIM))
    ang = pos.astype(jnp.float32)[:, None] * inv[None, :]
    cos = jnp.cos(ang)[:, None, :]
    sin = jnp.sin(ang)[:, None, :]
    xf = x.astype(jnp.float32)
    x1, x2 = xf[..., :half], xf[..., half:ROT_DIM]
    out = jnp.concatenate([x1 * cos - x2 * sin, x2 * cos + x1 * sin, xf[..., ROT_DIM:]], axis=-1)
    return out.astype(x.dtype)


def softmax_with_sink(logits, sink):
    s = sink.astype(jnp.float32)[:, :, None, None]
    m = jnp.maximum(jnp.max(logits, axis=-1, keepdims=True), s)
    e = jnp.exp(logits - m)
    return e / (jnp.sum(e, axis=-1, keepdims=True) + jnp.exp(s - m))


def banded_window_attention(q, k, v, sink):
    b, L = q.shape[:2]
    nb = L // WINDOW
    qb = q.reshape(b, nb, WINDOW, N_KV_HEADS, Q_PER_KV, HEAD_DIM)
    kb = k.reshape(b, nb, WINDOW, N_KV_HEADS, HEAD_DIM)
    vb = v.reshape(b, nb, WINDOW, N_KV_HEADS, HEAD_DIM)
    shift = ((0, 0), (1, 0), (0, 0), (0, 0), (0, 0))
    k2 = jnp.concatenate([jnp.pad(kb[:, :-1], shift), kb], axis=2)
    v2 = jnp.concatenate([jnp.pad(vb[:, :-1], shift), vb], axis=2)
    logits = jnp.einsum('bnqkgd,bnskd->bnkgqs', qb, k2).astype(jnp.float32) * ATTN_SCALE
    qi = jnp.arange(WINDOW)[:, None] + WINDOW
    kj = jnp.arange(2 * WINDOW)[None, :]
    diff = qi - kj
    local = (diff >= 0) & (diff < WINDOW)
    first = (jnp.arange(nb) == 0)[:, None, None] & (kj < WINDOW)[None]
    valid = local[None] & ~first
    logits = jnp.where(valid[None, :, None, None], logits, -jnp.inf)
    p = softmax_with_sink(logits, sink.reshape(N_KV_HEADS, Q_PER_KV))
    out = jnp.einsum('bnkgqs,bnskd->bnqkgd', p.astype(v.dtype), v2)
    return out.reshape(b, L, ATTN_WIDTH)


def cached_window_attention(q, k_all, v_all, sink, pos):
    b, L = q.shape[:2]
    wb = k_all.shape[1] - L
    kpos = jnp.concatenate([pos[0] - wb + jnp.arange(wb, dtype=jnp.int32), pos])
    diff = pos[:, None] - kpos[None, :]
    valid = (diff >= 0) & (diff < WINDOW) & (kpos >= 0)[None, :]
    qh = q.reshape(b, L, N_KV_HEADS, Q_PER_KV, HEAD_DIM)
    logits = jnp.einsum('bqkgd,bskd->bkgqs', qh, k_all).astype(jnp.float32) * ATTN_SCALE
    logits = jnp.where(valid, logits, -jnp.inf)
    p = softmax_with_sink(logits, sink.reshape(N_KV_HEADS, Q_PER_KV))
    out = jnp.einsum('bkgqs,bskd->bqkgd', p.astype(v_all.dtype), v_all)
    return out.reshape(b, L, ATTN_WIDTH)


def causal_depthwise_conv(u, buf, w):
    width = w.shape[0]
    L = u.shape[1]
    full = jnp.concatenate([buf, u], axis=1)
    out = full[:, 0:L] * w[0]
    for tap in range(1, width):
        out = out + full[:, tap:tap + L] * w[tap]
    return out, full[:, L:]


def ssd_chunked_scan(x, dt, a, bm, cm, h0):
    b, L = x.shape[:2]
    q = min(SSD_CHUNK, L)
    nc = -(-L // q)
    pad = nc * q - L

    def chunks(t):
        t = jnp.pad(t.astype(jnp.float32), [(0, 0), (0, pad)] + [(0, 0)] * (t.ndim - 2))
        return t.reshape((b, nc, q) + t.shape[2:])

    xc, dtc, bc, cc = chunks(x), chunks(dt), chunks(bm), chunks(cm)
    cs = jnp.cumsum(dtc * a, axis=2)
    seg = cs[:, :, :, None] - cs[:, :, None, :]
    causal = jnp.tril(jnp.ones((q, q), dtype=bool))[:, :, None, None]
    decay = jnp.where(causal, jnp.exp(jnp.where(causal, seg, 0.0)), 0.0)
    cb = jnp.einsum('bcign,bcjgn->bcijg', cc, bc)
    dtx = dtc[..., None] * xc
    y_diag = jnp.einsum('bcijgh,bcjghp->bcighp', cb[..., None] * decay, dtx)
    to_end = jnp.exp(cs[:, :, -1:] - cs)
    states = jnp.einsum('bcjgn,bcjghp->bcghpn', bc, to_end[..., None] * dtx)
    chunk_decay = jnp.exp(cs[:, :, -1])

    def step(h, inp):
        dec, st = inp
        return dec[..., None, None] * h + st, h

    h_last, h_prev = lax.scan(step, h0.astype(jnp.float32),
                              (jnp.moveaxis(chunk_decay, 1, 0), jnp.moveaxis(states, 1, 0)))
    h_prev = jnp.moveaxis(h_prev, 0, 1)
    y_off = jnp.einsum('bcign,bcghpn->bcighp', cc, h_prev) * jnp.exp(cs)[..., None]
    y = (y_diag + y_off).reshape(b, nc * q, x.shape[2], x.shape[3], x.shape[4])[:, :L]
    return y.astype(x.dtype), h_last.astype(h0.dtype)


def hier_moe(h, rg_w, rg_b, re_w, re_b, w_gate, w_up, w_down):
    b, L, D = h.shape
    t = h.reshape(b * L, D)
    T = t.shape[0]
    g_prob = jax.nn.softmax((t @ rg_w + rg_b).astype(jnp.float32), axis=-1)
    g_p, g_idx = lax.top_k(g_prob, 1)
    e_logits = (t @ re_w + re_b).astype(jnp.float32).reshape(T, MOE_GROUPS, EXPERTS_PER_GROUP)
    sel = e_logits[jnp.arange(T), g_idx[:, 0]]
    w_top, j_top = lax.top_k(jax.nn.softmax(sel, axis=-1), TOP_K)
    weights = g_p * (w_top / jnp.sum(w_top, axis=-1, keepdims=True))
    experts = (g_idx * EXPERTS_PER_GROUP + j_top).astype(jnp.int32)
    A = T * TOP_K
    flat_e = experts.reshape(A)
    counts = jnp.bincount(flat_e, length=N_EXPERTS)
    pad_counts = (counts + DISPATCH_BLOCK - 1) // DISPATCH_BLOCK * DISPATCH_BLOCK
    ends = jnp.cumsum(pad_counts)
    pad_starts = ends - pad_counts
    starts = jnp.cumsum(counts) - counts
    order = jnp.argsort(flat_e)
    sorted_e = flat_e[order]
    tok = order // TOP_K
    dest = pad_starts[sorted_e] + (jnp.arange(A, dtype=jnp.int32) - starts[sorted_e])
    n_blocks = -(-(A + N_EXPERTS * (DISPATCH_BLOCK - 1)) // DISPATCH_BLOCK)
    n_slots = n_blocks * DISPATCH_BLOCK
    slot_tok = jnp.full((n_slots,), T, dtype=jnp.int32).at[dest].set(tok.astype(jnp.int32))
    block_expert = jnp.clip(jnp.searchsorted(ends, jnp.arange(n_blocks, dtype=jnp.int32) * DISPATCH_BLOCK,
                                             side='right'), 0, N_EXPERTS - 1)
    t_pad = jnp.concatenate([t, jnp.zeros((1, D), t.dtype)], axis=0)
    xb = t_pad[slot_tok].reshape(n_blocks, DISPATCH_BLOCK, D)

    def run_block(args):
        xblk, e = args
        return (jax.nn.silu(xblk @ w_gate[e]) * (xblk @ w_up[e])) @ w_down[e]

    yb = lax.map(run_block, (xb, block_expert)).reshape(n_slots, D)
    w_sorted = weights.reshape(A)[order]
    out = jax.ops.segment_sum(yb[dest] * w_sorted[:, None].astype(yb.dtype), tok, num_segments=T)
    return out.reshape(b, L, D)


def setup_inputs(seed: int = 0) -> dict:
    key = jax.random.key(seed)
    ks = iter(jax.random.split(key, 48))

    def nrm(shape, scale):
        return jax.random.normal(next(ks), shape, jnp.float32) * scale

    wb = min(WINDOW, PAST_LEN)
    dt0 = jnp.exp(jax.random.uniform(next(ks), (DEPTH, SSD_HEADS), jnp.float32,
                                     np.log(1e-3), np.log(1e-1)))
    return {
        'x_prompt': nrm((BATCH, SEQ, D_MODEL), 1.0),
        'x_sample': nrm((DEC_BATCH, DEC_SEQ, D_MODEL), 1.0),
        'c_prompt': nrm((BATCH, D_MODEL), 1.0),
        'c_sample': nrm((DEC_BATCH, D_MODEL), 1.0),
        'cache_attn_k': nrm((DEPTH, DEC_BATCH, wb, N_KV_HEADS, HEAD_DIM), 1.0),
        'cache_attn_v': nrm((DEPTH, DEC_BATCH, wb, N_KV_HEADS, HEAD_DIM), 1.0),
        'state_ssm': nrm((DEPTH, DEC_BATCH, SSD_HEADS, SSD_HEAD_DIM, SSD_STATE), 0.5),
        'state_ssd_conv': nrm((DEPTH, DEC_BATCH, SSD_CONV - 1, SSD_CONV_DIM), 1.0),
        'state_short_conv': nrm((DEPTH, DEC_BATCH, SC_WIDTH - 1, SC_DIM), 1.0),
        'w_ada': nrm((DEPTH, D_MODEL, 6 * D_MODEL), 0.5 * D_MODEL ** -0.5),
        'b_ada': nrm((DEPTH, 6 * D_MODEL), 0.02),
        'w_in': nrm((DEPTH, D_MODEL, IN_DIM), D_MODEL ** -0.5),
        'attn_sink': nrm((DEPTH, N_HEADS), 0.5),
        'ssd_conv_w': nrm((DEPTH, SSD_CONV, SSD_CONV_DIM), SSD_CONV ** -0.5),
        'ssd_conv_b': nrm((DEPTH, SSD_CONV_DIM), 0.02),
        'ssd_dt_bias': dt0 + jnp.log(-jnp.expm1(-dt0)),
        'ssd_a_log': jnp.log(jax.random.uniform(next(ks), (DEPTH, SSD_HEADS), jnp.float32, 1.0, 16.0)),
        'ssd_d': 1.0 + nrm((DEPTH, SSD_HEADS), 0.02),
        'ssd_norm_w': 1.0 + nrm((DEPTH, SSD_D_INNER), 0.02),
        'sc_conv_w': nrm((DEPTH, SC_WIDTH, SC_DIM), SC_WIDTH ** -0.5),
        'w_pa': nrm((DEPTH, ATTN_WIDTH, D_MODEL), ATTN_WIDTH ** -0.5),
        'w_pb': nrm((DEPTH, SSD_D_INNER, D_MODEL), SSD_D_INNER ** -0.5),
        'w_pc': nrm((DEPTH, SC_DIM, D_MODEL), SC_DIM ** -0.5),
        'w_out': nrm((DEPTH, D_MODEL, D_MODEL), BETA * D_MODEL ** -0.5),
        'ln1_g': 1.0 + nrm((DEPTH, D_MODEL), 0.02),
        'ln1_b': nrm((DEPTH, D_MODEL), 0.02),
        'ln2_g': 1.0 + nrm((DEPTH, D_MODEL), 0.02),
        'ln2_b': nrm((DEPTH, D_MODEL), 0.02),
        'router_g_w': nrm((DEPTH, D_MODEL, MOE_GROUPS), D_MODEL ** -0.5),
        'router_g_b': nrm((DEPTH, MOE_GROUPS), 0.01),
        'router_e_w': nrm((DEPTH, D_MODEL, N_EXPERTS), D_MODEL ** -0.5),
        'router_e_b': nrm((DEPTH, N_EXPERTS), 0.01),
        'moe_w_gate': nrm((DEPTH, N_EXPERTS, D_MODEL, EXPERT_FF), D_MODEL ** -0.5),
        'moe_w_up': nrm((DEPTH, N_EXPERTS, D_MODEL, EXPERT_FF), D_MODEL ** -0.5),
        'moe_w_down': nrm((DEPTH, N_EXPERTS, EXPERT_FF, D_MODEL), BETA * EXPERT_FF ** -0.5),
    }


def reference(x_prompt, x_sample, c_prompt, c_sample, cache_attn_k, cache_attn_v, state_ssm,
              state_ssd_conv, state_short_conv, w_ada, b_ada, w_in, attn_sink, ssd_conv_w, ssd_conv_b,
              ssd_dt_bias, ssd_a_log, ssd_d, ssd_norm_w, sc_conv_w, w_pa, w_pb, w_pc, w_out,
              ln1_g, ln1_b, ln2_g, ln2_b, router_g_w, router_g_b, router_e_w, router_e_b,
              moe_w_gate, moe_w_up, moe_w_down):

    def layer(l, x, c, pos, k_buf, v_buf, h0, ssd_buf, sc_buf, banded):
        b, L, _ = x.shape
        mod = (jax.nn.silu(c) @ w_ada[l] + b_ada[l])[:, None, :]
        sh1, sc1, g1, sh2, sc2, g2 = split_cols(mod, (D_MODEL,) * 6)
        u = x * (1 + sc1) + sh1
        q, k, v, z, xbc, dt_raw, sc_b, sc_c, sc_v, gates = split_cols(u @ w_in[l], IN_SIZES)
        q = partial_rope(q.reshape(b, L, N_HEADS, HEAD_DIM), pos)
        k = partial_rope(k.reshape(b, L, N_KV_HEADS, HEAD_DIM), pos)
        v = v.reshape(b, L, N_KV_HEADS, HEAD_DIM)
        k_all = jnp.concatenate([k_buf, k], axis=1)
        v_all = jnp.concatenate([v_buf, v], axis=1)
        if banded:
            y_a = banded_window_attention(q, k, v, attn_sink[l])
        else:
            y_a = cached_window_attention(q, k_all, v_all, attn_sink[l], pos)
        xbc_conv, new_ssd_buf = causal_depthwise_conv(xbc, ssd_buf, ssd_conv_w[l])
        xbc_act = jax.nn.silu(xbc_conv + ssd_conv_b[l])
        xs, bm, cm = split_cols(xbc_act, (SSD_D_INNER, SSD_GROUPS * SSD_STATE, SSD_GROUPS * SSD_STATE))
        xs = xs.reshape(b, L, SSD_GROUPS, SSD_HEADS_PER_GROUP, SSD_HEAD_DIM)
        bm = bm.reshape(b, L, SSD_GROUPS, SSD_STATE)
        cm = cm.reshape(b, L, SSD_GROUPS, SSD_STATE)
        dt = jax.nn.softplus((dt_raw + ssd_dt_bias[l]).astype(jnp.float32))
        dt = dt.reshape(b, L, SSD_GROUPS, SSD_HEADS_PER_GROUP)
        a = -jnp.exp(ssd_a_log[l].astype(jnp.float32)).reshape(SSD_GROUPS, SSD_HEADS_PER_GROUP)
        h0g = h0.reshape(b, SSD_GROUPS, SSD_HEADS_PER_GROUP, SSD_HEAD_DIM, SSD_STATE)
        y_ssd, h_new = ssd_chunked_scan(xs, dt, a, bm, cm, h0g)
        y_ssd = y_ssd + ssd_d[l].reshape(SSD_GROUPS, SSD_HEADS_PER_GROUP, 1) * xs
        y_b = group_rmsnorm(y_ssd.reshape(b, L, SSD_D_INNER) * jax.nn.silu(z), ssd_norm_w[l])
        conv_c, new_sc_buf = causal_depthwise_conv(sc_c * sc_v, sc_buf, sc_conv_w[l])
        y_c = sc_b * conv_c
        g_a, g_b, g_c = split_cols(jax.nn.sigmoid(gates), (D_MODEL,) * 3)
        mix = (g_a * (y_a @ w_pa[l]) + g_b * (y_b @ w_pb[l]) + g_c * (y_c @ w_pc[l])) @ w_out[l]
        x = layer_norm(ALPHA * x + g1 * mix, ln1_g[l], ln1_b[l])
        ffn = hier_moe(x * (1 + sc2) + sh2, router_g_w[l], router_g_b[l], router_e_w[l], router_e_b[l],
                       moe_w_gate[l], moe_w_up[l], moe_w_down[l])
        x = layer_norm(ALPHA * x + g2 * ffn, ln2_g[l], ln2_b[l])
        h_out = h_new.reshape(b, SSD_HEADS, SSD_HEAD_DIM, SSD_STATE)
        return x, k_all[:, L:], v_all[:, L:], h_out, new_ssd_buf, new_sc_buf

    wb = cache_attn_k.shape[2]
    bp, lp = x_prompt.shape[0], x_prompt.shape[1]
    pos_p = jnp.arange(lp, dtype=jnp.int32)
    pos_s = PAST_LEN + jnp.arange(x_sample.shape[1], dtype=jnp.int32)
    zero_kv = jnp.zeros((bp, wb, N_KV_HEADS, HEAD_DIM), x_prompt.dtype)
    zero_h = jnp.zeros((bp, SSD_HEADS, SSD_HEAD_DIM, SSD_STATE), state_ssm.dtype)
    zero_ssd_buf = jnp.zeros((bp, SSD_CONV - 1, SSD_CONV_DIM), x_prompt.dtype)
    zero_sc_buf = jnp.zeros((bp, SC_WIDTH - 1, SC_DIM), x_prompt.dtype)

    xp, xs = x_prompt, x_sample
    st_p, st_s = [], []
    for l in range(DEPTH):
        xp, *sp = layer(l, xp, c_prompt, pos_p, zero_kv, zero_kv, zero_h, zero_ssd_buf, zero_sc_buf, True)
        xs, *ss = layer(l, xs, c_sample, pos_s, cache_attn_k[l], cache_attn_v[l], state_ssm[l],
                        state_ssd_conv[l], state_short_conv[l], False)
        st_p.append(sp)
        st_s.append(ss)
    kp, vp, hp, sbp, cbp = [jnp.stack([s[i] for s in st_p]) for i in range(5)]
    ks, vs, hs, sbs, cbs = [jnp.stack([s[i] for s in st_s]) for i in range(5)]
    return (xp, xs, kp, vp, hp, sbp, cbp, ks, vs, hs, sbs, cbs)
```

```python
import functools

import jax
import jax.numpy as jnp
import numpy as np
from jax import lax
from jax.experimental import pallas as pl
from jax.experimental.pallas import tpu as pltpu

F32 = jnp.float32
BF16 = jnp.bfloat16

D_MODEL = 1024
HEAD_DIM = 64
N_HEADS = 8
N_KV_HEADS = 2
Q_PER_KV = N_HEADS // N_KV_HEADS
ATTN_WIDTH = N_HEADS * HEAD_DIM
KV_WIDTH = N_KV_HEADS * HEAD_DIM
WINDOW = 128
ROT_DIM = HEAD_DIM // 4
ROT_HALF = ROT_DIM // 2
ROPE_THETA = 500000.0
ATTN_SCALE = HEAD_DIM ** -0.5
SSD_D_INNER = 512
SSD_HEAD_DIM = 64
SSD_HEADS = 8
SSD_GROUPS = 2
SSD_HEADS_PER_GROUP = SSD_HEADS // SSD_GROUPS
SSD_STATE = 128
SSD_CONV = 4
SSD_CONV_DIM = SSD_D_INNER + 2 * SSD_GROUPS * SSD_STATE
SSD_CHUNK = 128
SC_DIM = 512
SC_WIDTH = 3
MOE_GROUPS = 4
EXPERTS_PER_GROUP = 8
N_EXPERTS = MOE_GROUPS * EXPERTS_PER_GROUP
EXPERT_FF = 512
LN_EPS = 1e-5
RMS_EPS = 1e-5

SUBLANES = 8
LANES = 128
VMEM_LIMIT = 56 * 1024 * 1024

COL_Q = 0
COL_Z = 512
COL_XBC = 1024
COL_SCB = 2048
COL_SCC = 2560
COL_GATES = 3072
COL_SCV = 6144
COL_KV = 6656
P_MAIN = 6912
DT_PAD = 128

ROW_TILE_IN = 1024
COL_TILE_IN = 768
ROW_TILE = 512
MIX_ROWS = 128
DEC_SEQ = 8
SEQS_PER_STEP = MIX_ROWS // DEC_SEQ
EXPERT_BLOCK = 256
ROW_CHUNKS = D_MODEL // LANES


def _silu(v):
    return v * jax.nn.sigmoid(v)


def _dot(a, b):
    return jnp.dot(a, b, preferred_element_type=F32)


def _dot_nt(a, b):
    return lax.dot_general(a, b, (((1,), (1,)), ((), ())), preferred_element_type=F32)


def _dot_exact(a, b):
    return jnp.dot(a, b, preferred_element_type=F32, precision=lax.Precision.HIGHEST)


def _params(sem):
    return pltpu.CompilerParams(dimension_semantics=sem, vmem_limit_bytes=VMEM_LIMIT)


def _pick_mod(is_sample, prompt_ref, sample_ref):
    return jnp.where(is_sample, sample_ref[0], prompt_ref[0])


def _layer_norm(v, g, b):
    mu = jnp.mean(v, axis=-1, keepdims=True)
    c = v - mu
    var = jnp.mean(c * c, axis=-1, keepdims=True)
    return c * lax.rsqrt(var + LN_EPS) * g + b


def _mod_kernel(c_ref, w_ref, b_ref, o_ref):
    s = _silu(c_ref[...]).astype(BF16)
    o_ref[0] = _dot(s, w_ref[0].astype(BF16)) + b_ref[0]


def _modulation(c_all, w_ada, b_ada):
    depth, _, width = w_ada.shape
    n = c_all.shape[0]
    tn = 1536
    return pl.pallas_call(
        _mod_kernel,
        grid=(depth, width // tn),
        in_specs=[
            pl.BlockSpec((n, D_MODEL), lambda l, j: (0, 0)),
            pl.BlockSpec((1, D_MODEL, tn), lambda l, j: (l, 0, j)),
            pl.BlockSpec((1, 1, tn), lambda l, j: (l, 0, j)),
        ],
        out_specs=pl.BlockSpec((1, n, tn), lambda l, j: (l, 0, j)),
        out_shape=jax.ShapeDtypeStruct((depth, n, width), F32),
        compiler_params=_params(("parallel", "parallel")),
    )(c_all, w_ada, b_ada.reshape(depth, 1, width))


def _inproj_kernel(n_prompt_tiles, x_ref, shp_ref, shs_ref, scp_ref, scs_ref, w_ref, wdt_ref,
                   p_ref, dt_ref, u_scr):
    i = pl.program_id(0)
    j = pl.program_id(1)

    @pl.when(j == 0)
    def _():
        is_sample = i >= n_prompt_tiles
        sh = _pick_mod(is_sample, shp_ref, shs_ref)
        sc = _pick_mod(is_sample, scp_ref, scs_ref)
        u = (x_ref[...] * (1.0 + sc) + sh).astype(BF16)
        u_scr[...] = u
        dt_ref[...] = _dot(u, wdt_ref[...])

    p_ref[...] = _dot(u_scr[...], w_ref[...]).astype(BF16)


def _mod_specs(col, row_tile, n_prompt_tiles, tiles_per_seq, n_seq, grid_rank):
    def prompt_map(i, *_):
        return (jnp.minimum(i // tiles_per_seq, n_seq - 1), 0, col)

    def sample_map(i, *_):
        return (0, jnp.maximum(i - n_prompt_tiles, 0), col)

    del grid_rank
    return (pl.BlockSpec((1, 1, D_MODEL), prompt_map),
            pl.BlockSpec((1, row_tile, D_MODEL), sample_map))


def _input_projection(x, mod_p, mod_s, w_main, w_dt, n_prompt, seq_len):
    t_all = x.shape[0]
    tm, tn = ROW_TILE_IN, COL_TILE_IN
    npt = n_prompt // tm
    n_seq = mod_p.shape[0]
    shp, shs = _mod_specs(0, tm, npt, seq_len // tm, n_seq, 2)
    scp, scs = _mod_specs(1, tm, npt, seq_len // tm, n_seq, 2)
    return pl.pallas_call(
        functools.partial(_inproj_kernel, npt),
        grid=(t_all // tm, P_MAIN // tn),
        in_specs=[
            pl.BlockSpec((tm, D_MODEL), lambda i, j: (i, 0)),
            shp, shs, scp, scs,
            pl.BlockSpec((D_MODEL, tn), lambda i, j: (0, j)),
            pl.BlockSpec((D_MODEL, DT_PAD), lambda i, j: (0, 0)),
        ],
        out_specs=[
            pl.BlockSpec((tm, tn), lambda i, j: (i, j)),
            pl.BlockSpec((tm, DT_PAD), lambda i, j: (i, 0)),
        ],
        out_shape=[
            jax.ShapeDtypeStruct((t_all, P_MAIN), BF16),
            jax.ShapeDtypeStruct((t_all, DT_PAD), F32),
        ],
        scratch_shapes=[pltpu.VMEM((tm, D_MODEL), BF16)],
        compiler_params=_params(("parallel", "arbitrary")),
    )(x, mod_p, mod_s, mod_p, mod_s, w_main, w_dt)


def _rope(v, cos, sin):
    width = v.shape[-1]
    reps = width // LANES
    if reps > 1:
        cos = jnp.concatenate([cos] * reps, axis=-1)
        sin = jnp.concatenate([sin] * reps, axis=-1)
    lane = lax.broadcasted_iota(jnp.int32, v.shape, 1) % HEAD_DIM
    partner = jnp.where(lane < ROT_HALF,
                        pltpu.roll(v, width - ROT_HALF, 1),
                        pltpu.roll(v, ROT_HALF, 1))
    return v * cos + partner * sin


def _shift_rows(cur, prev, k):
    axis = cur.ndim - 2
    idx = lax.broadcasted_iota(jnp.int32, cur.shape, axis)
    return jnp.where(idx < k, pltpu.roll(prev, k, axis), pltpu.roll(cur, k, axis))


def _causal_conv(cur, prev, w_ref, width):
    out = cur * w_ref[width - 1:width, :]
    for k in range(1, width):
        out = out + _shift_rows(cur, prev, k) * w_ref[width - 1 - k:width - k, :]
    return out


def _head_expand():
    r = lax.broadcasted_iota(jnp.int32, (LANES, SSD_D_INNER), 0)
    c = lax.broadcasted_iota(jnp.int32, (LANES, SSD_D_INNER), 1)
    return (c // SSD_HEAD_DIM == r).astype(F32)


def _head_expand_t():
    r = lax.broadcasted_iota(jnp.int32, (SSD_D_INNER, LANES), 0)
    c = lax.broadcasted_iota(jnp.int32, (SSD_D_INNER, LANES), 1)
    return (r // SSD_HEAD_DIM == c).astype(F32)


def _ssd_tile(act, dt_raw, dtb, alog, seq_rows):
    rows = MIX_ROWS
    xs = act[:, :SSD_D_INNER]
    bm = act[:, SSD_D_INNER:SSD_D_INNER + SSD_GROUPS * SSD_STATE]
    cm = act[:, SSD_D_INNER + SSD_GROUPS * SSD_STATE:]
    v = dt_raw + dtb
    dt = jnp.maximum(v, 0.0) + jnp.log1p(jnp.exp(-jnp.abs(v)))
    a = -jnp.exp(alog)
    dta = dt * a
    ri = lax.broadcasted_iota(jnp.int32, (rows, rows), 0)
    ci = lax.broadcasted_iota(jnp.int32, (rows, rows), 1)
    same = (ri // seq_rows) == (ci // seq_rows)
    causal = same & (ci <= ri)
    cs = _dot_exact(causal.astype(F32), dta)
    tot = _dot_exact(same.astype(F32), dta)
    expand = _head_expand()
    dt_e = _dot_exact(dt, expand)
    cs_e = _dot_exact(cs, expand)
    tot_e = _dot_exact(tot, expand)
    cs_t = cs.T
    dec_t = jnp.exp(_dot_exact(_head_expand_t(), tot.T))
    dtx = xs * dt_e
    xw_t = (dtx * jnp.exp(tot_e - cs_e)).T
    b_groups, c_groups, y_parts = [], [], []
    for g in range(SSD_GROUPS):
        bg = bm[:, g * SSD_STATE:(g + 1) * SSD_STATE].astype(BF16)
        cg = cm[:, g * SSD_STATE:(g + 1) * SSD_STATE].astype(BF16)
        b_groups.append(bg)
        c_groups.append(cg)
        cb = _dot_nt(cg, bg)
        for hh in range(SSD_HEADS_PER_GROUP):
            h = g * SSD_HEADS_PER_GROUP + hh
            seg = cs[:, h:h + 1] - cs_t[h:h + 1, :]
            decay = jnp.where(causal, jnp.exp(jnp.where(causal, seg, 0.0)), 0.0)
            y_parts.append(_dot((cb * decay).astype(BF16),
                                dtx[:, h * SSD_HEAD_DIM:(h + 1) * SSD_HEAD_DIM].astype(BF16)))
    y_diag = jnp.concatenate(y_parts, axis=-1)
    return xs, y_diag, jnp.exp(cs_e), xw_t, dec_t, b_groups, c_groups


def _gated_group_norm(y, z, nw):
    y = y * _silu(z)
    half = SSD_D_INNER // SSD_GROUPS
    parts = []
    for g in range(SSD_GROUPS):
        yg = y[:, g * half:(g + 1) * half]
        parts.append(yg * lax.rsqrt(jnp.mean(yg * yg, axis=-1, keepdims=True) + RMS_EPS))
    return jnp.concatenate(parts, axis=-1) * nw


def _attn_prompt_kernel(sink_ref, q_ref, kvc_ref, kvp_ref, cosc_ref, sinc_ref, cosp_ref, sinp_ref,
                        ya_ref, krot_ref):
    i = pl.program_id(1)
    nb = pl.num_programs(1)
    w = WINDOW
    q = _rope(q_ref[...].astype(F32), cosc_ref[...], sinc_ref[...])
    kvc = kvc_ref[...].astype(F32)
    kvp = kvp_ref[...].astype(F32)
    kc = _rope(kvc[:, :KV_WIDTH], cosc_ref[...], sinc_ref[...])
    kp = _rope(kvp[:, :KV_WIDTH], cosp_ref[...], sinp_ref[...])
    vc = kvc[:, KV_WIDTH:]
    vp = kvp[:, KV_WIDTH:]

    @pl.when(i == nb - 1)
    def _():
        krot_ref[0] = kc

    rows = Q_PER_KV * w
    r = lax.broadcasted_iota(jnp.int32, (rows, 2 * w), 0)
    s = lax.broadcasted_iota(jnp.int32, (rows, 2 * w), 1)
    diff = w + (r % w) - s
    first_key = jnp.where(i > 0, 0, w)
    valid = (diff >= 0) & (diff < w) & (s >= first_key)
    rcol = lax.broadcasted_iota(jnp.int32, (rows, 1), 0)
    outs = []
    for kh in range(N_KV_HEADS):
        hs = slice(kh * HEAD_DIM, (kh + 1) * HEAD_DIM)
        k2 = jnp.concatenate([kp[:, hs], kc[:, hs]], axis=0).astype(BF16)
        v2 = jnp.concatenate([vp[:, hs], vc[:, hs]], axis=0).astype(BF16)
        qg = jnp.concatenate(
            [q[:, (kh * Q_PER_KV + g) * HEAD_DIM:(kh * Q_PER_KV + g + 1) * HEAD_DIM] for g in range(Q_PER_KV)],
            axis=0).astype(BF16)
        sink = jnp.zeros((rows, 1), F32)
        for g in range(Q_PER_KV):
            sink = jnp.where(rcol // w == g, sink_ref[kh * Q_PER_KV + g], sink)
        logits = jnp.where(valid, _dot_nt(qg, k2) * ATTN_SCALE, -jnp.inf)
        m = jnp.maximum(jnp.max(logits, axis=-1, keepdims=True), sink)
        e = jnp.exp(logits - m)
        den = jnp.sum(e, axis=-1, keepdims=True) + jnp.exp(sink - m)
        o = _dot(e.astype(BF16), v2) / den
        for g in range(Q_PER_KV):
            outs.append(o[g * w:(g + 1) * w, :])
    ya_ref[...] = jnp.concatenate(outs, axis=-1).astype(BF16)


def _attention_prompt(p, sink, cos_p, sin_p, n_seq, seq_len):
    nb = seq_len // WINDOW
    n_prompt = n_seq * seq_len
    w = WINDOW

    def cur(b, i):
        return b * nb + i

    def prev(b, i):
        return jnp.maximum(b * nb + i - 1, 0)

    return pl.pallas_call(
        _attn_prompt_kernel,
        grid=(n_seq, nb),
        in_specs=[
            pl.BlockSpec(memory_space=pltpu.SMEM),
            pl.BlockSpec((w, ATTN_WIDTH), lambda b, i: (cur(b, i), COL_Q // ATTN_WIDTH)),
            pl.BlockSpec((w, 2 * KV_WIDTH), lambda b, i: (cur(b, i), COL_KV // (2 * KV_WIDTH))),
            pl.BlockSpec((w, 2 * KV_WIDTH), lambda b, i: (prev(b, i), COL_KV // (2 * KV_WIDTH))),
            pl.BlockSpec((w, LANES), lambda b, i: (i, 0)),
            pl.BlockSpec((w, LANES), lambda b, i: (i, 0)),
            pl.BlockSpec((w, LANES), lambda b, i: (jnp.maximum(i - 1, 0), 0)),
            pl.BlockSpec((w, LANES), lambda b, i: (jnp.maximum(i - 1, 0), 0)),
        ],
        out_specs=[
            pl.BlockSpec((w, ATTN_WIDTH), lambda b, i: (cur(b, i), 0)),
            pl.BlockSpec((1, w, KV_WIDTH), lambda b, i: (b, 0, 0)),
        ],
        out_shape=[
            jax.ShapeDtypeStruct((n_prompt, ATTN_WIDTH), BF16),
            jax.ShapeDtypeStruct((n_seq, w, KV_WIDTH), F32),
        ],
        compiler_params=_params(("parallel", "arbitrary")),
    )(sink, p, p, p, cos_p, sin_p, cos_p, sin_p)


def _ssd_prompt_kernel(z_ref, xc_ref, xp_ref, dt_ref, scb_ref, sccc_ref, sccp_ref, scvc_ref, scvp_ref,
                       cw_ref, cb_ref, dtb_ref, alog_ref, de_ref, nw_ref, scw_ref,
                       yb_ref, yc_ref, hout_ref, cvlast_ref, h_scr):
    i = pl.program_id(1)
    nc = pl.num_programs(1)
    first = i == 0

    @pl.when(first)
    def _():
        h_scr[...] = jnp.zeros_like(h_scr)

    xc = xc_ref[...].astype(F32)
    xp = jnp.where(first, 0.0, xp_ref[...].astype(F32))
    act = _silu(_causal_conv(xc, xp, cw_ref, SSD_CONV) + cb_ref[...])
    xs, y_diag, ecs_e, xw_t, dec_t, b_groups, c_groups = _ssd_tile(
        act, dt_ref[...], dtb_ref[...], alog_ref[...], MIX_ROWS)
    gw = SSD_HEADS_PER_GROUP * SSD_HEAD_DIM
    y_off = []
    for g in range(SSD_GROUPS):
        rs = slice(g * gw, (g + 1) * gw)
        hg = h_scr[rs, :]
        y_off.append(_dot_nt(c_groups[g], hg.astype(BF16)))
        h_scr[rs, :] = dec_t[rs, MIX_ROWS - 1:MIX_ROWS] * hg + _dot(xw_t[rs, :].astype(BF16), b_groups[g])
    y = y_diag + jnp.concatenate(y_off, axis=-1) * ecs_e + de_ref[...] * xs
    yb_ref[...] = _gated_group_norm(y, z_ref[...].astype(F32), nw_ref[...]).astype(BF16)

    cvc = sccc_ref[...].astype(F32) * scvc_ref[...].astype(F32)
    cvp = jnp.where(first, 0.0, sccp_ref[...].astype(F32) * scvp_ref[...].astype(F32))
    conv_c = _causal_conv(cvc, cvp, scw_ref, SC_WIDTH)
    yc_ref[...] = (scb_ref[...].astype(F32) * conv_c).astype(BF16)

    @pl.when(i == nc - 1)
    def _():
        hout_ref[0] = h_scr[...]
        cvlast_ref[0] = cvc[MIX_ROWS - SUBLANES:, :]


def _ssd_prompt(p, dt, cw, cb, dtb, alog, de, nw, scw, n_seq, seq_len):
    nc = seq_len // MIX_ROWS
    r = MIX_ROWS
    n_prompt = n_seq * seq_len

    def cur(b, i):
        return b * nc + i

    def prev(b, i):
        return jnp.maximum(b * nc + i - 1, 0)

    def col(width, offset, which):
        return pl.BlockSpec((r, width), lambda b, i: (which(b, i), offset // width))

    def const(shape):
        return pl.BlockSpec(shape, lambda b, i: (0,) * len(shape))

    return pl.pallas_call(
        _ssd_prompt_kernel,
        grid=(n_seq, nc),
        in_specs=[
            col(SSD_D_INNER, COL_Z, cur),
            col(SSD_CONV_DIM, COL_XBC, cur), col(SSD_CONV_DIM, COL_XBC, prev),
            pl.BlockSpec((r, DT_PAD), lambda b, i: (cur(b, i), 0)),
            col(SC_DIM, COL_SCB, cur),
            col(SC_DIM, COL_SCC, cur), col(SC_DIM, COL_SCC, prev),
            col(SC_DIM, COL_SCV, cur), col(SC_DIM, COL_SCV, prev),
            const((SSD_CONV, SSD_CONV_DIM)), const((1, SSD_CONV_DIM)),
            const((1, DT_PAD)), const((1, DT_PAD)), const((1, SSD_D_INNER)), const((1, SSD_D_INNER)),
            const((SC_WIDTH, SC_DIM)),
        ],
        out_specs=[
            pl.BlockSpec((r, SSD_D_INNER), lambda b, i: (cur(b, i), 0)),
            pl.BlockSpec((r, SC_DIM), lambda b, i: (cur(b, i), 0)),
            pl.BlockSpec((1, SSD_D_INNER, SSD_STATE), lambda b, i: (b, 0, 0)),
            pl.BlockSpec((1, SUBLANES, SC_DIM), lambda b, i: (b, 0, 0)),
        ],
        out_shape=[
            jax.ShapeDtypeStruct((n_prompt, SSD_D_INNER), BF16),
            jax.ShapeDtypeStruct((n_prompt, SC_DIM), BF16),
            jax.ShapeDtypeStruct((n_seq, SSD_D_INNER, SSD_STATE), F32),
            jax.ShapeDtypeStruct((n_seq, SUBLANES, SC_DIM), F32),
        ],
        scratch_shapes=[pltpu.VMEM((SSD_D_INNER, SSD_STATE), F32)],
        compiler_params=_params(("parallel", "arbitrary")),
    )(p, p, p, dt, p, p, p, p, p, cw, cb, dtb, alog, de, nw, scw)


def _mix_sample_kernel(sink_ref, q_ref, kv_ref, z_ref, x_ref, dt_ref, scb_ref, scc_ref, scv_ref,
                       cos_ref, sin_ref, ck_ref, cv_ref, h0_ref, xbuf_ref, cbuf_ref,
                       cw_ref, cb_ref, dtb_ref, alog_ref, de_ref, nw_ref, scw_ref,
                       ya_ref, yb_ref, yc_ref, krot_ref, hnew_ref, cvout_ref):
    ns, t = SEQS_PER_STEP, DEC_SEQ
    wb = ck_ref.shape[1]

    q = _rope(q_ref[...].astype(F32), cos_ref[...], sin_ref[...])
    kv = kv_ref[...].astype(F32)
    kn = _rope(kv[:, :KV_WIDTH], cos_ref[...], sin_ref[...])
    vn = kv[:, KV_WIDTH:]
    krot_ref[...] = kn
    q3 = q.reshape(ns, t, ATTN_WIDTH)
    kn3 = kn.reshape(ns, t, KV_WIDTH)
    vn3 = vn.reshape(ns, t, KV_WIDTH)
    nq = Q_PER_KV * t
    qi = lax.broadcasted_iota(jnp.int32, (ns, nq, wb + t), 1) % t
    si = lax.broadcasted_iota(jnp.int32, (ns, nq, wb + t), 2)
    valid = ((si < wb) & (si > qi + (wb - WINDOW))) | ((si >= wb) & (si - wb <= qi))
    hrow = lax.broadcasted_iota(jnp.int32, (ns, nq, 1), 1) // t
    heads = [None] * N_HEADS
    for kh in range(N_KV_HEADS):
        hs = slice(kh * HEAD_DIM, (kh + 1) * HEAD_DIM)
        k_all = jnp.concatenate([ck_ref[:, :, hs], kn3[:, :, hs]], axis=1).astype(BF16)
        v_all = jnp.concatenate([cv_ref[:, :, hs], vn3[:, :, hs]], axis=1).astype(BF16)
        qg = jnp.concatenate(
            [q3[:, :, (kh * Q_PER_KV + g) * HEAD_DIM:(kh * Q_PER_KV + g + 1) * HEAD_DIM] for g in range(Q_PER_KV)],
            axis=1).astype(BF16)
        sink = jnp.zeros((ns, nq, 1), F32)
        for g in range(Q_PER_KV):
            sink = jnp.where(hrow == g, sink_ref[kh * Q_PER_KV + g], sink)
        logits = jnp.einsum('bqd,bsd->bqs', qg, k_all, preferred_element_type=F32) * ATTN_SCALE
        logits = jnp.where(valid, logits, -jnp.inf)
        m = jnp.maximum(jnp.max(logits, axis=-1, keepdims=True), sink)
        e = jnp.exp(logits - m)
        den = jnp.sum(e, axis=-1, keepdims=True) + jnp.exp(sink - m)
        o = jnp.einsum('bqs,bsd->bqd', e.astype(BF16), v_all, preferred_element_type=F32) / den
        for g in range(Q_PER_KV):
            heads[kh * Q_PER_KV + g] = o[:, g * t:(g + 1) * t, :]
    ya_ref[...] = jnp.concatenate(heads, axis=-1).reshape(ns * t, ATTN_WIDTH).astype(BF16)

    xc3 = x_ref[...].astype(F32).reshape(ns, t, SSD_CONV_DIM)
    xp3 = xbuf_ref[...].reshape(ns, t, SSD_CONV_DIM)
    conv = _causal_conv(xc3, xp3, cw_ref, SSD_CONV).reshape(ns * t, SSD_CONV_DIM)
    act = _silu(conv + cb_ref[...])
    xs, y_diag, ecs_e, xw_t, dec_t, b_groups, c_groups = _ssd_tile(
        act, dt_ref[...], dtb_ref[...], alog_ref[...], t)
    gw = SSD_HEADS_PER_GROUP * SSD_HEAD_DIM
    y_off = []
    for g in range(SSD_GROUPS):
        c3 = c_groups[g].reshape(ns, t, SSD_STATE)
        hg = h0_ref[:, g * gw:(g + 1) * gw, :].astype(BF16)
        y_off.append(jnp.einsum('btn,bqn->btq', c3, hg, preferred_element_type=F32).reshape(ns * t, gw))
    y = y_diag + jnp.concatenate(y_off, axis=-1) * ecs_e + de_ref[...] * xs
    yb_ref[...] = _gated_group_norm(y, z_ref[...].astype(F32), nw_ref[...]).astype(BF16)
    col = lax.broadcasted_iota(jnp.int32, xw_t.shape, 1) // t
    for b in range(ns):
        xw_b = jnp.where(col == b, xw_t, jnp.zeros_like(xw_t))
        upd = jnp.concatenate(
            [_dot(xw_b[g * gw:(g + 1) * gw, :].astype(BF16), b_groups[g]) for g in range(SSD_GROUPS)], axis=0)
        hnew_ref[b] = dec_t[:, b * t:b * t + 1] * h0_ref[b] + upd

    cvc = scc_ref[...].astype(F32) * scv_ref[...].astype(F32)
    cvout_ref[...] = cvc
    conv_c = _causal_conv(cvc.reshape(ns, t, SC_DIM), cbuf_ref[...].reshape(ns, t, SC_DIM), scw_ref, SC_WIDTH)
    yc_ref[...] = (scb_ref[...].astype(F32) * conv_c.reshape(ns * t, SC_DIM)).astype(BF16)


def _mix_sample(p, dt, sink, cos_s, sin_s, cache_k, cache_v, h0, xbuf, cbuf,
                cw, cb, dtb, alog, de, nw, scw, n_prompt, n_dec):
    r = MIX_ROWS
    ns = SEQS_PER_STEP
    steps = n_dec // ns
    base = n_prompt // r
    wb = cache_k.shape[1]

    def col(width, offset):
        return pl.BlockSpec((r, width), lambda i: (base + i, offset // width))

    def const(shape):
        return pl.BlockSpec(shape, lambda i: (0,) * len(shape))

    def rows(width):
        return pl.BlockSpec((r, width), lambda i: (i, 0))

    n_rows = n_dec * DEC_SEQ
    return pl.pallas_call(
        _mix_sample_kernel,
        grid=(steps,),
        in_specs=[
            pl.BlockSpec(memory_space=pltpu.SMEM),
            col(ATTN_WIDTH, COL_Q), col(2 * KV_WIDTH, COL_KV), col(SSD_D_INNER, COL_Z),
            col(SSD_CONV_DIM, COL_XBC),
            pl.BlockSpec((r, DT_PAD), lambda i: (base + i, 0)),
            col(SC_DIM, COL_SCB), col(SC_DIM, COL_SCC), col(SC_DIM, COL_SCV),
            const((r, LANES)), const((r, LANES)),
            pl.BlockSpec((ns, wb, KV_WIDTH), lambda i: (i, 0, 0)),
            pl.BlockSpec((ns, wb, KV_WIDTH), lambda i: (i, 0, 0)),
            pl.BlockSpec((ns, SSD_D_INNER, SSD_STATE), lambda i: (i, 0, 0)),
            rows(SSD_CONV_DIM), rows(SC_DIM),
            const((SSD_CONV, SSD_CONV_DIM)), const((1, SSD_CONV_DIM)),
            const((1, DT_PAD)), const((1, DT_PAD)), const((1, SSD_D_INNER)), const((1, SSD_D_INNER)),
            const((SC_WIDTH, SC_DIM)),
        ],
        out_specs=[
            rows(ATTN_WIDTH), rows(SSD_D_INNER), rows(SC_DIM),
            rows(KV_WIDTH),
            pl.BlockSpec((ns, SSD_D_INNER, SSD_STATE), lambda i: (i, 0, 0)),
            rows(SC_DIM),
        ],
        out_shape=[
            jax.ShapeDtypeStruct((n_rows, ATTN_WIDTH), BF16),
            jax.ShapeDtypeStruct((n_rows, SSD_D_INNER), BF16),
            jax.ShapeDtypeStruct((n_rows, SC_DIM), BF16),
            jax.ShapeDtypeStruct((n_rows, KV_WIDTH), F32),
            jax.ShapeDtypeStruct((n_dec, SSD_D_INNER, SSD_STATE), F32),
            jax.ShapeDtypeStruct((n_rows, SC_DIM), F32),
        ],
        compiler_params=_params(("parallel",)),
    )(sink, p, p, p, p, dt, p, p, p, cos_s, sin_s, cache_k, cache_v, h0, xbuf, cbuf,
      cw, cb, dtb, alog, de, nw, scw)


def _store_token_major(ref, v):
    n = v.shape[0]
    for s in range(ROW_CHUNKS):
        ref[pl.ds(s, n, stride=ROW_CHUNKS), :] = v[:, s * LANES:(s + 1) * LANES]


def _load_token_major(ref, n):
    return jnp.concatenate([ref[pl.ds(s, n, stride=ROW_CHUNKS), :] for s in range(ROW_CHUNKS)], axis=-1)


def _outproj_kernel(n_prompt_tiles, alpha,
                    x_ref, yap_ref, yas_ref, ybp_ref, ybs_ref, ycp_ref, ycs_ref, g_ref,
                    g1p_ref, g1s_ref, sh2p_ref, sh2s_ref, sc2p_ref, sc2s_ref,
                    wpa_ref, wpb_ref, wpc_ref, wout_ref, wr_ref, br_ref, lng_ref, lnb_ref,
                    x1_ref, h_ref, route_ref, cnt_ref, cnt_scr):
    i = pl.program_id(0)
    is_sample = i >= n_prompt_tiles
    tm = x_ref.shape[0]

    @pl.when(i == 0)
    def _():
        cnt_scr[...] = jnp.zeros_like(cnt_scr)

    ya = jnp.where(is_sample, yas_ref[...], yap_ref[...])
    yb = jnp.where(is_sample, ybs_ref[...], ybp_ref[...])
    yc = jnp.where(is_sample, ycs_ref[...], ycp_ref[...])
    gates = jax.nn.sigmoid(g_ref[...].astype(F32))
    merged = (gates[:, :D_MODEL] * _dot(ya, wpa_ref[...])
              + gates[:, D_MODEL:2 * D_MODEL] * _dot(yb, wpb_ref[...])
              + gates[:, 2 * D_MODEL:] * _dot(yc, wpc_ref[...]))
    mix = _dot(merged.astype(BF16), wout_ref[...])
    g1 = _pick_mod(is_sample, g1p_ref, g1s_ref)
    x1 = _layer_norm(alpha * x_ref[...] + g1 * mix, lng_ref[...], lnb_ref[...])
    x1_ref[...] = x1
    sh2 = _pick_mod(is_sample, sh2p_ref, sh2s_ref)
    sc2 = _pick_mod(is_sample, sc2p_ref, sc2s_ref)
    h = x1 * (1.0 + sc2) + sh2
    _store_token_major(h_ref, h)

    logits = _dot(h.astype(BF16), wr_ref[...]) + br_ref[...]
    lane = lax.broadcasted_iota(jnp.int32, logits.shape, 1).astype(F32)
    neg = -jnp.inf
    big = float(LANES)
    gl = jnp.where(lane < MOE_GROUPS, logits, neg)
    gmax = jnp.max(gl, axis=-1, keepdims=True)
    g_p = 1.0 / jnp.sum(jnp.exp(gl - gmax), axis=-1, keepdims=True)
    gidx = jnp.min(jnp.where(gl == gmax, lane, big), axis=-1, keepdims=True)
    lo = MOE_GROUPS + EXPERTS_PER_GROUP * gidx
    sel = jnp.where((lane >= lo) & (lane < lo + EXPERTS_PER_GROUP), logits, neg)
    m1 = jnp.max(sel, axis=-1, keepdims=True)
    i1 = jnp.min(jnp.where(sel == m1, lane, big), axis=-1, keepdims=True)
    sel2 = jnp.where(lane == i1, neg, sel)
    m2 = jnp.max(sel2, axis=-1, keepdims=True)
    i2 = jnp.min(jnp.where(sel2 == m2, lane, big), axis=-1, keepdims=True)
    ssum = jnp.sum(jnp.exp(sel - m1), axis=-1, keepdims=True)
    p1 = 1.0 / ssum
    p2 = jnp.exp(m2 - m1) / ssum
    w1 = g_p * (p1 / (p1 + p2))
    w2 = g_p * (p2 / (p1 + p2))
    e1 = i1 - MOE_GROUPS
    e2 = i2 - MOE_GROUPS

    onehot = ((lane == e1) | (lane == e2)).astype(F32)
    ri = lax.broadcasted_iota(jnp.int32, (tm, tm), 0)
    ci = lax.broadcasted_iota(jnp.int32, (tm, tm), 1)
    before = (ci < ri).astype(BF16)
    prefix = _dot(before, onehot.astype(BF16)) + cnt_scr[...]
    rank1 = jnp.sum(jnp.where(lane == e1, prefix, 0.0), axis=-1, keepdims=True)
    rank2 = jnp.sum(jnp.where(lane == e2, prefix, 0.0), axis=-1, keepdims=True)
    cnt_scr[...] = cnt_scr[...] + jnp.sum(onehot, axis=0, keepdims=True)
    cnt_ref[...] = cnt_scr[...]
    route = jnp.zeros_like(logits)
    for k, val in enumerate((e1, e2, w1, w2, rank1, rank2)):
        route = jnp.where(lane == float(k), val, route)
    route_ref[...] = route


def _output_projection(x, y_prompt, y_sample, p, mod_p, mod_s, wpa, wpb, wpc, wout, wr, br, lng, lnb,
                       n_prompt, seq_len, alpha):
    t_all = x.shape[0]
    tm = ROW_TILE
    npt = n_prompt // tm
    n_seq = mod_p.shape[0]
    mods = []
    for col in (2, 3, 4):
        mods.extend(_mod_specs(col, tm, npt, seq_len // tm, n_seq, 1))

    def rows(width):
        return pl.BlockSpec((tm, width), lambda i: (i, 0))

    def prompt_rows(width):
        return pl.BlockSpec((tm, width), lambda i: (jnp.minimum(i, npt - 1), 0))

    def sample_rows(width):
        return pl.BlockSpec((tm, width), lambda i: (jnp.maximum(i - npt, 0), 0))

    def const(shape):
        return pl.BlockSpec(shape, lambda i: (0,) * len(shape))

    return pl.pallas_call(
        functools.partial(_outproj_kernel, npt, alpha),
        grid=(t_all // tm,),
        in_specs=[
            rows(D_MODEL),
            prompt_rows(ATTN_WIDTH), sample_rows(ATTN_WIDTH),
            prompt_rows(SSD_D_INNER), sample_rows(SSD_D_INNER),
            prompt_rows(SC_DIM), sample_rows(SC_DIM),
            pl.BlockSpec((tm, 3 * D_MODEL), lambda i: (i, COL_GATES // (3 * D_MODEL))),
            *mods,
            const((ATTN_WIDTH, D_MODEL)), const((SSD_D_INNER, D_MODEL)), const((SC_DIM, D_MODEL)),
            const((D_MODEL, D_MODEL)), const((D_MODEL, LANES)), const((1, LANES)),
            const((1, D_MODEL)), const((1, D_MODEL)),
        ],
        out_specs=[
            rows(D_MODEL),
            pl.BlockSpec((tm * ROW_CHUNKS, LANES), lambda i: (i, 0)),
            rows(LANES),
            const((1, LANES)),
        ],
        out_shape=[
            jax.ShapeDtypeStruct((t_all, D_MODEL), F32),
            jax.ShapeDtypeStruct((t_all * ROW_CHUNKS, LANES), F32),
            jax.ShapeDtypeStruct((t_all, LANES), F32),
            jax.ShapeDtypeStruct((1, LANES), F32),
        ],
        scratch_shapes=[pltpu.VMEM((1, LANES), F32)],
        compiler_params=_params(("arbitrary",)),
    )(x, y_prompt[0], y_sample[0], y_prompt[1], y_sample[1], y_prompt[2], y_sample[2],
      p, mod_p, mod_s, mod_p, mod_s, mod_p, mod_s,
      wpa, wpb, wpc, wout, wr, br, lng, lnb)


def _token_copy(src, src_row, dst, dst_row, sem):
    return pltpu.make_async_copy(
        src.at[pl.ds(pl.multiple_of(src_row * ROW_CHUNKS, ROW_CHUNKS), ROW_CHUNKS)],
        dst.at[pl.ds(pl.multiple_of(dst_row * ROW_CHUNKS, ROW_CHUNKS), ROW_CHUNKS)],
        sem)


def _dispatch_kernel(d1_ref, d2_ref, pend_ref, cnt_ref, nb_ref, h_ref, xb_ref, zero_scr, zsem, sem):
    i = pl.program_id(0)
    tm = h_ref.shape[0] // ROW_CHUNKS
    blk_rows = EXPERT_BLOCK * ROW_CHUNKS
    n_blocks = xb_ref.shape[0] // blk_rows

    def zero_block(b):
        start = pl.multiple_of(b * blk_rows, blk_rows)
        return pltpu.make_async_copy(zero_scr, xb_ref.at[pl.ds(start, blk_rows)], zsem)

    def last_block(e):
        return pend_ref[e] // EXPERT_BLOCK - 1

    @pl.when(i == 0)
    def _():
        zero_scr[...] = jnp.zeros_like(zero_scr)
        for e in range(N_EXPERTS):
            @pl.when(cnt_ref[e] > 0)
            def _():
                zero_block(last_block(e)).start()
        lax.fori_loop(nb_ref[0], n_blocks, lambda b, c: (zero_block(b).start(), c)[1], 0)
        for e in range(N_EXPERTS):
            @pl.when(cnt_ref[e] > 0)
            def _():
                zero_block(last_block(e)).wait()
        lax.fori_loop(nb_ref[0], n_blocks, lambda b, c: (zero_block(b).wait(), c)[1], 0)

    def issue(t, carry):
        g = i * tm + t
        _token_copy(h_ref, t, xb_ref, d1_ref[g], sem.at[0]).start()
        _token_copy(h_ref, t, xb_ref, d2_ref[g], sem.at[1]).start()
        return carry

    lax.fori_loop(0, tm, issue, 0)
    for k in range(2):
        pltpu.make_async_copy(h_ref, xb_ref.at[pl.ds(0, tm * ROW_CHUNKS)], sem.at[k]).wait()


def _dispatch(h, dest1, dest2, pad_end, counts, n_used, n_slots):
    t_all = h.shape[0] // ROW_CHUNKS
    tm = ROW_TILE
    grid_spec = pltpu.PrefetchScalarGridSpec(
        num_scalar_prefetch=5,
        grid=(t_all // tm,),
        in_specs=[pl.BlockSpec((tm * ROW_CHUNKS, LANES), lambda i, *_: (i, 0))],
        out_specs=pl.BlockSpec(memory_space=pl.ANY),
        scratch_shapes=[
            pltpu.VMEM((EXPERT_BLOCK * ROW_CHUNKS, LANES), F32),
            pltpu.SemaphoreType.DMA(()),
            pltpu.SemaphoreType.DMA((2,)),
        ],
    )
    return pl.pallas_call(
        _dispatch_kernel,
        grid_spec=grid_spec,
        out_shape=jax.ShapeDtypeStruct((n_slots * ROW_CHUNKS, LANES), F32),
        compiler_params=_params(("arbitrary",)),
    )(dest1, dest2, pad_end, counts, n_used, h)


def _ffn_kernel(be_ref, nb_ref, xb_ref, wg_ref, wu_ref, wd_ref, yb_ref, x_scr, wg_scr, wu_scr, wd_scr):
    b = pl.program_id(0)

    @pl.when(b < nb_ref[0])
    def _():
        changed = (b == 0) | (be_ref[b] != be_ref[jnp.maximum(b - 1, 0)])

        @pl.when(changed)
        def _():
            wg_scr[...] = wg_ref[0].astype(BF16)
            wu_scr[...] = wu_ref[0].astype(BF16)
            wd_scr[...] = wd_ref[0].astype(BF16)

        for s in range(ROW_CHUNKS):
            x_scr[:, s * LANES:(s + 1) * LANES] = xb_ref[pl.ds(s, EXPERT_BLOCK, stride=ROW_CHUNKS), :].astype(BF16)
        x = x_scr[...]
        act = (_silu(_dot(x, wg_scr[...])) * _dot(x, wu_scr[...])).astype(BF16)
        _store_token_major(yb_ref, _dot(act, wd_scr[...]))

    @pl.when(b >= nb_ref[0])
    def _():
        yb_ref[...] = jnp.zeros_like(yb_ref)


def _expert_ffn(xb, block_expert, n_used, wg, wu, wd, n_blocks):
    blk_rows = EXPERT_BLOCK * ROW_CHUNKS

    def blk(b, be, nb):
        return jnp.minimum(b, nb[0] - 1)

    grid_spec = pltpu.PrefetchScalarGridSpec(
        num_scalar_prefetch=2,
        grid=(n_blocks,),
        in_specs=[
            pl.BlockSpec((blk_rows, LANES), lambda b, be, nb: (blk(b, be, nb), 0)),
            pl.BlockSpec((1, D_MODEL, EXPERT_FF), lambda b, be, nb: (be[blk(b, be, nb)], 0, 0)),
            pl.BlockSpec((1, D_MODEL, EXPERT_FF), lambda b, be, nb: (be[blk(b, be, nb)], 0, 0)),
            pl.BlockSpec((1, EXPERT_FF, D_MODEL), lambda b, be, nb: (be[blk(b, be, nb)], 0, 0)),
        ],
        out_specs=pl.BlockSpec((blk_rows, LANES), lambda b, be, nb: (b, 0)),
        scratch_shapes=[
            pltpu.VMEM((EXPERT_BLOCK, D_MODEL), BF16),
            pltpu.VMEM((D_MODEL, EXPERT_FF), BF16),
            pltpu.VMEM((D_MODEL, EXPERT_FF), BF16),
            pltpu.VMEM((EXPERT_FF, D_MODEL), BF16),
        ],
    )
    return pl.pallas_call(
        _ffn_kernel,
        grid_spec=grid_spec,
        out_shape=jax.ShapeDtypeStruct(xb.shape, F32),
        compiler_params=_params(("arbitrary",)),
    )(block_expert, n_used, xb, wg, wu, wd)


def _combine_kernel(n_prompt_tiles, alpha, d1_ref, d2_ref,
                    x1_ref, route_ref, g2p_ref, g2s_ref, lng_ref, lnb_ref, yb_ref,
                    xo_ref, buf_a, buf_b, sem):
    i = pl.program_id(0)
    tm = x1_ref.shape[0]

    def issue(t, carry):
        g = i * tm + t
        _token_copy(yb_ref, d1_ref[g], buf_a, t, sem.at[0]).start()
        _token_copy(yb_ref, d2_ref[g], buf_b, t, sem.at[1]).start()
        return carry

    lax.fori_loop(0, tm, issue, 0)
    pltpu.make_async_copy(yb_ref.at[pl.ds(0, tm * ROW_CHUNKS)], buf_a, sem.at[0]).wait()
    pltpu.make_async_copy(yb_ref.at[pl.ds(0, tm * ROW_CHUNKS)], buf_b, sem.at[1]).wait()
    route = route_ref[...]
    ffn = route[:, 2:3] * _load_token_major(buf_a, tm) + route[:, 3:4] * _load_token_major(buf_b, tm)
    g2 = _pick_mod(i >= n_prompt_tiles, g2p_ref, g2s_ref)
    xo_ref[...] = _layer_norm(alpha * x1_ref[...] + g2 * ffn, lng_ref[...], lnb_ref[...])


def _combine(x1, route, yb, dest1, dest2, mod_p, mod_s, lng, lnb, n_prompt, seq_len, alpha):
    t_all = x1.shape[0]
    tm = ROW_TILE
    npt = n_prompt // tm
    g2p, g2s = _mod_specs(5, tm, npt, seq_len // tm, mod_p.shape[0], 1)
    grid_spec = pltpu.PrefetchScalarGridSpec(
        num_scalar_prefetch=2,
        grid=(t_all // tm,),
        in_specs=[
            pl.BlockSpec((tm, D_MODEL), lambda i, *_: (i, 0)),
            pl.BlockSpec((tm, LANES), lambda i, *_: (i, 0)),
            g2p, g2s,
            pl.BlockSpec((1, D_MODEL), lambda i, *_: (0, 0)),
            pl.BlockSpec((1, D_MODEL), lambda i, *_: (0, 0)),
            pl.BlockSpec(memory_space=pl.ANY),
        ],
        out_specs=pl.BlockSpec((tm, D_MODEL), lambda i, *_: (i, 0)),
        scratch_shapes=[
            pltpu.VMEM((tm * ROW_CHUNKS, LANES), F32),
            pltpu.VMEM((tm * ROW_CHUNKS, LANES), F32),
            pltpu.SemaphoreType.DMA((2,)),
        ],
    )
    return pl.pallas_call(
        functools.partial(_combine_kernel, npt, alpha),
        grid_spec=grid_spec,
        out_shape=jax.ShapeDtypeStruct((t_all, D_MODEL), F32),
        compiler_params=_params(("arbitrary",)),
    )(dest1, dest2, x1, route, mod_p, mod_s, lng, lnb, yb)


def _rope_tables(pos):
    inv = jnp.power(ROPE_THETA, -jnp.arange(ROT_HALF, dtype=F32) * (2.0 / ROT_DIM))
    ang = pos.astype(F32)[:, None] * inv[None, :]
    cos, sin = jnp.cos(ang), jnp.sin(ang)
    rest = HEAD_DIM - ROT_DIM
    n = pos.shape[0]
    cos_h = jnp.concatenate([cos, cos, jnp.ones((n, rest), F32)], axis=-1)
    sin_h = jnp.concatenate([-sin, sin, jnp.zeros((n, rest), F32)], axis=-1)
    reps = LANES // HEAD_DIM
    return jnp.tile(cos_h, (1, reps)), jnp.tile(sin_h, (1, reps))


def _permute_w_in(w_in):
    sizes = (ATTN_WIDTH, KV_WIDTH, KV_WIDTH, SSD_D_INNER, SSD_CONV_DIM, SSD_HEADS,
             SC_DIM, SC_DIM, SC_DIM, 3 * D_MODEL)
    offs = np.concatenate([[0], np.cumsum(sizes)])
    q, k, v, z, xbc, dt, scb, scc, scv, gates = (w_in[..., offs[n]:offs[n + 1]] for n in range(len(sizes)))
    main = jnp.concatenate([q, z, xbc, scb, scc, gates, scv, k, v], axis=-1).astype(BF16)
    dt = jnp.pad(dt, ((0, 0), (0, 0), (0, DT_PAD - SSD_HEADS))).astype(BF16)
    return main, dt


def _pad_lanes(v, width):
    return jnp.pad(v, ((0, 0), (0, width - v.shape[-1])))


def kernel(x_prompt, x_sample, c_prompt, c_sample, cache_attn_k, cache_attn_v, state_ssm, state_ssd_conv, state_short_conv, w_ada, b_ada, w_in, attn_sink, ssd_conv_w, ssd_conv_b, ssd_dt_bias, ssd_a_log, ssd_d, ssd_norm_w, sc_conv_w, w_pa, w_pb, w_pc, w_out, ln1_g, ln1_b, ln2_g, ln2_b, router_g_w, router_g_b, router_e_w, router_e_b, moe_w_gate, moe_w_up, moe_w_down):
    depth = w_in.shape[0]
    n_seq, seq_len, _ = x_prompt.shape
    n_dec, dec_len, _ = x_sample.shape
    wb = cache_attn_k.shape[2]
    past_len = 8192
    assert dec_len == DEC_SEQ and wb == WINDOW
    assert seq_len % ROW_TILE_IN == 0 and (n_dec * dec_len) % ROW_TILE_IN == 0
    n_prompt = n_seq * seq_len
    n_sample = n_dec * dec_len
    t_all = n_prompt + n_sample
    alpha = (2 * depth) ** 0.25

    mod = _modulation(jnp.concatenate([c_prompt, c_sample], axis=0), w_ada, b_ada)
    mod_p_all = mod[:, :n_seq].reshape(depth, n_seq, 1, 6 * D_MODEL)
    mod_s_all = jnp.repeat(mod[:, n_seq:], dec_len, axis=1).reshape(depth, 1, n_sample, 6 * D_MODEL)

    w_main_all, w_dt_all = _permute_w_in(w_in)
    cos_p, sin_p = _rope_tables(jnp.arange(seq_len, dtype=jnp.int32))
    cos_s, sin_s = _rope_tables(past_len + (jnp.arange(MIX_ROWS, dtype=jnp.int32) % dec_len))

    a_total = 2 * t_all
    n_blocks = (a_total + N_EXPERTS * (EXPERT_BLOCK - 1)) // EXPERT_BLOCK
    n_slots = n_blocks * EXPERT_BLOCK

    x = jnp.concatenate([x_prompt.reshape(n_prompt, D_MODEL), x_sample.reshape(n_sample, D_MODEL)], axis=0)
    outs_p = [[] for _ in range(5)]
    outs_s = [[] for _ in range(5)]
    for l in range(depth):
        mod_p, mod_s = mod_p_all[l], mod_s_all[l]
        p, dt = _input_projection(x, mod_p, mod_s, w_main_all[l], w_dt_all[l], n_prompt, seq_len)

        cw, cb = ssd_conv_w[l], ssd_conv_b[l][None]
        dtb = _pad_lanes(ssd_dt_bias[l][None], DT_PAD)
        alog = _pad_lanes(ssd_a_log[l][None], DT_PAD)
        de = jnp.repeat(ssd_d[l], SSD_HEAD_DIM)[None]
        nw = ssd_norm_w[l][None]
        scw = sc_conv_w[l]

        ya_p, krot_p = _attention_prompt(p, attn_sink[l], cos_p, sin_p, n_seq, seq_len)
        yb_p, yc_p, h_p, cv_p = _ssd_prompt(p, dt, cw, cb, dtb, alog, de, nw, scw, n_seq, seq_len)

        xbuf = jnp.pad(state_ssd_conv[l], ((0, 0), (dec_len - (SSD_CONV - 1), 0), (0, 0)))
        cbuf = jnp.pad(state_short_conv[l], ((0, 0), (dec_len - (SC_WIDTH - 1), 0), (0, 0)))
        ya_s, yb_s, yc_s, krot_s, h_s, cv_s = _mix_sample(
            p, dt, attn_sink[l], cos_s, sin_s,
            cache_attn_k[l].reshape(n_dec, wb, KV_WIDTH), cache_attn_v[l].reshape(n_dec, wb, KV_WIDTH),
            state_ssm[l].reshape(n_dec, SSD_D_INNER, SSD_STATE),
            xbuf.reshape(n_sample, SSD_CONV_DIM), cbuf.reshape(n_sample, SC_DIM),
            cw, cb, dtb, alog, de, nw, scw, n_prompt, n_dec)

        wr = jnp.pad(jnp.concatenate([router_g_w[l], router_e_w[l]], axis=-1),
                     ((0, 0), (0, LANES - MOE_GROUPS - N_EXPERTS))).astype(BF16)
        br = _pad_lanes(jnp.concatenate([router_g_b[l], router_e_b[l]])[None], LANES)
        x1, h, route, counts = _output_projection(
            x, (ya_p, yb_p, yc_p), (ya_s, yb_s, yc_s), p, mod_p, mod_s,
            w_pa[l].astype(BF16), w_pb[l].astype(BF16), w_pc[l].astype(BF16), w_out[l].astype(BF16),
            wr, br, ln1_g[l][None], ln1_b[l][None], n_prompt, seq_len, alpha)

        cnt = counts[0, :N_EXPERTS].astype(jnp.int32)
        pad_cnt = (cnt + EXPERT_BLOCK - 1) // EXPERT_BLOCK * EXPERT_BLOCK
        pad_end = jnp.cumsum(pad_cnt)
        pad_start = pad_end - pad_cnt
        n_used = (pad_end[-1] // EXPERT_BLOCK).astype(jnp.int32)[None]
        block_expert = jnp.clip(
            jnp.searchsorted(pad_end, jnp.arange(n_blocks, dtype=jnp.int32) * EXPERT_BLOCK, side='right'),
            0, N_EXPERTS - 1).astype(jnp.int32)
        e1 = route[:, 0].astype(jnp.int32)
        e2 = route[:, 1].astype(jnp.int32)
        dest1 = pad_start[e1] + route[:, 4].astype(jnp.int32)
        dest2 = pad_start[e2] + route[:, 5].astype(jnp.int32)

        xb = _dispatch(h, dest1, dest2, pad_end.astype(jnp.int32), cnt, n_used, n_slots)
        yb_slots = _expert_ffn(xb, block_expert, n_used, moe_w_gate[l], moe_w_up[l], moe_w_down[l], n_blocks)
        x = _combine(x1, route, yb_slots, dest1, dest2, mod_p, mod_s, ln2_g[l][None], ln2_b[l][None],
                     n_prompt, seq_len, alpha)

        p_prompt = p[:n_prompt].reshape(n_seq, seq_len, P_MAIN)
        outs_p[0].append(krot_p.reshape(n_seq, wb, N_KV_HEADS, HEAD_DIM))
        outs_p[1].append(p_prompt[:, seq_len - wb:, COL_KV + KV_WIDTH:COL_KV + 2 * KV_WIDTH]
                         .astype(F32).reshape(n_seq, wb, N_KV_HEADS, HEAD_DIM))
        outs_p[2].append(h_p.reshape(n_seq, SSD_HEADS, SSD_HEAD_DIM, SSD_STATE))
        outs_p[3].append(p_prompt[:, seq_len - (SSD_CONV - 1):, COL_XBC:COL_XBC + SSD_CONV_DIM].astype(F32))
        outs_p[4].append(cv_p[:, SUBLANES - (SC_WIDTH - 1):, :])
        p_sample = p[n_prompt:].reshape(n_dec, dec_len, P_MAIN)
        k_new = krot_s.reshape(n_dec, dec_len, N_KV_HEADS, HEAD_DIM)
        v_new = (p_sample[:, :, COL_KV + KV_WIDTH:COL_KV + 2 * KV_WIDTH].astype(F32)
                 .reshape(n_dec, dec_len, N_KV_HEADS, HEAD_DIM))
        outs_s[0].append(jnp.concatenate([cache_attn_k[l][:, dec_len:], k_new], axis=1))
        outs_s[1].append(jnp.concatenate([cache_attn_v[l][:, dec_len:], v_new], axis=1))
        outs_s[2].append(h_s.reshape(n_dec, SSD_HEADS, SSD_HEAD_DIM, SSD_STATE))
        outs_s[3].append(p_sample[:, dec_len - (SSD_CONV - 1):, COL_XBC:COL_XBC + SSD_CONV_DIM].astype(F32))
        outs_s[4].append(cv_s.reshape(n_dec, dec_len, SC_DIM)[:, dec_len - (SC_WIDTH - 1):, :])

    y_prompt = x[:n_prompt].reshape(n_seq, seq_len, D_MODEL)
    y_sample = x[n_prompt:].reshape(n_dec, dec_len, D_MODEL)
    return (y_prompt, y_sample, *[jnp.stack(o) for o in outs_p], *[jnp.stack(o) for o in outs_s])
```

```python
import functools

import jax
import jax.numpy as jnp
import numpy as np
from jax import lax
from jax.experimental import pallas as pl
from jax.experimental.pallas import tpu as pltpu

F32 = jnp.float32
BF16 = jnp.bfloat16

D_MODEL = 1024
HEAD_DIM = 64
N_HEADS = 8
N_KV_HEADS = 2
Q_PER_KV = N_HEADS // N_KV_HEADS
ATTN_WIDTH = N_HEADS * HEAD_DIM
KV_WIDTH = N_KV_HEADS * HEAD_DIM
WINDOW = 128
ROT_DIM = HEAD_DIM // 4
ROT_HALF = ROT_DIM // 2
ROPE_THETA = 500000.0
ATTN_SCALE = HEAD_DIM ** -0.5
SSD_D_INNER = 512
SSD_HEAD_DIM = 64
SSD_HEADS = 8
SSD_GROUPS = 2
SSD_HEADS_PER_GROUP = SSD_HEADS // SSD_GROUPS
SSD_STATE = 128
SSD_CONV = 4
SSD_CONV_DIM = SSD_D_INNER + 2 * SSD_GROUPS * SSD_STATE
SSD_CHUNK = 128
SC_DIM = 512
SC_WIDTH = 3
MOE_GROUPS = 4
EXPERTS_PER_GROUP = 8
N_EXPERTS = MOE_GROUPS * EXPERTS_PER_GROUP
EXPERT_FF = 512
LN_EPS = 1e-5
RMS_EPS = 1e-5

SUBLANES = 8
LANES = 128
VMEM_LIMIT = 56 * 1024 * 1024

COL_Q = 0
COL_Z = 512
COL_XBC = 1024
COL_SCB = 2048
COL_SCC = 2560
COL_GATES = 3072
COL_SCV = 6144
COL_KV = 6656
P_MAIN = 6912
DT_PAD = 128

ROW_TILE_IN = 1024
COL_TILE_IN = 768
ROW_TILE = 512
MIX_ROWS = 128
DEC_SEQ = 8
SEQS_PER_STEP = MIX_ROWS // DEC_SEQ
EXPERT_BLOCK = 256
ROW_CHUNKS = D_MODEL // LANES
ISSUE_UNROLL = 8


def _silu(v):
    return v * jax.nn.sigmoid(v)


def _dot(a, b):
    return jnp.dot(a, b, preferred_element_type=F32)


def _dot_nt(a, b):
    return lax.dot_general(a, b, (((1,), (1,)), ((), ())), preferred_element_type=F32)


def _dot_exact(a, b):
    return jnp.dot(a, b, preferred_element_type=F32, precision=lax.Precision.HIGHEST)


def _params(sem):
    return pltpu.CompilerParams(dimension_semantics=sem, vmem_limit_bytes=VMEM_LIMIT)


def _pick_mod(is_sample, prompt_ref, sample_ref):
    return jnp.where(is_sample, sample_ref[0], prompt_ref[0])


def _layer_norm(v, g, b):
    mu = jnp.mean(v, axis=-1, keepdims=True)
    c = v - mu
    var = jnp.mean(c * c, axis=-1, keepdims=True)
    return c * lax.rsqrt(var + LN_EPS) * g + b


def _mod_kernel(c_ref, w_ref, b_ref, o_ref):
    s = _silu(c_ref[...]).astype(BF16)
    o_ref[0] = _dot(s, w_ref[0].astype(BF16)) + b_ref[0]


def _modulation(c_all, w_ada, b_ada):
    depth, _, width = w_ada.shape
    n = c_all.shape[0]
    tn = 1536
    return pl.pallas_call(
        _mod_kernel,
        grid=(depth, width // tn),
        in_specs=[
            pl.BlockSpec((n, D_MODEL), lambda l, j: (0, 0)),
            pl.BlockSpec((1, D_MODEL, tn), lambda l, j: (l, 0, j)),
            pl.BlockSpec((1, 1, tn), lambda l, j: (l, 0, j)),
        ],
        out_specs=pl.BlockSpec((1, n, tn), lambda l, j: (l, 0, j)),
        out_shape=jax.ShapeDtypeStruct((depth, n, width), F32),
        compiler_params=_params(("parallel", "parallel")),
    )(c_all, w_ada, b_ada.reshape(depth, 1, width))


def _inproj_kernel(n_prompt_tiles, x_ref, shp_ref, shs_ref, scp_ref, scs_ref, w_ref, wdt_ref,
                   p_ref, dt_ref, u_scr):
    i = pl.program_id(0)
    j = pl.program_id(1)

    @pl.when(j == 0)
    def _():
        is_sample = i >= n_prompt_tiles
        sh = _pick_mod(is_sample, shp_ref, shs_ref)
        sc = _pick_mod(is_sample, scp_ref, scs_ref)
        u = (x_ref[...] * (1.0 + sc) + sh).astype(BF16)
        u_scr[...] = u
        dt_ref[...] = _dot(u, wdt_ref[...])

    p_ref[...] = _dot(u_scr[...], w_ref[...]).astype(BF16)


def _mod_specs(layer, col, row_tile, n_prompt_tiles, tiles_per_seq, n_seq):
    def prompt_map(i, *_):
        return (layer, jnp.minimum(i // tiles_per_seq, n_seq - 1), 0, col)

    def sample_map(i, *_):
        return (layer, 0, jnp.maximum(i - n_prompt_tiles, 0), col)

    return (pl.BlockSpec((None, 1, 1, D_MODEL), prompt_map),
            pl.BlockSpec((None, 1, row_tile, D_MODEL), sample_map))


def _layer_spec(layer, shape):
    return pl.BlockSpec((None, *shape), lambda *_: (layer,) + (0,) * len(shape))


def _input_projection(x, mod_p, mod_s, w_main, w_dt, layer, n_prompt, seq_len):
    t_all = x.shape[0]
    tm, tn = ROW_TILE_IN, COL_TILE_IN
    npt = n_prompt // tm
    n_seq = mod_p.shape[1]
    shp, shs = _mod_specs(layer, 0, tm, npt, seq_len // tm, n_seq)
    scp, scs = _mod_specs(layer, 1, tm, npt, seq_len // tm, n_seq)
    return pl.pallas_call(
        functools.partial(_inproj_kernel, npt),
        grid=(t_all // tm, P_MAIN // tn),
        in_specs=[
            pl.BlockSpec((tm, D_MODEL), lambda i, j: (i, 0)),
            shp, shs, scp, scs,
            pl.BlockSpec((None, D_MODEL, tn), lambda i, j: (layer, 0, j)),
            _layer_spec(layer, (D_MODEL, DT_PAD)),
        ],
        out_specs=[
            pl.BlockSpec((tm, tn), lambda i, j: (i, j)),
            pl.BlockSpec((tm, DT_PAD), lambda i, j: (i, 0)),
        ],
        out_shape=[
            jax.ShapeDtypeStruct((t_all, P_MAIN), BF16),
            jax.ShapeDtypeStruct((t_all, DT_PAD), F32),
        ],
        scratch_shapes=[pltpu.VMEM((tm, D_MODEL), BF16)],
        compiler_params=_params(("parallel", "arbitrary")),
    )(x, mod_p, mod_s, mod_p, mod_s, w_main, w_dt)


def _rope(v, cos, sin):
    width = v.shape[-1]
    reps = width // LANES
    if reps > 1:
        cos = jnp.concatenate([cos] * reps, axis=-1)
        sin = jnp.concatenate([sin] * reps, axis=-1)
    lane = lax.broadcasted_iota(jnp.int32, v.shape, 1) % HEAD_DIM
    partner = jnp.where(lane < ROT_HALF,
                        pltpu.roll(v, width - ROT_HALF, 1),
                        pltpu.roll(v, ROT_HALF, 1))
    return v * cos + partner * sin


def _shift_rows(cur, prev, k):
    axis = cur.ndim - 2
    idx = lax.broadcasted_iota(jnp.int32, cur.shape, axis)
    return jnp.where(idx < k, pltpu.roll(prev, k, axis), pltpu.roll(cur, k, axis))


def _causal_conv(cur, prev, w_ref, width):
    out = cur * w_ref[width - 1:width, :]
    for k in range(1, width):
        out = out + _shift_rows(cur, prev, k) * w_ref[width - 1 - k:width - k, :]
    return out


def _head_expand():
    r = lax.broadcasted_iota(jnp.int32, (LANES, SSD_D_INNER), 0)
    c = lax.broadcasted_iota(jnp.int32, (LANES, SSD_D_INNER), 1)
    return (c // SSD_HEAD_DIM == r).astype(F32)


def _head_expand_t():
    r = lax.broadcasted_iota(jnp.int32, (SSD_D_INNER, LANES), 0)
    c = lax.broadcasted_iota(jnp.int32, (SSD_D_INNER, LANES), 1)
    return (r // SSD_HEAD_DIM == c).astype(F32)


def _ssd_tile(act, dt_raw, dtb, alog, seq_rows):
    rows = MIX_ROWS
    xs = act[:, :SSD_D_INNER]
    bm = act[:, SSD_D_INNER:SSD_D_INNER + SSD_GROUPS * SSD_STATE]
    cm = act[:, SSD_D_INNER + SSD_GROUPS * SSD_STATE:]
    v = dt_raw + dtb
    dt = jnp.maximum(v, 0.0) + jnp.log1p(jnp.exp(-jnp.abs(v)))
    a = -jnp.exp(alog)
    dta = dt * a
    ri = lax.broadcasted_iota(jnp.int32, (rows, rows), 0)
    ci = lax.broadcasted_iota(jnp.int32, (rows, rows), 1)
    same = (ri // seq_rows) == (ci // seq_rows)
    causal = same & (ci <= ri)
    cs = _dot_exact(causal.astype(F32), dta)
    tot = _dot_exact(same.astype(F32), dta)
    expand = _head_expand()
    dt_e = _dot_exact(dt, expand)
    cs_e = _dot_exact(cs, expand)
    tot_e = _dot_exact(tot, expand)
    cs_t = cs.T
    dec_t = jnp.exp(_dot_exact(_head_expand_t(), tot.T))
    dtx = xs * dt_e
    xw_t = (dtx * jnp.exp(tot_e - cs_e)).T
    b_groups, c_groups, y_parts = [], [], []
    for g in range(SSD_GROUPS):
        bg = bm[:, g * SSD_STATE:(g + 1) * SSD_STATE].astype(BF16)
        cg = cm[:, g * SSD_STATE:(g + 1) * SSD_STATE].astype(BF16)
        b_groups.append(bg)
        c_groups.append(cg)
        cb = _dot_nt(cg, bg)
        for hh in range(SSD_HEADS_PER_GROUP):
            h = g * SSD_HEADS_PER_GROUP + hh
            seg = cs[:, h:h + 1] - cs_t[h:h + 1, :]
            decay = jnp.where(causal, jnp.exp(jnp.where(causal, seg, 0.0)), 0.0)
            y_parts.append(_dot((cb * decay).astype(BF16),
                                dtx[:, h * SSD_HEAD_DIM:(h + 1) * SSD_HEAD_DIM].astype(BF16)))
    y_diag = jnp.concatenate(y_parts, axis=-1)
    return xs, y_diag, jnp.exp(cs_e), xw_t, dec_t, b_groups, c_groups


def _gated_group_norm(y, z, nw):
    y = y * _silu(z)
    half = SSD_D_INNER // SSD_GROUPS
    parts = []
    for g in range(SSD_GROUPS):
        yg = y[:, g * half:(g + 1) * half]
        parts.append(yg * lax.rsqrt(jnp.mean(yg * yg, axis=-1, keepdims=True) + RMS_EPS))
    return jnp.concatenate(parts, axis=-1) * nw


def _attn_prompt_kernel(sink_ref, q_ref, kvc_ref, kvp_ref, cosc_ref, sinc_ref, cosp_ref, sinp_ref,
                        ya_ref, krot_ref):
    i = pl.program_id(1)
    nb = pl.num_programs(1)
    w = WINDOW
    q = _rope(q_ref[...].astype(F32), cosc_ref[...], sinc_ref[...])
    kvc = kvc_ref[...].astype(F32)
    kvp = kvp_ref[...].astype(F32)
    kc = _rope(kvc[:, :KV_WIDTH], cosc_ref[...], sinc_ref[...])
    kp = _rope(kvp[:, :KV_WIDTH], cosp_ref[...], sinp_ref[...])
    vc = kvc[:, KV_WIDTH:]
    vp = kvp[:, KV_WIDTH:]

    @pl.when(i == nb - 1)
    def _():
        krot_ref[0] = kc

    rows = Q_PER_KV * w
    r = lax.broadcasted_iota(jnp.int32, (rows, 2 * w), 0)
    s = lax.broadcasted_iota(jnp.int32, (rows, 2 * w), 1)
    diff = w + (r % w) - s
    first_key = jnp.where(i > 0, 0, w)
    valid = (diff >= 0) & (diff < w) & (s >= first_key)
    rcol = lax.broadcasted_iota(jnp.int32, (rows, 1), 0)
    outs = []
    for kh in range(N_KV_HEADS):
        hs = slice(kh * HEAD_DIM, (kh + 1) * HEAD_DIM)
        k2 = jnp.concatenate([kp[:, hs], kc[:, hs]], axis=0).astype(BF16)
        v2 = jnp.concatenate([vp[:, hs], vc[:, hs]], axis=0).astype(BF16)
        qg = jnp.concatenate(
            [q[:, (kh * Q_PER_KV + g) * HEAD_DIM:(kh * Q_PER_KV + g + 1) * HEAD_DIM] for g in range(Q_PER_KV)],
            axis=0).astype(BF16)
        sink = jnp.zeros((rows, 1), F32)
        for g in range(Q_PER_KV):
            sink = jnp.where(rcol // w == g, sink_ref[kh * Q_PER_KV + g], sink)
        logits = jnp.where(valid, _dot_nt(qg, k2) * ATTN_SCALE, -jnp.inf)
        m = jnp.maximum(jnp.max(logits, axis=-1, keepdims=True), sink)
        e = jnp.exp(logits - m)
        den = jnp.sum(e, axis=-1, keepdims=True) + jnp.exp(sink - m)
        o = _dot(e.astype(BF16), v2) / den
        for g in range(Q_PER_KV):
            outs.append(o[g * w:(g + 1) * w, :])
    ya_ref[...] = jnp.concatenate(outs, axis=-1).astype(BF16)


def _attention_prompt(p, sink, cos_p, sin_p, n_seq, seq_len):
    nb = seq_len // WINDOW
    n_prompt = n_seq * seq_len
    w = WINDOW

    def cur(b, i):
        return b * nb + i

    def prev(b, i):
        return jnp.maximum(b * nb + i - 1, 0)

    return pl.pallas_call(
        _attn_prompt_kernel,
        grid=(n_seq, nb),
        in_specs=[
            pl.BlockSpec(memory_space=pltpu.SMEM),
            pl.BlockSpec((w, ATTN_WIDTH), lambda b, i: (cur(b, i), COL_Q // ATTN_WIDTH)),
            pl.BlockSpec((w, 2 * KV_WIDTH), lambda b, i: (cur(b, i), COL_KV // (2 * KV_WIDTH))),
            pl.BlockSpec((w, 2 * KV_WIDTH), lambda b, i: (prev(b, i), COL_KV // (2 * KV_WIDTH))),
            pl.BlockSpec((w, LANES), lambda b, i: (i, 0)),
            pl.BlockSpec((w, LANES), lambda b, i: (i, 0)),
            pl.BlockSpec((w, LANES), lambda b, i: (jnp.maximum(i - 1, 0), 0)),
            pl.BlockSpec((w, LANES), lambda b, i: (jnp.maximum(i - 1, 0), 0)),
        ],
        out_specs=[
            pl.BlockSpec((w, ATTN_WIDTH), lambda b, i: (cur(b, i), 0)),
            pl.BlockSpec((1, w, KV_WIDTH), lambda b, i: (b, 0, 0)),
        ],
        out_shape=[
            jax.ShapeDtypeStruct((n_prompt, ATTN_WIDTH), BF16),
            jax.ShapeDtypeStruct((n_seq, w, KV_WIDTH), F32),
        ],
        compiler_params=_params(("parallel", "arbitrary")),
    )(sink, p, p, p, cos_p, sin_p, cos_p, sin_p)


def _ssd_prompt_kernel(z_ref, xc_ref, xp_ref, dt_ref, scb_ref, sccc_ref, sccp_ref, scvc_ref, scvp_ref,
                       cw_ref, cb_ref, dtb_ref, alog_ref, de_ref, nw_ref, scw_ref,
                       yb_ref, yc_ref, hout_ref, cvlast_ref, h_scr):
    i = pl.program_id(1)
    nc = pl.num_programs(1)
    first = i == 0

    @pl.when(first)
    def _():
        h_scr[...] = jnp.zeros_like(h_scr)

    xc = xc_ref[...].astype(F32)
    xp = jnp.where(first, 0.0, xp_ref[...].astype(F32))
    act = _silu(_causal_conv(xc, xp, cw_ref, SSD_CONV) + cb_ref[...])
    xs, y_diag, ecs_e, xw_t, dec_t, b_groups, c_groups = _ssd_tile(
        act, dt_ref[...], dtb_ref[...], alog_ref[...], MIX_ROWS)
    gw = SSD_HEADS_PER_GROUP * SSD_HEAD_DIM
    y_off = []
    for g in range(SSD_GROUPS):
        rs = slice(g * gw, (g + 1) * gw)
        hg = h_scr[rs, :]
        y_off.append(_dot_nt(c_groups[g], hg.astype(BF16)))
        h_scr[rs, :] = dec_t[rs, MIX_ROWS - 1:MIX_ROWS] * hg + _dot(xw_t[rs, :].astype(BF16), b_groups[g])
    y = y_diag + jnp.concatenate(y_off, axis=-1) * ecs_e + de_ref[...] * xs
    yb_ref[...] = _gated_group_norm(y, z_ref[...].astype(F32), nw_ref[...]).astype(BF16)

    cvc = sccc_ref[...].astype(F32) * scvc_ref[...].astype(F32)
    cvp = jnp.where(first, 0.0, sccp_ref[...].astype(F32) * scvp_ref[...].astype(F32))
    conv_c = _causal_conv(cvc, cvp, scw_ref, SC_WIDTH)
    yc_ref[...] = (scb_ref[...].astype(F32) * conv_c).astype(BF16)

    @pl.when(i == nc - 1)
    def _():
        hout_ref[0] = h_scr[...]
        cvlast_ref[0] = cvc[MIX_ROWS - SUBLANES:, :]


def _ssd_prompt(p, dt, cw, cb, dtb, alog, de, nw, scw, n_seq, seq_len):
    nc = seq_len // MIX_ROWS
    r = MIX_ROWS
    n_prompt = n_seq * seq_len

    def cur(b, i):
        return b * nc + i

    def prev(b, i):
        return jnp.maximum(b * nc + i - 1, 0)

    def col(width, offset, which):
        return pl.BlockSpec((r, width), lambda b, i: (which(b, i), offset // width))

    def const(shape):
        return pl.BlockSpec(shape, lambda b, i: (0,) * len(shape))

    return pl.pallas_call(
        _ssd_prompt_kernel,
        grid=(n_seq, nc),
        in_specs=[
            col(SSD_D_INNER, COL_Z, cur),
            col(SSD_CONV_DIM, COL_XBC, cur), col(SSD_CONV_DIM, COL_XBC, prev),
            pl.BlockSpec((r, DT_PAD), lambda b, i: (cur(b, i), 0)),
            col(SC_DIM, COL_SCB, cur),
            col(SC_DIM, COL_SCC, cur), col(SC_DIM, COL_SCC, prev),
            col(SC_DIM, COL_SCV, cur), col(SC_DIM, COL_SCV, prev),
            const((SSD_CONV, SSD_CONV_DIM)), const((1, SSD_CONV_DIM)),
            const((1, DT_PAD)), const((1, DT_PAD)), const((1, SSD_D_INNER)), const((1, SSD_D_INNER)),
            const((SC_WIDTH, SC_DIM)),
        ],
        out_specs=[
            pl.BlockSpec((r, SSD_D_INNER), lambda b, i: (cur(b, i), 0)),
            pl.BlockSpec((r, SC_DIM), lambda b, i: (cur(b, i), 0)),
            pl.BlockSpec((1, SSD_D_INNER, SSD_STATE), lambda b, i: (b, 0, 0)),
            pl.BlockSpec((1, SUBLANES, SC_DIM), lambda b, i: (b, 0, 0)),
        ],
        out_shape=[
            jax.ShapeDtypeStruct((n_prompt, SSD_D_INNER), BF16),
            jax.ShapeDtypeStruct((n_prompt, SC_DIM), BF16),
            jax.ShapeDtypeStruct((n_seq, SSD_D_INNER, SSD_STATE), F32),
            jax.ShapeDtypeStruct((n_seq, SUBLANES, SC_DIM), F32),
        ],
        scratch_shapes=[pltpu.VMEM((SSD_D_INNER, SSD_STATE), F32)],
        compiler_params=_params(("parallel", "arbitrary")),
    )(p, p, p, dt, p, p, p, p, p, cw, cb, dtb, alog, de, nw, scw)


def _mix_sample_kernel(sink_ref, q_ref, kv_ref, z_ref, x_ref, dt_ref, scb_ref, scc_ref, scv_ref,
                       cos_ref, sin_ref, ck_ref, cv_ref, h0_ref, xbuf_ref, cbuf_ref,
                       cw_ref, cb_ref, dtb_ref, alog_ref, de_ref, nw_ref, scw_ref,
                       ya_ref, yb_ref, yc_ref, knew_ref, vnew_ref, hnew_ref, cvout_ref):
    ns, t = SEQS_PER_STEP, DEC_SEQ
    wb = ck_ref.shape[1]

    q = _rope(q_ref[...].astype(F32), cos_ref[...], sin_ref[...])
    kv = kv_ref[...].astype(F32)
    kn = _rope(kv[:, :KV_WIDTH], cos_ref[...], sin_ref[...])
    vn = kv[:, KV_WIDTH:]
    q3 = q.reshape(ns, t, ATTN_WIDTH)
    kn3 = kn.reshape(ns, t, KV_WIDTH)
    vn3 = vn.reshape(ns, t, KV_WIDTH)
    knew_ref[:, :wb - t, :] = ck_ref[:, t:, :]
    knew_ref[:, wb - t:, :] = kn3
    vnew_ref[:, :wb - t, :] = cv_ref[:, t:, :]
    vnew_ref[:, wb - t:, :] = vn3
    nq = Q_PER_KV * t
    qi = lax.broadcasted_iota(jnp.int32, (ns, nq, wb + t), 1) % t
    si = lax.broadcasted_iota(jnp.int32, (ns, nq, wb + t), 2)
    valid = ((si < wb) & (si > qi + (wb - WINDOW))) | ((si >= wb) & (si - wb <= qi))
    hrow = lax.broadcasted_iota(jnp.int32, (ns, nq, 1), 1) // t
    heads = [None] * N_HEADS
    for kh in range(N_KV_HEADS):
        hs = slice(kh * HEAD_DIM, (kh + 1) * HEAD_DIM)
        k_all = jnp.concatenate([ck_ref[:, :, hs], kn3[:, :, hs]], axis=1).astype(BF16)
        v_all = jnp.concatenate([cv_ref[:, :, hs], vn3[:, :, hs]], axis=1).astype(BF16)
        qg = jnp.concatenate(
            [q3[:, :, (kh * Q_PER_KV + g) * HEAD_DIM:(kh * Q_PER_KV + g + 1) * HEAD_DIM] for g in range(Q_PER_KV)],
            axis=1).astype(BF16)
        sink = jnp.zeros((ns, nq, 1), F32)
        for g in range(Q_PER_KV):
            sink = jnp.where(hrow == g, sink_ref[kh * Q_PER_KV + g], sink)
        logits = jnp.einsum('bqd,bsd->bqs', qg, k_all, preferred_element_type=F32) * ATTN_SCALE
        logits = jnp.where(valid, logits, -jnp.inf)
        m = jnp.maximum(jnp.max(logits, axis=-1, keepdims=True), sink)
        e = jnp.exp(logits - m)
        den = jnp.sum(e, axis=-1, keepdims=True) + jnp.exp(sink - m)
        o = jnp.einsum('bqs,bsd->bqd', e.astype(BF16), v_all, preferred_element_type=F32) / den
        for g in range(Q_PER_KV):
            heads[kh * Q_PER_KV + g] = o[:, g * t:(g + 1) * t, :]
    ya_ref[...] = jnp.concatenate(heads, axis=-1).reshape(ns * t, ATTN_WIDTH).astype(BF16)

    xc3 = x_ref[...].astype(F32).reshape(ns, t, SSD_CONV_DIM)
    xp3 = xbuf_ref[...].reshape(ns, t, SSD_CONV_DIM)
    conv = _causal_conv(xc3, xp3, cw_ref, SSD_CONV).reshape(ns * t, SSD_CONV_DIM)
    act = _silu(conv + cb_ref[...])
    xs, y_diag, ecs_e, xw_t, dec_t, b_groups, c_groups = _ssd_tile(
        act, dt_ref[...], dtb_ref[...], alog_ref[...], t)
    gw = SSD_HEADS_PER_GROUP * SSD_HEAD_DIM
    y_off = []
    for g in range(SSD_GROUPS):
        c3 = c_groups[g].reshape(ns, t, SSD_STATE)
        hg = h0_ref[:, g * gw:(g + 1) * gw, :].astype(BF16)
        y_off.append(jnp.einsum('btn,bqn->btq', c3, hg, preferred_element_type=F32).reshape(ns * t, gw))
    y = y_diag + jnp.concatenate(y_off, axis=-1) * ecs_e + de_ref[...] * xs
    yb_ref[...] = _gated_group_norm(y, z_ref[...].astype(F32), nw_ref[...]).astype(BF16)
    col = lax.broadcasted_iota(jnp.int32, xw_t.shape, 1) // t
    for b in range(ns):
        xw_b = jnp.where(col == b, xw_t, jnp.zeros_like(xw_t))
        upd = jnp.concatenate(
            [_dot(xw_b[g * gw:(g + 1) * gw, :].astype(BF16), b_groups[g]) for g in range(SSD_GROUPS)], axis=0)
        hnew_ref[b] = dec_t[:, b * t:b * t + 1] * h0_ref[b] + upd

    cvc = scc_ref[...].astype(F32) * scv_ref[...].astype(F32)
    cvout_ref[...] = cvc
    conv_c = _causal_conv(cvc.reshape(ns, t, SC_DIM), cbuf_ref[...].reshape(ns, t, SC_DIM), scw_ref, SC_WIDTH)
    yc_ref[...] = (scb_ref[...].astype(F32) * conv_c.reshape(ns * t, SC_DIM)).astype(BF16)


def _mix_sample(p, dt, sink, cos_s, sin_s, cache_k, cache_v, h0, xbuf, cbuf,
                cw, cb, dtb, alog, de, nw, scw, layer, n_prompt, n_dec):
    r = MIX_ROWS
    ns = SEQS_PER_STEP
    steps = n_dec // ns
    base = n_prompt // r
    wb = cache_k.shape[2]

    def col(width, offset):
        return pl.BlockSpec((r, width), lambda i: (base + i, offset // width))

    def state(shape):
        return pl.BlockSpec((None, *shape), lambda i: (layer, i) + (0,) * (len(shape) - 1))

    def const(shape):
        return pl.BlockSpec(shape, lambda i: (0,) * len(shape))

    def rows(width):
        return pl.BlockSpec((r, width), lambda i: (i, 0))

    n_rows = n_dec * DEC_SEQ
    return pl.pallas_call(
        _mix_sample_kernel,
        grid=(steps,),
        in_specs=[
            pl.BlockSpec(memory_space=pltpu.SMEM),
            col(ATTN_WIDTH, COL_Q), col(2 * KV_WIDTH, COL_KV), col(SSD_D_INNER, COL_Z),
            col(SSD_CONV_DIM, COL_XBC),
            pl.BlockSpec((r, DT_PAD), lambda i: (base + i, 0)),
            col(SC_DIM, COL_SCB), col(SC_DIM, COL_SCC), col(SC_DIM, COL_SCV),
            const((r, LANES)), const((r, LANES)),
            state((ns, wb, KV_WIDTH)), state((ns, wb, KV_WIDTH)),
            state((ns, SSD_D_INNER, SSD_STATE)),
            state((r, SSD_CONV_DIM)), state((r, SC_DIM)),
            const((SSD_CONV, SSD_CONV_DIM)), const((1, SSD_CONV_DIM)),
            const((1, DT_PAD)), const((1, DT_PAD)), const((1, SSD_D_INNER)), const((1, SSD_D_INNER)),
            const((SC_WIDTH, SC_DIM)),
        ],
        out_specs=[
            rows(ATTN_WIDTH), rows(SSD_D_INNER), rows(SC_DIM),
            pl.BlockSpec((ns, wb, KV_WIDTH), lambda i: (i, 0, 0)),
            pl.BlockSpec((ns, wb, KV_WIDTH), lambda i: (i, 0, 0)),
            pl.BlockSpec((ns, SSD_D_INNER, SSD_STATE), lambda i: (i, 0, 0)),
            rows(SC_DIM),
        ],
        out_shape=[
            jax.ShapeDtypeStruct((n_rows, ATTN_WIDTH), BF16),
            jax.ShapeDtypeStruct((n_rows, SSD_D_INNER), BF16),
            jax.ShapeDtypeStruct((n_rows, SC_DIM), BF16),
            jax.ShapeDtypeStruct((n_dec, wb, KV_WIDTH), F32),
            jax.ShapeDtypeStruct((n_dec, wb, KV_WIDTH), F32),
            jax.ShapeDtypeStruct((n_dec, SSD_D_INNER, SSD_STATE), F32),
            jax.ShapeDtypeStruct((n_rows, SC_DIM), F32),
        ],
        compiler_params=_params(("parallel",)),
    )(sink, p, p, p, p, dt, p, p, p, cos_s, sin_s, cache_k, cache_v, h0, xbuf, cbuf,
      cw, cb, dtb, alog, de, nw, scw)


def _store_token_major(ref, v):
    n = v.shape[0]
    for s in range(ROW_CHUNKS):
        ref[pl.ds(s, n, stride=ROW_CHUNKS), :] = v[:, s * LANES:(s + 1) * LANES]


def _load_token_major(ref, n):
    return jnp.concatenate([ref[pl.ds(s, n, stride=ROW_CHUNKS), :] for s in range(ROW_CHUNKS)], axis=-1)


def _outproj_kernel(n_prompt_tiles, alpha,
                    x_ref, yap_ref, yas_ref, ybp_ref, ybs_ref, ycp_ref, ycs_ref, g_ref,
                    g1p_ref, g1s_ref, sh2p_ref, sh2s_ref, sc2p_ref, sc2s_ref,
                    wpa_ref, wpb_ref, wpc_ref, wout_ref, wr_ref, br_ref, lng_ref, lnb_ref,
                    x1_ref, h_ref, route_ref, cnt_ref, cnt_scr):
    i = pl.program_id(0)
    is_sample = i >= n_prompt_tiles
    tm = x_ref.shape[0]

    @pl.when(i == 0)
    def _():
        cnt_scr[...] = jnp.zeros_like(cnt_scr)

    ya = jnp.where(is_sample, yas_ref[...], yap_ref[...])
    yb = jnp.where(is_sample, ybs_ref[...], ybp_ref[...])
    yc = jnp.where(is_sample, ycs_ref[...], ycp_ref[...])
    gates = jax.nn.sigmoid(g_ref[...].astype(F32))
    merged = (gates[:, :D_MODEL] * _dot(ya, wpa_ref[...])
              + gates[:, D_MODEL:2 * D_MODEL] * _dot(yb, wpb_ref[...])
              + gates[:, 2 * D_MODEL:] * _dot(yc, wpc_ref[...]))
    mix = _dot(merged.astype(BF16), wout_ref[...])
    g1 = _pick_mod(is_sample, g1p_ref, g1s_ref)
    x1 = _layer_norm(alpha * x_ref[...] + g1 * mix, lng_ref[...], lnb_ref[...])
    x1_ref[...] = x1
    sh2 = _pick_mod(is_sample, sh2p_ref, sh2s_ref)
    sc2 = _pick_mod(is_sample, sc2p_ref, sc2s_ref)
    h = x1 * (1.0 + sc2) + sh2
    _store_token_major(h_ref, h)

    logits = _dot(h.astype(BF16), wr_ref[...]) + br_ref[...]
    lane = lax.broadcasted_iota(jnp.int32, logits.shape, 1).astype(F32)
    neg = -jnp.inf
    big = float(LANES)
    gl = jnp.where(lane < MOE_GROUPS, logits, neg)
    gmax = jnp.max(gl, axis=-1, keepdims=True)
    g_p = 1.0 / jnp.sum(jnp.exp(gl - gmax), axis=-1, keepdims=True)
    gidx = jnp.min(jnp.where(gl == gmax, lane, big), axis=-1, keepdims=True)
    lo = MOE_GROUPS + EXPERTS_PER_GROUP * gidx
    sel = jnp.where((lane >= lo) & (lane < lo + EXPERTS_PER_GROUP), logits, neg)
    m1 = jnp.max(sel, axis=-1, keepdims=True)
    i1 = jnp.min(jnp.where(sel == m1, lane, big), axis=-1, keepdims=True)
    sel2 = jnp.where(lane == i1, neg, sel)
    m2 = jnp.max(sel2, axis=-1, keepdims=True)
    i2 = jnp.min(jnp.where(sel2 == m2, lane, big), axis=-1, keepdims=True)
    ssum = jnp.sum(jnp.exp(sel - m1), axis=-1, keepdims=True)
    p1 = 1.0 / ssum
    p2 = jnp.exp(m2 - m1) / ssum
    w1 = g_p * (p1 / (p1 + p2))
    w2 = g_p * (p2 / (p1 + p2))
    e1 = i1 - MOE_GROUPS
    e2 = i2 - MOE_GROUPS

    onehot = ((lane == e1) | (lane == e2)).astype(F32)
    ri = lax.broadcasted_iota(jnp.int32, (tm, tm), 0)
    ci = lax.broadcasted_iota(jnp.int32, (tm, tm), 1)
    before = (ci < ri).astype(BF16)
    prefix = _dot(before, onehot.astype(BF16)) + cnt_scr[...]
    rank1 = jnp.sum(jnp.where(lane == e1, prefix, 0.0), axis=-1, keepdims=True)
    rank2 = jnp.sum(jnp.where(lane == e2, prefix, 0.0), axis=-1, keepdims=True)
    cnt_scr[...] = cnt_scr[...] + jnp.sum(onehot, axis=0, keepdims=True)
    cnt_ref[...] = cnt_scr[...]
    route = jnp.zeros_like(logits)
    for k, val in enumerate((e1, e2, w1, w2, rank1, rank2)):
        route = jnp.where(lane == float(k), val, route)
    route_ref[...] = route


def _output_projection(x, y_prompt, y_sample, p, mod_p, mod_s, wpa, wpb, wpc, wout, wr, br, lng, lnb,
                       layer, n_prompt, seq_len, alpha):
    t_all = x.shape[0]
    tm = ROW_TILE
    npt = n_prompt // tm
    n_seq = mod_p.shape[1]
    mods = []
    for col in (2, 3, 4):
        mods.extend(_mod_specs(layer, col, tm, npt, seq_len // tm, n_seq))

    def rows(width):
        return pl.BlockSpec((tm, width), lambda i: (i, 0))

    def prompt_rows(width):
        return pl.BlockSpec((tm, width), lambda i: (jnp.minimum(i, npt - 1), 0))

    def sample_rows(width):
        return pl.BlockSpec((tm, width), lambda i: (jnp.maximum(i - npt, 0), 0))

    def const(shape):
        return pl.BlockSpec(shape, lambda i: (0,) * len(shape))

    return pl.pallas_call(
        functools.partial(_outproj_kernel, npt, alpha),
        grid=(t_all // tm,),
        in_specs=[
            rows(D_MODEL),
            prompt_rows(ATTN_WIDTH), sample_rows(ATTN_WIDTH),
            prompt_rows(SSD_D_INNER), sample_rows(SSD_D_INNER),
            prompt_rows(SC_DIM), sample_rows(SC_DIM),
            pl.BlockSpec((tm, 3 * D_MODEL), lambda i: (i, COL_GATES // (3 * D_MODEL))),
            *mods,
            _layer_spec(layer, (ATTN_WIDTH, D_MODEL)), _layer_spec(layer, (SSD_D_INNER, D_MODEL)),
            _layer_spec(layer, (SC_DIM, D_MODEL)), _layer_spec(layer, (D_MODEL, D_MODEL)),
            _layer_spec(layer, (D_MODEL, LANES)), _layer_spec(layer, (1, LANES)),
            const((1, D_MODEL)), const((1, D_MODEL)),
        ],
        out_specs=[
            rows(D_MODEL),
            pl.BlockSpec((tm * ROW_CHUNKS, LANES), lambda i: (i, 0)),
            rows(LANES),
            const((1, LANES)),
        ],
        out_shape=[
            jax.ShapeDtypeStruct((t_all, D_MODEL), F32),
            jax.ShapeDtypeStruct((t_all * ROW_CHUNKS, LANES), F32),
            jax.ShapeDtypeStruct((t_all, LANES), F32),
            jax.ShapeDtypeStruct((1, LANES), F32),
        ],
        scratch_shapes=[pltpu.VMEM((1, LANES), F32)],
        compiler_params=_params(("arbitrary",)),
    )(x, y_prompt[0], y_sample[0], y_prompt[1], y_sample[1], y_prompt[2], y_sample[2],
      p, mod_p, mod_s, mod_p, mod_s, mod_p, mod_s,
      wpa, wpb, wpc, wout, wr, br, lng, lnb)


def _slots_kernel(route_ref, start_ref, d1_ref, d2_ref):
    route = route_ref[...]
    tm = route.shape[0]
    lane = lax.broadcasted_iota(jnp.int32, route.shape, 1).astype(F32)
    start = start_ref[...]
    diag = (lax.broadcasted_iota(jnp.int32, (LANES, LANES), 0)
            == lax.broadcasted_iota(jnp.int32, (LANES, LANES), 1))
    for e_col, r_col, out_ref in ((0, 4, d1_ref), (1, 5, d2_ref)):
        first = jnp.sum(jnp.where(lane == route[:, e_col:e_col + 1], start, 0.0), axis=-1, keepdims=True)
        dest = first + route[:, r_col:r_col + 1]
        rows = [jnp.sum(jnp.where(diag, dest[g * LANES:(g + 1) * LANES], 0.0), axis=0, keepdims=True)
                for g in range(tm // LANES)]
        out_ref[...] = jnp.concatenate(rows, axis=0).astype(jnp.int32)


def _slots(route, pad_start_row):
    t_all = route.shape[0]
    tm = ROW_TILE_IN
    out = jax.ShapeDtypeStruct((t_all // LANES, LANES), jnp.int32)
    d1, d2 = pl.pallas_call(
        _slots_kernel,
        grid=(t_all // tm,),
        in_specs=[pl.BlockSpec((tm, LANES), lambda i: (i, 0)),
                  pl.BlockSpec((1, LANES), lambda i: (0, 0))],
        out_specs=[pl.BlockSpec((tm // LANES, LANES), lambda i: (i, 0))] * 2,
        out_shape=[out, out],
        compiler_params=_params(("parallel",)),
    )(route, pad_start_row)
    return d1.reshape(t_all), d2.reshape(t_all)


def _token_copy(src, src_row, dst, dst_row, sem):
    return pltpu.make_async_copy(
        src.at[pl.ds(pl.multiple_of(src_row * ROW_CHUNKS, ROW_CHUNKS), ROW_CHUNKS)],
        dst.at[pl.ds(pl.multiple_of(dst_row * ROW_CHUNKS, ROW_CHUNKS), ROW_CHUNKS)],
        sem)


def _dispatch_kernel(d1_ref, d2_ref, pend_ref, cnt_ref, nb_ref, h_ref, xb_ref, zero_scr, zsem, sem):
    i = pl.program_id(0)
    tm = h_ref.shape[0] // ROW_CHUNKS
    blk_rows = EXPERT_BLOCK * ROW_CHUNKS
    n_blocks = xb_ref.shape[0] // blk_rows

    def zero_block(b):
        start = pl.multiple_of(b * blk_rows, blk_rows)
        return pltpu.make_async_copy(zero_scr, xb_ref.at[pl.ds(start, blk_rows)], zsem)

    def last_block(e):
        return pend_ref[e] // EXPERT_BLOCK - 1

    @pl.when(i == 0)
    def _():
        zero_scr[...] = jnp.zeros_like(zero_scr)
        for e in range(N_EXPERTS):
            @pl.when(cnt_ref[e] > 0)
            def _():
                zero_block(last_block(e)).start()
        lax.fori_loop(nb_ref[0], n_blocks, lambda b, c: (zero_block(b).start(), c)[1], 0)
        for e in range(N_EXPERTS):
            @pl.when(cnt_ref[e] > 0)
            def _():
                zero_block(last_block(e)).wait()
        lax.fori_loop(nb_ref[0], n_blocks, lambda b, c: (zero_block(b).wait(), c)[1], 0)

    def issue(c, carry):
        for u in range(ISSUE_UNROLL):
            t = c * ISSUE_UNROLL + u
            g = i * tm + t
            _token_copy(h_ref, t, xb_ref, d1_ref[g], sem.at[0]).start()
            _token_copy(h_ref, t, xb_ref, d2_ref[g], sem.at[1]).start()
        return carry

    lax.fori_loop(0, tm // ISSUE_UNROLL, issue, 0)
    for k in range(2):
        pltpu.make_async_copy(h_ref, xb_ref.at[pl.ds(0, tm * ROW_CHUNKS)], sem.at[k]).wait()


def _dispatch(h, dest1, dest2, pad_end, counts, n_used, n_slots):
    t_all = h.shape[0] // ROW_CHUNKS
    tm = ROW_TILE
    grid_spec = pltpu.PrefetchScalarGridSpec(
        num_scalar_prefetch=5,
        grid=(t_all // tm,),
        in_specs=[pl.BlockSpec((tm * ROW_CHUNKS, LANES), lambda i, *_: (i, 0))],
        out_specs=pl.BlockSpec(memory_space=pl.ANY),
        scratch_shapes=[
            pltpu.VMEM((EXPERT_BLOCK * ROW_CHUNKS, LANES), F32),
            pltpu.SemaphoreType.DMA(()),
            pltpu.SemaphoreType.DMA((2,)),
        ],
    )
    return pl.pallas_call(
        _dispatch_kernel,
        grid_spec=grid_spec,
        out_shape=jax.ShapeDtypeStruct((n_slots * ROW_CHUNKS, LANES), F32),
        compiler_params=_params(("arbitrary",)),
    )(dest1, dest2, pad_end, counts, n_used, h)


def _ffn_kernel(be_ref, nb_ref, xb_ref, wg_ref, wu_ref, wd_ref, yb_ref, x_scr, wg_scr, wu_scr, wd_scr):
    b = pl.program_id(0)

    @pl.when(b < nb_ref[0])
    def _():
        changed = (b == 0) | (be_ref[b] != be_ref[jnp.maximum(b - 1, 0)])

        @pl.when(changed)
        def _():
            wg_scr[...] = wg_ref[...].astype(BF16)
            wu_scr[...] = wu_ref[...].astype(BF16)
            wd_scr[...] = wd_ref[...].astype(BF16)

        for s in range(ROW_CHUNKS):
            x_scr[:, s * LANES:(s + 1) * LANES] = xb_ref[pl.ds(s, EXPERT_BLOCK, stride=ROW_CHUNKS), :].astype(BF16)
        x = x_scr[...]
        act = (_silu(_dot(x, wg_scr[...])) * _dot(x, wu_scr[...])).astype(BF16)
        _store_token_major(yb_ref, _dot(act, wd_scr[...]))

    @pl.when(b >= nb_ref[0])
    def _():
        yb_ref[...] = jnp.zeros_like(yb_ref)


def _expert_ffn(xb, block_expert, n_used, wg, wu, wd, layer, n_blocks):
    blk_rows = EXPERT_BLOCK * ROW_CHUNKS

    def blk(b, be, nb):
        return jnp.minimum(b, nb[0] - 1)

    grid_spec = pltpu.PrefetchScalarGridSpec(
        num_scalar_prefetch=2,
        grid=(n_blocks,),
        in_specs=[
            pl.BlockSpec((blk_rows, LANES), lambda b, be, nb: (blk(b, be, nb), 0)),
            pl.BlockSpec((None, None, D_MODEL, EXPERT_FF), lambda b, be, nb: (layer, be[blk(b, be, nb)], 0, 0)),
            pl.BlockSpec((None, None, D_MODEL, EXPERT_FF), lambda b, be, nb: (layer, be[blk(b, be, nb)], 0, 0)),
            pl.BlockSpec((None, None, EXPERT_FF, D_MODEL), lambda b, be, nb: (layer, be[blk(b, be, nb)], 0, 0)),
        ],
        out_specs=pl.BlockSpec((blk_rows, LANES), lambda b, be, nb: (b, 0)),
        scratch_shapes=[
            pltpu.VMEM((EXPERT_BLOCK, D_MODEL), BF16),
            pltpu.VMEM((D_MODEL, EXPERT_FF), BF16),
            pltpu.VMEM((D_MODEL, EXPERT_FF), BF16),
            pltpu.VMEM((EXPERT_FF, D_MODEL), BF16),
        ],
    )
    return pl.pallas_call(
        _ffn_kernel,
        grid_spec=grid_spec,
        out_shape=jax.ShapeDtypeStruct(xb.shape, F32),
        compiler_params=_params(("arbitrary",)),
    )(block_expert, n_used, xb, wg, wu, wd)


def _combine_kernel(n_prompt_tiles, alpha, d1_ref, d2_ref,
                    x1_ref, route_ref, g2p_ref, g2s_ref, lng_ref, lnb_ref, yb_ref,
                    xo_ref, buf_a, buf_b, sem):
    i = pl.program_id(0)
    tm = x1_ref.shape[0]

    def issue(c, carry):
        for u in range(ISSUE_UNROLL):
            t = c * ISSUE_UNROLL + u
            g = i * tm + t
            _token_copy(yb_ref, d1_ref[g], buf_a, t, sem.at[0]).start()
            _token_copy(yb_ref, d2_ref[g], buf_b, t, sem.at[1]).start()
        return carry

    lax.fori_loop(0, tm // ISSUE_UNROLL, issue, 0)
    pltpu.make_async_copy(yb_ref.at[pl.ds(0, tm * ROW_CHUNKS)], buf_a, sem.at[0]).wait()
    pltpu.make_async_copy(yb_ref.at[pl.ds(0, tm * ROW_CHUNKS)], buf_b, sem.at[1]).wait()
    route = route_ref[...]
    ffn = route[:, 2:3] * _load_token_major(buf_a, tm) + route[:, 3:4] * _load_token_major(buf_b, tm)
    g2 = _pick_mod(i >= n_prompt_tiles, g2p_ref, g2s_ref)
    xo_ref[...] = _layer_norm(alpha * x1_ref[...] + g2 * ffn, lng_ref[...], lnb_ref[...])


def _combine(x1, route, yb, dest1, dest2, mod_p, mod_s, lng, lnb, layer, n_prompt, seq_len, alpha):
    t_all = x1.shape[0]
    tm = ROW_TILE
    npt = n_prompt // tm
    g2p, g2s = _mod_specs(layer, 5, tm, npt, seq_len // tm, mod_p.shape[1])
    grid_spec = pltpu.PrefetchScalarGridSpec(
        num_scalar_prefetch=2,
        grid=(t_all // tm,),
        in_specs=[
            pl.BlockSpec((tm, D_MODEL), lambda i, *_: (i, 0)),
            pl.BlockSpec((tm, LANES), lambda i, *_: (i, 0)),
            g2p, g2s,
            pl.BlockSpec((1, D_MODEL), lambda i, *_: (0, 0)),
            pl.BlockSpec((1, D_MODEL), lambda i, *_: (0, 0)),
            pl.BlockSpec(memory_space=pl.ANY),
        ],
        out_specs=pl.BlockSpec((tm, D_MODEL), lambda i, *_: (i, 0)),
        scratch_shapes=[
            pltpu.VMEM((tm * ROW_CHUNKS, LANES), F32),
            pltpu.VMEM((tm * ROW_CHUNKS, LANES), F32),
            pltpu.SemaphoreType.DMA((2,)),
        ],
    )
    return pl.pallas_call(
        functools.partial(_combine_kernel, npt, alpha),
        grid_spec=grid_spec,
        out_shape=jax.ShapeDtypeStruct((t_all, D_MODEL), F32),
        compiler_params=_params(("arbitrary",)),
    )(dest1, dest2, x1, route, mod_p, mod_s, lng, lnb, yb)


def _rope_tables(pos):
    inv = jnp.power(ROPE_THETA, -jnp.arange(ROT_HALF, dtype=F32) * (2.0 / ROT_DIM))
    ang = pos.astype(F32)[:, None] * inv[None, :]
    cos, sin = jnp.cos(ang), jnp.sin(ang)
    rest = HEAD_DIM - ROT_DIM
    n = pos.shape[0]
    cos_h = jnp.concatenate([cos, cos, jnp.ones((n, rest), F32)], axis=-1)
    sin_h = jnp.concatenate([-sin, sin, jnp.zeros((n, rest), F32)], axis=-1)
    reps = LANES // HEAD_DIM
    return jnp.tile(cos_h, (1, reps)), jnp.tile(sin_h, (1, reps))


def _permute_w_in(w_in):
    sizes = (ATTN_WIDTH, KV_WIDTH, KV_WIDTH, SSD_D_INNER, SSD_CONV_DIM, SSD_HEADS,
             SC_DIM, SC_DIM, SC_DIM, 3 * D_MODEL)
    offs = np.concatenate([[0], np.cumsum(sizes)])
    q, k, v, z, xbc, dt, scb, scc, scv, gates = (w_in[..., offs[n]:offs[n + 1]] for n in range(len(sizes)))
    main = jnp.concatenate([q, z, xbc, scb, scc, gates, scv, k, v], axis=-1).astype(BF16)
    dt = jnp.pad(dt, ((0, 0), (0, 0), (0, DT_PAD - SSD_HEADS))).astype(BF16)
    return main, dt


def _pad_lanes(v, width):
    return jnp.pad(v, ((0, 0), (0, width - v.shape[-1])))


def kernel(x_prompt, x_sample, c_prompt, c_sample, cache_attn_k, cache_attn_v, state_ssm, state_ssd_conv, state_short_conv, w_ada, b_ada, w_in, attn_sink, ssd_conv_w, ssd_conv_b, ssd_dt_bias, ssd_a_log, ssd_d, ssd_norm_w, sc_conv_w, w_pa, w_pb, w_pc, w_out, ln1_g, ln1_b, ln2_g, ln2_b, router_g_w, router_g_b, router_e_w, router_e_b, moe_w_gate, moe_w_up, moe_w_down):
    depth = w_in.shape[0]
    n_seq, seq_len, _ = x_prompt.shape
    n_dec, dec_len, _ = x_sample.shape
    wb = cache_attn_k.shape[2]
    past_len = 8192
    assert dec_len == DEC_SEQ and wb == WINDOW
    assert seq_len % ROW_TILE_IN == 0 and (n_dec * dec_len) % ROW_TILE_IN == 0
    n_prompt = n_seq * seq_len
    n_sample = n_dec * dec_len
    t_all = n_prompt + n_sample
    alpha = (2 * depth) ** 0.25

    mod = _modulation(jnp.concatenate([c_prompt, c_sample], axis=0), w_ada, b_ada)
    mod_p_all = mod[:, :n_seq].reshape(depth, n_seq, 1, 6 * D_MODEL)
    mod_s_all = jnp.repeat(mod[:, n_seq:], dec_len, axis=1).reshape(depth, 1, n_sample, 6 * D_MODEL)

    w_main_all, w_dt_all = _permute_w_in(w_in)
    cos_p, sin_p = _rope_tables(jnp.arange(seq_len, dtype=jnp.int32))
    cos_s, sin_s = _rope_tables(past_len + (jnp.arange(MIX_ROWS, dtype=jnp.int32) % dec_len))

    a_total = 2 * t_all
    n_blocks = (a_total + N_EXPERTS * (EXPERT_BLOCK - 1)) // EXPERT_BLOCK
    n_slots = n_blocks * EXPERT_BLOCK

    mod_p, mod_s = mod_p_all, mod_s_all
    cache_k = cache_attn_k.reshape(depth, n_dec, wb, KV_WIDTH)
    cache_v = cache_attn_v.reshape(depth, n_dec, wb, KV_WIDTH)
    h0 = state_ssm.reshape(depth, n_dec, SSD_D_INNER, SSD_STATE)
    xbuf = jnp.pad(state_ssd_conv, ((0, 0), (0, 0), (dec_len - (SSD_CONV - 1), 0), (0, 0))
                   ).reshape(depth, n_sample, SSD_CONV_DIM)
    cbuf = jnp.pad(state_short_conv, ((0, 0), (0, 0), (dec_len - (SC_WIDTH - 1), 0), (0, 0))
                   ).reshape(depth, n_sample, SC_DIM)
    wpa, wpb, wpc, wout = (w.astype(BF16) for w in (w_pa, w_pb, w_pc, w_out))
    wr = jnp.pad(jnp.concatenate([router_g_w, router_e_w], axis=-1),
                 ((0, 0), (0, 0), (0, LANES - MOE_GROUPS - N_EXPERTS))).astype(BF16)
    br = jnp.pad(jnp.concatenate([router_g_b, router_e_b], axis=-1),
                 ((0, 0), (0, LANES - MOE_GROUPS - N_EXPERTS)))[:, None, :]
    block_first_slot = jnp.arange(n_blocks, dtype=jnp.int32) * EXPERT_BLOCK

    x = jnp.concatenate([x_prompt.reshape(n_prompt, D_MODEL), x_sample.reshape(n_sample, D_MODEL)], axis=0)
    outs_p = [[] for _ in range(5)]
    outs_s = [[] for _ in range(5)]
    for l in range(depth):
        p, dt = _input_projection(x, mod_p, mod_s, w_main_all, w_dt_all, l, n_prompt, seq_len)

        cw, cb = ssd_conv_w[l], ssd_conv_b[l][None]
        dtb = _pad_lanes(ssd_dt_bias[l][None], DT_PAD)
        alog = _pad_lanes(ssd_a_log[l][None], DT_PAD)
        de = jnp.repeat(ssd_d[l], SSD_HEAD_DIM)[None]
        nw = ssd_norm_w[l][None]
        scw = sc_conv_w[l]

        ya_p, krot_p = _attention_prompt(p, attn_sink[l], cos_p, sin_p, n_seq, seq_len)
        yb_p, yc_p, h_p, cv_p = _ssd_prompt(p, dt, cw, cb, dtb, alog, de, nw, scw, n_seq, seq_len)

        ya_s, yb_s, yc_s, k_s, v_s, h_s, cv_s = _mix_sample(
            p, dt, attn_sink[l], cos_s, sin_s, cache_k, cache_v, h0, xbuf, cbuf,
            cw, cb, dtb, alog, de, nw, scw, l, n_prompt, n_dec)

        x1, h, route, counts = _output_projection(
            x, (ya_p, yb_p, yc_p), (ya_s, yb_s, yc_s), p, mod_p, mod_s,
            wpa, wpb, wpc, wout, wr, br, ln1_g[l][None], ln1_b[l][None], l, n_prompt, seq_len, alpha)

        cnt = counts[0, :N_EXPERTS].astype(jnp.int32)
        pad_cnt = (cnt + EXPERT_BLOCK - 1) // EXPERT_BLOCK * EXPERT_BLOCK
        pad_end = jnp.cumsum(pad_cnt)
        pad_start = pad_end - pad_cnt
        n_used = (pad_end[-1:] // EXPERT_BLOCK).astype(jnp.int32)
        block_expert = jnp.minimum(
            jnp.sum((pad_end[None, :] <= block_first_slot[:, None]).astype(jnp.int32), axis=1), N_EXPERTS - 1)
        dest1, dest2 = _slots(route, _pad_lanes(pad_start.astype(F32)[None], LANES))

        xb = _dispatch(h, dest1, dest2, pad_end.astype(jnp.int32), cnt, n_used, n_slots)
        yb_slots = _expert_ffn(xb, block_expert, n_used, moe_w_gate, moe_w_up, moe_w_down, l, n_blocks)
        x = _combine(x1, route, yb_slots, dest1, dest2, mod_p, mod_s, ln2_g[l][None], ln2_b[l][None],
                     l, n_prompt, seq_len, alpha)

        def prompt_tail(rows, c0, c1):
            return jnp.stack([p[(b + 1) * seq_len - rows:(b + 1) * seq_len, c0:c1] for b in range(n_seq)]
                             ).astype(F32)

        outs_p[0].append(krot_p.reshape(n_seq, wb, N_KV_HEADS, HEAD_DIM))
        outs_p[1].append(prompt_tail(wb, COL_KV + KV_WIDTH, COL_KV + 2 * KV_WIDTH)
                         .reshape(n_seq, wb, N_KV_HEADS, HEAD_DIM))
        outs_p[2].append(h_p.reshape(n_seq, SSD_HEADS, SSD_HEAD_DIM, SSD_STATE))
        outs_p[3].append(prompt_tail(SSD_CONV - 1, COL_XBC, COL_XBC + SSD_CONV_DIM))
        outs_p[4].append(cv_p[:, SUBLANES - (SC_WIDTH - 1):, :])
        outs_s[0].append(k_s.reshape(n_dec, wb, N_KV_HEADS, HEAD_DIM))
        outs_s[1].append(v_s.reshape(n_dec, wb, N_KV_HEADS, HEAD_DIM))
        outs_s[2].append(h_s.reshape(n_dec, SSD_HEADS, SSD_HEAD_DIM, SSD_STATE))
        outs_s[3].append(p[n_prompt:, COL_XBC:COL_XBC + SSD_CONV_DIM].astype(F32)
                         .reshape(n_dec, dec_len, SSD_CONV_DIM)[:, dec_len - (SSD_CONV - 1):, :])
        outs_s[4].append(cv_s.reshape(n_dec, dec_len, SC_DIM)[:, dec_len - (SC_WIDTH - 1):, :])

    y_prompt = x[:n_prompt].reshape(n_seq, seq_len, D_MODEL)
    y_sample = x[n_prompt:].reshape(n_dec, dec_len, D_MODEL)
    return (y_prompt, y_sample, *[jnp.stack(o) for o in outs_p], *[jnp.stack(o) for o in outs_s])
```

```python
import functools

import jax
import jax.numpy as jnp
import numpy as np
from jax import lax
from jax.experimental import pallas as pl
from jax.experimental.pallas import tpu as pltpu

F32 = jnp.float32
BF16 = jnp.bfloat16

D_MODEL = 1024
HEAD_DIM = 64
N_HEADS = 8
N_KV_HEADS = 2
Q_PER_KV = N_HEADS // N_KV_HEADS
ATTN_WIDTH = N_HEADS * HEAD_DIM
KV_WIDTH = N_KV_HEADS * HEAD_DIM
WINDOW = 128
ROT_DIM = HEAD_DIM // 4
ROT_HALF = ROT_DIM // 2
ROPE_THETA = 500000.0
ATTN_SCALE = HEAD_DIM ** -0.5
SSD_D_INNER = 512
SSD_HEAD_DIM = 64
SSD_HEADS = 8
SSD_GROUPS = 2
SSD_HEADS_PER_GROUP = SSD_HEADS // SSD_GROUPS
SSD_STATE = 128
SSD_CONV = 4
SSD_CONV_DIM = SSD_D_INNER + 2 * SSD_GROUPS * SSD_STATE
SSD_CHUNK = 128
SC_DIM = 512
SC_WIDTH = 3
MOE_GROUPS = 4
EXPERTS_PER_GROUP = 8
N_EXPERTS = MOE_GROUPS * EXPERTS_PER_GROUP
EXPERT_FF = 512
LN_EPS = 1e-5
RMS_EPS = 1e-5

SUBLANES = 8
LANES = 128
VMEM_LIMIT = 56 * 1024 * 1024

COL_Q = 0
COL_Z = 512
COL_XBC = 1024
COL_SCB = 2048
COL_SCC = 2560
COL_GATES = 3072
COL_SCV = 6144
COL_KV = 6656
P_MAIN = 6912
DT_PAD = 128

ROW_TILE_IN = 1024
COL_TILE_IN = 2304
ROW_TILE = 512
MIX_ROWS = 128
DEC_SEQ = 8
SEQS_PER_STEP = MIX_ROWS // DEC_SEQ
EXPERT_BLOCK = 256
ROW_CHUNKS = D_MODEL // LANES
ISSUE_UNROLL = 8


def _silu(v):
    return v * jax.nn.sigmoid(v)


def _dot(a, b):
    return jnp.dot(a, b, preferred_element_type=F32)


def _dot_nt(a, b):
    return lax.dot_general(a, b, (((1,), (1,)), ((), ())), preferred_element_type=F32)


def _dot_exact(a, b):
    return jnp.dot(a, b, preferred_element_type=F32, precision=lax.Precision.HIGHEST)


def _params(sem):
    return pltpu.CompilerParams(dimension_semantics=sem, vmem_limit_bytes=VMEM_LIMIT)


def _pick_mod(is_sample, prompt_ref, sample_ref):
    return jnp.where(is_sample, sample_ref[0], prompt_ref[0])


def _layer_norm(v, g, b):
    mu = jnp.mean(v, axis=-1, keepdims=True)
    c = v - mu
    var = jnp.mean(c * c, axis=-1, keepdims=True)
    return c * lax.rsqrt(var + LN_EPS) * g + b


def _mod_kernel(c_ref, w_ref, b_ref, o_ref):
    s = _silu(c_ref[...]).astype(BF16)
    o_ref[0] = _dot(s, w_ref[0].astype(BF16)) + b_ref[0]


def _modulation(c_all, w_ada, b_ada):
    depth, _, width = w_ada.shape
    n = c_all.shape[0]
    tn = 1536
    return pl.pallas_call(
        _mod_kernel,
        grid=(depth, width // tn),
        in_specs=[
            pl.BlockSpec((n, D_MODEL), lambda l, j: (0, 0)),
            pl.BlockSpec((1, D_MODEL, tn), lambda l, j: (l, 0, j)),
            pl.BlockSpec((1, 1, tn), lambda l, j: (l, 0, j)),
        ],
        out_specs=pl.BlockSpec((1, n, tn), lambda l, j: (l, 0, j)),
        out_shape=jax.ShapeDtypeStruct((depth, n, width), F32),
        compiler_params=_params(("parallel", "parallel")),
    )(c_all, w_ada, b_ada.reshape(depth, 1, width))


def _inproj_kernel(n_prompt_tiles, x_ref, shp_ref, shs_ref, scp_ref, scs_ref, w_ref, wdt_ref,
                   p_ref, dt_ref, u_scr):
    i = pl.program_id(0)
    j = pl.program_id(1)

    @pl.when(j == 0)
    def _():
        is_sample = i >= n_prompt_tiles
        sh = _pick_mod(is_sample, shp_ref, shs_ref)
        sc = _pick_mod(is_sample, scp_ref, scs_ref)
        u = (x_ref[...] * (1.0 + sc) + sh).astype(BF16)
        u_scr[...] = u
        dt_ref[...] = _dot(u, wdt_ref[...])

    p_ref[...] = _dot(u_scr[...], w_ref[...]).astype(BF16)


def _mod_specs(layer, col, row_tile, n_prompt_tiles, tiles_per_seq, n_seq):
    def prompt_map(i, *_):
        return (layer, jnp.minimum(i // tiles_per_seq, n_seq - 1), 0, col)

    def sample_map(i, *_):
        return (layer, 0, jnp.maximum(i - n_prompt_tiles, 0), col)

    return (pl.BlockSpec((None, 1, 1, D_MODEL), prompt_map),
            pl.BlockSpec((None, 1, row_tile, D_MODEL), sample_map))


def _layer_spec(layer, shape):
    return pl.BlockSpec((None, *shape), lambda *_: (layer,) + (0,) * len(shape))


def _input_projection(x, mod_p, mod_s, w_main, w_dt, layer, n_prompt, seq_len):
    t_all = x.shape[0]
    tm, tn = ROW_TILE_IN, COL_TILE_IN
    npt = n_prompt // tm
    n_seq = mod_p.shape[1]
    shp, shs = _mod_specs(layer, 0, tm, npt, seq_len // tm, n_seq)
    scp, scs = _mod_specs(layer, 1, tm, npt, seq_len // tm, n_seq)
    return pl.pallas_call(
        functools.partial(_inproj_kernel, npt),
        grid=(t_all // tm, P_MAIN // tn),
        in_specs=[
            pl.BlockSpec((tm, D_MODEL), lambda i, j: (i, 0)),
            shp, shs, scp, scs,
            pl.BlockSpec((None, D_MODEL, tn), lambda i, j: (layer, 0, j)),
            _layer_spec(layer, (D_MODEL, DT_PAD)),
        ],
        out_specs=[
            pl.BlockSpec((tm, tn), lambda i, j: (i, j)),
            pl.BlockSpec((tm, DT_PAD), lambda i, j: (i, 0)),
        ],
        out_shape=[
            jax.ShapeDtypeStruct((t_all, P_MAIN), BF16),
            jax.ShapeDtypeStruct((t_all, DT_PAD), F32),
        ],
        scratch_shapes=[pltpu.VMEM((tm, D_MODEL), BF16)],
        compiler_params=_params(("parallel", "arbitrary")),
    )(x, mod_p, mod_s, mod_p, mod_s, w_main, w_dt)


def _rope(v, cos, sin):
    width = v.shape[-1]
    reps = width // LANES
    if reps > 1:
        cos = jnp.concatenate([cos] * reps, axis=-1)
        sin = jnp.concatenate([sin] * reps, axis=-1)
    lane = lax.broadcasted_iota(jnp.int32, v.shape, 1) % HEAD_DIM
    partner = jnp.where(lane < ROT_HALF,
                        pltpu.roll(v, width - ROT_HALF, 1),
                        pltpu.roll(v, ROT_HALF, 1))
    return v * cos + partner * sin


def _shift_rows(cur, prev, k):
    axis = cur.ndim - 2
    idx = lax.broadcasted_iota(jnp.int32, cur.shape, axis)
    return jnp.where(idx < k, pltpu.roll(prev, k, axis), pltpu.roll(cur, k, axis))


def _causal_conv(cur, prev, w_ref, width):
    out = cur * w_ref[width - 1:width, :]
    for k in range(1, width):
        out = out + _shift_rows(cur, prev, k) * w_ref[width - 1 - k:width - k, :]
    return out


def _head_expand():
    r = lax.broadcasted_iota(jnp.int32, (LANES, SSD_D_INNER), 0)
    c = lax.broadcasted_iota(jnp.int32, (LANES, SSD_D_INNER), 1)
    return (c // SSD_HEAD_DIM == r).astype(F32)


def _head_expand_t():
    r = lax.broadcasted_iota(jnp.int32, (SSD_D_INNER, LANES), 0)
    c = lax.broadcasted_iota(jnp.int32, (SSD_D_INNER, LANES), 1)
    return (r // SSD_HEAD_DIM == c).astype(F32)


def _ssd_tile(act, dt_raw, dtb, alog, seq_rows):
    rows = MIX_ROWS
    xs = act[:, :SSD_D_INNER]
    bm = act[:, SSD_D_INNER:SSD_D_INNER + SSD_GROUPS * SSD_STATE]
    cm = act[:, SSD_D_INNER + SSD_GROUPS * SSD_STATE:]
    v = dt_raw + dtb
    dt = jnp.maximum(v, 0.0) + jnp.log1p(jnp.exp(-jnp.abs(v)))
    a = -jnp.exp(alog)
    dta = dt * a
    ri = lax.broadcasted_iota(jnp.int32, (rows, rows), 0)
    ci = lax.broadcasted_iota(jnp.int32, (rows, rows), 1)
    same = (ri // seq_rows) == (ci // seq_rows)
    causal = same & (ci <= ri)
    cs = _dot_exact(causal.astype(F32), dta)
    expand = _head_expand().astype(BF16)

    def per_head_lanes(v):
        hi = v.astype(BF16)
        lo = (v - hi.astype(F32)).astype(BF16)
        return _dot(hi, expand) + _dot(lo, expand)

    dt_e = per_head_lanes(dt)
    cs_e = per_head_lanes(cs)
    if seq_rows == rows:
        tot = None
        tot_e = cs_e[rows - 1:rows, :]
    else:
        tot = _dot_exact(same.astype(F32), dta)
        tot_e = per_head_lanes(tot)
    cs_t = cs.T
    dtx = xs * dt_e
    xw = dtx * jnp.exp(tot_e - cs_e)
    b_groups, c_groups, y_parts = [], [], []
    for g in range(SSD_GROUPS):
        bg = bm[:, g * SSD_STATE:(g + 1) * SSD_STATE].astype(BF16)
        cg = cm[:, g * SSD_STATE:(g + 1) * SSD_STATE].astype(BF16)
        b_groups.append(bg)
        c_groups.append(cg)
        cb = _dot_nt(cg, bg)
        for hh in range(SSD_HEADS_PER_GROUP):
            h = g * SSD_HEADS_PER_GROUP + hh
            seg = cs[:, h:h + 1] - cs_t[h:h + 1, :]
            decay = jnp.where(causal, jnp.exp(jnp.where(causal, seg, 0.0)), 0.0)
            y_parts.append(_dot((cb * decay).astype(BF16),
                                dtx[:, h * SSD_HEAD_DIM:(h + 1) * SSD_HEAD_DIM].astype(BF16)))
    y_diag = jnp.concatenate(y_parts, axis=-1)
    return xs, y_diag, jnp.exp(cs_e), xw, tot, tot_e, bm, b_groups, c_groups


def _gated_group_norm(y, z, nw):
    y = y * _silu(z)
    half = SSD_D_INNER // SSD_GROUPS
    parts = []
    for g in range(SSD_GROUPS):
        yg = y[:, g * half:(g + 1) * half]
        parts.append(yg * lax.rsqrt(jnp.mean(yg * yg, axis=-1, keepdims=True) + RMS_EPS))
    return jnp.concatenate(parts, axis=-1) * nw


def _attn_prompt_kernel(sink_ref, q_ref, kvc_ref, kvp_ref, cosc_ref, sinc_ref, cosp_ref, sinp_ref,
                        ya_ref, krot_ref):
    i = pl.program_id(1)
    nb = pl.num_programs(1)
    w = WINDOW
    q = _rope(q_ref[...].astype(F32), cosc_ref[...], sinc_ref[...])
    kvc = kvc_ref[...].astype(F32)
    kvp = kvp_ref[...].astype(F32)
    kc = _rope(kvc[:, :KV_WIDTH], cosc_ref[...], sinc_ref[...])
    kp = _rope(kvp[:, :KV_WIDTH], cosp_ref[...], sinp_ref[...])
    vc = kvc[:, KV_WIDTH:]
    vp = kvp[:, KV_WIDTH:]

    @pl.when(i == nb - 1)
    def _():
        krot_ref[0] = kc

    rows = Q_PER_KV * w
    r = lax.broadcasted_iota(jnp.int32, (rows, 2 * w), 0)
    s = lax.broadcasted_iota(jnp.int32, (rows, 2 * w), 1)
    diff = w + (r % w) - s
    first_key = jnp.where(i > 0, 0, w)
    valid = (diff >= 0) & (diff < w) & (s >= first_key)
    rcol = lax.broadcasted_iota(jnp.int32, (rows, 1), 0)
    outs = []
    for kh in range(N_KV_HEADS):
        hs = slice(kh * HEAD_DIM, (kh + 1) * HEAD_DIM)
        k2 = jnp.concatenate([kp[:, hs], kc[:, hs]], axis=0).astype(BF16)
        v2 = jnp.concatenate([vp[:, hs], vc[:, hs]], axis=0).astype(BF16)
        qg = jnp.concatenate(
            [q[:, (kh * Q_PER_KV + g) * HEAD_DIM:(kh * Q_PER_KV + g + 1) * HEAD_DIM] for g in range(Q_PER_KV)],
            axis=0).astype(BF16)
        sink = jnp.zeros((rows, 1), F32)
        for g in range(Q_PER_KV):
            sink = jnp.where(rcol // w == g, sink_ref[kh * Q_PER_KV + g], sink)
        logits = jnp.where(valid, _dot_nt(qg, k2) * ATTN_SCALE, -jnp.inf)
        m = jnp.maximum(jnp.max(logits, axis=-1, keepdims=True), sink)
        e = jnp.exp(logits - m)
        den = jnp.sum(e, axis=-1, keepdims=True) + jnp.exp(sink - m)
        o = _dot(e.astype(BF16), v2) / den
        for g in range(Q_PER_KV):
            outs.append(o[g * w:(g + 1) * w, :])
    ya_ref[...] = jnp.concatenate(outs, axis=-1).astype(BF16)


def _attention_prompt(p, sink, cos_p, sin_p, n_seq, seq_len):
    nb = seq_len // WINDOW
    n_prompt = n_seq * seq_len
    w = WINDOW

    def cur(b, i):
        return b * nb + i

    def prev(b, i):
        return jnp.maximum(b * nb + i - 1, 0)

    return pl.pallas_call(
        _attn_prompt_kernel,
        grid=(n_seq, nb),
        in_specs=[
            pl.BlockSpec(memory_space=pltpu.SMEM),
            pl.BlockSpec((w, ATTN_WIDTH), lambda b, i: (cur(b, i), COL_Q // ATTN_WIDTH)),
            pl.BlockSpec((w, 2 * KV_WIDTH), lambda b, i: (cur(b, i), COL_KV // (2 * KV_WIDTH))),
            pl.BlockSpec((w, 2 * KV_WIDTH), lambda b, i: (prev(b, i), COL_KV // (2 * KV_WIDTH))),
            pl.BlockSpec((w, LANES), lambda b, i: (i, 0)),
            pl.BlockSpec((w, LANES), lambda b, i: (i, 0)),
            pl.BlockSpec((w, LANES), lambda b, i: (jnp.maximum(i - 1, 0), 0)),
            pl.BlockSpec((w, LANES), lambda b, i: (jnp.maximum(i - 1, 0), 0)),
        ],
        out_specs=[
            pl.BlockSpec((w, ATTN_WIDTH), lambda b, i: (cur(b, i), 0)),
            pl.BlockSpec((1, w, KV_WIDTH), lambda b, i: (b, 0, 0)),
        ],
        out_shape=[
            jax.ShapeDtypeStruct((n_prompt, ATTN_WIDTH), BF16),
            jax.ShapeDtypeStruct((n_seq, w, KV_WIDTH), F32),
        ],
        compiler_params=_params(("parallel", "arbitrary")),
    )(sink, p, p, p, cos_p, sin_p, cos_p, sin_p)


def _ssd_prompt_kernel(z_ref, xc_ref, xp_ref, dt_ref, scb_ref, sccc_ref, sccp_ref, scvc_ref, scvp_ref,
                       cw_ref, cb_ref, dtb_ref, alog_ref, de_ref, nw_ref, scw_ref,
                       yb_ref, yc_ref, hout_ref, cvlast_ref, h_scr):
    i = pl.program_id(1)
    nc = pl.num_programs(1)
    first = i == 0

    @pl.when(first)
    def _():
        h_scr[...] = jnp.zeros_like(h_scr)

    xc = xc_ref[...].astype(F32)
    xp = jnp.where(first, 0.0, xp_ref[...].astype(F32))
    act = _silu(_causal_conv(xc, xp, cw_ref, SSD_CONV) + cb_ref[...])
    xs, y_diag, ecs_e, xw, _, tot_e, bm, _, c_groups = _ssd_tile(
        act, dt_ref[...], dtb_ref[...], alog_ref[...], MIX_ROWS)
    gw = SSD_HEADS_PER_GROUP * SSD_HEAD_DIM
    y_off = []
    for g in range(SSD_GROUPS):
        cols = slice(g * gw, (g + 1) * gw)
        hg = h_scr[:, cols]
        y_off.append(_dot(c_groups[g], hg.astype(BF16)))
        b_t = bm[:, g * SSD_STATE:(g + 1) * SSD_STATE].T.astype(BF16)
        h_scr[:, cols] = jnp.exp(tot_e[:, cols]) * hg + _dot(b_t, xw[:, cols].astype(BF16))
    y = y_diag + jnp.concatenate(y_off, axis=-1) * ecs_e + de_ref[...] * xs
    yb_ref[...] = _gated_group_norm(y, z_ref[...].astype(F32), nw_ref[...]).astype(BF16)

    cvc = sccc_ref[...].astype(F32) * scvc_ref[...].astype(F32)
    cvp = jnp.where(first, 0.0, sccp_ref[...].astype(F32) * scvp_ref[...].astype(F32))
    conv_c = _causal_conv(cvc, cvp, scw_ref, SC_WIDTH)
    yc_ref[...] = (scb_ref[...].astype(F32) * conv_c).astype(BF16)

    @pl.when(i == nc - 1)
    def _():
        hout_ref[0] = h_scr[...].T
        cvlast_ref[0] = cvc[MIX_ROWS - SUBLANES:, :]


def _ssd_prompt(p, dt, cw, cb, dtb, alog, de, nw, scw, n_seq, seq_len):
    nc = seq_len // MIX_ROWS
    r = MIX_ROWS
    n_prompt = n_seq * seq_len

    def cur(b, i):
        return b * nc + i

    def prev(b, i):
        return jnp.maximum(b * nc + i - 1, 0)

    def col(width, offset, which):
        return pl.BlockSpec((r, width), lambda b, i: (which(b, i), offset // width))

    def const(shape):
        return pl.BlockSpec(shape, lambda b, i: (0,) * len(shape))

    return pl.pallas_call(
        _ssd_prompt_kernel,
        grid=(n_seq, nc),
        in_specs=[
            col(SSD_D_INNER, COL_Z, cur),
            col(SSD_CONV_DIM, COL_XBC, cur), col(SSD_CONV_DIM, COL_XBC, prev),
            pl.BlockSpec((r, DT_PAD), lambda b, i: (cur(b, i), 0)),
            col(SC_DIM, COL_SCB, cur),
            col(SC_DIM, COL_SCC, cur), col(SC_DIM, COL_SCC, prev),
            col(SC_DIM, COL_SCV, cur), col(SC_DIM, COL_SCV, prev),
            const((SSD_CONV, SSD_CONV_DIM)), const((1, SSD_CONV_DIM)),
            const((1, DT_PAD)), const((1, DT_PAD)), const((1, SSD_D_INNER)), const((1, SSD_D_INNER)),
            const((SC_WIDTH, SC_DIM)),
        ],
        out_specs=[
            pl.BlockSpec((r, SSD_D_INNER), lambda b, i: (cur(b, i), 0)),
            pl.BlockSpec((r, SC_DIM), lambda b, i: (cur(b, i), 0)),
            pl.BlockSpec((1, SSD_D_INNER, SSD_STATE), lambda b, i: (b, 0, 0)),
            pl.BlockSpec((1, SUBLANES, SC_DIM), lambda b, i: (b, 0, 0)),
        ],
        out_shape=[
            jax.ShapeDtypeStruct((n_prompt, SSD_D_INNER), BF16),
            jax.ShapeDtypeStruct((n_prompt, SC_DIM), BF16),
            jax.ShapeDtypeStruct((n_seq, SSD_D_INNER, SSD_STATE), F32),
            jax.ShapeDtypeStruct((n_seq, SUBLANES, SC_DIM), F32),
        ],
        scratch_shapes=[pltpu.VMEM((SSD_STATE, SSD_D_INNER), F32)],
        compiler_params=_params(("parallel", "arbitrary")),
    )(p, p, p, dt, p, p, p, p, p, cw, cb, dtb, alog, de, nw, scw)


def _mix_sample_kernel(sink_ref, q_ref, kv_ref, z_ref, x_ref, dt_ref, scb_ref, scc_ref, scv_ref,
                       cos_ref, sin_ref, ck_ref, cv_ref, h0_ref, xbuf_ref, cbuf_ref,
                       cw_ref, cb_ref, dtb_ref, alog_ref, de_ref, nw_ref, scw_ref,
                       ya_ref, yb_ref, yc_ref, knew_ref, vnew_ref, hnew_ref, cvout_ref):
    ns, t = SEQS_PER_STEP, DEC_SEQ
    wb = ck_ref.shape[1]

    q = _rope(q_ref[...].astype(F32), cos_ref[...], sin_ref[...])
    kv = kv_ref[...].astype(F32)
    kn = _rope(kv[:, :KV_WIDTH], cos_ref[...], sin_ref[...])
    vn = kv[:, KV_WIDTH:]
    q3 = q.reshape(ns, t, ATTN_WIDTH)
    kn3 = kn.reshape(ns, t, KV_WIDTH)
    vn3 = vn.reshape(ns, t, KV_WIDTH)
    knew_ref[:, :wb - t, :] = ck_ref[:, t:, :]
    knew_ref[:, wb - t:, :] = kn3
    vnew_ref[:, :wb - t, :] = cv_ref[:, t:, :]
    vnew_ref[:, wb - t:, :] = vn3
    nq = Q_PER_KV * t
    qi = lax.broadcasted_iota(jnp.int32, (ns, nq, wb + t), 1) % t
    si = lax.broadcasted_iota(jnp.int32, (ns, nq, wb + t), 2)
    valid = ((si < wb) & (si > qi + (wb - WINDOW))) | ((si >= wb) & (si - wb <= qi))
    hrow = lax.broadcasted_iota(jnp.int32, (ns, nq, 1), 1) // t
    heads = [None] * N_HEADS
    for kh in range(N_KV_HEADS):
        hs = slice(kh * HEAD_DIM, (kh + 1) * HEAD_DIM)
        k_all = jnp.concatenate([ck_ref[:, :, hs], kn3[:, :, hs]], axis=1).astype(BF16)
        v_all = jnp.concatenate([cv_ref[:, :, hs], vn3[:, :, hs]], axis=1).astype(BF16)
        qg = jnp.concatenate(
            [q3[:, :, (kh * Q_PER_KV + g) * HEAD_DIM:(kh * Q_PER_KV + g + 1) * HEAD_DIM] for g in range(Q_PER_KV)],
            axis=1).astype(BF16)
        sink = jnp.zeros((ns, nq, 1), F32)
        for g in range(Q_PER_KV):
            sink = jnp.where(hrow == g, sink_ref[kh * Q_PER_KV + g], sink)
        logits = jnp.einsum('bqd,bsd->bqs', qg, k_all, preferred_element_type=F32) * ATTN_SCALE
        logits = jnp.where(valid, logits, -jnp.inf)
        m = jnp.maximum(jnp.max(logits, axis=-1, keepdims=True), sink)
        e = jnp.exp(logits - m)
        den = jnp.sum(e, axis=-1, keepdims=True) + jnp.exp(sink - m)
        o = jnp.einsum('bqs,bsd->bqd', e.astype(BF16), v_all, preferred_element_type=F32) / den
        for g in range(Q_PER_KV):
            heads[kh * Q_PER_KV + g] = o[:, g * t:(g + 1) * t, :]
    ya_ref[...] = jnp.concatenate(heads, axis=-1).reshape(ns * t, ATTN_WIDTH).astype(BF16)

    xc3 = x_ref[...].astype(F32).reshape(ns, t, SSD_CONV_DIM)
    xp3 = xbuf_ref[...].reshape(ns, t, SSD_CONV_DIM)
    conv = _causal_conv(xc3, xp3, cw_ref, SSD_CONV).reshape(ns * t, SSD_CONV_DIM)
    act = _silu(conv + cb_ref[...])
    xs, y_diag, ecs_e, xw, tot, _, _, b_groups, c_groups = _ssd_tile(
        act, dt_ref[...], dtb_ref[...], alog_ref[...], t)
    xw_t = xw.T
    dec_t = jnp.exp(_dot_exact(_head_expand_t(), tot.T))
    gw = SSD_HEADS_PER_GROUP * SSD_HEAD_DIM
    y_off = []
    for g in range(SSD_GROUPS):
        c3 = c_groups[g].reshape(ns, t, SSD_STATE)
        hg = h0_ref[:, g * gw:(g + 1) * gw, :].astype(BF16)
        y_off.append(jnp.einsum('btn,bqn->btq', c3, hg, preferred_element_type=F32).reshape(ns * t, gw))
    y = y_diag + jnp.concatenate(y_off, axis=-1) * ecs_e + de_ref[...] * xs
    yb_ref[...] = _gated_group_norm(y, z_ref[...].astype(F32), nw_ref[...]).astype(BF16)
    col = lax.broadcasted_iota(jnp.int32, xw_t.shape, 1) // t
    for b in range(ns):
        xw_b = jnp.where(col == b, xw_t, jnp.zeros_like(xw_t))
        upd = jnp.concatenate(
            [_dot(xw_b[g * gw:(g + 1) * gw, :].astype(BF16), b_groups[g]) for g in range(SSD_GROUPS)], axis=0)
        hnew_ref[b] = dec_t[:, b * t:b * t + 1] * h0_ref[b] + upd

    cvc = scc_ref[...].astype(F32) * scv_ref[...].astype(F32)
    cvout_ref[...] = cvc
    conv_c = _causal_conv(cvc.reshape(ns, t, SC_DIM), cbuf_ref[...].reshape(ns, t, SC_DIM), scw_ref, SC_WIDTH)
    yc_ref[...] = (scb_ref[...].astype(F32) * conv_c.reshape(ns * t, SC_DIM)).astype(BF16)


def _mix_sample(p, dt, sink, cos_s, sin_s, cache_k, cache_v, h0, xbuf, cbuf,
                cw, cb, dtb, alog, de, nw, scw, layer, n_prompt, n_dec):
    r = MIX_ROWS
    ns = SEQS_PER_STEP
    steps = n_dec // ns
    base = n_prompt // r
    wb = cache_k.shape[2]

    def col(width, offset):
        return pl.BlockSpec((r, width), lambda i: (base + i, offset // width))

    def state(shape):
        return pl.BlockSpec((None, *shape), lambda i: (layer, i) + (0,) * (len(shape) - 1))

    def const(shape):
        return pl.BlockSpec(shape, lambda i: (0,) * len(shape))

    def rows(width):
        return pl.BlockSpec((r, width), lambda i: (i, 0))

    n_rows = n_dec * DEC_SEQ
    return pl.pallas_call(
        _mix_sample_kernel,
        grid=(steps,),
        in_specs=[
            pl.BlockSpec(memory_space=pltpu.SMEM),
            col(ATTN_WIDTH, COL_Q), col(2 * KV_WIDTH, COL_KV), col(SSD_D_INNER, COL_Z),
            col(SSD_CONV_DIM, COL_XBC),
            pl.BlockSpec((r, DT_PAD), lambda i: (base + i, 0)),
            col(SC_DIM, COL_SCB), col(SC_DIM, COL_SCC), col(SC_DIM, COL_SCV),
            const((r, LANES)), const((r, LANES)),
            state((ns, wb, KV_WIDTH)), state((ns, wb, KV_WIDTH)),
            state((ns, SSD_D_INNER, SSD_STATE)),
            state((r, SSD_CONV_DIM)), state((r, SC_DIM)),
            const((SSD_CONV, SSD_CONV_DIM)), const((1, SSD_CONV_DIM)),
            const((1, DT_PAD)), const((1, DT_PAD)), const((1, SSD_D_INNER)), const((1, SSD_D_INNER)),
            const((SC_WIDTH, SC_DIM)),
        ],
        out_specs=[
            rows(ATTN_WIDTH), rows(SSD_D_INNER), rows(SC_DIM),
            pl.BlockSpec((ns, wb, KV_WIDTH), lambda i: (i, 0, 0)),
            pl.BlockSpec((ns, wb, KV_WIDTH), lambda i: (i, 0, 0)),
            pl.BlockSpec((ns, SSD_D_INNER, SSD_STATE), lambda i: (i, 0, 0)),
            rows(SC_DIM),
        ],
        out_shape=[
            jax.ShapeDtypeStruct((n_rows, ATTN_WIDTH), BF16),
            jax.ShapeDtypeStruct((n_rows, SSD_D_INNER), BF16),
            jax.ShapeDtypeStruct((n_rows, SC_DIM), BF16),
            jax.ShapeDtypeStruct((n_dec, wb, KV_WIDTH), F32),
            jax.ShapeDtypeStruct((n_dec, wb, KV_WIDTH), F32),
            jax.ShapeDtypeStruct((n_dec, SSD_D_INNER, SSD_STATE), F32),
            jax.ShapeDtypeStruct((n_rows, SC_DIM), F32),
        ],
        compiler_params=_params(("parallel",)),
    )(sink, p, p, p, p, dt, p, p, p, cos_s, sin_s, cache_k, cache_v, h0, xbuf, cbuf,
      cw, cb, dtb, alog, de, nw, scw)


def _store_token_major(ref, v):
    n = v.shape[0]
    for s in range(ROW_CHUNKS):
        ref[pl.ds(s, n, stride=ROW_CHUNKS), :] = v[:, s * LANES:(s + 1) * LANES]


def _load_token_major(ref, n):
    return jnp.concatenate([ref[pl.ds(s, n, stride=ROW_CHUNKS), :] for s in range(ROW_CHUNKS)], axis=-1)


def _outproj_kernel(n_prompt_tiles, alpha,
                    x_ref, yap_ref, yas_ref, ybp_ref, ybs_ref, ycp_ref, ycs_ref, g_ref,
                    g1p_ref, g1s_ref, sh2p_ref, sh2s_ref, sc2p_ref, sc2s_ref,
                    wpa_ref, wpb_ref, wpc_ref, wout_ref, wr_ref, br_ref, lng_ref, lnb_ref,
                    x1_ref, h_ref, route_ref, cnt_ref, cnt_scr):
    i = pl.program_id(0)
    is_sample = i >= n_prompt_tiles
    tm = x_ref.shape[0]

    @pl.when(i == 0)
    def _():
        cnt_scr[...] = jnp.zeros_like(cnt_scr)

    ya = jnp.where(is_sample, yas_ref[...], yap_ref[...])
    yb = jnp.where(is_sample, ybs_ref[...], ybp_ref[...])
    yc = jnp.where(is_sample, ycs_ref[...], ycp_ref[...])
    gates = jax.nn.sigmoid(g_ref[...].astype(F32))
    merged = (gates[:, :D_MODEL] * _dot(ya, wpa_ref[...])
              + gates[:, D_MODEL:2 * D_MODEL] * _dot(yb, wpb_ref[...])
              + gates[:, 2 * D_MODEL:] * _dot(yc, wpc_ref[...]))
    mix = _dot(merged.astype(BF16), wout_ref[...])
    g1 = _pick_mod(is_sample, g1p_ref, g1s_ref)
    x1 = _layer_norm(alpha * x_ref[...] + g1 * mix, lng_ref[...], lnb_ref[...])
    x1_ref[...] = x1
    sh2 = _pick_mod(is_sample, sh2p_ref, sh2s_ref)
    sc2 = _pick_mod(is_sample, sc2p_ref, sc2s_ref)
    h = x1 * (1.0 + sc2) + sh2
    _store_token_major(h_ref, h)

    logits = _dot(h.astype(BF16), wr_ref[...]) + br_ref[...]
    lane = lax.broadcasted_iota(jnp.int32, logits.shape, 1).astype(F32)
    neg = -jnp.inf
    big = float(LANES)
    gl = jnp.where(lane < MOE_GROUPS, logits, neg)
    gmax = jnp.max(gl, axis=-1, keepdims=True)
    g_p = 1.0 / jnp.sum(jnp.exp(gl - gmax), axis=-1, keepdims=True)
    gidx = jnp.min(jnp.where(gl == gmax, lane, big), axis=-1, keepdims=True)
    lo = MOE_GROUPS + EXPERTS_PER_GROUP * gidx
    sel = jnp.where((lane >= lo) & (lane < lo + EXPERTS_PER_GROUP), logits, neg)
    m1 = jnp.max(sel, axis=-1, keepdims=True)
    i1 = jnp.min(jnp.where(sel == m1, lane, big), axis=-1, keepdims=True)
    sel2 = jnp.where(lane == i1, neg, sel)
    m2 = jnp.max(sel2, axis=-1, keepdims=True)
    i2 = jnp.min(jnp.where(sel2 == m2, lane, big), axis=-1, keepdims=True)
    ssum = jnp.sum(jnp.exp(sel - m1), axis=-1, keepdims=True)
    p1 = 1.0 / ssum
    p2 = jnp.exp(m2 - m1) / ssum
    w1 = g_p * (p1 / (p1 + p2))
    w2 = g_p * (p2 / (p1 + p2))
    e1 = i1 - MOE_GROUPS
    e2 = i2 - MOE_GROUPS

    onehot = ((lane == e1) | (lane == e2)).astype(F32)
    ri = lax.broadcasted_iota(jnp.int32, (tm, tm), 0)
    ci = lax.broadcasted_iota(jnp.int32, (tm, tm), 1)
    before = (ci < ri).astype(BF16)
    prefix = _dot(before, onehot.astype(BF16)) + cnt_scr[...]
    rank1 = jnp.sum(jnp.where(lane == e1, prefix, 0.0), axis=-1, keepdims=True)
    rank2 = jnp.sum(jnp.where(lane == e2, prefix, 0.0), axis=-1, keepdims=True)
    cnt_scr[...] = cnt_scr[...] + jnp.sum(onehot, axis=0, keepdims=True)
    cnt_ref[...] = cnt_scr[...]
    route = jnp.zeros_like(logits)
    for k, val in enumerate((e1, e2, w1, w2, rank1, rank2)):
        route = jnp.where(lane == float(k), val, route)
    route_ref[...] = route


def _output_projection(x, y_prompt, y_sample, p, mod_p, mod_s, wpa, wpb, wpc, wout, wr, br, lng, lnb,
                       layer, n_prompt, seq_len, alpha):
    t_all = x.shape[0]
    tm = ROW_TILE
    npt = n_prompt // tm
    n_seq = mod_p.shape[1]
    mods = []
    for col in (2, 3, 4):
        mods.extend(_mod_specs(layer, col, tm, npt, seq_len // tm, n_seq))

    def rows(width):
        return pl.BlockSpec((tm, width), lambda i: (i, 0))

    def prompt_rows(width):
        return pl.BlockSpec((tm, width), lambda i: (jnp.minimum(i, npt - 1), 0))

    def sample_rows(width):
        return pl.BlockSpec((tm, width), lambda i: (jnp.maximum(i - npt, 0), 0))

    def const(shape):
        return pl.BlockSpec(shape, lambda i: (0,) * len(shape))

    return pl.pallas_call(
        functools.partial(_outproj_kernel, npt, alpha),
        grid=(t_all // tm,),
        in_specs=[
            rows(D_MODEL),
            prompt_rows(ATTN_WIDTH), sample_rows(ATTN_WIDTH),
            prompt_rows(SSD_D_INNER), sample_rows(SSD_D_INNER),
            prompt_rows(SC_DIM), sample_rows(SC_DIM),
            pl.BlockSpec((tm, 3 * D_MODEL), lambda i: (i, COL_GATES // (3 * D_MODEL))),
            *mods,
            _layer_spec(layer, (ATTN_WIDTH, D_MODEL)), _layer_spec(layer, (SSD_D_INNER, D_MODEL)),
            _layer_spec(layer, (SC_DIM, D_MODEL)), _layer_spec(layer, (D_MODEL, D_MODEL)),
            _layer_spec(layer, (D_MODEL, LANES)), _layer_spec(layer, (1, LANES)),
            const((1, D_MODEL)), const((1, D_MODEL)),
        ],
        out_specs=[
            rows(D_MODEL),
            pl.BlockSpec((tm * ROW_CHUNKS, LANES), lambda i: (i, 0)),
            rows(LANES),
            const((1, LANES)),
        ],
        out_shape=[
            jax.ShapeDtypeStruct((t_all, D_MODEL), F32),
            jax.ShapeDtypeStruct((t_all * ROW_CHUNKS, LANES), F32),
            jax.ShapeDtypeStruct((t_all, LANES), F32),
            jax.ShapeDtypeStruct((1, LANES), F32),
        ],
        scratch_shapes=[pltpu.VMEM((1, LANES), F32)],
        compiler_params=_params(("arbitrary",)),
    )(x, y_prompt[0], y_sample[0], y_prompt[1], y_sample[1], y_prompt[2], y_sample[2],
      p, mod_p, mod_s, mod_p, mod_s, mod_p, mod_s,
      wpa, wpb, wpc, wout, wr, br, lng, lnb)


def _slots_kernel(route_ref, start_ref, d1_ref, d2_ref):
    route = route_ref[...]
    tm = route.shape[0]
    lane = lax.broadcasted_iota(jnp.int32, route.shape, 1).astype(F32)
    start = start_ref[...]
    diag = (lax.broadcasted_iota(jnp.int32, (LANES, LANES), 0)
            == lax.broadcasted_iota(jnp.int32, (LANES, LANES), 1))
    for e_col, r_col, out_ref in ((0, 4, d1_ref), (1, 5, d2_ref)):
        first = jnp.sum(jnp.where(lane == route[:, e_col:e_col + 1], start, 0.0), axis=-1, keepdims=True)
        dest = first + route[:, r_col:r_col + 1]
        rows = [jnp.sum(jnp.where(diag, dest[g * LANES:(g + 1) * LANES], 0.0), axis=0, keepdims=True)
                for g in range(tm // LANES)]
        out_ref[...] = jnp.concatenate(rows, axis=0).astype(jnp.int32)


def _slots(route, pad_start_row):
    t_all = route.shape[0]
    tm = ROW_TILE_IN
    out = jax.ShapeDtypeStruct((t_all // LANES, LANES), jnp.int32)
    d1, d2 = pl.pallas_call(
        _slots_kernel,
        grid=(t_all // tm,),
        in_specs=[pl.BlockSpec((tm, LANES), lambda i: (i, 0)),
                  pl.BlockSpec((1, LANES), lambda i: (0, 0))],
        out_specs=[pl.BlockSpec((tm // LANES, LANES), lambda i: (i, 0))] * 2,
        out_shape=[out, out],
        compiler_params=_params(("parallel",)),
    )(route, pad_start_row)
    return d1.reshape(t_all), d2.reshape(t_all)


def _token_copy(src, src_row, dst, dst_row, sem):
    return pltpu.make_async_copy(
        src.at[pl.ds(pl.multiple_of(src_row * ROW_CHUNKS, ROW_CHUNKS), ROW_CHUNKS)],
        dst.at[pl.ds(pl.multiple_of(dst_row * ROW_CHUNKS, ROW_CHUNKS), ROW_CHUNKS)],
        sem)


def _dispatch_kernel(d1_ref, d2_ref, pend_ref, cnt_ref, nb_ref, h_ref, xb_ref, zero_scr, zsem, sem):
    i = pl.program_id(0)
    tm = h_ref.shape[0] // ROW_CHUNKS
    blk_rows = EXPERT_BLOCK * ROW_CHUNKS
    n_blocks = xb_ref.shape[0] // blk_rows

    def zero_block(b):
        start = pl.multiple_of(b * blk_rows, blk_rows)
        return pltpu.make_async_copy(zero_scr, xb_ref.at[pl.ds(start, blk_rows)], zsem)

    def last_block(e):
        return pend_ref[e] // EXPERT_BLOCK - 1

    @pl.when(i == 0)
    def _():
        zero_scr[...] = jnp.zeros_like(zero_scr)
        for e in range(N_EXPERTS):
            @pl.when(cnt_ref[e] > 0)
            def _():
                zero_block(last_block(e)).start()
        lax.fori_loop(nb_ref[0], n_blocks, lambda b, c: (zero_block(b).start(), c)[1], 0)
        for e in range(N_EXPERTS):
            @pl.when(cnt_ref[e] > 0)
            def _():
                zero_block(last_block(e)).wait()
        lax.fori_loop(nb_ref[0], n_blocks, lambda b, c: (zero_block(b).wait(), c)[1], 0)

    def issue(c, carry):
        for u in range(ISSUE_UNROLL):
            t = c * ISSUE_UNROLL + u
            g = i * tm + t
            _token_copy(h_ref, t, xb_ref, d1_ref[g], sem.at[0]).start(priority=0)
            _token_copy(h_ref, t, xb_ref, d2_ref[g], sem.at[1]).start(priority=1)
        return carry

    lax.fori_loop(0, tm // ISSUE_UNROLL, issue, 0)
    for k in range(2):
        pltpu.make_async_copy(h_ref, xb_ref.at[pl.ds(0, tm * ROW_CHUNKS)], sem.at[k]).wait()


def _dispatch(h, dest1, dest2, pad_end, counts, n_used, n_slots):
    t_all = h.shape[0] // ROW_CHUNKS
    tm = ROW_TILE
    grid_spec = pltpu.PrefetchScalarGridSpec(
        num_scalar_prefetch=5,
        grid=(t_all // tm,),
        in_specs=[pl.BlockSpec((tm * ROW_CHUNKS, LANES), lambda i, *_: (i, 0))],
        out_specs=pl.BlockSpec(memory_space=pl.ANY),
        scratch_shapes=[
            pltpu.VMEM((EXPERT_BLOCK * ROW_CHUNKS, LANES), F32),
            pltpu.SemaphoreType.DMA(()),
            pltpu.SemaphoreType.DMA((2,)),
        ],
    )
    return pl.pallas_call(
        _dispatch_kernel,
        grid_spec=grid_spec,
        out_shape=jax.ShapeDtypeStruct((n_slots * ROW_CHUNKS, LANES), F32),
        compiler_params=_params(("arbitrary",)),
    )(dest1, dest2, pad_end, counts, n_used, h)


def _ffn_kernel(be_ref, nb_ref, xb_ref, wg_ref, wu_ref, wd_ref, yb_ref, x_scr, wg_scr, wu_scr, wd_scr):
    b = pl.program_id(0)

    @pl.when(b < nb_ref[0])
    def _():
        changed = (b == 0) | (be_ref[b] != be_ref[jnp.maximum(b - 1, 0)])

        @pl.when(changed)
        def _():
            wg_scr[...] = wg_ref[...].astype(BF16)
            wu_scr[...] = wu_ref[...].astype(BF16)
            wd_scr[...] = wd_ref[...].astype(BF16)

        for s in range(ROW_CHUNKS):
            x_scr[:, s * LANES:(s + 1) * LANES] = xb_ref[pl.ds(s, EXPERT_BLOCK, stride=ROW_CHUNKS), :].astype(BF16)
        x = x_scr[...]
        act = (_silu(_dot(x, wg_scr[...])) * _dot(x, wu_scr[...])).astype(BF16)
        _store_token_major(yb_ref, _dot(act, wd_scr[...]))

    @pl.when(b >= nb_ref[0])
    def _():
        yb_ref[...] = jnp.zeros_like(yb_ref)


def _expert_ffn(xb, block_expert, n_used, wg, wu, wd, layer, n_blocks):
    blk_rows = EXPERT_BLOCK * ROW_CHUNKS

    def blk(b, be, nb):
        return jnp.minimum(b, nb[0] - 1)

    grid_spec = pltpu.PrefetchScalarGridSpec(
        num_scalar_prefetch=2,
        grid=(n_blocks,),
        in_specs=[
            pl.BlockSpec((blk_rows, LANES), lambda b, be, nb: (blk(b, be, nb), 0)),
            pl.BlockSpec((None, None, D_MODEL, EXPERT_FF), lambda b, be, nb: (layer, be[blk(b, be, nb)], 0, 0)),
            pl.BlockSpec((None, None, D_MODEL, EXPERT_FF), lambda b, be, nb: (layer, be[blk(b, be, nb)], 0, 0)),
            pl.BlockSpec((None, None, EXPERT_FF, D_MODEL), lambda b, be, nb: (layer, be[blk(b, be, nb)], 0, 0)),
        ],
        out_specs=pl.BlockSpec((blk_rows, LANES), lambda b, be, nb: (b, 0)),
        scratch_shapes=[
            pltpu.VMEM((EXPERT_BLOCK, D_MODEL), BF16),
            pltpu.VMEM((D_MODEL, EXPERT_FF), BF16),
            pltpu.VMEM((D_MODEL, EXPERT_FF), BF16),
            pltpu.VMEM((EXPERT_FF, D_MODEL), BF16),
        ],
    )
    return pl.pallas_call(
        _ffn_kernel,
        grid_spec=grid_spec,
        out_shape=jax.ShapeDtypeStruct(xb.shape, F32),
        compiler_params=_params(("arbitrary",)),
    )(block_expert, n_used, xb, wg, wu, wd)


def _combine_kernel(n_prompt_tiles, alpha, d1_ref, d2_ref,
                    x1_ref, route_ref, g2p_ref, g2s_ref, lng_ref, lnb_ref, yb_ref,
                    xo_ref, buf_a, buf_b, sem):
    i = pl.program_id(0)
    tm = x1_ref.shape[0]

    def issue(c, carry):
        for u in range(ISSUE_UNROLL):
            t = c * ISSUE_UNROLL + u
            g = i * tm + t
            _token_copy(yb_ref, d1_ref[g], buf_a, t, sem.at[0]).start(priority=0)
            _token_copy(yb_ref, d2_ref[g], buf_b, t, sem.at[1]).start(priority=1)
        return carry

    lax.fori_loop(0, tm // ISSUE_UNROLL, issue, 0)
    pltpu.make_async_copy(yb_ref.at[pl.ds(0, tm * ROW_CHUNKS)], buf_a, sem.at[0]).wait()
    pltpu.make_async_copy(yb_ref.at[pl.ds(0, tm * ROW_CHUNKS)], buf_b, sem.at[1]).wait()
    route = route_ref[...]
    ffn = route[:, 2:3] * _load_token_major(buf_a, tm) + route[:, 3:4] * _load_token_major(buf_b, tm)
    g2 = _pick_mod(i >= n_prompt_tiles, g2p_ref, g2s_ref)
    xo_ref[...] = _layer_norm(alpha * x1_ref[...] + g2 * ffn, lng_ref[...], lnb_ref[...])


def _combine(x1, route, yb, dest1, dest2, mod_p, mod_s, lng, lnb, layer, n_prompt, seq_len, alpha):
    t_all = x1.shape[0]
    tm = ROW_TILE
    npt = n_prompt // tm
    g2p, g2s = _mod_specs(layer, 5, tm, npt, seq_len // tm, mod_p.shape[1])
    grid_spec = pltpu.PrefetchScalarGridSpec(
        num_scalar_prefetch=2,
        grid=(t_all // tm,),
        in_specs=[
            pl.BlockSpec((tm, D_MODEL), lambda i, *_: (i, 0)),
            pl.BlockSpec((tm, LANES), lambda i, *_: (i, 0)),
            g2p, g2s,
            pl.BlockSpec((1, D_MODEL), lambda i, *_: (0, 0)),
            pl.BlockSpec((1, D_MODEL), lambda i, *_: (0, 0)),
            pl.BlockSpec(memory_space=pl.ANY),
        ],
        out_specs=pl.BlockSpec((tm, D_MODEL), lambda i, *_: (i, 0)),
        scratch_shapes=[
            pltpu.VMEM((tm * ROW_CHUNKS, LANES), F32),
            pltpu.VMEM((tm * ROW_CHUNKS, LANES), F32),
            pltpu.SemaphoreType.DMA((2,)),
        ],
    )
    return pl.pallas_call(
        functools.partial(_combine_kernel, npt, alpha),
        grid_spec=grid_spec,
        out_shape=jax.ShapeDtypeStruct((t_all, D_MODEL), F32),
        compiler_params=_params(("arbitrary",)),
    )(dest1, dest2, x1, route, mod_p, mod_s, lng, lnb, yb)


def _rope_tables(pos):
    inv = jnp.power(ROPE_THETA, -jnp.arange(ROT_HALF, dtype=F32) * (2.0 / ROT_DIM))
    ang = pos.astype(F32)[:, None] * inv[None, :]
    cos, sin = jnp.cos(ang), jnp.sin(ang)
    rest = HEAD_DIM - ROT_DIM
    n = pos.shape[0]
    cos_h = jnp.concatenate([cos, cos, jnp.ones((n, rest), F32)], axis=-1)
    sin_h = jnp.concatenate([-sin, sin, jnp.zeros((n, rest), F32)], axis=-1)
    reps = LANES // HEAD_DIM
    return jnp.tile(cos_h, (1, reps)), jnp.tile(sin_h, (1, reps))


def _permute_w_in(w_in):
    sizes = (ATTN_WIDTH, KV_WIDTH, KV_WIDTH, SSD_D_INNER, SSD_CONV_DIM, SSD_HEADS,
             SC_DIM, SC_DIM, SC_DIM, 3 * D_MODEL)
    offs = np.concatenate([[0], np.cumsum(sizes)])
    q, k, v, z, xbc, dt, scb, scc, scv, gates = (w_in[..., offs[n]:offs[n + 1]] for n in range(len(sizes)))
    main = jnp.concatenate([q, z, xbc, scb, scc, gates, scv, k, v], axis=-1).astype(BF16)
    dt = jnp.pad(dt, ((0, 0), (0, 0), (0, DT_PAD - SSD_HEADS))).astype(BF16)
    return main, dt


def _pad_lanes(v, width):
    return jnp.pad(v, ((0, 0), (0, width - v.shape[-1])))


def kernel(x_prompt, x_sample, c_prompt, c_sample, cache_attn_k, cache_attn_v, state_ssm, state_ssd_conv, state_short_conv, w_ada, b_ada, w_in, attn_sink, ssd_conv_w, ssd_conv_b, ssd_dt_bias, ssd_a_log, ssd_d, ssd_norm_w, sc_conv_w, w_pa, w_pb, w_pc, w_out, ln1_g, ln1_b, ln2_g, ln2_b, router_g_w, router_g_b, router_e_w, router_e_b, moe_w_gate, moe_w_up, moe_w_down):
    depth = w_in.shape[0]
    n_seq, seq_len, _ = x_prompt.shape
    n_dec, dec_len, _ = x_sample.shape
    wb = cache_attn_k.shape[2]
    past_len = 8192
    assert dec_len == DEC_SEQ and wb == WINDOW
    assert seq_len % ROW_TILE_IN == 0 and (n_dec * dec_len) % ROW_TILE_IN == 0
    n_prompt = n_seq * seq_len
    n_sample = n_dec * dec_len
    t_all = n_prompt + n_sample
    alpha = (2 * depth) ** 0.25

    mod = _modulation(jnp.concatenate([c_prompt, c_sample], axis=0), w_ada, b_ada)
    mod_p_all = mod[:, :n_seq].reshape(depth, n_seq, 1, 6 * D_MODEL)
    mod_s_all = jnp.repeat(mod[:, n_seq:], dec_len, axis=1).reshape(depth, 1, n_sample, 6 * D_MODEL)

    w_main_all, w_dt_all = _permute_w_in(w_in)
    cos_p, sin_p = _rope_tables(jnp.arange(seq_len, dtype=jnp.int32))
    cos_s, sin_s = _rope_tables(past_len + (jnp.arange(MIX_ROWS, dtype=jnp.int32) % dec_len))

    a_total = 2 * t_all
    n_blocks = (a_total + N_EXPERTS * (EXPERT_BLOCK - 1)) // EXPERT_BLOCK
    n_slots = n_blocks * EXPERT_BLOCK

    mod_p, mod_s = mod_p_all, mod_s_all
    cache_k = cache_attn_k.reshape(depth, n_dec, wb, KV_WIDTH)
    cache_v = cache_attn_v.reshape(depth, n_dec, wb, KV_WIDTH)
    h0 = state_ssm.reshape(depth, n_dec, SSD_D_INNER, SSD_STATE)
    xbuf = jnp.pad(state_ssd_conv, ((0, 0), (0, 0), (dec_len - (SSD_CONV - 1), 0), (0, 0))
                   ).reshape(depth, n_sample, SSD_CONV_DIM)
    cbuf = jnp.pad(state_short_conv, ((0, 0), (0, 0), (dec_len - (SC_WIDTH - 1), 0), (0, 0))
                   ).reshape(depth, n_sample, SC_DIM)
    wpa, wpb, wpc, wout = (w.astype(BF16) for w in (w_pa, w_pb, w_pc, w_out))
    wr = jnp.pad(jnp.concatenate([router_g_w, router_e_w], axis=-1),
                 ((0, 0), (0, 0), (0, LANES - MOE_GROUPS - N_EXPERTS))).astype(BF16)
    br = jnp.pad(jnp.concatenate([router_g_b, router_e_b], axis=-1),
                 ((0, 0), (0, LANES - MOE_GROUPS - N_EXPERTS)))[:, None, :]
    block_first_slot = jnp.arange(n_blocks, dtype=jnp.int32) * EXPERT_BLOCK

    x = jnp.concatenate([x_prompt.reshape(n_prompt, D_MODEL), x_sample.reshape(n_sample, D_MODEL)], axis=0)
    outs_p = [[] for _ in range(5)]
    outs_s = [[] for _ in range(5)]
    for l in range(depth):
        p, dt = _input_projection(x, mod_p, mod_s, w_main_all, w_dt_all, l, n_prompt, seq_len)

        cw, cb = ssd_conv_w[l], ssd_conv_b[l][None]
        dtb = _pad_lanes(ssd_dt_bias[l][None], DT_PAD)
        alog = _pad_lanes(ssd_a_log[l][None], DT_PAD)
        de = jnp.repeat(ssd_d[l], SSD_HEAD_DIM)[None]
        nw = ssd_norm_w[l][None]
        scw = sc_conv_w[l]

        ya_p, krot_p = _attention_prompt(p, attn_sink[l], cos_p, sin_p, n_seq, seq_len)
        yb_p, yc_p, h_p, cv_p = _ssd_prompt(p, dt, cw, cb, dtb, alog, de, nw, scw, n_seq, seq_len)

        ya_s, yb_s, yc_s, k_s, v_s, h_s, cv_s = _mix_sample(
            p, dt, attn_sink[l], cos_s, sin_s, cache_k, cache_v, h0, xbuf, cbuf,
            cw, cb, dtb, alog, de, nw, scw, l, n_prompt, n_dec)

        x1, h, route, counts = _output_projection(
            x, (ya_p, yb_p, yc_p), (ya_s, yb_s, yc_s), p, mod_p, mod_s,
            wpa, wpb, wpc, wout, wr, br, ln1_g[l][None], ln1_b[l][None], l, n_prompt, seq_len, alpha)

        cnt = counts[0, :N_EXPERTS].astype(jnp.int32)
        pad_cnt = (cnt + EXPERT_BLOCK - 1) // EXPERT_BLOCK * EXPERT_BLOCK
        pad_end = jnp.cumsum(pad_cnt)
        pad_start = pad_end - pad_cnt
        n_used = (pad_end[-1:] // EXPERT_BLOCK).astype(jnp.int32)
        block_expert = jnp.minimum(
            jnp.sum((pad_end[None, :] <= block_first_slot[:, None]).astype(jnp.int32), axis=1), N_EXPERTS - 1)
        dest1, dest2 = _slots(route, _pad_lanes(pad_start.astype(F32)[None], LANES))

        xb = _dispatch(h, dest1, dest2, pad_end.astype(jnp.int32), cnt, n_used, n_slots)
        yb_slots = _expert_ffn(xb, block_expert, n_used, moe_w_gate, moe_w_up, moe_w_down, l, n_blocks)
        x = _combine(x1, route, yb_slots, dest1, dest2, mod_p, mod_s, ln2_g[l][None], ln2_b[l][None],
                     l, n_prompt, seq_len, alpha)

        def prompt_tail(rows, c0, c1):
            return jnp.stack([p[(b + 1) * seq_len - rows:(b + 1) * seq_len, c0:c1] for b in range(n_seq)]
                             ).astype(F32)

        outs_p[0].append(krot_p.reshape(n_seq, wb, N_KV_HEADS, HEAD_DIM))
        outs_p[1].append(prompt_tail(wb, COL_KV + KV_WIDTH, COL_KV + 2 * KV_WIDTH)
                         .reshape(n_seq, wb, N_KV_HEADS, HEAD_DIM))
        outs_p[2].append(h_p.reshape(n_seq, SSD_HEADS, SSD_HEAD_DIM, SSD_STATE))
        outs_p[3].append(prompt_tail(SSD_CONV - 1, COL_XBC, COL_XBC + SSD_CONV_DIM))
        outs_p[4].append(cv_p[:, SUBLANES - (SC_WIDTH - 1):, :])
        outs_s[0].append(k_s.reshape(n_dec, wb, N_KV_HEADS, HEAD_DIM))
        outs_s[1].append(v_s.reshape(n_dec, wb, N_KV_HEADS, HEAD_DIM))
        outs_s[2].append(h_s.reshape(n_dec, SSD_HEADS, SSD_HEAD_DIM, SSD_STATE))
        outs_s[3].append(p[n_prompt:, COL_XBC:COL_XBC + SSD_CONV_DIM].astype(F32)
                         .reshape(n_dec, dec_len, SSD_CONV_DIM)[:, dec_len - (SSD_CONV - 1):, :])
        outs_s[4].append(cv_s.reshape(n_dec, dec_len, SC_DIM)[:, dec_len - (SC_WIDTH - 1):, :])

    y_prompt = x[:n_prompt].reshape(n_seq, seq_len, D_MODEL)
    y_sample = x[n_prompt:].reshape(n_dec, dec_len, D_MODEL)
    return (y_prompt, y_sample, *[jnp.stack(o) for o in outs_p], *[jnp.stack(o) for o in outs_s])
```

```python
import functools

import jax
import jax.numpy as jnp
import numpy as np
from jax import lax
from jax.experimental import pallas as pl
from jax.experimental.pallas import tpu as pltpu

F32 = jnp.float32
BF16 = jnp.bfloat16

D_MODEL = 1024
HEAD_DIM = 64
N_HEADS = 8
N_KV_HEADS = 2
Q_PER_KV = N_HEADS // N_KV_HEADS
ATTN_WIDTH = N_HEADS * HEAD_DIM
KV_WIDTH = N_KV_HEADS * HEAD_DIM
WINDOW = 128
ROT_DIM = HEAD_DIM // 4
ROT_HALF = ROT_DIM // 2
ROPE_THETA = 500000.0
ATTN_SCALE = HEAD_DIM ** -0.5
SSD_D_INNER = 512
SSD_HEAD_DIM = 64
SSD_HEADS = 8
SSD_GROUPS = 2
SSD_HEADS_PER_GROUP = SSD_HEADS // SSD_GROUPS
SSD_STATE = 128
SSD_CONV = 4
SSD_CONV_DIM = SSD_D_INNER + 2 * SSD_GROUPS * SSD_STATE
SSD_CHUNK = 128
SC_DIM = 512
SC_WIDTH = 3
MOE_GROUPS = 4
EXPERTS_PER_GROUP = 8
N_EXPERTS = MOE_GROUPS * EXPERTS_PER_GROUP
EXPERT_FF = 512
LN_EPS = 1e-5
RMS_EPS = 1e-5

SUBLANES = 8
LANES = 128
VMEM_LIMIT = 56 * 1024 * 1024

COL_Q = 0
COL_Z = 512
COL_XBC = 1024
COL_SCB = 2048
COL_SCC = 2560
COL_GATES = 3072
COL_SCV = 6144
COL_KV = 6656
P_MAIN = 6912
DT_PAD = 128

ROW_TILE_IN = 1024
COL_TILE_IN = 2304
ROW_TILE = 512
MIX_ROWS = 128
DEC_SEQ = 8
SEQS_PER_STEP = MIX_ROWS // DEC_SEQ
EXPERT_BLOCK = 256
ROW_CHUNKS = D_MODEL // LANES
ISSUE_UNROLL = 8
ROUTE_EXPERT_ROW = 8
ROUTE_ROWS = 48


def _silu(v):
    return v * jax.nn.sigmoid(v)


def _dot(a, b):
    return jnp.dot(a, b, preferred_element_type=F32)


def _dot_nt(a, b):
    return lax.dot_general(a, b, (((1,), (1,)), ((), ())), preferred_element_type=F32)


def _dot_exact(a, b):
    return jnp.dot(a, b, preferred_element_type=F32, precision=lax.Precision.HIGHEST)


def _params(sem):
    return pltpu.CompilerParams(dimension_semantics=sem, vmem_limit_bytes=VMEM_LIMIT)


def _pick_mod(is_sample, prompt_ref, sample_ref):
    return jnp.where(is_sample, sample_ref[0], prompt_ref[0])


def _layer_norm(v, g, b):
    mu = jnp.mean(v, axis=-1, keepdims=True)
    c = v - mu
    var = jnp.mean(c * c, axis=-1, keepdims=True)
    return c * lax.rsqrt(var + LN_EPS) * g + b


def _mod_kernel(c_ref, w_ref, b_ref, o_ref):
    s = _silu(c_ref[...]).astype(BF16)
    o_ref[0] = _dot(s, w_ref[0].astype(BF16)) + b_ref[0]


def _modulation(c_all, w_ada, b_ada):
    depth, _, width = w_ada.shape
    n = c_all.shape[0]
    tn = 1536
    return pl.pallas_call(
        _mod_kernel,
        grid=(depth, width // tn),
        in_specs=[
            pl.BlockSpec((n, D_MODEL), lambda l, j: (0, 0)),
            pl.BlockSpec((1, D_MODEL, tn), lambda l, j: (l, 0, j)),
            pl.BlockSpec((1, 1, tn), lambda l, j: (l, 0, j)),
        ],
        out_specs=pl.BlockSpec((1, n, tn), lambda l, j: (l, 0, j)),
        out_shape=jax.ShapeDtypeStruct((depth, n, width), F32),
        compiler_params=_params(("parallel", "parallel")),
    )(c_all, w_ada, b_ada.reshape(depth, 1, width))


def _inproj_kernel(n_prompt_tiles, x_ref, shp_ref, shs_ref, scp_ref, scs_ref, w_ref, wdt_ref,
                   p_ref, dt_ref, u_scr):
    i = pl.program_id(0)
    j = pl.program_id(1)

    @pl.when(j == 0)
    def _():
        is_sample = i >= n_prompt_tiles
        sh = _pick_mod(is_sample, shp_ref, shs_ref)
        sc = _pick_mod(is_sample, scp_ref, scs_ref)
        u = (x_ref[...] * (1.0 + sc) + sh).astype(BF16)
        u_scr[...] = u
        dt_ref[...] = _dot(u, wdt_ref[...])

    p_ref[...] = _dot(u_scr[...], w_ref[...]).astype(BF16)


def _mod_specs(layer, col, row_tile, n_prompt_tiles, tiles_per_seq, n_seq):
    def prompt_map(i, *_):
        return (layer, jnp.minimum(i // tiles_per_seq, n_seq - 1), 0, col)

    def sample_map(i, *_):
        return (layer, 0, jnp.maximum(i - n_prompt_tiles, 0), col)

    return (pl.BlockSpec((None, 1, 1, D_MODEL), prompt_map),
            pl.BlockSpec((None, 1, row_tile, D_MODEL), sample_map))


def _layer_spec(layer, shape):
    return pl.BlockSpec((None, *shape), lambda *_: (layer,) + (0,) * len(shape))


def _input_projection(x, mod_p, mod_s, w_main, w_dt, layer, n_prompt, seq_len):
    t_all = x.shape[0]
    tm, tn = ROW_TILE_IN, COL_TILE_IN
    npt = n_prompt // tm
    n_seq = mod_p.shape[1]
    shp, shs = _mod_specs(layer, 0, tm, npt, seq_len // tm, n_seq)
    scp, scs = _mod_specs(layer, 1, tm, npt, seq_len // tm, n_seq)
    return pl.pallas_call(
        functools.partial(_inproj_kernel, npt),
        grid=(t_all // tm, P_MAIN // tn),
        in_specs=[
            pl.BlockSpec((tm, D_MODEL), lambda i, j: (i, 0)),
            shp, shs, scp, scs,
            pl.BlockSpec((None, D_MODEL, tn), lambda i, j: (layer, 0, j)),
            _layer_spec(layer, (D_MODEL, DT_PAD)),
        ],
        out_specs=[
            pl.BlockSpec((tm, tn), lambda i, j: (i, j)),
            pl.BlockSpec((tm, DT_PAD), lambda i, j: (i, 0)),
        ],
        out_shape=[
            jax.ShapeDtypeStruct((t_all, P_MAIN), BF16),
            jax.ShapeDtypeStruct((t_all, DT_PAD), F32),
        ],
        scratch_shapes=[pltpu.VMEM((tm, D_MODEL), BF16)],
        compiler_params=_params(("parallel", "arbitrary")),
    )(x, mod_p, mod_s, mod_p, mod_s, w_main, w_dt)


def _rope(v, cos, sin):
    width = v.shape[-1]
    reps = width // LANES
    if reps > 1:
        cos = jnp.concatenate([cos] * reps, axis=-1)
        sin = jnp.concatenate([sin] * reps, axis=-1)
    lane = lax.broadcasted_iota(jnp.int32, v.shape, 1) % HEAD_DIM
    partner = jnp.where(lane < ROT_HALF,
                        pltpu.roll(v, width - ROT_HALF, 1),
                        pltpu.roll(v, ROT_HALF, 1))
    return v * cos + partner * sin


def _shift_rows(cur, prev, k):
    axis = cur.ndim - 2
    idx = lax.broadcasted_iota(jnp.int32, cur.shape, axis)
    return jnp.where(idx < k, pltpu.roll(prev, k, axis), pltpu.roll(cur, k, axis))


def _causal_conv(cur, prev, w_ref, width):
    out = cur * w_ref[width - 1:width, :]
    for k in range(1, width):
        out = out + _shift_rows(cur, prev, k) * w_ref[width - 1 - k:width - k, :]
    return out


def _head_expand():
    r = lax.broadcasted_iota(jnp.int32, (LANES, SSD_D_INNER), 0)
    c = lax.broadcasted_iota(jnp.int32, (LANES, SSD_D_INNER), 1)
    return (c // SSD_HEAD_DIM == r).astype(F32)


def _head_expand_t():
    r = lax.broadcasted_iota(jnp.int32, (SSD_D_INNER, LANES), 0)
    c = lax.broadcasted_iota(jnp.int32, (SSD_D_INNER, LANES), 1)
    return (r // SSD_HEAD_DIM == c).astype(F32)


def _ssd_tile(act, dt_raw, dtb, alog, seq_rows):
    rows = MIX_ROWS
    xs = act[:, :SSD_D_INNER]
    bm = act[:, SSD_D_INNER:SSD_D_INNER + SSD_GROUPS * SSD_STATE]
    cm = act[:, SSD_D_INNER + SSD_GROUPS * SSD_STATE:]
    v = dt_raw + dtb
    dt = jnp.maximum(v, 0.0) + jnp.log1p(jnp.exp(-jnp.abs(v)))
    a = -jnp.exp(alog)
    dta = dt * a
    ri = lax.broadcasted_iota(jnp.int32, (rows, rows), 0)
    ci = lax.broadcasted_iota(jnp.int32, (rows, rows), 1)
    same = (ri // seq_rows) == (ci // seq_rows)
    causal = same & (ci <= ri)
    cs = _dot_exact(causal.astype(F32), dta)
    expand = _head_expand().astype(BF16)

    def per_head_lanes(v):
        hi = v.astype(BF16)
        lo = (v - hi.astype(F32)).astype(BF16)
        return _dot(hi, expand) + _dot(lo, expand)

    dt_e = per_head_lanes(dt)
    cs_e = per_head_lanes(cs)
    if seq_rows == rows:
        tot = None
        tot_e = cs_e[rows - 1:rows, :]
    else:
        tot = _dot_exact(same.astype(F32), dta)
        tot_e = per_head_lanes(tot)
    cs_t = cs.T
    dtx = xs * dt_e
    xw = dtx * jnp.exp(tot_e - cs_e)
    b_groups, c_groups, y_parts = [], [], []
    for g in range(SSD_GROUPS):
        bg = bm[:, g * SSD_STATE:(g + 1) * SSD_STATE].astype(BF16)
        cg = cm[:, g * SSD_STATE:(g + 1) * SSD_STATE].astype(BF16)
        b_groups.append(bg)
        c_groups.append(cg)
        cb = _dot_nt(cg, bg)
        for hh in range(SSD_HEADS_PER_GROUP):
            h = g * SSD_HEADS_PER_GROUP + hh
            seg = cs[:, h:h + 1] - cs_t[h:h + 1, :]
            decay = jnp.where(causal, jnp.exp(jnp.where(causal, seg, 0.0)), 0.0)
            y_parts.append(_dot((cb * decay).astype(BF16),
                                dtx[:, h * SSD_HEAD_DIM:(h + 1) * SSD_HEAD_DIM].astype(BF16)))
    y_diag = jnp.concatenate(y_parts, axis=-1)
    return xs, y_diag, jnp.exp(cs_e), xw, tot, tot_e, bm, b_groups, c_groups


def _gated_group_norm(y, z, nw):
    y = y * _silu(z)
    half = SSD_D_INNER // SSD_GROUPS
    parts = []
    for g in range(SSD_GROUPS):
        yg = y[:, g * half:(g + 1) * half]
        parts.append(yg * lax.rsqrt(jnp.mean(yg * yg, axis=-1, keepdims=True) + RMS_EPS))
    return jnp.concatenate(parts, axis=-1) * nw


def _attn_prompt_kernel(sink_ref, q_ref, kvc_ref, kvp_ref, cosc_ref, sinc_ref, cosp_ref, sinp_ref,
                        ya_ref, krot_ref):
    i = pl.program_id(1)
    nb = pl.num_programs(1)
    w = WINDOW
    q = _rope(q_ref[...].astype(F32), cosc_ref[...], sinc_ref[...])
    kvc = kvc_ref[...].astype(F32)
    kvp = kvp_ref[...].astype(F32)
    kc = _rope(kvc[:, :KV_WIDTH], cosc_ref[...], sinc_ref[...])
    kp = _rope(kvp[:, :KV_WIDTH], cosp_ref[...], sinp_ref[...])
    vc = kvc[:, KV_WIDTH:]
    vp = kvp[:, KV_WIDTH:]

    @pl.when(i == nb - 1)
    def _():
        krot_ref[0] = kc

    rows = Q_PER_KV * w
    r = lax.broadcasted_iota(jnp.int32, (rows, 2 * w), 0)
    s = lax.broadcasted_iota(jnp.int32, (rows, 2 * w), 1)
    diff = w + (r % w) - s
    first_key = jnp.where(i > 0, 0, w)
    valid = (diff >= 0) & (diff < w) & (s >= first_key)
    rcol = lax.broadcasted_iota(jnp.int32, (rows, 1), 0)
    outs = []
    for kh in range(N_KV_HEADS):
        hs = slice(kh * HEAD_DIM, (kh + 1) * HEAD_DIM)
        k2 = jnp.concatenate([kp[:, hs], kc[:, hs]], axis=0).astype(BF16)
        v2 = jnp.concatenate([vp[:, hs], vc[:, hs]], axis=0).astype(BF16)
        qg = jnp.concatenate(
            [q[:, (kh * Q_PER_KV + g) * HEAD_DIM:(kh * Q_PER_KV + g + 1) * HEAD_DIM] for g in range(Q_PER_KV)],
            axis=0).astype(BF16)
        sink = jnp.zeros((rows, 1), F32)
        for g in range(Q_PER_KV):
            sink = jnp.where(rcol // w == g, sink_ref[kh * Q_PER_KV + g], sink)
        logits = jnp.where(valid, _dot_nt(qg, k2) * ATTN_SCALE, -jnp.inf)
        m = jnp.maximum(jnp.max(logits, axis=-1, keepdims=True), sink)
        e = jnp.exp(logits - m)
        den = jnp.sum(e, axis=-1, keepdims=True) + jnp.exp(sink - m)
        o = _dot(e.astype(BF16), v2) / den
        for g in range(Q_PER_KV):
            outs.append(o[g * w:(g + 1) * w, :])
    ya_ref[...] = jnp.concatenate(outs, axis=-1).astype(BF16)


def _attention_prompt(p, sink, cos_p, sin_p, n_seq, seq_len):
    nb = seq_len // WINDOW
    n_prompt = n_seq * seq_len
    w = WINDOW

    def cur(b, i):
        return b * nb + i

    def prev(b, i):
        return jnp.maximum(b * nb + i - 1, 0)

    return pl.pallas_call(
        _attn_prompt_kernel,
        grid=(n_seq, nb),
        in_specs=[
            pl.BlockSpec(memory_space=pltpu.SMEM),
            pl.BlockSpec((w, ATTN_WIDTH), lambda b, i: (cur(b, i), COL_Q // ATTN_WIDTH)),
            pl.BlockSpec((w, 2 * KV_WIDTH), lambda b, i: (cur(b, i), COL_KV // (2 * KV_WIDTH))),
            pl.BlockSpec((w, 2 * KV_WIDTH), lambda b, i: (prev(b, i), COL_KV // (2 * KV_WIDTH))),
            pl.BlockSpec((w, LANES), lambda b, i: (i, 0)),
            pl.BlockSpec((w, LANES), lambda b, i: (i, 0)),
            pl.BlockSpec((w, LANES), lambda b, i: (jnp.maximum(i - 1, 0), 0)),
            pl.BlockSpec((w, LANES), lambda b, i: (jnp.maximum(i - 1, 0), 0)),
        ],
        out_specs=[
            pl.BlockSpec((w, ATTN_WIDTH), lambda b, i: (cur(b, i), 0)),
            pl.BlockSpec((1, w, KV_WIDTH), lambda b, i: (b, 0, 0)),
        ],
        out_shape=[
            jax.ShapeDtypeStruct((n_prompt, ATTN_WIDTH), BF16),
            jax.ShapeDtypeStruct((n_seq, w, KV_WIDTH), F32),
        ],
        compiler_params=_params(("parallel", "arbitrary")),
    )(sink, p, p, p, cos_p, sin_p, cos_p, sin_p)


def _ssd_prompt_kernel(z_ref, xc_ref, xp_ref, dt_ref, scb_ref, sccc_ref, sccp_ref, scvc_ref, scvp_ref,
                       cw_ref, cb_ref, dtb_ref, alog_ref, de_ref, nw_ref, scw_ref,
                       yb_ref, yc_ref, hout_ref, cvlast_ref, h_scr):
    i = pl.program_id(1)
    nc = pl.num_programs(1)
    first = i == 0

    @pl.when(first)
    def _():
        h_scr[...] = jnp.zeros_like(h_scr)

    xc = xc_ref[...].astype(F32)
    xp = jnp.where(first, 0.0, xp_ref[...].astype(F32))
    act = _silu(_causal_conv(xc, xp, cw_ref, SSD_CONV) + cb_ref[...])
    xs, y_diag, ecs_e, xw, _, tot_e, bm, _, c_groups = _ssd_tile(
        act, dt_ref[...], dtb_ref[...], alog_ref[...], MIX_ROWS)
    gw = SSD_HEADS_PER_GROUP * SSD_HEAD_DIM
    y_off = []
    for g in range(SSD_GROUPS):
        cols = slice(g * gw, (g + 1) * gw)
        hg = h_scr[:, cols]
        y_off.append(_dot(c_groups[g], hg.astype(BF16)))
        b_t = bm[:, g * SSD_STATE:(g + 1) * SSD_STATE].T.astype(BF16)
        h_scr[:, cols] = jnp.exp(tot_e[:, cols]) * hg + _dot(b_t, xw[:, cols].astype(BF16))
    y = y_diag + jnp.concatenate(y_off, axis=-1) * ecs_e + de_ref[...] * xs
    yb_ref[...] = _gated_group_norm(y, z_ref[...].astype(F32), nw_ref[...]).astype(BF16)

    cvc = sccc_ref[...].astype(F32) * scvc_ref[...].astype(F32)
    cvp = jnp.where(first, 0.0, sccp_ref[...].astype(F32) * scvp_ref[...].astype(F32))
    conv_c = _causal_conv(cvc, cvp, scw_ref, SC_WIDTH)
    yc_ref[...] = (scb_ref[...].astype(F32) * conv_c).astype(BF16)

    @pl.when(i == nc - 1)
    def _():
        hout_ref[0] = h_scr[...].T
        cvlast_ref[0] = cvc[MIX_ROWS - SUBLANES:, :]


def _ssd_prompt(p, dt, cw, cb, dtb, alog, de, nw, scw, n_seq, seq_len):
    nc = seq_len // MIX_ROWS
    r = MIX_ROWS
    n_prompt = n_seq * seq_len

    def cur(b, i):
        return b * nc + i

    def prev(b, i):
        return jnp.maximum(b * nc + i - 1, 0)

    def col(width, offset, which):
        return pl.BlockSpec((r, width), lambda b, i: (which(b, i), offset // width))

    def const(shape):
        return pl.BlockSpec(shape, lambda b, i: (0,) * len(shape))

    return pl.pallas_call(
        _ssd_prompt_kernel,
        grid=(n_seq, nc),
        in_specs=[
            col(SSD_D_INNER, COL_Z, cur),
            col(SSD_CONV_DIM, COL_XBC, cur), col(SSD_CONV_DIM, COL_XBC, prev),
            pl.BlockSpec((r, DT_PAD), lambda b, i: (cur(b, i), 0)),
            col(SC_DIM, COL_SCB, cur),
            col(SC_DIM, COL_SCC, cur), col(SC_DIM, COL_SCC, prev),
            col(SC_DIM, COL_SCV, cur), col(SC_DIM, COL_SCV, prev),
            const((SSD_CONV, SSD_CONV_DIM)), const((1, SSD_CONV_DIM)),
            const((1, DT_PAD)), const((1, DT_PAD)), const((1, SSD_D_INNER)), const((1, SSD_D_INNER)),
            const((SC_WIDTH, SC_DIM)),
        ],
        out_specs=[
            pl.BlockSpec((r, SSD_D_INNER), lambda b, i: (cur(b, i), 0)),
            pl.BlockSpec((r, SC_DIM), lambda b, i: (cur(b, i), 0)),
            pl.BlockSpec((1, SSD_D_INNER, SSD_STATE), lambda b, i: (b, 0, 0)),
            pl.BlockSpec((1, SUBLANES, SC_DIM), lambda b, i: (b, 0, 0)),
        ],
        out_shape=[
            jax.ShapeDtypeStruct((n_prompt, SSD_D_INNER), BF16),
            jax.ShapeDtypeStruct((n_prompt, SC_DIM), BF16),
            jax.ShapeDtypeStruct((n_seq, SSD_D_INNER, SSD_STATE), F32),
            jax.ShapeDtypeStruct((n_seq, SUBLANES, SC_DIM), F32),
        ],
        scratch_shapes=[pltpu.VMEM((SSD_STATE, SSD_D_INNER), F32)],
        compiler_params=_params(("parallel", "arbitrary")),
    )(p, p, p, dt, p, p, p, p, p, cw, cb, dtb, alog, de, nw, scw)


def _mix_sample_kernel(sink_ref, q_ref, kv_ref, z_ref, x_ref, dt_ref, scb_ref, scc_ref, scv_ref,
                       cos_ref, sin_ref, ck_ref, cv_ref, h0_ref, xbuf_ref, cbuf_ref,
                       cw_ref, cb_ref, dtb_ref, alog_ref, de_ref, nw_ref, scw_ref,
                       ya_ref, yb_ref, yc_ref, knew_ref, vnew_ref, hnew_ref, cvout_ref):
    ns, t = SEQS_PER_STEP, DEC_SEQ
    wb = ck_ref.shape[1]

    q = _rope(q_ref[...].astype(F32), cos_ref[...], sin_ref[...])
    kv = kv_ref[...].astype(F32)
    kn = _rope(kv[:, :KV_WIDTH], cos_ref[...], sin_ref[...])
    vn = kv[:, KV_WIDTH:]
    q3 = q.reshape(ns, t, ATTN_WIDTH)
    kn3 = kn.reshape(ns, t, KV_WIDTH)
    vn3 = vn.reshape(ns, t, KV_WIDTH)
    knew_ref[:, :wb - t, :] = ck_ref[:, t:, :]
    knew_ref[:, wb - t:, :] = kn3
    vnew_ref[:, :wb - t, :] = cv_ref[:, t:, :]
    vnew_ref[:, wb - t:, :] = vn3
    nq = Q_PER_KV * t
    qi = lax.broadcasted_iota(jnp.int32, (ns, nq, wb + t), 1) % t
    si = lax.broadcasted_iota(jnp.int32, (ns, nq, wb + t), 2)
    valid = ((si < wb) & (si > qi + (wb - WINDOW))) | ((si >= wb) & (si - wb <= qi))
    hrow = lax.broadcasted_iota(jnp.int32, (ns, nq, 1), 1) // t
    heads = [None] * N_HEADS
    for kh in range(N_KV_HEADS):
        hs = slice(kh * HEAD_DIM, (kh + 1) * HEAD_DIM)
        k_all = jnp.concatenate([ck_ref[:, :, hs], kn3[:, :, hs]], axis=1).astype(BF16)
        v_all = jnp.concatenate([cv_ref[:, :, hs], vn3[:, :, hs]], axis=1).astype(BF16)
        qg = jnp.concatenate(
            [q3[:, :, (kh * Q_PER_KV + g) * HEAD_DIM:(kh * Q_PER_KV + g + 1) * HEAD_DIM] for g in range(Q_PER_KV)],
            axis=1).astype(BF16)
        sink = jnp.zeros((ns, nq, 1), F32)
        for g in range(Q_PER_KV):
            sink = jnp.where(hrow == g, sink_ref[kh * Q_PER_KV + g], sink)
        logits = jnp.einsum('bqd,bsd->bqs', qg, k_all, preferred_element_type=F32) * ATTN_SCALE
        logits = jnp.where(valid, logits, -jnp.inf)
        m = jnp.maximum(jnp.max(logits, axis=-1, keepdims=True), sink)
        e = jnp.exp(logits - m)
        den = jnp.sum(e, axis=-1, keepdims=True) + jnp.exp(sink - m)
        o = jnp.einsum('bqs,bsd->bqd', e.astype(BF16), v_all, preferred_element_type=F32) / den
        for g in range(Q_PER_KV):
            heads[kh * Q_PER_KV + g] = o[:, g * t:(g + 1) * t, :]
    ya_ref[...] = jnp.concatenate(heads, axis=-1).reshape(ns * t, ATTN_WIDTH).astype(BF16)

    xc3 = x_ref[...].astype(F32).reshape(ns, t, SSD_CONV_DIM)
    xp3 = xbuf_ref[...].reshape(ns, t, SSD_CONV_DIM)
    conv = _causal_conv(xc3, xp3, cw_ref, SSD_CONV).reshape(ns * t, SSD_CONV_DIM)
    act = _silu(conv + cb_ref[...])
    xs, y_diag, ecs_e, xw, tot, _, _, b_groups, c_groups = _ssd_tile(
        act, dt_ref[...], dtb_ref[...], alog_ref[...], t)
    xw_t = xw.T
    dec_t = jnp.exp(_dot_exact(_head_expand_t(), tot.T))
    gw = SSD_HEADS_PER_GROUP * SSD_HEAD_DIM
    y_off = []
    for g in range(SSD_GROUPS):
        c3 = c_groups[g].reshape(ns, t, SSD_STATE)
        hg = h0_ref[:, g * gw:(g + 1) * gw, :].astype(BF16)
        y_off.append(jnp.einsum('btn,bqn->btq', c3, hg, preferred_element_type=F32).reshape(ns * t, gw))
    y = y_diag + jnp.concatenate(y_off, axis=-1) * ecs_e + de_ref[...] * xs
    yb_ref[...] = _gated_group_norm(y, z_ref[...].astype(F32), nw_ref[...]).astype(BF16)
    col = lax.broadcasted_iota(jnp.int32, xw_t.shape, 1) // t
    for b in range(ns):
        xw_b = jnp.where(col == b, xw_t, jnp.zeros_like(xw_t))
        upd = jnp.concatenate(
            [_dot(xw_b[g * gw:(g + 1) * gw, :].astype(BF16), b_groups[g]) for g in range(SSD_GROUPS)], axis=0)
        hnew_ref[b] = dec_t[:, b * t:b * t + 1] * h0_ref[b] + upd

    cvc = scc_ref[...].astype(F32) * scv_ref[...].astype(F32)
    cvout_ref[...] = cvc
    conv_c = _causal_conv(cvc.reshape(ns, t, SC_DIM), cbuf_ref[...].reshape(ns, t, SC_DIM), scw_ref, SC_WIDTH)
    yc_ref[...] = (scb_ref[...].astype(F32) * conv_c.reshape(ns * t, SC_DIM)).astype(BF16)


def _mix_sample(p, dt, sink, cos_s, sin_s, cache_k, cache_v, h0, xbuf, cbuf,
                cw, cb, dtb, alog, de, nw, scw, layer, n_prompt, n_dec):
    r = MIX_ROWS
    ns = SEQS_PER_STEP
    steps = n_dec // ns
    base = n_prompt // r
    wb = cache_k.shape[2]

    def col(width, offset):
        return pl.BlockSpec((r, width), lambda i: (base + i, offset // width))

    def state(shape):
        return pl.BlockSpec((None, *shape), lambda i: (layer, i) + (0,) * (len(shape) - 1))

    def const(shape):
        return pl.BlockSpec(shape, lambda i: (0,) * len(shape))

    def rows(width):
        return pl.BlockSpec((r, width), lambda i: (i, 0))

    n_rows = n_dec * DEC_SEQ
    return pl.pallas_call(
        _mix_sample_kernel,
        grid=(steps,),
        in_specs=[
            pl.BlockSpec(memory_space=pltpu.SMEM),
            col(ATTN_WIDTH, COL_Q), col(2 * KV_WIDTH, COL_KV), col(SSD_D_INNER, COL_Z),
            col(SSD_CONV_DIM, COL_XBC),
            pl.BlockSpec((r, DT_PAD), lambda i: (base + i, 0)),
            col(SC_DIM, COL_SCB), col(SC_DIM, COL_SCC), col(SC_DIM, COL_SCV),
            const((r, LANES)), const((r, LANES)),
            state((ns, wb, KV_WIDTH)), state((ns, wb, KV_WIDTH)),
            state((ns, SSD_D_INNER, SSD_STATE)),
            state((r, SSD_CONV_DIM)), state((r, SC_DIM)),
            const((SSD_CONV, SSD_CONV_DIM)), const((1, SSD_CONV_DIM)),
            const((1, DT_PAD)), const((1, DT_PAD)), const((1, SSD_D_INNER)), const((1, SSD_D_INNER)),
            const((SC_WIDTH, SC_DIM)),
        ],
        out_specs=[
            rows(ATTN_WIDTH), rows(SSD_D_INNER), rows(SC_DIM),
            pl.BlockSpec((ns, wb, KV_WIDTH), lambda i: (i, 0, 0)),
            pl.BlockSpec((ns, wb, KV_WIDTH), lambda i: (i, 0, 0)),
            pl.BlockSpec((ns, SSD_D_INNER, SSD_STATE), lambda i: (i, 0, 0)),
            rows(SC_DIM),
        ],
        out_shape=[
            jax.ShapeDtypeStruct((n_rows, ATTN_WIDTH), BF16),
            jax.ShapeDtypeStruct((n_rows, SSD_D_INNER), BF16),
            jax.ShapeDtypeStruct((n_rows, SC_DIM), BF16),
            jax.ShapeDtypeStruct((n_dec, wb, KV_WIDTH), F32),
            jax.ShapeDtypeStruct((n_dec, wb, KV_WIDTH), F32),
            jax.ShapeDtypeStruct((n_dec, SSD_D_INNER, SSD_STATE), F32),
            jax.ShapeDtypeStruct((n_rows, SC_DIM), F32),
        ],
        compiler_params=_params(("parallel",)),
    )(sink, p, p, p, p, dt, p, p, p, cos_s, sin_s, cache_k, cache_v, h0, xbuf, cbuf,
      cw, cb, dtb, alog, de, nw, scw)


def _store_token_major(ref, v):
    n = v.shape[0]
    for s in range(ROW_CHUNKS):
        ref[pl.ds(s, n, stride=ROW_CHUNKS), :] = v[:, s * LANES:(s + 1) * LANES]


def _load_token_major(ref, n):
    return jnp.concatenate([ref[pl.ds(s, n, stride=ROW_CHUNKS), :] for s in range(ROW_CHUNKS)], axis=-1)


def _outproj_kernel(n_prompt_tiles, alpha,
                    x_ref, yap_ref, yas_ref, ybp_ref, ybs_ref, ycp_ref, ycs_ref, g_ref,
                    g1p_ref, g1s_ref, sh2p_ref, sh2s_ref, sc2p_ref, sc2s_ref,
                    wpa_ref, wpb_ref, wpc_ref, wout_ref, wr_ref, br_ref, lng_ref, lnb_ref,
                    x1_ref, h_ref, route_ref, cnt_ref, cnt_scr):
    i = pl.program_id(0)
    is_sample = i >= n_prompt_tiles
    tm = x_ref.shape[0]

    @pl.when(i == 0)
    def _():
        cnt_scr[...] = jnp.zeros_like(cnt_scr)

    n = tm
    cnt = cnt_scr[...]
    for r0 in range(0, tm, n):
        rs = slice(r0, r0 + n)

        def pick(prompt_ref, sample_ref):
            return jnp.where(is_sample, sample_ref[0, rs, :], prompt_ref[0])

        ya = jnp.where(is_sample, yas_ref[rs, :], yap_ref[rs, :])
        yb = jnp.where(is_sample, ybs_ref[rs, :], ybp_ref[rs, :])
        yc = jnp.where(is_sample, ycs_ref[rs, :], ycp_ref[rs, :])
        gates = 0.5 * jnp.tanh(0.5 * g_ref[rs, :]) + 0.5
        merged = (gates[:, :D_MODEL] * _dot(ya, wpa_ref[...]).astype(BF16)
                  + gates[:, D_MODEL:2 * D_MODEL] * _dot(yb, wpb_ref[...]).astype(BF16)
                  + gates[:, 2 * D_MODEL:] * _dot(yc, wpc_ref[...]).astype(BF16))
        mix = _dot(merged, wout_ref[...])
        g1 = pick(g1p_ref, g1s_ref)
        x1 = _layer_norm(alpha * x_ref[rs, :] + g1 * mix, lng_ref[...], lnb_ref[...])
        x1_ref[rs, :] = x1
        sh2 = pick(sh2p_ref, sh2s_ref)
        sc2 = pick(sc2p_ref, sc2s_ref)
        h = x1 * (1.0 + sc2) + sh2
        _store_token_major(h_ref.at[pl.ds(r0 * ROW_CHUNKS, n * ROW_CHUNKS)], h)

        lt = (_dot(h.astype(BF16), wr_ref[...]) + br_ref[...]).T[:ROUTE_ROWS]
        row = lax.broadcasted_iota(jnp.int32, lt.shape, 0).astype(F32)
        neg = -jnp.inf
        big = float(ROUTE_ROWS)
        gl = jnp.where(row < MOE_GROUPS, lt, neg)
        gmax = jnp.max(gl, axis=0, keepdims=True)
        g_p = 1.0 / jnp.sum(jnp.exp(gl - gmax), axis=0, keepdims=True)
        gidx = jnp.min(jnp.where(gl == gmax, row, big), axis=0, keepdims=True)
        lo = ROUTE_EXPERT_ROW + EXPERTS_PER_GROUP * gidx
        sel = jnp.where((row >= lo) & (row < lo + EXPERTS_PER_GROUP), lt, neg)
        m1 = jnp.max(sel, axis=0, keepdims=True)
        i1 = jnp.min(jnp.where(sel == m1, row, big), axis=0, keepdims=True)
        sel2 = jnp.where(row == i1, neg, sel)
        m2 = jnp.max(sel2, axis=0, keepdims=True)
        i2 = jnp.min(jnp.where(sel2 == m2, row, big), axis=0, keepdims=True)
        ssum = jnp.sum(jnp.exp(sel - m1), axis=0, keepdims=True)
        p1 = 1.0 / ssum
        p2 = jnp.exp(m2 - m1) / ssum
        w1 = g_p * (p1 / (p1 + p2))
        w2 = g_p * (p2 / (p1 + p2))

        onehot = jnp.where((row == i1) | (row == i2), 1.0, 0.0)
        ri = lax.broadcasted_iota(jnp.int32, (n, n), 0)
        ci = lax.broadcasted_iota(jnp.int32, (n, n), 1)
        earlier = jnp.where(ri < ci, 1.0, 0.0).astype(BF16)
        prefix = _dot(onehot.astype(BF16), earlier) + cnt[:, 0:1]
        rank1 = jnp.sum(jnp.where(row == i1, prefix, 0.0), axis=0, keepdims=True)
        rank2 = jnp.sum(jnp.where(row == i2, prefix, 0.0), axis=0, keepdims=True)
        cnt = cnt + jnp.sum(onehot, axis=1, keepdims=True)
        zero = jnp.zeros_like(w1)
        route_ref[:, rs] = jnp.concatenate(
            [i1 - ROUTE_EXPERT_ROW, i2 - ROUTE_EXPERT_ROW, w1, w2, rank1, rank2, zero, zero], axis=0)

    cnt_scr[...] = cnt
    cnt_ref[...] = cnt


def _output_projection(x, y_prompt, y_sample, p, mod_p, mod_s, wpa, wpb, wpc, wout, wr, br, lng, lnb,
                       layer, n_prompt, seq_len, alpha):
    t_all = x.shape[0]
    tm = ROW_TILE
    npt = n_prompt // tm
    n_seq = mod_p.shape[1]
    mods = []
    for col in (2, 3, 4):
        mods.extend(_mod_specs(layer, col, tm, npt, seq_len // tm, n_seq))

    def rows(width):
        return pl.BlockSpec((tm, width), lambda i: (i, 0))

    def prompt_rows(width):
        return pl.BlockSpec((tm, width), lambda i: (jnp.minimum(i, npt - 1), 0))

    def sample_rows(width):
        return pl.BlockSpec((tm, width), lambda i: (jnp.maximum(i - npt, 0), 0))

    def const(shape):
        return pl.BlockSpec(shape, lambda i: (0,) * len(shape))

    return pl.pallas_call(
        functools.partial(_outproj_kernel, npt, alpha),
        grid=(t_all // tm,),
        in_specs=[
            rows(D_MODEL),
            prompt_rows(ATTN_WIDTH), sample_rows(ATTN_WIDTH),
            prompt_rows(SSD_D_INNER), sample_rows(SSD_D_INNER),
            prompt_rows(SC_DIM), sample_rows(SC_DIM),
            pl.BlockSpec((tm, 3 * D_MODEL), lambda i: (i, COL_GATES // (3 * D_MODEL))),
            *mods,
            _layer_spec(layer, (ATTN_WIDTH, D_MODEL)), _layer_spec(layer, (SSD_D_INNER, D_MODEL)),
            _layer_spec(layer, (SC_DIM, D_MODEL)), _layer_spec(layer, (D_MODEL, D_MODEL)),
            _layer_spec(layer, (D_MODEL, LANES)), _layer_spec(layer, (1, LANES)),
            const((1, D_MODEL)), const((1, D_MODEL)),
        ],
        out_specs=[
            rows(D_MODEL),
            pl.BlockSpec((tm * ROW_CHUNKS, LANES), lambda i: (i, 0)),
            pl.BlockSpec((SUBLANES, tm), lambda i: (0, i)),
            const((ROUTE_ROWS, LANES)),
        ],
        out_shape=[
            jax.ShapeDtypeStruct((t_all, D_MODEL), F32),
            jax.ShapeDtypeStruct((t_all * ROW_CHUNKS, LANES), F32),
            jax.ShapeDtypeStruct((SUBLANES, t_all), F32),
            jax.ShapeDtypeStruct((ROUTE_ROWS, LANES), F32),
        ],
        scratch_shapes=[pltpu.VMEM((ROUTE_ROWS, LANES), F32)],
        compiler_params=_params(("arbitrary",)),
    )(x, y_prompt[0], y_sample[0], y_prompt[1], y_sample[1], y_prompt[2], y_sample[2],
      p, mod_p, mod_s, mod_p, mod_s, mod_p, mod_s,
      wpa, wpb, wpc, wout, wr, br, lng, lnb)


def _slots_kernel(route_ref, start_ref, dest_ref):
    route = route_ref[...]
    tm = route.shape[1]
    expert = lax.broadcasted_iota(jnp.int32, (N_EXPERTS, tm), 0).astype(F32)
    start = start_ref[:, 0:1]
    rows = []
    for e_row, r_row in ((0, 4), (1, 5)):
        first = jnp.sum(jnp.where(expert == route[e_row:e_row + 1], start, 0.0), axis=0, keepdims=True)
        rows.append(first + route[r_row:r_row + 1])
    rows.append(jnp.zeros((SUBLANES - 2, tm), F32))
    dest_ref[...] = jnp.concatenate(rows, axis=0).astype(jnp.int32)


def _slots(route, pad_start):
    t_all = route.shape[1]
    tm = ROW_TILE_IN
    dest = pl.pallas_call(
        _slots_kernel,
        grid=(t_all // tm,),
        in_specs=[pl.BlockSpec((SUBLANES, tm), lambda i: (0, i)),
                  pl.BlockSpec((N_EXPERTS, LANES), lambda i: (0, 0))],
        out_specs=pl.BlockSpec((SUBLANES, tm), lambda i: (0, i)),
        out_shape=jax.ShapeDtypeStruct((SUBLANES, t_all), jnp.int32),
        compiler_params=_params(("parallel",)),
    )(route, jnp.broadcast_to(pad_start.astype(F32)[:, None], (N_EXPERTS, LANES)))
    return dest[0], dest[1]


def _token_copy(src, src_row, dst, dst_row, sem):
    return pltpu.make_async_copy(
        src.at[pl.ds(pl.multiple_of(src_row * ROW_CHUNKS, ROW_CHUNKS), ROW_CHUNKS)],
        dst.at[pl.ds(pl.multiple_of(dst_row * ROW_CHUNKS, ROW_CHUNKS), ROW_CHUNKS)],
        sem)


def _dispatch_kernel(d1_ref, d2_ref, pend_ref, cnt_ref, nb_ref, h_ref, xb_ref, zero_scr, zsem, sem):
    i = pl.program_id(0)
    tm = h_ref.shape[0] // ROW_CHUNKS
    blk_rows = EXPERT_BLOCK * ROW_CHUNKS
    n_blocks = xb_ref.shape[0] // blk_rows

    def zero_block(b):
        start = pl.multiple_of(b * blk_rows, blk_rows)
        return pltpu.make_async_copy(zero_scr, xb_ref.at[pl.ds(start, blk_rows)], zsem)

    def last_block(e):
        return pend_ref[e] // EXPERT_BLOCK - 1

    @pl.when(i == 0)
    def _():
        zero_scr[...] = jnp.zeros_like(zero_scr)
        for e in range(N_EXPERTS):
            @pl.when(cnt_ref[e] > 0)
            def _():
                zero_block(last_block(e)).start()
        lax.fori_loop(nb_ref[0], n_blocks, lambda b, c: (zero_block(b).start(), c)[1], 0)
        for e in range(N_EXPERTS):
            @pl.when(cnt_ref[e] > 0)
            def _():
                zero_block(last_block(e)).wait()
        lax.fori_loop(nb_ref[0], n_blocks, lambda b, c: (zero_block(b).wait(), c)[1], 0)

    def issue(c, carry):
        for u in range(ISSUE_UNROLL):
            t = c * ISSUE_UNROLL + u
            g = i * tm + t
            _token_copy(h_ref, t, xb_ref, d1_ref[g], sem.at[0]).start(priority=0)
            _token_copy(h_ref, t, xb_ref, d2_ref[g], sem.at[1]).start(priority=1)
        return carry

    lax.fori_loop(0, tm // ISSUE_UNROLL, issue, 0)
    for k in range(2):
        pltpu.make_async_copy(h_ref, xb_ref.at[pl.ds(0, tm * ROW_CHUNKS)], sem.at[k]).wait()


def _dispatch(h, dest1, dest2, pad_end, counts, n_used, n_slots):
    t_all = h.shape[0] // ROW_CHUNKS
    tm = ROW_TILE
    grid_spec = pltpu.PrefetchScalarGridSpec(
        num_scalar_prefetch=5,
        grid=(t_all // tm,),
        in_specs=[pl.BlockSpec((tm * ROW_CHUNKS, LANES), lambda i, *_: (i, 0))],
        out_specs=pl.BlockSpec(memory_space=pl.ANY),
        scratch_shapes=[
            pltpu.VMEM((EXPERT_BLOCK * ROW_CHUNKS, LANES), F32),
            pltpu.SemaphoreType.DMA(()),
            pltpu.SemaphoreType.DMA((2,)),
        ],
    )
    return pl.pallas_call(
        _dispatch_kernel,
        grid_spec=grid_spec,
        out_shape=jax.ShapeDtypeStruct((n_slots * ROW_CHUNKS, LANES), F32),
        compiler_params=_params(("arbitrary",)),
    )(dest1, dest2, pad_end, counts, n_used, h)


def _ffn_kernel(layer, first_ref, nblk_ref, nused_ref, xb_ref, wg_ref, wu_ref, wd_ref, yb_ref,
                x_in, y_out, wg_f, wu_f, wd_f, wg_b, wu_b, wd_b, x_scr, xsem, ysem, wsem):
    blk_rows = EXPERT_BLOCK * ROW_CHUNKS
    n_blocks = yb_ref.shape[0] // blk_rows
    n_used = nused_ref[0]

    def block_rows(ref, b):
        return ref.at[pl.ds(pl.multiple_of(b * blk_rows, blk_rows), blk_rows)]

    def x_copy(b, s):
        return pltpu.make_async_copy(block_rows(xb_ref, b), x_in.at[s], xsem.at[s])

    def y_copy(b, s):
        return pltpu.make_async_copy(y_out.at[s], block_rows(yb_ref, b), ysem.at[s])

    def w_copies(e, s):
        return (pltpu.make_async_copy(wg_ref.at[layer, e], wg_f.at[s], wsem.at[s, 0]),
                pltpu.make_async_copy(wu_ref.at[layer, e], wu_f.at[s], wsem.at[s, 1]),
                pltpu.make_async_copy(wd_ref.at[layer, e], wd_f.at[s], wsem.at[s, 2]))

    for c in w_copies(0, 0):
        c.start()
    x_copy(0, 0).start()

    def expert(e, carry):
        ws = e % 2
        for c in w_copies(e, ws):
            c.wait()

        @pl.when(e + 1 < N_EXPERTS)
        def _():
            for c in w_copies(e + 1, 1 - ws):
                c.start()

        @pl.when(nblk_ref[e] > 0)
        def _():
            wg_b[...] = wg_f[ws].astype(BF16)
            wu_b[...] = wu_f[ws].astype(BF16)
            wd_b[...] = wd_f[ws].astype(BF16)

        def block(k, inner):
            b = first_ref[e] + k
            s = b % 2
            x_copy(b, s).wait()

            @pl.when(b + 1 < n_used)
            def _():
                x_copy(b + 1, 1 - s).start()

            x_tok = x_in.at[s]
            for c in range(ROW_CHUNKS):
                x_scr[:, c * LANES:(c + 1) * LANES] = (
                    x_tok[pl.ds(c, EXPERT_BLOCK, stride=ROW_CHUNKS), :].astype(BF16))
            x = x_scr[...]
            act = (_silu(_dot(x, wg_b[...])) * _dot(x, wu_b[...])).astype(BF16)
            y = _dot(act, wd_b[...])

            @pl.when(b >= 2)
            def _():
                y_copy(b - 2, s).wait()

            _store_token_major(y_out.at[s], y)
            y_copy(b, s).start()
            return inner

        lax.fori_loop(0, nblk_ref[e], block, 0)
        return carry

    lax.fori_loop(0, N_EXPERTS, expert, 0)

    @pl.when(n_used >= 2)
    def _():
        y_copy(n_used - 2, n_used % 2).wait()

    y_copy(n_used - 1, (n_used - 1) % 2).wait()

    y_out[0] = jnp.zeros(y_out.shape[1:], F32)
    lax.fori_loop(n_used, n_blocks, lambda b, c: (y_copy(b, 0).start(), c)[1], 0)
    lax.fori_loop(n_used, n_blocks, lambda b, c: (y_copy(b, 0).wait(), c)[1], 0)


def _expert_ffn(xb, first_block, n_expert_blocks, n_used, wg, wu, wd, layer):
    blk_rows = EXPERT_BLOCK * ROW_CHUNKS
    any_spec = pl.BlockSpec(memory_space=pl.ANY)
    grid_spec = pltpu.PrefetchScalarGridSpec(
        num_scalar_prefetch=3,
        grid=(1,),
        in_specs=[any_spec] * 4,
        out_specs=any_spec,
        scratch_shapes=[
            pltpu.VMEM((2, blk_rows, LANES), F32),
            pltpu.VMEM((2, blk_rows, LANES), F32),
            pltpu.VMEM((2, D_MODEL, EXPERT_FF), F32),
            pltpu.VMEM((2, D_MODEL, EXPERT_FF), F32),
            pltpu.VMEM((2, EXPERT_FF, D_MODEL), F32),
            pltpu.VMEM((D_MODEL, EXPERT_FF), BF16),
            pltpu.VMEM((D_MODEL, EXPERT_FF), BF16),
            pltpu.VMEM((EXPERT_FF, D_MODEL), BF16),
            pltpu.VMEM((EXPERT_BLOCK, D_MODEL), BF16),
            pltpu.SemaphoreType.DMA((2,)),
            pltpu.SemaphoreType.DMA((2,)),
            pltpu.SemaphoreType.DMA((2, 3)),
        ],
    )
    return pl.pallas_call(
        functools.partial(_ffn_kernel, layer),
        grid_spec=grid_spec,
        out_shape=jax.ShapeDtypeStruct(xb.shape, F32),
        compiler_params=_params(("arbitrary",)),
    )(first_block, n_expert_blocks, n_used, xb, wg, wu, wd)


def _combine_kernel(n_prompt_tiles, alpha, d1_ref, d2_ref,
                    x1_ref, route_ref, g2p_ref, g2s_ref, lng_ref, lnb_ref, yb_ref,
                    xo_ref, buf_a, buf_b, sem):
    i = pl.program_id(0)
    n_tiles = pl.num_programs(0)
    tm = x1_ref.shape[0]
    slot = i % 2

    def gather_tile(tile, s):
        def issue(c, carry):
            for u in range(ISSUE_UNROLL):
                t = c * ISSUE_UNROLL + u
                g = tile * tm + t
                _token_copy(yb_ref, d1_ref[g], buf_a.at[s], t, sem.at[s, 0]).start(priority=0)
                _token_copy(yb_ref, d2_ref[g], buf_b.at[s], t, sem.at[s, 1]).start(priority=1)
            return carry

        lax.fori_loop(0, tm // ISSUE_UNROLL, issue, 0)

    @pl.when(i == 0)
    def _():
        gather_tile(0, 0)

    @pl.when(i + 1 < n_tiles)
    def _():
        gather_tile(i + 1, 1 - slot)

    whole = yb_ref.at[pl.ds(0, tm * ROW_CHUNKS)]
    pltpu.make_async_copy(whole, buf_a.at[slot], sem.at[slot, 0]).wait()
    pltpu.make_async_copy(whole, buf_b.at[slot], sem.at[slot, 1]).wait()
    route_t = jnp.concatenate(
        [route_ref[...], jnp.zeros((LANES - SUBLANES, tm), F32)], axis=0).T
    ffn = (route_t[:, 2:3] * _load_token_major(buf_a.at[slot], tm)
           + route_t[:, 3:4] * _load_token_major(buf_b.at[slot], tm))
    g2 = _pick_mod(i >= n_prompt_tiles, g2p_ref, g2s_ref)
    xo_ref[...] = _layer_norm(alpha * x1_ref[...] + g2 * ffn, lng_ref[...], lnb_ref[...])


def _combine(x1, route, yb, dest1, dest2, mod_p, mod_s, lng, lnb, layer, n_prompt, seq_len, alpha):
    t_all = x1.shape[0]
    tm = ROW_TILE
    npt = n_prompt // tm
    g2p, g2s = _mod_specs(layer, 5, tm, npt, seq_len // tm, mod_p.shape[1])
    grid_spec = pltpu.PrefetchScalarGridSpec(
        num_scalar_prefetch=2,
        grid=(t_all // tm,),
        in_specs=[
            pl.BlockSpec((tm, D_MODEL), lambda i, *_: (i, 0)),
            pl.BlockSpec((SUBLANES, tm), lambda i, *_: (0, i)),
            g2p, g2s,
            pl.BlockSpec((1, D_MODEL), lambda i, *_: (0, 0)),
            pl.BlockSpec((1, D_MODEL), lambda i, *_: (0, 0)),
            pl.BlockSpec(memory_space=pl.ANY),
        ],
        out_specs=pl.BlockSpec((tm, D_MODEL), lambda i, *_: (i, 0)),
        scratch_shapes=[
            pltpu.VMEM((2, tm * ROW_CHUNKS, LANES), F32),
            pltpu.VMEM((2, tm * ROW_CHUNKS, LANES), F32),
            pltpu.SemaphoreType.DMA((2, 2)),
        ],
    )
    return pl.pallas_call(
        functools.partial(_combine_kernel, npt, alpha),
        grid_spec=grid_spec,
        out_shape=jax.ShapeDtypeStruct((t_all, D_MODEL), F32),
        compiler_params=_params(("arbitrary",)),
    )(dest1, dest2, x1, route, mod_p, mod_s, lng, lnb, yb)


def _rope_tables(pos):
    inv = jnp.power(ROPE_THETA, -jnp.arange(ROT_HALF, dtype=F32) * (2.0 / ROT_DIM))
    ang = pos.astype(F32)[:, None] * inv[None, :]
    cos, sin = jnp.cos(ang), jnp.sin(ang)
    rest = HEAD_DIM - ROT_DIM
    n = pos.shape[0]
    cos_h = jnp.concatenate([cos, cos, jnp.ones((n, rest), F32)], axis=-1)
    sin_h = jnp.concatenate([-sin, sin, jnp.zeros((n, rest), F32)], axis=-1)
    reps = LANES // HEAD_DIM
    return jnp.tile(cos_h, (1, reps)), jnp.tile(sin_h, (1, reps))


def _permute_w_in(w_in):
    sizes = (ATTN_WIDTH, KV_WIDTH, KV_WIDTH, SSD_D_INNER, SSD_CONV_DIM, SSD_HEADS,
             SC_DIM, SC_DIM, SC_DIM, 3 * D_MODEL)
    offs = np.concatenate([[0], np.cumsum(sizes)])
    q, k, v, z, xbc, dt, scb, scc, scv, gates = (w_in[..., offs[n]:offs[n + 1]] for n in range(len(sizes)))
    main = jnp.concatenate([q, z, xbc, scb, scc, gates, scv, k, v], axis=-1).astype(BF16)
    dt = jnp.pad(dt, ((0, 0), (0, 0), (0, DT_PAD - SSD_HEADS))).astype(BF16)
    return main, dt


def _pad_lanes(v, width):
    return jnp.pad(v, ((0, 0), (0, width - v.shape[-1])))


def kernel(x_prompt, x_sample, c_prompt, c_sample, cache_attn_k, cache_attn_v, state_ssm, state_ssd_conv, state_short_conv, w_ada, b_ada, w_in, attn_sink, ssd_conv_w, ssd_conv_b, ssd_dt_bias, ssd_a_log, ssd_d, ssd_norm_w, sc_conv_w, w_pa, w_pb, w_pc, w_out, ln1_g, ln1_b, ln2_g, ln2_b, router_g_w, router_g_b, router_e_w, router_e_b, moe_w_gate, moe_w_up, moe_w_down):
    depth = w_in.shape[0]
    n_seq, seq_len, _ = x_prompt.shape
    n_dec, dec_len, _ = x_sample.shape
    wb = cache_attn_k.shape[2]
    past_len = 8192
    assert dec_len == DEC_SEQ and wb == WINDOW
    assert seq_len % ROW_TILE_IN == 0 and (n_dec * dec_len) % ROW_TILE_IN == 0
    n_prompt = n_seq * seq_len
    n_sample = n_dec * dec_len
    t_all = n_prompt + n_sample
    alpha = (2 * depth) ** 0.25

    mod = _modulation(jnp.concatenate([c_prompt, c_sample], axis=0), w_ada, b_ada)
    mod_p_all = mod[:, :n_seq].reshape(depth, n_seq, 1, 6 * D_MODEL)
    mod_s_all = jnp.repeat(mod[:, n_seq:], dec_len, axis=1).reshape(depth, 1, n_sample, 6 * D_MODEL)

    w_main_all, w_dt_all = _permute_w_in(w_in)
    cos_p, sin_p = _rope_tables(jnp.arange(seq_len, dtype=jnp.int32))
    cos_s, sin_s = _rope_tables(past_len + (jnp.arange(MIX_ROWS, dtype=jnp.int32) % dec_len))

    a_total = 2 * t_all
    n_blocks = (a_total + N_EXPERTS * (EXPERT_BLOCK - 1)) // EXPERT_BLOCK
    n_slots = n_blocks * EXPERT_BLOCK

    mod_p, mod_s = mod_p_all, mod_s_all
    cache_k = cache_attn_k.reshape(depth, n_dec, wb, KV_WIDTH)
    cache_v = cache_attn_v.reshape(depth, n_dec, wb, KV_WIDTH)
    h0 = state_ssm.reshape(depth, n_dec, SSD_D_INNER, SSD_STATE)
    xbuf = jnp.pad(state_ssd_conv, ((0, 0), (0, 0), (dec_len - (SSD_CONV - 1), 0), (0, 0))
                   ).reshape(depth, n_sample, SSD_CONV_DIM)
    cbuf = jnp.pad(state_short_conv, ((0, 0), (0, 0), (dec_len - (SC_WIDTH - 1), 0), (0, 0))
                   ).reshape(depth, n_sample, SC_DIM)
    wpa, wpb, wpc, wout = (w.astype(BF16) for w in (w_pa, w_pb, w_pc, w_out))
    gap = ROUTE_EXPERT_ROW - MOE_GROUPS
    tail = LANES - ROUTE_EXPERT_ROW - N_EXPERTS
    wr = jnp.concatenate([router_g_w, jnp.zeros((depth, D_MODEL, gap), F32), router_e_w,
                          jnp.zeros((depth, D_MODEL, tail), F32)], axis=-1).astype(BF16)
    br = jnp.concatenate([router_g_b, jnp.zeros((depth, gap), F32), router_e_b,
                          jnp.zeros((depth, tail), F32)], axis=-1)[:, None, :]

    x = jnp.concatenate([x_prompt.reshape(n_prompt, D_MODEL), x_sample.reshape(n_sample, D_MODEL)], axis=0)
    outs_p = [[] for _ in range(5)]
    outs_s = [[] for _ in range(5)]
    for l in range(depth):
        p, dt = _input_projection(x, mod_p, mod_s, w_main_all, w_dt_all, l, n_prompt, seq_len)

        cw, cb = ssd_conv_w[l], ssd_conv_b[l][None]
        dtb = _pad_lanes(ssd_dt_bias[l][None], DT_PAD)
        alog = _pad_lanes(ssd_a_log[l][None], DT_PAD)
        de = jnp.repeat(ssd_d[l], SSD_HEAD_DIM)[None]
        nw = ssd_norm_w[l][None]
        scw = sc_conv_w[l]

        ya_p, krot_p = _attention_prompt(p, attn_sink[l], cos_p, sin_p, n_seq, seq_len)
        yb_p, yc_p, h_p, cv_p = _ssd_prompt(p, dt, cw, cb, dtb, alog, de, nw, scw, n_seq, seq_len)

        ya_s, yb_s, yc_s, k_s, v_s, h_s, cv_s = _mix_sample(
            p, dt, attn_sink[l], cos_s, sin_s, cache_k, cache_v, h0, xbuf, cbuf,
            cw, cb, dtb, alog, de, nw, scw, l, n_prompt, n_dec)

        x1, h, route, counts = _output_projection(
            x, (ya_p, yb_p, yc_p), (ya_s, yb_s, yc_s), p, mod_p, mod_s,
            wpa, wpb, wpc, wout, wr, br, ln1_g[l][None], ln1_b[l][None], l, n_prompt, seq_len, alpha)

        cnt = counts[ROUTE_EXPERT_ROW:ROUTE_EXPERT_ROW + N_EXPERTS, 0].astype(jnp.int32)
        pad_cnt = (cnt + EXPERT_BLOCK - 1) // EXPERT_BLOCK * EXPERT_BLOCK
        pad_end = jnp.cumsum(pad_cnt)
        pad_start = pad_end - pad_cnt
        n_used = (pad_end[-1:] // EXPERT_BLOCK).astype(jnp.int32)
        dest1, dest2 = _slots(route, pad_start)

        xb = _dispatch(h, dest1, dest2, pad_end.astype(jnp.int32), cnt, n_used, n_slots)
        yb_slots = _expert_ffn(xb, (pad_start // EXPERT_BLOCK).astype(jnp.int32),
                               (pad_cnt // EXPERT_BLOCK).astype(jnp.int32), n_used,
                               moe_w_gate, moe_w_up, moe_w_down, l)
        x = _combine(x1, route, yb_slots, dest1, dest2, mod_p, mod_s, ln2_g[l][None], ln2_b[l][None],
                     l, n_prompt, seq_len, alpha)

        def prompt_tail(rows, c0, c1):
            return jnp.stack([p[(b + 1) * seq_len - rows:(b + 1) * seq_len, c0:c1] for b in range(n_seq)]
                             ).astype(F32)

        outs_p[0].append(krot_p.reshape(n_seq, wb, N_KV_HEADS, HEAD_DIM))
        outs_p[1].append(prompt_tail(wb, COL_KV + KV_WIDTH, COL_KV + 2 * KV_WIDTH)
                         .reshape(n_seq, wb, N_KV_HEADS, HEAD_DIM))
        outs_p[2].append(h_p.reshape(n_seq, SSD_HEADS, SSD_HEAD_DIM, SSD_STATE))
        outs_p[3].append(prompt_tail(SSD_CONV - 1, COL_XBC, COL_XBC + SSD_CONV_DIM))
        outs_p[4].append(cv_p[:, SUBLANES - (SC_WIDTH - 1):, :])
        outs_s[0].append(k_s.reshape(n_dec, wb, N_KV_HEADS, HEAD_DIM))
        outs_s[1].append(v_s.reshape(n_dec, wb, N_KV_HEADS, HEAD_DIM))
        outs_s[2].append(h_s.reshape(n_dec, SSD_HEADS, SSD_HEAD_DIM, SSD_STATE))
        outs_s[3].append(p[n_prompt:, COL_XBC:COL_XBC + SSD_CONV_DIM].astype(F32)
                         .reshape(n_dec, dec_len, SSD_CONV_DIM)[:, dec_len - (SSD_CONV - 1):, :])
        outs_s[4].append(cv_s.reshape(n_dec, dec_len, SC_DIM)[:, dec_len - (SC_WIDTH - 1):, :])

    y_prompt = x[:n_prompt].reshape(n_seq, seq_len, D_MODEL)
    y_sample = x[n_prompt:].reshape(n_dec, dec_len, D_MODEL)
    return (y_prompt, y_sample, *[jnp.stack(o) for o in outs_p], *[jnp.stack(o) for o in outs_s])
```

```python
import functools

import jax
import jax.numpy as jnp
import numpy as np
from jax import lax
from jax.experimental import pallas as pl
from jax.experimental.pallas import tpu as pltpu

F32 = jnp.float32
BF16 = jnp.bfloat16

D_MODEL = 1024
HEAD_DIM = 64
N_HEADS = 8
N_KV_HEADS = 2
Q_PER_KV = N_HEADS // N_KV_HEADS
ATTN_WIDTH = N_HEADS * HEAD_DIM
KV_WIDTH = N_KV_HEADS * HEAD_DIM
WINDOW = 128
ROT_DIM = HEAD_DIM // 4
ROT_HALF = ROT_DIM // 2
ROPE_THETA = 500000.0
ATTN_SCALE = HEAD_DIM ** -0.5
SSD_D_INNER = 512
SSD_HEAD_DIM = 64
SSD_HEADS = 8
SSD_GROUPS = 2
SSD_HEADS_PER_GROUP = SSD_HEADS // SSD_GROUPS
SSD_STATE = 128
SSD_CONV = 4
SSD_CONV_DIM = SSD_D_INNER + 2 * SSD_GROUPS * SSD_STATE
SSD_CHUNK = 128
SC_DIM = 512
SC_WIDTH = 3
MOE_GROUPS = 4
EXPERTS_PER_GROUP = 8
N_EXPERTS = MOE_GROUPS * EXPERTS_PER_GROUP
EXPERT_FF = 512
LN_EPS = 1e-5
RMS_EPS = 1e-5

SUBLANES = 8
LANES = 128
VMEM_LIMIT = 56 * 1024 * 1024

COL_Q = 0
COL_Z = 512
COL_XBC = 1024
COL_SCB = 2048
COL_SCC = 2560
COL_GATES = 3072
COL_SCV = 6144
COL_KV = 6656
P_MAIN = 6912
DT_PAD = 128

ROW_TILE_IN = 1024
COL_TILE_IN = 2304
ROW_TILE = 512
MIX_ROWS = 128
DEC_SEQ = 8
SEQS_PER_STEP = MIX_ROWS // DEC_SEQ
EXPERT_BLOCK = 256
ISSUE_UNROLL = 8
ROUTE_EXPERT_ROW = 8
ROUTE_ROWS = 48


def _silu(v):
    return v * jax.nn.sigmoid(v)


def _dot(a, b):
    return jnp.dot(a, b, preferred_element_type=F32)


def _dot_nt(a, b):
    return lax.dot_general(a, b, (((1,), (1,)), ((), ())), preferred_element_type=F32)


def _dot_exact(a, b):
    return jnp.dot(a, b, preferred_element_type=F32, precision=lax.Precision.HIGHEST)


def _params(sem):
    return pltpu.CompilerParams(dimension_semantics=sem, vmem_limit_bytes=VMEM_LIMIT)


def _pick_mod(is_sample, prompt_ref, sample_ref):
    return jnp.where(is_sample, sample_ref[0], prompt_ref[0])


def _layer_norm(v, g, b):
    mu = jnp.mean(v, axis=-1, keepdims=True)
    c = v - mu
    var = jnp.mean(c * c, axis=-1, keepdims=True)
    return c * lax.rsqrt(var + LN_EPS) * g + b


def _mod_kernel(c_ref, w_ref, b_ref, o_ref):
    s = _silu(c_ref[...]).astype(BF16)
    o_ref[0] = _dot(s, w_ref[0].astype(BF16)) + b_ref[0]


def _modulation(c_all, w_ada, b_ada):
    depth, _, width = w_ada.shape
    n = c_all.shape[0]
    tn = 1536
    return pl.pallas_call(
        _mod_kernel,
        grid=(depth, width // tn),
        in_specs=[
            pl.BlockSpec((n, D_MODEL), lambda l, j: (0, 0)),
            pl.BlockSpec((1, D_MODEL, tn), lambda l, j: (l, 0, j)),
            pl.BlockSpec((1, 1, tn), lambda l, j: (l, 0, j)),
        ],
        out_specs=pl.BlockSpec((1, n, tn), lambda l, j: (l, 0, j)),
        out_shape=jax.ShapeDtypeStruct((depth, n, width), F32),
        compiler_params=_params(("parallel", "parallel")),
    )(c_all, w_ada, b_ada.reshape(depth, 1, width))


def _inproj_kernel(n_prompt_tiles, x_ref, shp_ref, shs_ref, scp_ref, scs_ref, w_ref, wdt_ref,
                   p_ref, dt_ref, u_scr):
    i = pl.program_id(0)
    j = pl.program_id(1)

    @pl.when(j == 0)
    def _():
        is_sample = i >= n_prompt_tiles
        sh = _pick_mod(is_sample, shp_ref, shs_ref)
        sc = _pick_mod(is_sample, scp_ref, scs_ref)
        u = (x_ref[...] * (1.0 + sc) + sh).astype(BF16)
        u_scr[...] = u
        dt_ref[...] = _dot(u, wdt_ref[...])

    p_ref[...] = _dot(u_scr[...], w_ref[...]).astype(BF16)


def _mod_specs(layer, col, row_tile, n_prompt_tiles, tiles_per_seq, n_seq):
    def prompt_map(i, *_):
        return (layer, jnp.minimum(i // tiles_per_seq, n_seq - 1), 0, col)

    def sample_map(i, *_):
        return (layer, 0, jnp.maximum(i - n_prompt_tiles, 0), col)

    return (pl.BlockSpec((None, 1, 1, D_MODEL), prompt_map),
            pl.BlockSpec((None, 1, row_tile, D_MODEL), sample_map))


def _layer_spec(layer, shape):
    return pl.BlockSpec((None, *shape), lambda *_: (layer,) + (0,) * len(shape))


def _input_projection(x, mod_p, mod_s, w_main, w_dt, layer, n_prompt, seq_len):
    t_all = x.shape[0]
    tm, tn = ROW_TILE_IN, COL_TILE_IN
    npt = n_prompt // tm
    n_seq = mod_p.shape[1]
    shp, shs = _mod_specs(layer, 0, tm, npt, seq_len // tm, n_seq)
    scp, scs = _mod_specs(layer, 1, tm, npt, seq_len // tm, n_seq)
    return pl.pallas_call(
        functools.partial(_inproj_kernel, npt),
        grid=(t_all // tm, P_MAIN // tn),
        in_specs=[
            pl.BlockSpec((tm, D_MODEL), lambda i, j: (i, 0)),
            shp, shs, scp, scs,
            pl.BlockSpec((None, D_MODEL, tn), lambda i, j: (layer, 0, j)),
            _layer_spec(layer, (D_MODEL, DT_PAD)),
        ],
        out_specs=[
            pl.BlockSpec((tm, tn), lambda i, j: (i, j)),
            pl.BlockSpec((tm, DT_PAD), lambda i, j: (i, 0)),
        ],
        out_shape=[
            jax.ShapeDtypeStruct((t_all, P_MAIN), BF16),
            jax.ShapeDtypeStruct((t_all, DT_PAD), F32),
        ],
        scratch_shapes=[pltpu.VMEM((tm, D_MODEL), BF16)],
        compiler_params=_params(("parallel", "arbitrary")),
    )(x, mod_p, mod_s, mod_p, mod_s, w_main, w_dt)


def _rope(v, cos, sin):
    width = v.shape[-1]
    reps = width // LANES
    if reps > 1:
        cos = jnp.concatenate([cos] * reps, axis=-1)
        sin = jnp.concatenate([sin] * reps, axis=-1)
    lane = lax.broadcasted_iota(jnp.int32, v.shape, 1) % HEAD_DIM
    partner = jnp.where(lane < ROT_HALF,
                        pltpu.roll(v, width - ROT_HALF, 1),
                        pltpu.roll(v, ROT_HALF, 1))
    return v * cos + partner * sin


def _shift_rows(cur, prev, k):
    axis = cur.ndim - 2
    idx = lax.broadcasted_iota(jnp.int32, cur.shape, axis)
    return jnp.where(idx < k, pltpu.roll(prev, k, axis), pltpu.roll(cur, k, axis))


def _causal_conv(cur, prev, w_ref, width):
    out = cur * w_ref[width - 1:width, :]
    for k in range(1, width):
        out = out + _shift_rows(cur, prev, k) * w_ref[width - 1 - k:width - k, :]
    return out


def _head_expand():
    r = lax.broadcasted_iota(jnp.int32, (LANES, SSD_D_INNER), 0)
    c = lax.broadcasted_iota(jnp.int32, (LANES, SSD_D_INNER), 1)
    return (c // SSD_HEAD_DIM == r).astype(F32)


def _head_expand_t():
    r = lax.broadcasted_iota(jnp.int32, (SSD_D_INNER, LANES), 0)
    c = lax.broadcasted_iota(jnp.int32, (SSD_D_INNER, LANES), 1)
    return (r // SSD_HEAD_DIM == c).astype(F32)


def _ssd_tile(act, dt_raw, dtb, alog, seq_rows):
    rows = MIX_ROWS
    xs = act[:, :SSD_D_INNER]
    bm = act[:, SSD_D_INNER:SSD_D_INNER + SSD_GROUPS * SSD_STATE]
    cm = act[:, SSD_D_INNER + SSD_GROUPS * SSD_STATE:]
    v = dt_raw + dtb
    dt = jnp.maximum(v, 0.0) + jnp.log1p(jnp.exp(-jnp.abs(v)))
    a = -jnp.exp(alog)
    dta = dt * a
    ri = lax.broadcasted_iota(jnp.int32, (rows, rows), 0)
    ci = lax.broadcasted_iota(jnp.int32, (rows, rows), 1)
    same = (ri // seq_rows) == (ci // seq_rows)
    causal = same & (ci <= ri)
    cs = _dot_exact(causal.astype(F32), dta)
    expand = _head_expand().astype(BF16)

    def per_head_lanes(v):
        hi = v.astype(BF16)
        lo = (v - hi.astype(F32)).astype(BF16)
        return _dot(hi, expand) + _dot(lo, expand)

    dt_e = per_head_lanes(dt)
    cs_e = per_head_lanes(cs)
    if seq_rows == rows:
        tot = None
        tot_e = cs_e[rows - 1:rows, :]
    else:
        tot = _dot_exact(same.astype(F32), dta)
        tot_e = per_head_lanes(tot)
    cs_t = cs.T
    dtx = xs * dt_e
    xw = dtx * jnp.exp(tot_e - cs_e)
    b_groups, c_groups, y_parts = [], [], []
    for g in range(SSD_GROUPS):
        bg = bm[:, g * SSD_STATE:(g + 1) * SSD_STATE].astype(BF16)
        cg = cm[:, g * SSD_STATE:(g + 1) * SSD_STATE].astype(BF16)
        b_groups.append(bg)
        c_groups.append(cg)
        cb = _dot_nt(cg, bg)
        for hh in range(SSD_HEADS_PER_GROUP):
            h = g * SSD_HEADS_PER_GROUP + hh
            seg = cs[:, h:h + 1] - cs_t[h:h + 1, :]
            decay = jnp.where(causal, jnp.exp(jnp.where(causal, seg, 0.0)), 0.0)
            y_parts.append(_dot((cb * decay).astype(BF16),
                                dtx[:, h * SSD_HEAD_DIM:(h + 1) * SSD_HEAD_DIM].astype(BF16)))
    y_diag = jnp.concatenate(y_parts, axis=-1)
    return xs, y_diag, jnp.exp(cs_e), xw, tot, tot_e, bm, b_groups, c_groups


def _gated_group_norm(y, z, nw):
    y = y * _silu(z)
    half = SSD_D_INNER // SSD_GROUPS
    parts = []
    for g in range(SSD_GROUPS):
        yg = y[:, g * half:(g + 1) * half]
        parts.append(yg * lax.rsqrt(jnp.mean(yg * yg, axis=-1, keepdims=True) + RMS_EPS))
    return jnp.concatenate(parts, axis=-1) * nw


def _attn_prompt_kernel(sink_ref, q_ref, kvc_ref, kvp_ref, cosc_ref, sinc_ref, cosp_ref, sinp_ref,
                        ya_ref, krot_ref):
    i = pl.program_id(1)
    nb = pl.num_programs(1)
    w = WINDOW
    q = _rope(q_ref[...].astype(F32), cosc_ref[...], sinc_ref[...])
    kvc = kvc_ref[...].astype(F32)
    kvp = kvp_ref[...].astype(F32)
    kc = _rope(kvc[:, :KV_WIDTH], cosc_ref[...], sinc_ref[...])
    kp = _rope(kvp[:, :KV_WIDTH], cosp_ref[...], sinp_ref[...])
    vc = kvc[:, KV_WIDTH:]
    vp = kvp[:, KV_WIDTH:]

    @pl.when(i == nb - 1)
    def _():
        krot_ref[0] = kc

    rows = Q_PER_KV * w
    r = lax.broadcasted_iota(jnp.int32, (rows, 2 * w), 0)
    s = lax.broadcasted_iota(jnp.int32, (rows, 2 * w), 1)
    diff = w + (r % w) - s
    first_key = jnp.where(i > 0, 0, w)
    valid = (diff >= 0) & (diff < w) & (s >= first_key)
    rcol = lax.broadcasted_iota(jnp.int32, (rows, 1), 0)
    outs = []
    for kh in range(N_KV_HEADS):
        hs = slice(kh * HEAD_DIM, (kh + 1) * HEAD_DIM)
        k2 = jnp.concatenate([kp[:, hs], kc[:, hs]], axis=0).astype(BF16)
        v2 = jnp.concatenate([vp[:, hs], vc[:, hs]], axis=0).astype(BF16)
        qg = jnp.concatenate(
            [q[:, (kh * Q_PER_KV + g) * HEAD_DIM:(kh * Q_PER_KV + g + 1) * HEAD_DIM] for g in range(Q_PER_KV)],
            axis=0).astype(BF16)
        sink = jnp.zeros((rows, 1), F32)
        for g in range(Q_PER_KV):
            sink = jnp.where(rcol // w == g, sink_ref[kh * Q_PER_KV + g], sink)
        logits = jnp.where(valid, _dot_nt(qg, k2) * ATTN_SCALE, -jnp.inf)
        m = jnp.maximum(jnp.max(logits, axis=-1, keepdims=True), sink)
        e = jnp.exp(logits - m)
        den = jnp.sum(e, axis=-1, keepdims=True) + jnp.exp(sink - m)
        o = _dot(e.astype(BF16), v2) / den
        for g in range(Q_PER_KV):
            outs.append(o[g * w:(g + 1) * w, :])
    ya_ref[...] = jnp.concatenate(outs, axis=-1).astype(BF16)


def _attention_prompt(p, sink, cos_p, sin_p, n_seq, seq_len):
    nb = seq_len // WINDOW
    n_prompt = n_seq * seq_len
    w = WINDOW

    def cur(b, i):
        return b * nb + i

    def prev(b, i):
        return jnp.maximum(b * nb + i - 1, 0)

    return pl.pallas_call(
        _attn_prompt_kernel,
        grid=(n_seq, nb),
        in_specs=[
            pl.BlockSpec(memory_space=pltpu.SMEM),
            pl.BlockSpec((w, ATTN_WIDTH), lambda b, i: (cur(b, i), COL_Q // ATTN_WIDTH)),
            pl.BlockSpec((w, 2 * KV_WIDTH), lambda b, i: (cur(b, i), COL_KV // (2 * KV_WIDTH))),
            pl.BlockSpec((w, 2 * KV_WIDTH), lambda b, i: (prev(b, i), COL_KV // (2 * KV_WIDTH))),
            pl.BlockSpec((w, LANES), lambda b, i: (i, 0)),
            pl.BlockSpec((w, LANES), lambda b, i: (i, 0)),
            pl.BlockSpec((w, LANES), lambda b, i: (jnp.maximum(i - 1, 0), 0)),
            pl.BlockSpec((w, LANES), lambda b, i: (jnp.maximum(i - 1, 0), 0)),
        ],
        out_specs=[
            pl.BlockSpec((w, ATTN_WIDTH), lambda b, i: (cur(b, i), 0)),
            pl.BlockSpec((1, w, KV_WIDTH), lambda b, i: (b, 0, 0)),
        ],
        out_shape=[
            jax.ShapeDtypeStruct((n_prompt, ATTN_WIDTH), BF16),
            jax.ShapeDtypeStruct((n_seq, w, KV_WIDTH), F32),
        ],
        compiler_params=_params(("parallel", "arbitrary")),
    )(sink, p, p, p, cos_p, sin_p, cos_p, sin_p)


def _ssd_prompt_kernel(z_ref, xc_ref, xp_ref, dt_ref, scb_ref, sccc_ref, sccp_ref, scvc_ref, scvp_ref,
                       cw_ref, cb_ref, dtb_ref, alog_ref, de_ref, nw_ref, scw_ref,
                       yb_ref, yc_ref, hout_ref, cvlast_ref, h_scr):
    i = pl.program_id(1)
    nc = pl.num_programs(1)
    first = i == 0

    @pl.when(first)
    def _():
        h_scr[...] = jnp.zeros_like(h_scr)

    xc = xc_ref[...].astype(F32)
    xp = jnp.where(first, 0.0, xp_ref[...].astype(F32))
    act = _silu(_causal_conv(xc, xp, cw_ref, SSD_CONV) + cb_ref[...])
    xs, y_diag, ecs_e, xw, _, tot_e, bm, _, c_groups = _ssd_tile(
        act, dt_ref[...], dtb_ref[...], alog_ref[...], MIX_ROWS)
    gw = SSD_HEADS_PER_GROUP * SSD_HEAD_DIM
    y_off = []
    for g in range(SSD_GROUPS):
        cols = slice(g * gw, (g + 1) * gw)
        hg = h_scr[:, cols]
        y_off.append(_dot(c_groups[g], hg.astype(BF16)))
        b_t = bm[:, g * SSD_STATE:(g + 1) * SSD_STATE].T.astype(BF16)
        h_scr[:, cols] = jnp.exp(tot_e[:, cols]) * hg + _dot(b_t, xw[:, cols].astype(BF16))
    y = y_diag + jnp.concatenate(y_off, axis=-1) * ecs_e + de_ref[...] * xs
    yb_ref[...] = _gated_group_norm(y, z_ref[...].astype(F32), nw_ref[...]).astype(BF16)

    cvc = sccc_ref[...].astype(F32) * scvc_ref[...].astype(F32)
    cvp = jnp.where(first, 0.0, sccp_ref[...].astype(F32) * scvp_ref[...].astype(F32))
    conv_c = _causal_conv(cvc, cvp, scw_ref, SC_WIDTH)
    yc_ref[...] = (scb_ref[...].astype(F32) * conv_c).astype(BF16)

    @pl.when(i == nc - 1)
    def _():
        hout_ref[0] = h_scr[...].T
        cvlast_ref[0] = cvc[MIX_ROWS - SUBLANES:, :]


def _ssd_prompt(p, dt, cw, cb, dtb, alog, de, nw, scw, n_seq, seq_len):
    nc = seq_len // MIX_ROWS
    r = MIX_ROWS
    n_prompt = n_seq * seq_len

    def cur(b, i):
        return b * nc + i

    def prev(b, i):
        return jnp.maximum(b * nc + i - 1, 0)

    def col(width, offset, which):
        return pl.BlockSpec((r, width), lambda b, i: (which(b, i), offset // width))

    def const(shape):
        return pl.BlockSpec(shape, lambda b, i: (0,) * len(shape))

    return pl.pallas_call(
        _ssd_prompt_kernel,
        grid=(n_seq, nc),
        in_specs=[
            col(SSD_D_INNER, COL_Z, cur),
            col(SSD_CONV_DIM, COL_XBC, cur), col(SSD_CONV_DIM, COL_XBC, prev),
            pl.BlockSpec((r, DT_PAD), lambda b, i: (cur(b, i), 0)),
            col(SC_DIM, COL_SCB, cur),
            col(SC_DIM, COL_SCC, cur), col(SC_DIM, COL_SCC, prev),
            col(SC_DIM, COL_SCV, cur), col(SC_DIM, COL_SCV, prev),
            const((SSD_CONV, SSD_CONV_DIM)), const((1, SSD_CONV_DIM)),
            const((1, DT_PAD)), const((1, DT_PAD)), const((1, SSD_D_INNER)), const((1, SSD_D_INNER)),
            const((SC_WIDTH, SC_DIM)),
        ],
        out_specs=[
            pl.BlockSpec((r, SSD_D_INNER), lambda b, i: (cur(b, i), 0)),
            pl.BlockSpec((r, SC_DIM), lambda b, i: (cur(b, i), 0)),
            pl.BlockSpec((1, SSD_D_INNER, SSD_STATE), lambda b, i: (b, 0, 0)),
            pl.BlockSpec((1, SUBLANES, SC_DIM), lambda b, i: (b, 0, 0)),
        ],
        out_shape=[
            jax.ShapeDtypeStruct((n_prompt, SSD_D_INNER), BF16),
            jax.ShapeDtypeStruct((n_prompt, SC_DIM), BF16),
            jax.ShapeDtypeStruct((n_seq, SSD_D_INNER, SSD_STATE), F32),
            jax.ShapeDtypeStruct((n_seq, SUBLANES, SC_DIM), F32),
        ],
        scratch_shapes=[pltpu.VMEM((SSD_STATE, SSD_D_INNER), F32)],
        compiler_params=_params(("parallel", "arbitrary")),
    )(p, p, p, dt, p, p, p, p, p, cw, cb, dtb, alog, de, nw, scw)


def _mix_sample_kernel(sink_ref, q_ref, kv_ref, z_ref, x_ref, dt_ref, scb_ref, scc_ref, scv_ref,
                       cos_ref, sin_ref, ck_ref, cv_ref, h0_ref, xbuf_ref, cbuf_ref,
                       cw_ref, cb_ref, dtb_ref, alog_ref, de_ref, nw_ref, scw_ref,
                       ya_ref, yb_ref, yc_ref, knew_ref, vnew_ref, hnew_ref, cvout_ref):
    ns, t = SEQS_PER_STEP, DEC_SEQ
    wb = ck_ref.shape[1]

    q = _rope(q_ref[...].astype(F32), cos_ref[...], sin_ref[...])
    kv = kv_ref[...].astype(F32)
    kn = _rope(kv[:, :KV_WIDTH], cos_ref[...], sin_ref[...])
    vn = kv[:, KV_WIDTH:]
    q3 = q.reshape(ns, t, ATTN_WIDTH)
    kn3 = kn.reshape(ns, t, KV_WIDTH)
    vn3 = vn.reshape(ns, t, KV_WIDTH)
    knew_ref[:, :wb - t, :] = ck_ref[:, t:, :]
    knew_ref[:, wb - t:, :] = kn3
    vnew_ref[:, :wb - t, :] = cv_ref[:, t:, :]
    vnew_ref[:, wb - t:, :] = vn3
    nq = Q_PER_KV * t
    qi = lax.broadcasted_iota(jnp.int32, (ns, nq, wb + t), 1) % t
    si = lax.broadcasted_iota(jnp.int32, (ns, nq, wb + t), 2)
    valid = ((si < wb) & (si > qi + (wb - WINDOW))) | ((si >= wb) & (si - wb <= qi))
    hrow = lax.broadcasted_iota(jnp.int32, (ns, nq, 1), 1) // t
    heads = [None] * N_HEADS
    for kh in range(N_KV_HEADS):
        hs = slice(kh * HEAD_DIM, (kh + 1) * HEAD_DIM)
        k_all = jnp.concatenate([ck_ref[:, :, hs], kn3[:, :, hs]], axis=1).astype(BF16)
        v_all = jnp.concatenate([cv_ref[:, :, hs], vn3[:, :, hs]], axis=1).astype(BF16)
        qg = jnp.concatenate(
            [q3[:, :, (kh * Q_PER_KV + g) * HEAD_DIM:(kh * Q_PER_KV + g + 1) * HEAD_DIM] for g in range(Q_PER_KV)],
            axis=1).astype(BF16)
        sink = jnp.zeros((ns, nq, 1), F32)
        for g in range(Q_PER_KV):
            sink = jnp.where(hrow == g, sink_ref[kh * Q_PER_KV + g], sink)
        logits = jnp.einsum('bqd,bsd->bqs', qg, k_all, preferred_element_type=F32) * ATTN_SCALE
        logits = jnp.where(valid, logits, -jnp.inf)
        m = jnp.maximum(jnp.max(logits, axis=-1, keepdims=True), sink)
        e = jnp.exp(logits - m)
        den = jnp.sum(e, axis=-1, keepdims=True) + jnp.exp(sink - m)
        o = jnp.einsum('bqs,bsd->bqd', e.astype(BF16), v_all, preferred_element_type=F32) / den
        for g in range(Q_PER_KV):
            heads[kh * Q_PER_KV + g] = o[:, g * t:(g + 1) * t, :]
    ya_ref[...] = jnp.concatenate(heads, axis=-1).reshape(ns * t, ATTN_WIDTH).astype(BF16)

    xc3 = x_ref[...].astype(F32).reshape(ns, t, SSD_CONV_DIM)
    xp3 = xbuf_ref[...].reshape(ns, t, SSD_CONV_DIM)
    conv = _causal_conv(xc3, xp3, cw_ref, SSD_CONV).reshape(ns * t, SSD_CONV_DIM)
    act = _silu(conv + cb_ref[...])
    xs, y_diag, ecs_e, xw, tot, _, _, b_groups, c_groups = _ssd_tile(
        act, dt_ref[...], dtb_ref[...], alog_ref[...], t)
    xw_t = xw.T
    dec_t = jnp.exp(_dot_exact(_head_expand_t(), tot.T))
    gw = SSD_HEADS_PER_GROUP * SSD_HEAD_DIM
    y_off = []
    for g in range(SSD_GROUPS):
        c3 = c_groups[g].reshape(ns, t, SSD_STATE)
        hg = h0_ref[:, g * gw:(g + 1) * gw, :].astype(BF16)
        y_off.append(jnp.einsum('btn,bqn->btq', c3, hg, preferred_element_type=F32).reshape(ns * t, gw))
    y = y_diag + jnp.concatenate(y_off, axis=-1) * ecs_e + de_ref[...] * xs
    yb_ref[...] = _gated_group_norm(y, z_ref[...].astype(F32), nw_ref[...]).astype(BF16)
    col = lax.broadcasted_iota(jnp.int32, xw_t.shape, 1) // t
    for b in range(ns):
        xw_b = jnp.where(col == b, xw_t, jnp.zeros_like(xw_t))
        upd = jnp.concatenate(
            [_dot(xw_b[g * gw:(g + 1) * gw, :].astype(BF16), b_groups[g]) for g in range(SSD_GROUPS)], axis=0)
        hnew_ref[b] = dec_t[:, b * t:b * t + 1] * h0_ref[b] + upd

    cvc = scc_ref[...].astype(F32) * scv_ref[...].astype(F32)
    cvout_ref[...] = cvc
    conv_c = _causal_conv(cvc.reshape(ns, t, SC_DIM), cbuf_ref[...].reshape(ns, t, SC_DIM), scw_ref, SC_WIDTH)
    yc_ref[...] = (scb_ref[...].astype(F32) * conv_c.reshape(ns * t, SC_DIM)).astype(BF16)


def _mix_sample(p, dt, sink, cos_s, sin_s, cache_k, cache_v, h0, xbuf, cbuf,
                cw, cb, dtb, alog, de, nw, scw, layer, n_prompt, n_dec):
    r = MIX_ROWS
    ns = SEQS_PER_STEP
    steps = n_dec // ns
    base = n_prompt // r
    wb = cache_k.shape[2]

    def col(width, offset):
        return pl.BlockSpec((r, width), lambda i: (base + i, offset // width))

    def state(shape):
        return pl.BlockSpec((None, *shape), lambda i: (layer, i) + (0,) * (len(shape) - 1))

    def const(shape):
        return pl.BlockSpec(shape, lambda i: (0,) * len(shape))

    def rows(width):
        return pl.BlockSpec((r, width), lambda i: (i, 0))

    n_rows = n_dec * DEC_SEQ
    return pl.pallas_call(
        _mix_sample_kernel,
        grid=(steps,),
        in_specs=[
            pl.BlockSpec(memory_space=pltpu.SMEM),
            col(ATTN_WIDTH, COL_Q), col(2 * KV_WIDTH, COL_KV), col(SSD_D_INNER, COL_Z),
            col(SSD_CONV_DIM, COL_XBC),
            pl.BlockSpec((r, DT_PAD), lambda i: (base + i, 0)),
            col(SC_DIM, COL_SCB), col(SC_DIM, COL_SCC), col(SC_DIM, COL_SCV),
            const((r, LANES)), const((r, LANES)),
            state((ns, wb, KV_WIDTH)), state((ns, wb, KV_WIDTH)),
            state((ns, SSD_D_INNER, SSD_STATE)),
            state((r, SSD_CONV_DIM)), state((r, SC_DIM)),
            const((SSD_CONV, SSD_CONV_DIM)), const((1, SSD_CONV_DIM)),
            const((1, DT_PAD)), const((1, DT_PAD)), const((1, SSD_D_INNER)), const((1, SSD_D_INNER)),
            const((SC_WIDTH, SC_DIM)),
        ],
        out_specs=[
            rows(ATTN_WIDTH), rows(SSD_D_INNER), rows(SC_DIM),
            pl.BlockSpec((ns, wb, KV_WIDTH), lambda i: (i, 0, 0)),
            pl.BlockSpec((ns, wb, KV_WIDTH), lambda i: (i, 0, 0)),
            pl.BlockSpec((ns, SSD_D_INNER, SSD_STATE), lambda i: (i, 0, 0)),
            rows(SC_DIM),
        ],
        out_shape=[
            jax.ShapeDtypeStruct((n_rows, ATTN_WIDTH), BF16),
            jax.ShapeDtypeStruct((n_rows, SSD_D_INNER), BF16),
            jax.ShapeDtypeStruct((n_rows, SC_DIM), BF16),
            jax.ShapeDtypeStruct((n_dec, wb, KV_WIDTH), F32),
            jax.ShapeDtypeStruct((n_dec, wb, KV_WIDTH), F32),
            jax.ShapeDtypeStruct((n_dec, SSD_D_INNER, SSD_STATE), F32),
            jax.ShapeDtypeStruct((n_rows, SC_DIM), F32),
        ],
        compiler_params=_params(("parallel",)),
    )(sink, p, p, p, p, dt, p, p, p, cos_s, sin_s, cache_k, cache_v, h0, xbuf, cbuf,
      cw, cb, dtb, alog, de, nw, scw)


def _outproj_kernel(n_prompt_tiles, alpha,
                    x_ref, yap_ref, yas_ref, ybp_ref, ybs_ref, ycp_ref, ycs_ref, g_ref,
                    g1p_ref, g1s_ref, sh2p_ref, sh2s_ref, sc2p_ref, sc2s_ref,
                    wpa_ref, wpb_ref, wpc_ref, wout_ref, wr_ref, br_ref, lng_ref, lnb_ref,
                    x1_ref, h_ref, route_ref, cnt_ref, cnt_scr):
    i = pl.program_id(0)
    is_sample = i >= n_prompt_tiles
    tm = x_ref.shape[0]

    @pl.when(i == 0)
    def _():
        cnt_scr[...] = jnp.zeros_like(cnt_scr)

    n = tm
    cnt = cnt_scr[...]
    for r0 in range(0, tm, n):
        rs = slice(r0, r0 + n)

        def pick(prompt_ref, sample_ref):
            return jnp.where(is_sample, sample_ref[0, rs, :], prompt_ref[0])

        ya = jnp.where(is_sample, yas_ref[rs, :], yap_ref[rs, :])
        yb = jnp.where(is_sample, ybs_ref[rs, :], ybp_ref[rs, :])
        yc = jnp.where(is_sample, ycs_ref[rs, :], ycp_ref[rs, :])
        gates = 0.5 * jnp.tanh(0.5 * g_ref[rs, :]) + 0.5
        merged = (gates[:, :D_MODEL] * _dot(ya, wpa_ref[...]).astype(BF16)
                  + gates[:, D_MODEL:2 * D_MODEL] * _dot(yb, wpb_ref[...]).astype(BF16)
                  + gates[:, 2 * D_MODEL:] * _dot(yc, wpc_ref[...]).astype(BF16))
        mix = _dot(merged, wout_ref[...])
        g1 = pick(g1p_ref, g1s_ref)
        x1 = _layer_norm(alpha * x_ref[rs, :] + g1 * mix, lng_ref[...], lnb_ref[...])
        x1_ref[rs, :] = x1
        sh2 = pick(sh2p_ref, sh2s_ref)
        sc2 = pick(sc2p_ref, sc2s_ref)
        h = x1 * (1.0 + sc2) + sh2
        h_ref[rs, :] = h

        lt = (_dot(h.astype(BF16), wr_ref[...]) + br_ref[...]).T[:ROUTE_ROWS]
        row = lax.broadcasted_iota(jnp.int32, lt.shape, 0).astype(F32)
        neg = -jnp.inf
        big = float(ROUTE_ROWS)
        gl = jnp.where(row < MOE_GROUPS, lt, neg)
        gmax = jnp.max(gl, axis=0, keepdims=True)
        g_p = 1.0 / jnp.sum(jnp.exp(gl - gmax), axis=0, keepdims=True)
        gidx = jnp.min(jnp.where(gl == gmax, row, big), axis=0, keepdims=True)
        lo = ROUTE_EXPERT_ROW + EXPERTS_PER_GROUP * gidx
        sel = jnp.where((row >= lo) & (row < lo + EXPERTS_PER_GROUP), lt, neg)
        m1 = jnp.max(sel, axis=0, keepdims=True)
        i1 = jnp.min(jnp.where(sel == m1, row, big), axis=0, keepdims=True)
        sel2 = jnp.where(row == i1, neg, sel)
        m2 = jnp.max(sel2, axis=0, keepdims=True)
        i2 = jnp.min(jnp.where(sel2 == m2, row, big), axis=0, keepdims=True)
        ssum = jnp.sum(jnp.exp(sel - m1), axis=0, keepdims=True)
        p1 = 1.0 / ssum
        p2 = jnp.exp(m2 - m1) / ssum
        w1 = g_p * (p1 / (p1 + p2))
        w2 = g_p * (p2 / (p1 + p2))

        onehot = jnp.where((row == i1) | (row == i2), 1.0, 0.0)
        ri = lax.broadcasted_iota(jnp.int32, (n, n), 0)
        ci = lax.broadcasted_iota(jnp.int32, (n, n), 1)
        earlier = jnp.where(ri < ci, 1.0, 0.0).astype(BF16)
        prefix = _dot(onehot.astype(BF16), earlier) + cnt[:, 0:1]
        rank1 = jnp.sum(jnp.where(row == i1, prefix, 0.0), axis=0, keepdims=True)
        rank2 = jnp.sum(jnp.where(row == i2, prefix, 0.0), axis=0, keepdims=True)
        cnt = cnt + jnp.sum(onehot, axis=1, keepdims=True)
        zero = jnp.zeros_like(w1)
        route_ref[:, rs] = jnp.concatenate(
            [i1 - ROUTE_EXPERT_ROW, i2 - ROUTE_EXPERT_ROW, w1, w2, rank1, rank2, zero, zero], axis=0)

    cnt_scr[...] = cnt
    cnt_ref[...] = cnt


def _output_projection(x, y_prompt, y_sample, p, mod_p, mod_s, wpa, wpb, wpc, wout, wr, br, lng, lnb,
                       layer, n_prompt, seq_len, alpha):
    t_all = x.shape[0]
    tm = ROW_TILE
    npt = n_prompt // tm
    n_seq = mod_p.shape[1]
    mods = []
    for col in (2, 3, 4):
        mods.extend(_mod_specs(layer, col, tm, npt, seq_len // tm, n_seq))

    def rows(width):
        return pl.BlockSpec((tm, width), lambda i: (i, 0))

    def prompt_rows(width):
        return pl.BlockSpec((tm, width), lambda i: (jnp.minimum(i, npt - 1), 0))

    def sample_rows(width):
        return pl.BlockSpec((tm, width), lambda i: (jnp.maximum(i - npt, 0), 0))

    def const(shape):
        return pl.BlockSpec(shape, lambda i: (0,) * len(shape))

    return pl.pallas_call(
        functools.partial(_outproj_kernel, npt, alpha),
        grid=(t_all // tm,),
        in_specs=[
            rows(D_MODEL),
            prompt_rows(ATTN_WIDTH), sample_rows(ATTN_WIDTH),
            prompt_rows(SSD_D_INNER), sample_rows(SSD_D_INNER),
            prompt_rows(SC_DIM), sample_rows(SC_DIM),
            pl.BlockSpec((tm, 3 * D_MODEL), lambda i: (i, COL_GATES // (3 * D_MODEL))),
            *mods,
            _layer_spec(layer, (ATTN_WIDTH, D_MODEL)), _layer_spec(layer, (SSD_D_INNER, D_MODEL)),
            _layer_spec(layer, (SC_DIM, D_MODEL)), _layer_spec(layer, (D_MODEL, D_MODEL)),
            _layer_spec(layer, (D_MODEL, LANES)), _layer_spec(layer, (1, LANES)),
            const((1, D_MODEL)), const((1, D_MODEL)),
        ],
        out_specs=[
            rows(D_MODEL),
            rows(D_MODEL),
            pl.BlockSpec((SUBLANES, tm), lambda i: (0, i)),
            const((ROUTE_ROWS, LANES)),
        ],
        out_shape=[
            jax.ShapeDtypeStruct((t_all, D_MODEL), F32),
            jax.ShapeDtypeStruct((t_all, D_MODEL), F32),
            jax.ShapeDtypeStruct((SUBLANES, t_all), F32),
            jax.ShapeDtypeStruct((ROUTE_ROWS, LANES), F32),
        ],
        scratch_shapes=[pltpu.VMEM((ROUTE_ROWS, LANES), F32)],
        compiler_params=_params(("arbitrary",)),
    )(x, y_prompt[0], y_sample[0], y_prompt[1], y_sample[1], y_prompt[2], y_sample[2],
      p, mod_p, mod_s, mod_p, mod_s, mod_p, mod_s,
      wpa, wpb, wpc, wout, wr, br, lng, lnb)


def _slots_kernel(route_ref, start_ref, dest_ref):
    route = route_ref[...]
    tm = route.shape[1]
    expert = lax.broadcasted_iota(jnp.int32, (N_EXPERTS, tm), 0).astype(F32)
    start = start_ref[:, 0:1]
    rows = []
    for e_row, r_row in ((0, 4), (1, 5)):
        first = jnp.sum(jnp.where(expert == route[e_row:e_row + 1], start, 0.0), axis=0, keepdims=True)
        rows.append(first + route[r_row:r_row + 1])
    rows.append(jnp.zeros((SUBLANES - 2, tm), F32))
    dest_ref[...] = jnp.concatenate(rows, axis=0).astype(jnp.int32)


def _slots(route, pad_start):
    t_all = route.shape[1]
    tm = ROW_TILE_IN
    dest = pl.pallas_call(
        _slots_kernel,
        grid=(t_all // tm,),
        in_specs=[pl.BlockSpec((SUBLANES, tm), lambda i: (0, i)),
                  pl.BlockSpec((N_EXPERTS, LANES), lambda i: (0, 0))],
        out_specs=pl.BlockSpec((SUBLANES, tm), lambda i: (0, i)),
        out_shape=jax.ShapeDtypeStruct((SUBLANES, t_all), jnp.int32),
        compiler_params=_params(("parallel",)),
    )(route, jnp.broadcast_to(pad_start.astype(F32)[:, None], (N_EXPERTS, LANES)))
    return dest[0], dest[1]


def _token_copy(src, src_row, dst, dst_row, sem):
    return pltpu.make_async_copy(src.at[pl.ds(src_row, 1), :], dst.at[pl.ds(dst_row, 1), :], sem)


def _dispatch_kernel(d1_ref, d2_ref, pend_ref, cnt_ref, nb_ref, h_ref, xb_ref, zero_scr, zsem, sem):
    i = pl.program_id(0)
    tm = h_ref.shape[0]
    blk_rows = EXPERT_BLOCK
    n_blocks = xb_ref.shape[0] // blk_rows

    def zero_block(b):
        start = pl.multiple_of(b * blk_rows, blk_rows)
        return pltpu.make_async_copy(zero_scr, xb_ref.at[pl.ds(start, blk_rows)], zsem)

    def last_block(e):
        return pend_ref[e] // EXPERT_BLOCK - 1

    @pl.when(i == 0)
    def _():
        zero_scr[...] = jnp.zeros_like(zero_scr)
        for e in range(N_EXPERTS):
            @pl.when(cnt_ref[e] > 0)
            def _():
                zero_block(last_block(e)).start()
        lax.fori_loop(nb_ref[0], n_blocks, lambda b, c: (zero_block(b).start(), c)[1], 0)
        for e in range(N_EXPERTS):
            @pl.when(cnt_ref[e] > 0)
            def _():
                zero_block(last_block(e)).wait()
        lax.fori_loop(nb_ref[0], n_blocks, lambda b, c: (zero_block(b).wait(), c)[1], 0)

    def issue(c, carry):
        for u in range(ISSUE_UNROLL):
            t = pl.multiple_of(c * ISSUE_UNROLL, ISSUE_UNROLL) + u
            g = i * tm + t
            _token_copy(h_ref, t, xb_ref, d1_ref[g], sem.at[0]).start(priority=0)
            _token_copy(h_ref, t, xb_ref, d2_ref[g], sem.at[1]).start(priority=1)
        return carry

    lax.fori_loop(0, tm // ISSUE_UNROLL, issue, 0)
    for k in range(2):
        pltpu.make_async_copy(h_ref, xb_ref.at[pl.ds(0, tm)], sem.at[k]).wait()


def _dispatch(h, dest1, dest2, pad_end, counts, n_used, n_slots):
    t_all = h.shape[0]
    tm = ROW_TILE
    grid_spec = pltpu.PrefetchScalarGridSpec(
        num_scalar_prefetch=5,
        grid=(t_all // tm,),
        in_specs=[pl.BlockSpec((tm, D_MODEL), lambda i, *_: (i, 0))],
        out_specs=pl.BlockSpec(memory_space=pl.ANY),
        scratch_shapes=[
            pltpu.VMEM((EXPERT_BLOCK, D_MODEL), F32),
            pltpu.SemaphoreType.DMA(()),
            pltpu.SemaphoreType.DMA((2,)),
        ],
    )
    return pl.pallas_call(
        _dispatch_kernel,
        grid_spec=grid_spec,
        out_shape=jax.ShapeDtypeStruct((n_slots, D_MODEL), F32),
        compiler_params=_params(("arbitrary",)),
    )(dest1, dest2, pad_end, counts, n_used, h)


def _ffn_kernel(layer, first_ref, nblk_ref, nused_ref, xb_ref, wg_ref, wu_ref, wd_ref, yb_ref,
                x_in, y_out, wg_f, wu_f, wd_f, wg_b, wu_b, wd_b, xsem, ysem, wsem):
    blk_rows = EXPERT_BLOCK
    n_blocks = yb_ref.shape[0] // blk_rows
    n_used = nused_ref[0]

    def block_rows(ref, b):
        return ref.at[pl.ds(pl.multiple_of(b * blk_rows, blk_rows), blk_rows)]

    def x_copy(b, s):
        return pltpu.make_async_copy(block_rows(xb_ref, b), x_in.at[s], xsem.at[s])

    def y_copy(b, s):
        return pltpu.make_async_copy(y_out.at[s], block_rows(yb_ref, b), ysem.at[s])

    def w_copies(e, s):
        return (pltpu.make_async_copy(wg_ref.at[layer, e], wg_f.at[s], wsem.at[s, 0]),
                pltpu.make_async_copy(wu_ref.at[layer, e], wu_f.at[s], wsem.at[s, 1]),
                pltpu.make_async_copy(wd_ref.at[layer, e], wd_f.at[s], wsem.at[s, 2]))

    for c in w_copies(0, 0):
        c.start()
    x_copy(0, 0).start()

    def expert(e, carry):
        ws = e % 2
        for c in w_copies(e, ws):
            c.wait()

        @pl.when(e + 1 < N_EXPERTS)
        def _():
            for c in w_copies(e + 1, 1 - ws):
                c.start()

        @pl.when(nblk_ref[e] > 0)
        def _():
            wg_b[...] = wg_f[ws].astype(BF16)
            wu_b[...] = wu_f[ws].astype(BF16)
            wd_b[...] = wd_f[ws].astype(BF16)

        def block(k, inner):
            b = first_ref[e] + k
            s = b % 2
            x_copy(b, s).wait()

            @pl.when(b + 1 < n_used)
            def _():
                x_copy(b + 1, 1 - s).start()

            x = x_in[s].astype(BF16)
            act = (_silu(_dot(x, wg_b[...])) * _dot(x, wu_b[...])).astype(BF16)
            y = _dot(act, wd_b[...])

            @pl.when(b >= 2)
            def _():
                y_copy(b - 2, s).wait()

            y_out[s] = y
            y_copy(b, s).start()
            return inner

        lax.fori_loop(0, nblk_ref[e], block, 0)
        return carry

    lax.fori_loop(0, N_EXPERTS, expert, 0)

    @pl.when(n_used >= 2)
    def _():
        y_copy(n_used - 2, n_used % 2).wait()

    y_copy(n_used - 1, (n_used - 1) % 2).wait()

    y_out[0] = jnp.zeros(y_out.shape[1:], F32)
    lax.fori_loop(n_used, n_blocks, lambda b, c: (y_copy(b, 0).start(), c)[1], 0)
    lax.fori_loop(n_used, n_blocks, lambda b, c: (y_copy(b, 0).wait(), c)[1], 0)


def _expert_ffn(xb, first_block, n_expert_blocks, n_used, wg, wu, wd, layer):
    any_spec = pl.BlockSpec(memory_space=pl.ANY)
    grid_spec = pltpu.PrefetchScalarGridSpec(
        num_scalar_prefetch=3,
        grid=(1,),
        in_specs=[any_spec] * 4,
        out_specs=any_spec,
        scratch_shapes=[
            pltpu.VMEM((2, EXPERT_BLOCK, D_MODEL), F32),
            pltpu.VMEM((2, EXPERT_BLOCK, D_MODEL), F32),
            pltpu.VMEM((2, D_MODEL, EXPERT_FF), F32),
            pltpu.VMEM((2, D_MODEL, EXPERT_FF), F32),
            pltpu.VMEM((2, EXPERT_FF, D_MODEL), F32),
            pltpu.VMEM((D_MODEL, EXPERT_FF), BF16),
            pltpu.VMEM((D_MODEL, EXPERT_FF), BF16),
            pltpu.VMEM((EXPERT_FF, D_MODEL), BF16),
            pltpu.SemaphoreType.DMA((2,)),
            pltpu.SemaphoreType.DMA((2,)),
            pltpu.SemaphoreType.DMA((2, 3)),
        ],
    )
    return pl.pallas_call(
        functools.partial(_ffn_kernel, layer),
        grid_spec=grid_spec,
        out_shape=jax.ShapeDtypeStruct(xb.shape, F32),
        compiler_params=_params(("arbitrary",)),
    )(first_block, n_expert_blocks, n_used, xb, wg, wu, wd)


def _combine_kernel(n_prompt_tiles, alpha, d1_ref, d2_ref,
                    x1_ref, route_ref, g2p_ref, g2s_ref, lng_ref, lnb_ref, yb_ref,
                    xo_ref, buf_a, buf_b, sem):
    i = pl.program_id(0)
    n_tiles = pl.num_programs(0)
    tm = x1_ref.shape[0]
    slot = i % 2

    def gather_tile(tile, s):
        def issue(c, carry):
            for u in range(ISSUE_UNROLL):
                t = pl.multiple_of(c * ISSUE_UNROLL, ISSUE_UNROLL) + u
                g = tile * tm + t
                _token_copy(yb_ref, d1_ref[g], buf_a.at[s], t, sem.at[s, 0]).start(priority=0)
                _token_copy(yb_ref, d2_ref[g], buf_b.at[s], t, sem.at[s, 1]).start(priority=1)
            return carry

        lax.fori_loop(0, tm // ISSUE_UNROLL, issue, 0)

    @pl.when(i == 0)
    def _():
        gather_tile(0, 0)

    @pl.when(i + 1 < n_tiles)
    def _():
        gather_tile(i + 1, 1 - slot)

    whole = yb_ref.at[pl.ds(0, tm)]
    pltpu.make_async_copy(whole, buf_a.at[slot], sem.at[slot, 0]).wait()
    pltpu.make_async_copy(whole, buf_b.at[slot], sem.at[slot, 1]).wait()
    route_t = jnp.concatenate(
        [route_ref[...], jnp.zeros((LANES - SUBLANES, tm), F32)], axis=0).T
    ffn = route_t[:, 2:3] * buf_a[slot] + route_t[:, 3:4] * buf_b[slot]
    g2 = _pick_mod(i >= n_prompt_tiles, g2p_ref, g2s_ref)
    xo_ref[...] = _layer_norm(alpha * x1_ref[...] + g2 * ffn, lng_ref[...], lnb_ref[...])


def _combine(x1, route, yb, dest1, dest2, mod_p, mod_s, lng, lnb, layer, n_prompt, seq_len, alpha):
    t_all = x1.shape[0]
    tm = ROW_TILE
    npt = n_prompt // tm
    g2p, g2s = _mod_specs(layer, 5, tm, npt, seq_len // tm, mod_p.shape[1])
    grid_spec = pltpu.PrefetchScalarGridSpec(
        num_scalar_prefetch=2,
        grid=(t_all // tm,),
        in_specs=[
            pl.BlockSpec((tm, D_MODEL), lambda i, *_: (i, 0)),
            pl.BlockSpec((SUBLANES, tm), lambda i, *_: (0, i)),
            g2p, g2s,
            pl.BlockSpec((1, D_MODEL), lambda i, *_: (0, 0)),
            pl.BlockSpec((1, D_MODEL), lambda i, *_: (0, 0)),
            pl.BlockSpec(memory_space=pl.ANY),
        ],
        out_specs=pl.BlockSpec((tm, D_MODEL), lambda i, *_: (i, 0)),
        scratch_shapes=[
            pltpu.VMEM((2, tm, D_MODEL), F32),
            pltpu.VMEM((2, tm, D_MODEL), F32),
            pltpu.SemaphoreType.DMA((2, 2)),
        ],
    )
    return pl.pallas_call(
        functools.partial(_combine_kernel, npt, alpha),
        grid_spec=grid_spec,
        out_shape=jax.ShapeDtypeStruct((t_all, D_MODEL), F32),
        compiler_params=_params(("arbitrary",)),
    )(dest1, dest2, x1, route, mod_p, mod_s, lng, lnb, yb)


def _rope_tables(pos):
    inv = jnp.power(ROPE_THETA, -jnp.arange(ROT_HALF, dtype=F32) * (2.0 / ROT_DIM))
    ang = pos.astype(F32)[:, None] * inv[None, :]
    cos, sin = jnp.cos(ang), jnp.sin(ang)
    rest = HEAD_DIM - ROT_DIM
    n = pos.shape[0]
    cos_h = jnp.concatenate([cos, cos, jnp.ones((n, rest), F32)], axis=-1)
    sin_h = jnp.concatenate([-sin, sin, jnp.zeros((n, rest), F32)], axis=-1)
    reps = LANES // HEAD_DIM
    return jnp.tile(cos_h, (1, reps)), jnp.tile(sin_h, (1, reps))


def _permute_w_in(w_in):
    sizes = (ATTN_WIDTH, KV_WIDTH, KV_WIDTH, SSD_D_INNER, SSD_CONV_DIM, SSD_HEADS,
             SC_DIM, SC_DIM, SC_DIM, 3 * D_MODEL)
    offs = np.concatenate([[0], np.cumsum(sizes)])
    q, k, v, z, xbc, dt, scb, scc, scv, gates = (w_in[..., offs[n]:offs[n + 1]] for n in range(len(sizes)))
    main = jnp.concatenate([q, z, xbc, scb, scc, gates, scv, k, v], axis=-1).astype(BF16)
    dt = jnp.pad(dt, ((0, 0), (0, 0), (0, DT_PAD - SSD_HEADS))).astype(BF16)
    return main, dt


def _pad_lanes(v, width):
    return jnp.pad(v, ((0, 0), (0, width - v.shape[-1])))


def kernel(x_prompt, x_sample, c_prompt, c_sample, cache_attn_k, cache_attn_v, state_ssm, state_ssd_conv, state_short_conv, w_ada, b_ada, w_in, attn_sink, ssd_conv_w, ssd_conv_b, ssd_dt_bias, ssd_a_log, ssd_d, ssd_norm_w, sc_conv_w, w_pa, w_pb, w_pc, w_out, ln1_g, ln1_b, ln2_g, ln2_b, router_g_w, router_g_b, router_e_w, router_e_b, moe_w_gate, moe_w_up, moe_w_down):
    depth = w_in.shape[0]
    n_seq, seq_len, _ = x_prompt.shape
    n_dec, dec_len, _ = x_sample.shape
    wb = cache_attn_k.shape[2]
    past_len = 8192
    assert dec_len == DEC_SEQ and wb == WINDOW
    assert seq_len % ROW_TILE_IN == 0 and (n_dec * dec_len) % ROW_TILE_IN == 0
    n_prompt = n_seq * seq_len
    n_sample = n_dec * dec_len
    t_all = n_prompt + n_sample
    alpha = (2 * depth) ** 0.25

    mod = _modulation(jnp.concatenate([c_prompt, c_sample], axis=0), w_ada, b_ada)
    mod_p_all = mod[:, :n_seq].reshape(depth, n_seq, 1, 6 * D_MODEL)
    mod_s_all = jnp.repeat(mod[:, n_seq:], dec_len, axis=1).reshape(depth, 1, n_sample, 6 * D_MODEL)

    w_main_all, w_dt_all = _permute_w_in(w_in)
    cos_p, sin_p = _rope_tables(jnp.arange(seq_len, dtype=jnp.int32))
    cos_s, sin_s = _rope_tables(past_len + (jnp.arange(MIX_ROWS, dtype=jnp.int32) % dec_len))

    a_total = 2 * t_all
    n_blocks = (a_total + N_EXPERTS * (EXPERT_BLOCK - 1)) // EXPERT_BLOCK
    n_slots = n_blocks * EXPERT_BLOCK

    mod_p, mod_s = mod_p_all, mod_s_all
    cache_k = cache_attn_k.reshape(depth, n_dec, wb, KV_WIDTH)
    cache_v = cache_attn_v.reshape(depth, n_dec, wb, KV_WIDTH)
    h0 = state_ssm.reshape(depth, n_dec, SSD_D_INNER, SSD_STATE)
    xbuf = jnp.pad(state_ssd_conv, ((0, 0), (0, 0), (dec_len - (SSD_CONV - 1), 0), (0, 0))
                   ).reshape(depth, n_sample, SSD_CONV_DIM)
    cbuf = jnp.pad(state_short_conv, ((0, 0), (0, 0), (dec_len - (SC_WIDTH - 1), 0), (0, 0))
                   ).reshape(depth, n_sample, SC_DIM)
    wpa, wpb, wpc, wout = (w.astype(BF16) for w in (w_pa, w_pb, w_pc, w_out))
    gap = ROUTE_EXPERT_ROW - MOE_GROUPS
    tail = LANES - ROUTE_EXPERT_ROW - N_EXPERTS
    wr = jnp.concatenate([router_g_w, jnp.zeros((depth, D_MODEL, gap), F32), router_e_w,
                          jnp.zeros((depth, D_MODEL, tail), F32)], axis=-1).astype(BF16)
    br = jnp.concatenate([router_g_b, jnp.zeros((depth, gap), F32), router_e_b,
                          jnp.zeros((depth, tail), F32)], axis=-1)[:, None, :]

    x = jnp.concatenate([x_prompt.reshape(n_prompt, D_MODEL), x_sample.reshape(n_sample, D_MODEL)], axis=0)
    outs_p = [[] for _ in range(5)]
    outs_s = [[] for _ in range(5)]
    for l in range(depth):
        p, dt = _input_projection(x, mod_p, mod_s, w_main_all, w_dt_all, l, n_prompt, seq_len)

        cw, cb = ssd_conv_w[l], ssd_conv_b[l][None]
        dtb = _pad_lanes(ssd_dt_bias[l][None], DT_PAD)
        alog = _pad_lanes(ssd_a_log[l][None], DT_PAD)
        de = jnp.repeat(ssd_d[l], SSD_HEAD_DIM)[None]
        nw = ssd_norm_w[l][None]
        scw = sc_conv_w[l]

        ya_p, krot_p = _attention_prompt(p, attn_sink[l], cos_p, sin_p, n_seq, seq_len)
        yb_p, yc_p, h_p, cv_p = _ssd_prompt(p, dt, cw, cb, dtb, alog, de, nw, scw, n_seq, seq_len)

        ya_s, yb_s, yc_s, k_s, v_s, h_s, cv_s = _mix_sample(
            p, dt, attn_sink[l], cos_s, sin_s, cache_k, cache_v, h0, xbuf, cbuf,
            cw, cb, dtb, alog, de, nw, scw, l, n_prompt, n_dec)

        x1, h, route, counts = _output_projection(
            x, (ya_p, yb_p, yc_p), (ya_s, yb_s, yc_s), p, mod_p, mod_s,
            wpa, wpb, wpc, wout, wr, br, ln1_g[l][None], ln1_b[l][None], l, n_prompt, seq_len, alpha)

        cnt = counts[ROUTE_EXPERT_ROW:ROUTE_EXPERT_ROW + N_EXPERTS, 0].astype(jnp.int32)
        pad_cnt = (cnt + EXPERT_BLOCK - 1) // EXPERT_BLOCK * EXPERT_BLOCK
        pad_end = jnp.cumsum(pad_cnt)
        pad_start = pad_end - pad_cnt
        n_used = (pad_end[-1:] // EXPERT_BLOCK).astype(jnp.int32)
        dest1, dest2 = _slots(route, pad_start)

        xb = _dispatch(h, dest1, dest2, pad_end.astype(jnp.int32), cnt, n_used, n_slots)
        yb_slots = _expert_ffn(xb, (pad_start // EXPERT_BLOCK).astype(jnp.int32),
                               (pad_cnt // EXPERT_BLOCK).astype(jnp.int32), n_used,
                               moe_w_gate, moe_w_up, moe_w_down, l)
        x = _combine(x1, route, yb_slots, dest1, dest2, mod_p, mod_s, ln2_g[l][None], ln2_b[l][None],
                     l, n_prompt, seq_len, alpha)

        def prompt_tail(rows, c0, c1):
            return jnp.stack([p[(b + 1) * seq_len - rows:(b + 1) * seq_len, c0:c1] for b in range(n_seq)]
                             ).astype(F32)

        outs_p[0].append(krot_p.reshape(n_seq, wb, N_KV_HEADS, HEAD_DIM))
        outs_p[1].append(prompt_tail(wb, COL_KV + KV_WIDTH, COL_KV + 2 * KV_WIDTH)
                         .reshape(n_seq, wb, N_KV_HEADS, HEAD_DIM))
        outs_p[2].append(h_p.reshape(n_seq, SSD_HEADS, SSD_HEAD_DIM, SSD_STATE))
        outs_p[3].append(prompt_tail(SSD_CONV - 1, COL_XBC, COL_XBC + SSD_CONV_DIM))
        outs_p[4].append(cv_p[:, SUBLANES - (SC_WIDTH - 1):, :])
        outs_s[0].append(k_s.reshape(n_dec, wb, N_KV_HEADS, HEAD_DIM))
        outs_s[1].append(v_s.reshape(n_dec, wb, N_KV_HEADS, HEAD_DIM))
        outs_s[2].append(h_s.reshape(n_dec, SSD_HEADS, SSD_HEAD_DIM, SSD_STATE))
        outs_s[3].append(p[n_prompt:, COL_XBC:COL_XBC + SSD_CONV_DIM].astype(F32)
                         .reshape(n_dec, dec_len, SSD_CONV_DIM)[:, dec_len - (SSD_CONV - 1):, :])
        outs_s[4].append(cv_s.reshape(n_dec, dec_len, SC_DIM)[:, dec_len - (SC_WIDTH - 1):, :])

    y_prompt = x[:n_prompt].reshape(n_seq, seq_len, D_MODEL)
    y_sample = x[n_prompt:].reshape(n_dec, dec_len, D_MODEL)
    return (y_prompt, y_sample, *[jnp.stack(o) for o in outs_p], *[jnp.stack(o) for o in outs_s])
```

```python
import functools

import jax
import jax.numpy as jnp
import numpy as np
from jax import lax
from jax.experimental import pallas as pl
from jax.experimental.pallas import tpu as pltpu

F32 = jnp.float32
BF16 = jnp.bfloat16

D_MODEL = 1024
HEAD_DIM = 64
N_HEADS = 8
N_KV_HEADS = 2
Q_PER_KV = N_HEADS // N_KV_HEADS
ATTN_WIDTH = N_HEADS * HEAD_DIM
KV_WIDTH = N_KV_HEADS * HEAD_DIM
WINDOW = 128
ROT_DIM = HEAD_DIM // 4
ROT_HALF = ROT_DIM // 2
ROPE_THETA = 500000.0
ATTN_SCALE = HEAD_DIM ** -0.5
SSD_D_INNER = 512
SSD_HEAD_DIM = 64
SSD_HEADS = 8
SSD_GROUPS = 2
SSD_HEADS_PER_GROUP = SSD_HEADS // SSD_GROUPS
SSD_STATE = 128
SSD_CONV = 4
SSD_CONV_DIM = SSD_D_INNER + 2 * SSD_GROUPS * SSD_STATE
SSD_CHUNK = 128
SC_DIM = 512
SC_WIDTH = 3
MOE_GROUPS = 4
EXPERTS_PER_GROUP = 8
N_EXPERTS = MOE_GROUPS * EXPERTS_PER_GROUP
EXPERT_FF = 512
LN_EPS = 1e-5
RMS_EPS = 1e-5

SUBLANES = 8
LANES = 128
VMEM_LIMIT = 56 * 1024 * 1024

COL_Q = 0
COL_Z = 512
COL_XBC = 1024
COL_SCB = 2048
COL_SCC = 2560
COL_GATES = 3072
COL_SCV = 6144
COL_KV = 6656
P_MAIN = 6912
DT_PAD = 128

ROW_TILE_IN = 1024
COL_TILE_IN = 2304
ROW_TILE = 512
MIX_ROWS = 128
DEC_SEQ = 8
SEQS_PER_STEP = MIX_ROWS // DEC_SEQ
EXPERT_BLOCK = 256
CHUNK_BLOCKS = 4
ROW_CHUNKS = D_MODEL // LANES
ISSUE_UNROLL = 8
ROUTE_EXPERT_ROW = 8
ROUTE_ROWS = 48


def _silu(v):
    return v * jax.nn.sigmoid(v)


def _dot(a, b):
    return jnp.dot(a, b, preferred_element_type=F32)


def _dot_nt(a, b):
    return lax.dot_general(a, b, (((1,), (1,)), ((), ())), preferred_element_type=F32)


def _dot_exact(a, b):
    return jnp.dot(a, b, preferred_element_type=F32, precision=lax.Precision.HIGHEST)


def _params(sem):
    return pltpu.CompilerParams(dimension_semantics=sem, vmem_limit_bytes=VMEM_LIMIT)


def _pick_mod(is_sample, prompt_ref, sample_ref):
    return jnp.where(is_sample, sample_ref[0], prompt_ref[0])


def _layer_norm(v, g, b):
    mu = jnp.mean(v, axis=-1, keepdims=True)
    c = v - mu
    var = jnp.mean(c * c, axis=-1, keepdims=True)
    return c * lax.rsqrt(var + LN_EPS) * g + b


def _mod_kernel(c_ref, w_ref, b_ref, o_ref):
    s = _silu(c_ref[...]).astype(BF16)
    o_ref[0] = _dot(s, w_ref[0].astype(BF16)) + b_ref[0]


def _modulation(c_all, w_ada, b_ada):
    depth, _, width = w_ada.shape
    n = c_all.shape[0]
    tn = 1536
    return pl.pallas_call(
        _mod_kernel,
        grid=(depth, width // tn),
        in_specs=[
            pl.BlockSpec((n, D_MODEL), lambda l, j: (0, 0)),
            pl.BlockSpec((1, D_MODEL, tn), lambda l, j: (l, 0, j)),
            pl.BlockSpec((1, 1, tn), lambda l, j: (l, 0, j)),
        ],
        out_specs=pl.BlockSpec((1, n, tn), lambda l, j: (l, 0, j)),
        out_shape=jax.ShapeDtypeStruct((depth, n, width), F32),
        compiler_params=_params(("parallel", "parallel")),
    )(c_all, w_ada, b_ada.reshape(depth, 1, width))


def _inproj_kernel(n_prompt_tiles, x_ref, shp_ref, shs_ref, scp_ref, scs_ref, w_ref, wdt_ref,
                   p_ref, dt_ref, u_scr):
    i = pl.program_id(0)
    j = pl.program_id(1)

    @pl.when(j == 0)
    def _():
        is_sample = i >= n_prompt_tiles
        sh = _pick_mod(is_sample, shp_ref, shs_ref)
        sc = _pick_mod(is_sample, scp_ref, scs_ref)
        u = (x_ref[...] * (1.0 + sc) + sh).astype(BF16)
        u_scr[...] = u
        dt_ref[...] = _dot(u, wdt_ref[...])

    p_ref[...] = _dot(u_scr[...], w_ref[...]).astype(BF16)


def _mod_specs(layer, col, row_tile, n_prompt_tiles, tiles_per_seq, n_seq):
    def prompt_map(i, *_):
        return (layer, jnp.minimum(i // tiles_per_seq, n_seq - 1), 0, col)

    def sample_map(i, *_):
        return (layer, 0, jnp.maximum(i - n_prompt_tiles, 0), col)

    return (pl.BlockSpec((None, 1, 1, D_MODEL), prompt_map),
            pl.BlockSpec((None, 1, row_tile, D_MODEL), sample_map))


def _layer_spec(layer, shape):
    return pl.BlockSpec((None, *shape), lambda *_: (layer,) + (0,) * len(shape))


def _input_projection(x, mod_p, mod_s, w_main, w_dt, layer, n_prompt, seq_len):
    t_all = x.shape[0]
    tm, tn = ROW_TILE_IN, COL_TILE_IN
    npt = n_prompt // tm
    n_seq = mod_p.shape[1]
    shp, shs = _mod_specs(layer, 0, tm, npt, seq_len // tm, n_seq)
    scp, scs = _mod_specs(layer, 1, tm, npt, seq_len // tm, n_seq)
    return pl.pallas_call(
        functools.partial(_inproj_kernel, npt),
        grid=(t_all // tm, P_MAIN // tn),
        in_specs=[
            pl.BlockSpec((tm, D_MODEL), lambda i, j: (i, 0)),
            shp, shs, scp, scs,
            pl.BlockSpec((None, D_MODEL, tn), lambda i, j: (layer, 0, j)),
            _layer_spec(layer, (D_MODEL, DT_PAD)),
        ],
        out_specs=[
            pl.BlockSpec((tm, tn), lambda i, j: (i, j)),
            pl.BlockSpec((tm, DT_PAD), lambda i, j: (i, 0)),
        ],
        out_shape=[
            jax.ShapeDtypeStruct((t_all, P_MAIN), BF16),
            jax.ShapeDtypeStruct((t_all, DT_PAD), F32),
        ],
        scratch_shapes=[pltpu.VMEM((tm, D_MODEL), BF16)],
        compiler_params=_params(("parallel", "arbitrary")),
    )(x, mod_p, mod_s, mod_p, mod_s, w_main, w_dt)


def _rope(v, cos, sin):
    width = v.shape[-1]
    reps = width // LANES
    if reps > 1:
        cos = jnp.concatenate([cos] * reps, axis=-1)
        sin = jnp.concatenate([sin] * reps, axis=-1)
    lane = lax.broadcasted_iota(jnp.int32, v.shape, 1) % HEAD_DIM
    partner = jnp.where(lane < ROT_HALF,
                        pltpu.roll(v, width - ROT_HALF, 1),
                        pltpu.roll(v, ROT_HALF, 1))
    return v * cos + partner * sin


def _shift_rows(cur, prev, k):
    axis = cur.ndim - 2
    idx = lax.broadcasted_iota(jnp.int32, cur.shape, axis)
    return jnp.where(idx < k, pltpu.roll(prev, k, axis), pltpu.roll(cur, k, axis))


def _causal_conv(cur, prev, w_ref, width):
    out = cur * w_ref[width - 1:width, :]
    for k in range(1, width):
        out = out + _shift_rows(cur, prev, k) * w_ref[width - 1 - k:width - k, :]
    return out


def _head_expand():
    r = lax.broadcasted_iota(jnp.int32, (LANES, SSD_D_INNER), 0)
    c = lax.broadcasted_iota(jnp.int32, (LANES, SSD_D_INNER), 1)
    return (c // SSD_HEAD_DIM == r).astype(F32)


def _head_expand_t():
    r = lax.broadcasted_iota(jnp.int32, (SSD_D_INNER, LANES), 0)
    c = lax.broadcasted_iota(jnp.int32, (SSD_D_INNER, LANES), 1)
    return (r // SSD_HEAD_DIM == c).astype(F32)


def _ssd_tile(act, dt_raw, dtb, alog, seq_rows):
    rows = MIX_ROWS
    xs = act[:, :SSD_D_INNER]
    bm = act[:, SSD_D_INNER:SSD_D_INNER + SSD_GROUPS * SSD_STATE]
    cm = act[:, SSD_D_INNER + SSD_GROUPS * SSD_STATE:]
    v = dt_raw + dtb
    dt = jnp.maximum(v, 0.0) + jnp.log1p(jnp.exp(-jnp.abs(v)))
    a = -jnp.exp(alog)
    dta = dt * a
    ri = lax.broadcasted_iota(jnp.int32, (rows, rows), 0)
    ci = lax.broadcasted_iota(jnp.int32, (rows, rows), 1)
    same = (ri // seq_rows) == (ci // seq_rows)
    causal = same & (ci <= ri)
    cs = _dot_exact(causal.astype(F32), dta)
    expand = _head_expand().astype(BF16)

    def per_head_lanes(v):
        hi = v.astype(BF16)
        lo = (v - hi.astype(F32)).astype(BF16)
        return _dot(hi, expand) + _dot(lo, expand)

    dt_e = per_head_lanes(dt)
    cs_e = per_head_lanes(cs)
    if seq_rows == rows:
        tot = None
        tot_e = cs_e[rows - 1:rows, :]
    else:
        tot = _dot_exact(same.astype(F32), dta)
        tot_e = per_head_lanes(tot)
    cs_t = cs.T
    dtx = xs * dt_e
    xw = dtx * jnp.exp(tot_e - cs_e)
    b_groups, c_groups, y_parts = [], [], []
    for g in range(SSD_GROUPS):
        bg = bm[:, g * SSD_STATE:(g + 1) * SSD_STATE].astype(BF16)
        cg = cm[:, g * SSD_STATE:(g + 1) * SSD_STATE].astype(BF16)
        b_groups.append(bg)
        c_groups.append(cg)
        cb = _dot_nt(cg, bg)
        for hh in range(SSD_HEADS_PER_GROUP):
            h = g * SSD_HEADS_PER_GROUP + hh
            seg = cs[:, h:h + 1] - cs_t[h:h + 1, :]
            decay = jnp.where(causal, jnp.exp(jnp.where(causal, seg, 0.0)), 0.0)
            y_parts.append(_dot((cb * decay).astype(BF16),
                                dtx[:, h * SSD_HEAD_DIM:(h + 1) * SSD_HEAD_DIM].astype(BF16)))
    y_diag = jnp.concatenate(y_parts, axis=-1)
    return xs, y_diag, jnp.exp(cs_e), xw, tot, tot_e, bm, b_groups, c_groups


def _gated_group_norm(y, z, nw):
    y = y * _silu(z)
    half = SSD_D_INNER // SSD_GROUPS
    parts = []
    for g in range(SSD_GROUPS):
        yg = y[:, g * half:(g + 1) * half]
        parts.append(yg * lax.rsqrt(jnp.mean(yg * yg, axis=-1, keepdims=True) + RMS_EPS))
    return jnp.concatenate(parts, axis=-1) * nw


def _attn_prompt_kernel(sink_ref, q_ref, kvc_ref, kvp_ref, cosc_ref, sinc_ref, cosp_ref, sinp_ref,
                        ya_ref, krot_ref):
    i = pl.program_id(1)
    nb = pl.num_programs(1)
    w = WINDOW
    q = _rope(q_ref[...].astype(F32), cosc_ref[...], sinc_ref[...])
    kvc = kvc_ref[...].astype(F32)
    kvp = kvp_ref[...].astype(F32)
    kc = _rope(kvc[:, :KV_WIDTH], cosc_ref[...], sinc_ref[...])
    kp = _rope(kvp[:, :KV_WIDTH], cosp_ref[...], sinp_ref[...])
    vc = kvc[:, KV_WIDTH:]
    vp = kvp[:, KV_WIDTH:]

    @pl.when(i == nb - 1)
    def _():
        krot_ref[0] = kc

    rows = Q_PER_KV * w
    r = lax.broadcasted_iota(jnp.int32, (rows, 2 * w), 0)
    s = lax.broadcasted_iota(jnp.int32, (rows, 2 * w), 1)
    diff = w + (r % w) - s
    first_key = jnp.where(i > 0, 0, w)
    valid = (diff >= 0) & (diff < w) & (s >= first_key)
    rcol = lax.broadcasted_iota(jnp.int32, (rows, 1), 0)
    outs = []
    for kh in range(N_KV_HEADS):
        hs = slice(kh * HEAD_DIM, (kh + 1) * HEAD_DIM)
        k2 = jnp.concatenate([kp[:, hs], kc[:, hs]], axis=0).astype(BF16)
        v2 = jnp.concatenate([vp[:, hs], vc[:, hs]], axis=0).astype(BF16)
        qg = jnp.concatenate(
            [q[:, (kh * Q_PER_KV + g) * HEAD_DIM:(kh * Q_PER_KV + g + 1) * HEAD_DIM] for g in range(Q_PER_KV)],
            axis=0).astype(BF16)
        sink = jnp.zeros((rows, 1), F32)
        for g in range(Q_PER_KV):
            sink = jnp.where(rcol // w == g, sink_ref[kh * Q_PER_KV + g], sink)
        logits = jnp.where(valid, _dot_nt(qg, k2) * ATTN_SCALE, -jnp.inf)
        m = jnp.maximum(jnp.max(logits, axis=-1, keepdims=True), sink)
        e = jnp.exp(logits - m)
        den = jnp.sum(e, axis=-1, keepdims=True) + jnp.exp(sink - m)
        o = _dot(e.astype(BF16), v2) / den
        for g in range(Q_PER_KV):
            outs.append(o[g * w:(g + 1) * w, :])
    ya_ref[...] = jnp.concatenate(outs, axis=-1).astype(BF16)


def _attention_prompt(p, sink, cos_p, sin_p, n_seq, seq_len):
    nb = seq_len // WINDOW
    n_prompt = n_seq * seq_len
    w = WINDOW

    def cur(b, i):
        return b * nb + i

    def prev(b, i):
        return jnp.maximum(b * nb + i - 1, 0)

    return pl.pallas_call(
        _attn_prompt_kernel,
        grid=(n_seq, nb),
        in_specs=[
            pl.BlockSpec(memory_space=pltpu.SMEM),
            pl.BlockSpec((w, ATTN_WIDTH), lambda b, i: (cur(b, i), COL_Q // ATTN_WIDTH)),
            pl.BlockSpec((w, 2 * KV_WIDTH), lambda b, i: (cur(b, i), COL_KV // (2 * KV_WIDTH))),
            pl.BlockSpec((w, 2 * KV_WIDTH), lambda b, i: (prev(b, i), COL_KV // (2 * KV_WIDTH))),
            pl.BlockSpec((w, LANES), lambda b, i: (i, 0)),
            pl.BlockSpec((w, LANES), lambda b, i: (i, 0)),
            pl.BlockSpec((w, LANES), lambda b, i: (jnp.maximum(i - 1, 0), 0)),
            pl.BlockSpec((w, LANES), lambda b, i: (jnp.maximum(i - 1, 0), 0)),
        ],
        out_specs=[
            pl.BlockSpec((w, ATTN_WIDTH), lambda b, i: (cur(b, i), 0)),
            pl.BlockSpec((1, w, KV_WIDTH), lambda b, i: (b, 0, 0)),
        ],
        out_shape=[
            jax.ShapeDtypeStruct((n_prompt, ATTN_WIDTH), BF16),
            jax.ShapeDtypeStruct((n_seq, w, KV_WIDTH), F32),
        ],
        compiler_params=_params(("parallel", "arbitrary")),
    )(sink, p, p, p, cos_p, sin_p, cos_p, sin_p)


def _ssd_prompt_kernel(z_ref, xc_ref, xp_ref, dt_ref, scb_ref, sccc_ref, sccp_ref, scvc_ref, scvp_ref,
                       cw_ref, cb_ref, dtb_ref, alog_ref, de_ref, nw_ref, scw_ref,
                       yb_ref, yc_ref, hout_ref, cvlast_ref, h_scr):
    i = pl.program_id(1)
    nc = pl.num_programs(1)
    first = i == 0

    @pl.when(first)
    def _():
        h_scr[...] = jnp.zeros_like(h_scr)

    xc = xc_ref[...].astype(F32)
    xp = jnp.where(first, 0.0, xp_ref[...].astype(F32))
    act = _silu(_causal_conv(xc, xp, cw_ref, SSD_CONV) + cb_ref[...])
    xs, y_diag, ecs_e, xw, _, tot_e, bm, _, c_groups = _ssd_tile(
        act, dt_ref[...], dtb_ref[...], alog_ref[...], MIX_ROWS)
    gw = SSD_HEADS_PER_GROUP * SSD_HEAD_DIM
    y_off = []
    for g in range(SSD_GROUPS):
        cols = slice(g * gw, (g + 1) * gw)
        hg = h_scr[:, cols]
        y_off.append(_dot(c_groups[g], hg.astype(BF16)))
        b_t = bm[:, g * SSD_STATE:(g + 1) * SSD_STATE].T.astype(BF16)
        h_scr[:, cols] = jnp.exp(tot_e[:, cols]) * hg + _dot(b_t, xw[:, cols].astype(BF16))
    y = y_diag + jnp.concatenate(y_off, axis=-1) * ecs_e + de_ref[...] * xs
    yb_ref[...] = _gated_group_norm(y, z_ref[...].astype(F32), nw_ref[...]).astype(BF16)

    cvc = sccc_ref[...].astype(F32) * scvc_ref[...].astype(F32)
    cvp = jnp.where(first, 0.0, sccp_ref[...].astype(F32) * scvp_ref[...].astype(F32))
    conv_c = _causal_conv(cvc, cvp, scw_ref, SC_WIDTH)
    yc_ref[...] = (scb_ref[...].astype(F32) * conv_c).astype(BF16)

    @pl.when(i == nc - 1)
    def _():
        hout_ref[0] = h_scr[...].T
        cvlast_ref[0] = cvc[MIX_ROWS - SUBLANES:, :]


def _ssd_prompt(p, dt, cw, cb, dtb, alog, de, nw, scw, n_seq, seq_len):
    nc = seq_len // MIX_ROWS
    r = MIX_ROWS
    n_prompt = n_seq * seq_len

    def cur(b, i):
        return b * nc + i

    def prev(b, i):
        return jnp.maximum(b * nc + i - 1, 0)

    def col(width, offset, which):
        return pl.BlockSpec((r, width), lambda b, i: (which(b, i), offset // width))

    def const(shape):
        return pl.BlockSpec(shape, lambda b, i: (0,) * len(shape))

    return pl.pallas_call(
        _ssd_prompt_kernel,
        grid=(n_seq, nc),
        in_specs=[
            col(SSD_D_INNER, COL_Z, cur),
            col(SSD_CONV_DIM, COL_XBC, cur), col(SSD_CONV_DIM, COL_XBC, prev),
            pl.BlockSpec((r, DT_PAD), lambda b, i: (cur(b, i), 0)),
            col(SC_DIM, COL_SCB, cur),
            col(SC_DIM, COL_SCC, cur), col(SC_DIM, COL_SCC, prev),
            col(SC_DIM, COL_SCV, cur), col(SC_DIM, COL_SCV, prev),
            const((SSD_CONV, SSD_CONV_DIM)), const((1, SSD_CONV_DIM)),
            const((1, DT_PAD)), const((1, DT_PAD)), const((1, SSD_D_INNER)), const((1, SSD_D_INNER)),
            const((SC_WIDTH, SC_DIM)),
        ],
        out_specs=[
            pl.BlockSpec((r, SSD_D_INNER), lambda b, i: (cur(b, i), 0)),
            pl.BlockSpec((r, SC_DIM), lambda b, i: (cur(b, i), 0)),
            pl.BlockSpec((1, SSD_D_INNER, SSD_STATE), lambda b, i: (b, 0, 0)),
            pl.BlockSpec((1, SUBLANES, SC_DIM), lambda b, i: (b, 0, 0)),
        ],
        out_shape=[
            jax.ShapeDtypeStruct((n_prompt, SSD_D_INNER), BF16),
            jax.ShapeDtypeStruct((n_prompt, SC_DIM), BF16),
            jax.ShapeDtypeStruct((n_seq, SSD_D_INNER, SSD_STATE), F32),
            jax.ShapeDtypeStruct((n_seq, SUBLANES, SC_DIM), F32),
        ],
        scratch_shapes=[pltpu.VMEM((SSD_STATE, SSD_D_INNER), F32)],
        compiler_params=_params(("parallel", "arbitrary")),
    )(p, p, p, dt, p, p, p, p, p, cw, cb, dtb, alog, de, nw, scw)


def _mix_sample_kernel(sink_ref, q_ref, kv_ref, z_ref, x_ref, dt_ref, scb_ref, scc_ref, scv_ref,
                       cos_ref, sin_ref, ck_ref, cv_ref, h0_ref, xbuf_ref, cbuf_ref,
                       cw_ref, cb_ref, dtb_ref, alog_ref, de_ref, nw_ref, scw_ref,
                       ya_ref, yb_ref, yc_ref, knew_ref, vnew_ref, hnew_ref, cvout_ref):
    ns, t = SEQS_PER_STEP, DEC_SEQ
    wb = ck_ref.shape[1]

    q = _rope(q_ref[...].astype(F32), cos_ref[...], sin_ref[...])
    kv = kv_ref[...].astype(F32)
    kn = _rope(kv[:, :KV_WIDTH], cos_ref[...], sin_ref[...])
    vn = kv[:, KV_WIDTH:]
    q3 = q.reshape(ns, t, ATTN_WIDTH)
    kn3 = kn.reshape(ns, t, KV_WIDTH)
    vn3 = vn.reshape(ns, t, KV_WIDTH)
    knew_ref[:, :wb - t, :] = ck_ref[:, t:, :]
    knew_ref[:, wb - t:, :] = kn3
    vnew_ref[:, :wb - t, :] = cv_ref[:, t:, :]
    vnew_ref[:, wb - t:, :] = vn3
    nq = Q_PER_KV * t
    qi = lax.broadcasted_iota(jnp.int32, (ns, nq, wb + t), 1) % t
    si = lax.broadcasted_iota(jnp.int32, (ns, nq, wb + t), 2)
    valid = ((si < wb) & (si > qi + (wb - WINDOW))) | ((si >= wb) & (si - wb <= qi))
    hrow = lax.broadcasted_iota(jnp.int32, (ns, nq, 1), 1) // t
    heads = [None] * N_HEADS
    for kh in range(N_KV_HEADS):
        hs = slice(kh * HEAD_DIM, (kh + 1) * HEAD_DIM)
        k_all = jnp.concatenate([ck_ref[:, :, hs], kn3[:, :, hs]], axis=1).astype(BF16)
        v_all = jnp.concatenate([cv_ref[:, :, hs], vn3[:, :, hs]], axis=1).astype(BF16)
        qg = jnp.concatenate(
            [q3[:, :, (kh * Q_PER_KV + g) * HEAD_DIM:(kh * Q_PER_KV + g + 1) * HEAD_DIM] for g in range(Q_PER_KV)],
            axis=1).astype(BF16)
        sink = jnp.zeros((ns, nq, 1), F32)
        for g in range(Q_PER_KV):
            sink = jnp.where(hrow == g, sink_ref[kh * Q_PER_KV + g], sink)
        logits = jnp.einsum('bqd,bsd->bqs', qg, k_all, preferred_element_type=F32) * ATTN_SCALE
        logits = jnp.where(valid, logits, -jnp.inf)
        m = jnp.maximum(jnp.max(logits, axis=-1, keepdims=True), sink)
        e = jnp.exp(logits - m)
        den = jnp.sum(e, axis=-1, keepdims=True) + jnp.exp(sink - m)
        o = jnp.einsum('bqs,bsd->bqd', e.astype(BF16), v_all, preferred_element_type=F32) / den
        for g in range(Q_PER_KV):
            heads[kh * Q_PER_KV + g] = o[:, g * t:(g + 1) * t, :]
    ya_ref[...] = jnp.concatenate(heads, axis=-1).reshape(ns * t, ATTN_WIDTH).astype(BF16)

    xc3 = x_ref[...].astype(F32).reshape(ns, t, SSD_CONV_DIM)
    xp3 = xbuf_ref[...].reshape(ns, t, SSD_CONV_DIM)
    conv = _causal_conv(xc3, xp3, cw_ref, SSD_CONV).reshape(ns * t, SSD_CONV_DIM)
    act = _silu(conv + cb_ref[...])
    xs, y_diag, ecs_e, xw, tot, _, _, b_groups, c_groups = _ssd_tile(
        act, dt_ref[...], dtb_ref[...], alog_ref[...], t)
    xw_t = xw.T
    dec_t = jnp.exp(_dot_exact(_head_expand_t(), tot.T))
    gw = SSD_HEADS_PER_GROUP * SSD_HEAD_DIM
    y_off = []
    for g in range(SSD_GROUPS):
        c3 = c_groups[g].reshape(ns, t, SSD_STATE)
        hg = h0_ref[:, g * gw:(g + 1) * gw, :].astype(BF16)
        y_off.append(jnp.einsum('btn,bqn->btq', c3, hg, preferred_element_type=F32).reshape(ns * t, gw))
    y = y_diag + jnp.concatenate(y_off, axis=-1) * ecs_e + de_ref[...] * xs
    yb_ref[...] = _gated_group_norm(y, z_ref[...].astype(F32), nw_ref[...]).astype(BF16)
    col = lax.broadcasted_iota(jnp.int32, xw_t.shape, 1) // t
    for b in range(ns):
        xw_b = jnp.where(col == b, xw_t, jnp.zeros_like(xw_t))
        upd = jnp.concatenate(
            [_dot(xw_b[g * gw:(g + 1) * gw, :].astype(BF16), b_groups[g]) for g in range(SSD_GROUPS)], axis=0)
        hnew_ref[b] = dec_t[:, b * t:b * t + 1] * h0_ref[b] + upd

    cvc = scc_ref[...].astype(F32) * scv_ref[...].astype(F32)
    cvout_ref[...] = cvc
    conv_c = _causal_conv(cvc.reshape(ns, t, SC_DIM), cbuf_ref[...].reshape(ns, t, SC_DIM), scw_ref, SC_WIDTH)
    yc_ref[...] = (scb_ref[...].astype(F32) * conv_c.reshape(ns * t, SC_DIM)).astype(BF16)


def _mix_sample(p, dt, sink, cos_s, sin_s, cache_k, cache_v, h0, xbuf, cbuf,
                cw, cb, dtb, alog, de, nw, scw, layer, n_prompt, n_dec):
    r = MIX_ROWS
    ns = SEQS_PER_STEP
    steps = n_dec // ns
    base = n_prompt // r
    wb = cache_k.shape[2]

    def col(width, offset):
        return pl.BlockSpec((r, width), lambda i: (base + i, offset // width))

    def state(shape):
        return pl.BlockSpec((None, *shape), lambda i: (layer, i) + (0,) * (len(shape) - 1))

    def const(shape):
        return pl.BlockSpec(shape, lambda i: (0,) * len(shape))

    def rows(width):
        return pl.BlockSpec((r, width), lambda i: (i, 0))

    n_rows = n_dec * DEC_SEQ
    return pl.pallas_call(
        _mix_sample_kernel,
        grid=(steps,),
        in_specs=[
            pl.BlockSpec(memory_space=pltpu.SMEM),
            col(ATTN_WIDTH, COL_Q), col(2 * KV_WIDTH, COL_KV), col(SSD_D_INNER, COL_Z),
            col(SSD_CONV_DIM, COL_XBC),
            pl.BlockSpec((r, DT_PAD), lambda i: (base + i, 0)),
            col(SC_DIM, COL_SCB), col(SC_DIM, COL_SCC), col(SC_DIM, COL_SCV),
            const((r, LANES)), const((r, LANES)),
            state((ns, wb, KV_WIDTH)), state((ns, wb, KV_WIDTH)),
            state((ns, SSD_D_INNER, SSD_STATE)),
            state((r, SSD_CONV_DIM)), state((r, SC_DIM)),
            const((SSD_CONV, SSD_CONV_DIM)), const((1, SSD_CONV_DIM)),
            const((1, DT_PAD)), const((1, DT_PAD)), const((1, SSD_D_INNER)), const((1, SSD_D_INNER)),
            const((SC_WIDTH, SC_DIM)),
        ],
        out_specs=[
            rows(ATTN_WIDTH), rows(SSD_D_INNER), rows(SC_DIM),
            pl.BlockSpec((ns, wb, KV_WIDTH), lambda i: (i, 0, 0)),
            pl.BlockSpec((ns, wb, KV_WIDTH), lambda i: (i, 0, 0)),
            pl.BlockSpec((ns, SSD_D_INNER, SSD_STATE), lambda i: (i, 0, 0)),
            rows(SC_DIM),
        ],
        out_shape=[
            jax.ShapeDtypeStruct((n_rows, ATTN_WIDTH), BF16),
            jax.ShapeDtypeStruct((n_rows, SSD_D_INNER), BF16),
            jax.ShapeDtypeStruct((n_rows, SC_DIM), BF16),
            jax.ShapeDtypeStruct((n_dec, wb, KV_WIDTH), F32),
            jax.ShapeDtypeStruct((n_dec, wb, KV_WIDTH), F32),
            jax.ShapeDtypeStruct((n_dec, SSD_D_INNER, SSD_STATE), F32),
            jax.ShapeDtypeStruct((n_rows, SC_DIM), F32),
        ],
        compiler_params=_params(("parallel",)),
    )(sink, p, p, p, p, dt, p, p, p, cos_s, sin_s, cache_k, cache_v, h0, xbuf, cbuf,
      cw, cb, dtb, alog, de, nw, scw)


def _store_token_major(ref, v):
    n = v.shape[0]
    for s in range(ROW_CHUNKS):
        ref[pl.ds(s, n, stride=ROW_CHUNKS), :] = v[:, s * LANES:(s + 1) * LANES]


def _load_token_major(ref, n):
    return jnp.concatenate([ref[pl.ds(s, n, stride=ROW_CHUNKS), :] for s in range(ROW_CHUNKS)], axis=-1)


def _outproj_kernel(n_prompt_tiles, alpha,
                    x_ref, yap_ref, yas_ref, ybp_ref, ybs_ref, ycp_ref, ycs_ref, g_ref,
                    g1p_ref, g1s_ref, sh2p_ref, sh2s_ref, sc2p_ref, sc2s_ref,
                    wpa_ref, wpb_ref, wpc_ref, wout_ref, wr_ref, br_ref, lng_ref, lnb_ref,
                    x1_ref, h_ref, route_ref, cnt_ref, cnt_scr):
    i = pl.program_id(0)
    is_sample = i >= n_prompt_tiles
    tm = x_ref.shape[0]

    @pl.when(i == 0)
    def _():
        cnt_scr[...] = jnp.zeros_like(cnt_scr)

    n = tm
    cnt = cnt_scr[...]
    for r0 in range(0, tm, n):
        rs = slice(r0, r0 + n)

        def pick(prompt_ref, sample_ref):
            return jnp.where(is_sample, sample_ref[0, rs, :], prompt_ref[0])

        ya = jnp.where(is_sample, yas_ref[rs, :], yap_ref[rs, :])
        yb = jnp.where(is_sample, ybs_ref[rs, :], ybp_ref[rs, :])
        yc = jnp.where(is_sample, ycs_ref[rs, :], ycp_ref[rs, :])
        gates = 0.5 * jnp.tanh(0.5 * g_ref[rs, :]) + 0.5
        merged = (gates[:, :D_MODEL] * _dot(ya, wpa_ref[...]).astype(BF16)
                  + gates[:, D_MODEL:2 * D_MODEL] * _dot(yb, wpb_ref[...]).astype(BF16)
                  + gates[:, 2 * D_MODEL:] * _dot(yc, wpc_ref[...]).astype(BF16))
        mix = _dot(merged, wout_ref[...])
        g1 = pick(g1p_ref, g1s_ref)
        x1 = _layer_norm(alpha * x_ref[rs, :] + g1 * mix, lng_ref[...], lnb_ref[...])
        x1_ref[rs, :] = x1
        sh2 = pick(sh2p_ref, sh2s_ref)
        sc2 = pick(sc2p_ref, sc2s_ref)
        h = x1 * (1.0 + sc2) + sh2
        _store_token_major(h_ref.at[pl.ds(r0 * ROW_CHUNKS, n * ROW_CHUNKS)], h)

        lt = (_dot(h.astype(BF16), wr_ref[...]) + br_ref[...]).T[:ROUTE_ROWS]
        row = lax.broadcasted_iota(jnp.int32, lt.shape, 0).astype(F32)
        neg = -jnp.inf
        big = float(ROUTE_ROWS)
        gl = jnp.where(row < MOE_GROUPS, lt, neg)
        gmax = jnp.max(gl, axis=0, keepdims=True)
        g_p = 1.0 / jnp.sum(jnp.exp(gl - gmax), axis=0, keepdims=True)
        gidx = jnp.min(jnp.where(gl == gmax, row, big), axis=0, keepdims=True)
        lo = ROUTE_EXPERT_ROW + EXPERTS_PER_GROUP * gidx
        sel = jnp.where((row >= lo) & (row < lo + EXPERTS_PER_GROUP), lt, neg)
        m1 = jnp.max(sel, axis=0, keepdims=True)
        i1 = jnp.min(jnp.where(sel == m1, row, big), axis=0, keepdims=True)
        sel2 = jnp.where(row == i1, neg, sel)
        m2 = jnp.max(sel2, axis=0, keepdims=True)
        i2 = jnp.min(jnp.where(sel2 == m2, row, big), axis=0, keepdims=True)
        ssum = jnp.sum(jnp.exp(sel - m1), axis=0, keepdims=True)
        p1 = 1.0 / ssum
        p2 = jnp.exp(m2 - m1) / ssum
        w1 = g_p * (p1 / (p1 + p2))
        w2 = g_p * (p2 / (p1 + p2))

        onehot = jnp.where((row == i1) | (row == i2), 1.0, 0.0)
        ri = lax.broadcasted_iota(jnp.int32, (n, n), 0)
        ci = lax.broadcasted_iota(jnp.int32, (n, n), 1)
        earlier = jnp.where(ri < ci, 1.0, 0.0).astype(BF16)
        prefix = _dot(onehot.astype(BF16), earlier) + cnt[:, 0:1]
        rank1 = jnp.sum(jnp.where(row == i1, prefix, 0.0), axis=0, keepdims=True)
        rank2 = jnp.sum(jnp.where(row == i2, prefix, 0.0), axis=0, keepdims=True)
        cnt = cnt + jnp.sum(onehot, axis=1, keepdims=True)
        zero = jnp.zeros_like(w1)
        route_ref[:, rs] = jnp.concatenate(
            [i1 - ROUTE_EXPERT_ROW, i2 - ROUTE_EXPERT_ROW, w1, w2, rank1, rank2, zero, zero], axis=0)

    cnt_scr[...] = cnt
    cnt_ref[...] = cnt


def _output_projection(x, y_prompt, y_sample, p, mod_p, mod_s, wpa, wpb, wpc, wout, wr, br, lng, lnb,
                       layer, n_prompt, seq_len, alpha):
    t_all = x.shape[0]
    tm = ROW_TILE
    npt = n_prompt // tm
    n_seq = mod_p.shape[1]
    mods = []
    for col in (2, 3, 4):
        mods.extend(_mod_specs(layer, col, tm, npt, seq_len // tm, n_seq))

    def rows(width):
        return pl.BlockSpec((tm, width), lambda i: (i, 0))

    def prompt_rows(width):
        return pl.BlockSpec((tm, width), lambda i: (jnp.minimum(i, npt - 1), 0))

    def sample_rows(width):
        return pl.BlockSpec((tm, width), lambda i: (jnp.maximum(i - npt, 0), 0))

    def const(shape):
        return pl.BlockSpec(shape, lambda i: (0,) * len(shape))

    return pl.pallas_call(
        functools.partial(_outproj_kernel, npt, alpha),
        grid=(t_all // tm,),
        in_specs=[
            rows(D_MODEL),
            prompt_rows(ATTN_WIDTH), sample_rows(ATTN_WIDTH),
            prompt_rows(SSD_D_INNER), sample_rows(SSD_D_INNER),
            prompt_rows(SC_DIM), sample_rows(SC_DIM),
            pl.BlockSpec((tm, 3 * D_MODEL), lambda i: (i, COL_GATES // (3 * D_MODEL))),
            *mods,
            _layer_spec(layer, (ATTN_WIDTH, D_MODEL)), _layer_spec(layer, (SSD_D_INNER, D_MODEL)),
            _layer_spec(layer, (SC_DIM, D_MODEL)), _layer_spec(layer, (D_MODEL, D_MODEL)),
            _layer_spec(layer, (D_MODEL, LANES)), _layer_spec(layer, (1, LANES)),
            const((1, D_MODEL)), const((1, D_MODEL)),
        ],
        out_specs=[
            rows(D_MODEL),
            pl.BlockSpec((tm * ROW_CHUNKS, LANES), lambda i: (i, 0)),
            pl.BlockSpec((SUBLANES, tm), lambda i: (0, i)),
            const((ROUTE_ROWS, LANES)),
        ],
        out_shape=[
            jax.ShapeDtypeStruct((t_all, D_MODEL), F32),
            jax.ShapeDtypeStruct((t_all * ROW_CHUNKS, LANES), F32),
            jax.ShapeDtypeStruct((SUBLANES, t_all), F32),
            jax.ShapeDtypeStruct((ROUTE_ROWS, LANES), F32),
        ],
        scratch_shapes=[pltpu.VMEM((ROUTE_ROWS, LANES), F32)],
        compiler_params=_params(("arbitrary",)),
    )(x, y_prompt[0], y_sample[0], y_prompt[1], y_sample[1], y_prompt[2], y_sample[2],
      p, mod_p, mod_s, mod_p, mod_s, mod_p, mod_s,
      wpa, wpb, wpc, wout, wr, br, lng, lnb)


def _slots_kernel(route_ref, start_ref, dest_ref):
    route = route_ref[...]
    tm = route.shape[1]
    expert = lax.broadcasted_iota(jnp.int32, (N_EXPERTS, tm), 0).astype(F32)
    start = start_ref[:, 0:1]
    rows = []
    for e_row, r_row in ((0, 4), (1, 5)):
        first = jnp.sum(jnp.where(expert == route[e_row:e_row + 1], start, 0.0), axis=0, keepdims=True)
        rows.append(first + route[r_row:r_row + 1])
    rows.append(jnp.zeros((SUBLANES - 2, tm), F32))
    dest_ref[...] = jnp.concatenate(rows, axis=0).astype(jnp.int32)


def _slots(route, pad_start):
    t_all = route.shape[1]
    tm = ROW_TILE_IN
    dest = pl.pallas_call(
        _slots_kernel,
        grid=(t_all // tm,),
        in_specs=[pl.BlockSpec((SUBLANES, tm), lambda i: (0, i)),
                  pl.BlockSpec((N_EXPERTS, LANES), lambda i: (0, 0))],
        out_specs=pl.BlockSpec((SUBLANES, tm), lambda i: (0, i)),
        out_shape=jax.ShapeDtypeStruct((SUBLANES, t_all), jnp.int32),
        compiler_params=_params(("parallel",)),
    )(route, jnp.broadcast_to(pad_start.astype(F32)[:, None], (N_EXPERTS, LANES)))
    return dest[0], dest[1]


def _token_copy(src, src_row, dst, dst_row, sem):
    return pltpu.make_async_copy(
        src.at[pl.ds(pl.multiple_of(src_row * ROW_CHUNKS, ROW_CHUNKS), ROW_CHUNKS)],
        dst.at[pl.ds(pl.multiple_of(dst_row * ROW_CHUNKS, ROW_CHUNKS), ROW_CHUNKS)],
        sem)


def _dispatch_kernel(d1_ref, d2_ref, pend_ref, cnt_ref, nb_ref, h_ref, xb_ref, zero_scr, zsem, sem):
    i = pl.program_id(0)
    tm = h_ref.shape[0] // ROW_CHUNKS
    blk_rows = EXPERT_BLOCK * ROW_CHUNKS
    n_blocks = xb_ref.shape[0] // blk_rows

    def zero_block(b):
        start = pl.multiple_of(b * blk_rows, blk_rows)
        return pltpu.make_async_copy(zero_scr, xb_ref.at[pl.ds(start, blk_rows)], zsem)

    def last_block(e):
        return pend_ref[e] // EXPERT_BLOCK - 1

    @pl.when(i == 0)
    def _():
        zero_scr[...] = jnp.zeros_like(zero_scr)
        for e in range(N_EXPERTS):
            @pl.when(cnt_ref[e] > 0)
            def _():
                zero_block(last_block(e)).start()
        lax.fori_loop(nb_ref[0], n_blocks, lambda b, c: (zero_block(b).start(), c)[1], 0)
        for e in range(N_EXPERTS):
            @pl.when(cnt_ref[e] > 0)
            def _():
                zero_block(last_block(e)).wait()
        lax.fori_loop(nb_ref[0], n_blocks, lambda b, c: (zero_block(b).wait(), c)[1], 0)

    def issue(c, carry):
        for u in range(ISSUE_UNROLL):
            t = c * ISSUE_UNROLL + u
            g = i * tm + t
            _token_copy(h_ref, t, xb_ref, d1_ref[g], sem.at[0]).start(priority=0)
            _token_copy(h_ref, t, xb_ref, d2_ref[g], sem.at[1]).start(priority=1)
        return carry

    lax.fori_loop(0, tm // ISSUE_UNROLL, issue, 0)
    for k in range(2):
        pltpu.make_async_copy(h_ref, xb_ref.at[pl.ds(0, tm * ROW_CHUNKS)], sem.at[k]).wait()


def _dispatch(h, dest1, dest2, pad_end, counts, n_used, n_slots):
    t_all = h.shape[0] // ROW_CHUNKS
    tm = ROW_TILE
    grid_spec = pltpu.PrefetchScalarGridSpec(
        num_scalar_prefetch=5,
        grid=(t_all // tm,),
        in_specs=[pl.BlockSpec((tm * ROW_CHUNKS, LANES), lambda i, *_: (i, 0))],
        out_specs=pl.BlockSpec(memory_space=pl.ANY),
        scratch_shapes=[
            pltpu.VMEM((EXPERT_BLOCK * ROW_CHUNKS, LANES), F32),
            pltpu.SemaphoreType.DMA(()),
            pltpu.SemaphoreType.DMA((2,)),
        ],
    )
    return pl.pallas_call(
        _dispatch_kernel,
        grid_spec=grid_spec,
        out_shape=jax.ShapeDtypeStruct((n_slots * ROW_CHUNKS, LANES), F32),
        compiler_params=_params(("arbitrary",)),
    )(dest1, dest2, pad_end, counts, n_used, h)


def _ffn_kernel(layer, cstart_ref, cbig_ref, cexp_ref, next_ref, meta_ref,
                xb_ref, wg_ref, wu_ref, wd_ref, yb_ref,
                x_in, y_out, wg_f, wu_f, wd_f, wg_b, wu_b, wd_b, x_scr, xsem, ysem, wsem):
    blk_rows = EXPERT_BLOCK * ROW_CHUNKS
    n_blocks = yb_ref.shape[0] // blk_rows
    n_chunks, first_expert, n_used = meta_ref[0], meta_ref[1], meta_ref[2]

    def chunk_blocks(big):
        return CHUNK_BLOCKS if big else 1

    def hbm_rows(ref, j, big):
        start = pl.multiple_of(cstart_ref[j] * blk_rows, blk_rows)
        return ref.at[pl.ds(start, chunk_blocks(big) * blk_rows)]

    def x_copy(j, s, big):
        return pltpu.make_async_copy(hbm_rows(xb_ref, j, big),
                                     x_in.at[s, pl.ds(0, chunk_blocks(big) * blk_rows)], xsem.at[s])

    def y_copy(j, s, big):
        return pltpu.make_async_copy(y_out.at[s, pl.ds(0, chunk_blocks(big) * blk_rows)],
                                     hbm_rows(yb_ref, j, big), ysem.at[s])

    def by_size(j, fn):
        for big in (True, False):
            @pl.when(cbig_ref[j] == int(big))
            def _():
                fn(big)

    def w_copies(e, s):
        return (pltpu.make_async_copy(wg_ref.at[layer, e], wg_f.at[s], wsem.at[s, 0]),
                pltpu.make_async_copy(wu_ref.at[layer, e], wu_f.at[s], wsem.at[s, 1]),
                pltpu.make_async_copy(wd_ref.at[layer, e], wd_f.at[s], wsem.at[s, 2]))

    for c in w_copies(first_expert, 0):
        c.start(priority=1)
    by_size(0, lambda big: x_copy(0, 0, big).start())

    def chunk(j, wslot):
        e = cexp_ref[j]
        new_expert = (j == 0) | (cexp_ref[jnp.maximum(j - 1, 0)] != e)
        wslot = jnp.where(new_expert & (j > 0), 1 - wslot, wslot)

        @pl.when(new_expert)
        def _():
            for c in w_copies(e, wslot):
                c.wait()

            @pl.when(next_ref[e] < N_EXPERTS)
            def _():
                for c in w_copies(next_ref[e], 1 - wslot):
                    c.start(priority=1)

            wg_b[...] = wg_f[wslot].astype(BF16)
            wu_b[...] = wu_f[wslot].astype(BF16)
            wd_b[...] = wd_f[wslot].astype(BF16)

        s = j % 2
        by_size(j, lambda big: x_copy(j, s, big).wait())

        @pl.when(j + 1 < n_chunks)
        def _():
            by_size(j + 1, lambda big: x_copy(j + 1, 1 - s, big).start())

        @pl.when(j >= 2)
        def _():
            by_size(j - 2, lambda big: y_copy(j - 2, s, big).wait())

        def compute(big):
            rows = chunk_blocks(big) * EXPERT_BLOCK
            x_tok = x_in.at[s]
            for c in range(ROW_CHUNKS):
                x_scr[0:rows, c * LANES:(c + 1) * LANES] = (
                    x_tok[pl.ds(c, rows, stride=ROW_CHUNKS), :].astype(BF16))
            x = x_scr[0:rows, :]
            act = (_silu(_dot(x, wg_b[...])) * _dot(x, wu_b[...])).astype(BF16)
            _store_token_major(y_out.at[s, pl.ds(0, rows * ROW_CHUNKS)], _dot(act, wd_b[...]))
            y_copy(j, s, big).start()

        by_size(j, compute)
        return wslot

    lax.fori_loop(0, n_chunks, chunk, 0)

    @pl.when(n_chunks >= 2)
    def _():
        by_size(n_chunks - 2, lambda big: y_copy(n_chunks - 2, n_chunks % 2, big).wait())

    by_size(n_chunks - 1, lambda big: y_copy(n_chunks - 1, (n_chunks - 1) % 2, big).wait())

    zero_rows = y_out.at[0, pl.ds(0, blk_rows)]
    zero_rows[...] = jnp.zeros((blk_rows, LANES), F32)

    def tail_copy(b):
        start = pl.multiple_of(b * blk_rows, blk_rows)
        return pltpu.make_async_copy(zero_rows, yb_ref.at[pl.ds(start, blk_rows)], ysem.at[0])

    lax.fori_loop(n_used, n_blocks, lambda b, c: (tail_copy(b).start(), c)[1], 0)
    lax.fori_loop(n_used, n_blocks, lambda b, c: (tail_copy(b).wait(), c)[1], 0)


def _chunk_plan(first_block, n_expert_blocks, n_blocks):
    experts = jnp.arange(N_EXPERTS, dtype=jnp.int32)
    n_big = n_expert_blocks // CHUNK_BLOCKS
    n_chunks_e = n_big + n_expert_blocks % CHUNK_BLOCKS
    chunk_end = jnp.cumsum(n_chunks_e)
    chunk_start = chunk_end - n_chunks_e
    j = jnp.arange(n_blocks, dtype=jnp.int32)
    e_j = jnp.minimum(jnp.sum((chunk_end[None, :] <= j[:, None]).astype(jnp.int32), axis=1), N_EXPERTS - 1)
    pick = (e_j[:, None] == experts[None, :]).astype(jnp.int32)

    def of_expert(v):
        return jnp.sum(pick * v[None, :], axis=1)

    c = j - of_expert(chunk_start)
    big = c < of_expert(n_big)
    start = of_expert(first_block) + jnp.where(big, CHUNK_BLOCKS * c,
                                                (CHUNK_BLOCKS - 1) * of_expert(n_big) + c)
    owner = jnp.where(n_chunks_e > 0, experts, N_EXPERTS)
    later = jnp.flip(lax.cummin(jnp.flip(owner)))
    next_owner = jnp.concatenate([later[1:], jnp.full((1,), N_EXPERTS, jnp.int32)])
    meta = jnp.stack([chunk_end[-1], later[0]]).astype(jnp.int32)
    return start.astype(jnp.int32), big.astype(jnp.int32), e_j, next_owner.astype(jnp.int32), meta


def _expert_ffn(xb, first_block, n_expert_blocks, n_used, wg, wu, wd, layer):
    blk_rows = EXPERT_BLOCK * ROW_CHUNKS
    n_blocks = xb.shape[0] // blk_rows
    cstart, cbig, cexp, next_owner, meta = _chunk_plan(first_block, n_expert_blocks, n_blocks)
    meta = jnp.concatenate([meta, n_used])
    any_spec = pl.BlockSpec(memory_space=pl.ANY)
    grid_spec = pltpu.PrefetchScalarGridSpec(
        num_scalar_prefetch=5,
        grid=(1,),
        in_specs=[any_spec] * 4,
        out_specs=any_spec,
        scratch_shapes=[
            pltpu.VMEM((2, CHUNK_BLOCKS * blk_rows, LANES), F32),
            pltpu.VMEM((2, CHUNK_BLOCKS * blk_rows, LANES), F32),
            pltpu.VMEM((2, D_MODEL, EXPERT_FF), F32),
            pltpu.VMEM((2, D_MODEL, EXPERT_FF), F32),
            pltpu.VMEM((2, EXPERT_FF, D_MODEL), F32),
            pltpu.VMEM((D_MODEL, EXPERT_FF), BF16),
            pltpu.VMEM((D_MODEL, EXPERT_FF), BF16),
            pltpu.VMEM((EXPERT_FF, D_MODEL), BF16),
            pltpu.VMEM((CHUNK_BLOCKS * EXPERT_BLOCK, D_MODEL), BF16),
            pltpu.SemaphoreType.DMA((2,)),
            pltpu.SemaphoreType.DMA((2,)),
            pltpu.SemaphoreType.DMA((2, 3)),
        ],
    )
    return pl.pallas_call(
        functools.partial(_ffn_kernel, layer),
        grid_spec=grid_spec,
        out_shape=jax.ShapeDtypeStruct(xb.shape, F32),
        compiler_params=_params(("arbitrary",)),
    )(cstart, cbig, cexp, next_owner, meta, xb, wg, wu, wd)


def _combine_kernel(n_prompt_tiles, alpha, d1_ref, d2_ref,
                    x1_ref, route_ref, g2p_ref, g2s_ref, lng_ref, lnb_ref, yb_ref,
                    xo_ref, buf_a, buf_b, sem):
    i = pl.program_id(0)
    n_tiles = pl.num_programs(0)
    tm = x1_ref.shape[0]
    slot = i % 2

    def gather_tile(tile, s):
        def issue(c, carry):
            for u in range(ISSUE_UNROLL):
                t = c * ISSUE_UNROLL + u
                g = tile * tm + t
                _token_copy(yb_ref, d1_ref[g], buf_a.at[s], t, sem.at[s, 0]).start(priority=0)
                _token_copy(yb_ref, d2_ref[g], buf_b.at[s], t, sem.at[s, 1]).start(priority=1)
            return carry

        lax.fori_loop(0, tm // ISSUE_UNROLL, issue, 0)

    @pl.when(i == 0)
    def _():
        gather_tile(0, 0)

    @pl.when(i + 1 < n_tiles)
    def _():
        gather_tile(i + 1, 1 - slot)

    whole = yb_ref.at[pl.ds(0, tm * ROW_CHUNKS)]
    pltpu.make_async_copy(whole, buf_a.at[slot], sem.at[slot, 0]).wait()
    pltpu.make_async_copy(whole, buf_b.at[slot], sem.at[slot, 1]).wait()
    route_t = jnp.concatenate(
        [route_ref[...], jnp.zeros((LANES - SUBLANES, tm), F32)], axis=0).T
    ffn = (route_t[:, 2:3] * _load_token_major(buf_a.at[slot], tm)
           + route_t[:, 3:4] * _load_token_major(buf_b.at[slot], tm))
    g2 = _pick_mod(i >= n_prompt_tiles, g2p_ref, g2s_ref)
    xo_ref[...] = _layer_norm(alpha * x1_ref[...] + g2 * ffn, lng_ref[...], lnb_ref[...])


def _combine(x1, route, yb, dest1, dest2, mod_p, mod_s, lng, lnb, layer, n_prompt, seq_len, alpha):
    t_all = x1.shape[0]
    tm = ROW_TILE
    npt = n_prompt // tm
    g2p, g2s = _mod_specs(layer, 5, tm, npt, seq_len // tm, mod_p.shape[1])
    grid_spec = pltpu.PrefetchScalarGridSpec(
        num_scalar_prefetch=2,
        grid=(t_all // tm,),
        in_specs=[
            pl.BlockSpec((tm, D_MODEL), lambda i, *_: (i, 0)),
            pl.BlockSpec((SUBLANES, tm), lambda i, *_: (0, i)),
            g2p, g2s,
            pl.BlockSpec((1, D_MODEL), lambda i, *_: (0, 0)),
            pl.BlockSpec((1, D_MODEL), lambda i, *_: (0, 0)),
            pl.BlockSpec(memory_space=pl.ANY),
        ],
        out_specs=pl.BlockSpec((tm, D_MODEL), lambda i, *_: (i, 0)),
        scratch_shapes=[
            pltpu.VMEM((2, tm * ROW_CHUNKS, LANES), F32),
            pltpu.VMEM((2, tm * ROW_CHUNKS, LANES), F32),
            pltpu.SemaphoreType.DMA((2, 2)),
        ],
    )
    return pl.pallas_call(
        functools.partial(_combine_kernel, npt, alpha),
        grid_spec=grid_spec,
        out_shape=jax.ShapeDtypeStruct((t_all, D_MODEL), F32),
        compiler_params=_params(("arbitrary",)),
    )(dest1, dest2, x1, route, mod_p, mod_s, lng, lnb, yb)


def _rope_tables(pos):
    inv = jnp.power(ROPE_THETA, -jnp.arange(ROT_HALF, dtype=F32) * (2.0 / ROT_DIM))
    ang = pos.astype(F32)[:, None] * inv[None, :]
    cos, sin = jnp.cos(ang), jnp.sin(ang)
    rest = HEAD_DIM - ROT_DIM
    n = pos.shape[0]
    cos_h = jnp.concatenate([cos, cos, jnp.ones((n, rest), F32)], axis=-1)
    sin_h = jnp.concatenate([-sin, sin, jnp.zeros((n, rest), F32)], axis=-1)
    reps = LANES // HEAD_DIM
    return jnp.tile(cos_h, (1, reps)), jnp.tile(sin_h, (1, reps))


def _permute_w_in(w_in):
    sizes = (ATTN_WIDTH, KV_WIDTH, KV_WIDTH, SSD_D_INNER, SSD_CONV_DIM, SSD_HEADS,
             SC_DIM, SC_DIM, SC_DIM, 3 * D_MODEL)
    offs = np.concatenate([[0], np.cumsum(sizes)])
    q, k, v, z, xbc, dt, scb, scc, scv, gates = (w_in[..., offs[n]:offs[n + 1]] for n in range(len(sizes)))
    main = jnp.concatenate([q, z, xbc, scb, scc, gates, scv, k, v], axis=-1).astype(BF16)
    dt = jnp.pad(dt, ((0, 0), (0, 0), (0, DT_PAD - SSD_HEADS))).astype(BF16)
    return main, dt


def _pad_lanes(v, width):
    return jnp.pad(v, ((0, 0), (0, width - v.shape[-1])))


def kernel(x_prompt, x_sample, c_prompt, c_sample, cache_attn_k, cache_attn_v, state_ssm, state_ssd_conv, state_short_conv, w_ada, b_ada, w_in, attn_sink, ssd_conv_w, ssd_conv_b, ssd_dt_bias, ssd_a_log, ssd_d, ssd_norm_w, sc_conv_w, w_pa, w_pb, w_pc, w_out, ln1_g, ln1_b, ln2_g, ln2_b, router_g_w, router_g_b, router_e_w, router_e_b, moe_w_gate, moe_w_up, moe_w_down):
    depth = w_in.shape[0]
    n_seq, seq_len, _ = x_prompt.shape
    n_dec, dec_len, _ = x_sample.shape
    wb = cache_attn_k.shape[2]
    past_len = 8192
    assert dec_len == DEC_SEQ and wb == WINDOW
    assert seq_len % ROW_TILE_IN == 0 and (n_dec * dec_len) % ROW_TILE_IN == 0
    n_prompt = n_seq * seq_len
    n_sample = n_dec * dec_len
    t_all = n_prompt + n_sample
    alpha = (2 * depth) ** 0.25

    mod = _modulation(jnp.concatenate([c_prompt, c_sample], axis=0), w_ada, b_ada)
    mod_p_all = mod[:, :n_seq].reshape(depth, n_seq, 1, 6 * D_MODEL)
    mod_s_all = jnp.repeat(mod[:, n_seq:], dec_len, axis=1).reshape(depth, 1, n_sample, 6 * D_MODEL)

    w_main_all, w_dt_all = _permute_w_in(w_in)
    cos_p, sin_p = _rope_tables(jnp.arange(seq_len, dtype=jnp.int32))
    cos_s, sin_s = _rope_tables(past_len + (jnp.arange(MIX_ROWS, dtype=jnp.int32) % dec_len))

    a_total = 2 * t_all
    n_blocks = (a_total + N_EXPERTS * (EXPERT_BLOCK - 1)) // EXPERT_BLOCK
    n_slots = n_blocks * EXPERT_BLOCK

    mod_p, mod_s = mod_p_all, mod_s_all
    cache_k = cache_attn_k.reshape(depth, n_dec, wb, KV_WIDTH)
    cache_v = cache_attn_v.reshape(depth, n_dec, wb, KV_WIDTH)
    h0 = state_ssm.reshape(depth, n_dec, SSD_D_INNER, SSD_STATE)
    xbuf = jnp.pad(state_ssd_conv, ((0, 0), (0, 0), (dec_len - (SSD_CONV - 1), 0), (0, 0))
                   ).reshape(depth, n_sample, SSD_CONV_DIM)
    cbuf = jnp.pad(state_short_conv, ((0, 0), (0, 0), (dec_len - (SC_WIDTH - 1), 0), (0, 0))
                   ).reshape(depth, n_sample, SC_DIM)
    wpa, wpb, wpc, wout = (w.astype(BF16) for w in (w_pa, w_pb, w_pc, w_out))
    gap = ROUTE_EXPERT_ROW - MOE_GROUPS
    tail = LANES - ROUTE_EXPERT_ROW - N_EXPERTS
    wr = jnp.concatenate([router_g_w, jnp.zeros((depth, D_MODEL, gap), F32), router_e_w,
                          jnp.zeros((depth, D_MODEL, tail), F32)], axis=-1).astype(BF16)
    br = jnp.concatenate([router_g_b, jnp.zeros((depth, gap), F32), router_e_b,
                          jnp.zeros((depth, tail), F32)], axis=-1)[:, None, :]

    x = jnp.concatenate([x_prompt.reshape(n_prompt, D_MODEL), x_sample.reshape(n_sample, D_MODEL)], axis=0)
    outs_p = [[] for _ in range(5)]
    outs_s = [[] for _ in range(5)]
    for l in range(depth):
        p, dt = _input_projection(x, mod_p, mod_s, w_main_all, w_dt_all, l, n_prompt, seq_len)

        cw, cb = ssd_conv_w[l], ssd_conv_b[l][None]
        dtb = _pad_lanes(ssd_dt_bias[l][None], DT_PAD)
        alog = _pad_lanes(ssd_a_log[l][None], DT_PAD)
        de = jnp.repeat(ssd_d[l], SSD_HEAD_DIM)[None]
        nw = ssd_norm_w[l][None]
        scw = sc_conv_w[l]

        ya_p, krot_p = _attention_prompt(p, attn_sink[l], cos_p, sin_p, n_seq, seq_len)
        yb_p, yc_p, h_p, cv_p = _ssd_prompt(p, dt, cw, cb, dtb, alog, de, nw, scw, n_seq, seq_len)

        ya_s, yb_s, yc_s, k_s, v_s, h_s, cv_s = _mix_sample(
            p, dt, attn_sink[l], cos_s, sin_s, cache_k, cache_v, h0, xbuf, cbuf,
            cw, cb, dtb, alog, de, nw, scw, l, n_prompt, n_dec)

        x1, h, route, counts = _output_projection(
            x, (ya_p, yb_p, yc_p), (ya_s, yb_s, yc_s), p, mod_p, mod_s,
            wpa, wpb, wpc, wout, wr, br, ln1_g[l][None], ln1_b[l][None], l, n_prompt, seq_len, alpha)

        cnt = counts[ROUTE_EXPERT_ROW:ROUTE_EXPERT_ROW + N_EXPERTS, 0].astype(jnp.int32)
        pad_cnt = (cnt + EXPERT_BLOCK - 1) // EXPERT_BLOCK * EXPERT_BLOCK
        pad_end = jnp.cumsum(pad_cnt)
        pad_start = pad_end - pad_cnt
        n_used = (pad_end[-1:] // EXPERT_BLOCK).astype(jnp.int32)
        dest1, dest2 = _slots(route, pad_start)

        xb = _dispatch(h, dest1, dest2, pad_end.astype(jnp.int32), cnt, n_used, n_slots)
        yb_slots = _expert_ffn(xb, (pad_start // EXPERT_BLOCK).astype(jnp.int32),
                               (pad_cnt // EXPERT_BLOCK).astype(jnp.int32), n_used,
                               moe_w_gate, moe_w_up, moe_w_down, l)
        x = _combine(x1, route, yb_slots, dest1, dest2, mod_p, mod_s, ln2_g[l][None], ln2_b[l][None],
                     l, n_prompt, seq_len, alpha)

        def prompt_tail(rows, c0, c1):
            return jnp.stack([p[(b + 1) * seq_len - rows:(b + 1) * seq_len, c0:c1] for b in range(n_seq)]
                             ).astype(F32)

        outs_p[0].append(krot_p.reshape(n_seq, wb, N_KV_HEADS, HEAD_DIM))
        outs_p[1].append(prompt_tail(wb, COL_KV + KV_WIDTH, COL_KV + 2 * KV_WIDTH)
                         .reshape(n_seq, wb, N_KV_HEADS, HEAD_DIM))
        outs_p[2].append(h_p.reshape(n_seq, SSD_HEADS, SSD_HEAD_DIM, SSD_STATE))
        outs_p[3].append(prompt_tail(SSD_CONV - 1, COL_XBC, COL_XBC + SSD_CONV_DIM))
        outs_p[4].append(cv_p[:, SUBLANES - (SC_WIDTH - 1):, :])
        outs_s[0].append(k_s.reshape(n_dec, wb, N_KV_HEADS, HEAD_DIM))
        outs_s[1].append(v_s.reshape(n_dec, wb, N_KV_HEADS, HEAD_DIM))
        outs_s[2].append(h_s.reshape(n_dec, SSD_HEADS, SSD_HEAD_DIM, SSD_STATE))
        outs_s[3].append(p[n_prompt:, COL_XBC:COL_XBC + SSD_CONV_DIM].astype(F32)
                         .reshape(n_dec, dec_len, SSD_CONV_DIM)[:, dec_len - (SSD_CONV - 1):, :])
        outs_s[4].append(cv_s.reshape(n_dec, dec_len, SC_DIM)[:, dec_len - (SC_WIDTH - 1):, :])

    y_prompt = x[:n_prompt].reshape(n_seq, seq_len, D_MODEL)
    y_sample = x[n_prompt:].reshape(n_dec, dec_len, D_MODEL)
    return (y_prompt, y_sample, *[jnp.stack(o) for o in outs_p], *[jnp.stack(o) for o in outs_s])
```

```python
import functools

import jax
import jax.numpy as jnp
import numpy as np
from jax import lax
from jax.experimental import pallas as pl
from jax.experimental.pallas import tpu as pltpu

F32 = jnp.float32
BF16 = jnp.bfloat16

D_MODEL = 1024
HEAD_DIM = 64
N_HEADS = 8
N_KV_HEADS = 2
Q_PER_KV = N_HEADS // N_KV_HEADS
ATTN_WIDTH = N_HEADS * HEAD_DIM
KV_WIDTH = N_KV_HEADS * HEAD_DIM
WINDOW = 128
ROT_DIM = HEAD_DIM // 4
ROT_HALF = ROT_DIM // 2
ROPE_THETA = 500000.0
ATTN_SCALE = HEAD_DIM ** -0.5
SSD_D_INNER = 512
SSD_HEAD_DIM = 64
SSD_HEADS = 8
SSD_GROUPS = 2
SSD_HEADS_PER_GROUP = SSD_HEADS // SSD_GROUPS
SSD_STATE = 128
SSD_CONV = 4
SSD_CONV_DIM = SSD_D_INNER + 2 * SSD_GROUPS * SSD_STATE
SSD_CHUNK = 128
SC_DIM = 512
SC_WIDTH = 3
MOE_GROUPS = 4
EXPERTS_PER_GROUP = 8
N_EXPERTS = MOE_GROUPS * EXPERTS_PER_GROUP
EXPERT_FF = 512
LN_EPS = 1e-5
RMS_EPS = 1e-5

SUBLANES = 8
LANES = 128
VMEM_LIMIT = 56 * 1024 * 1024

COL_Q = 0
COL_Z = 512
COL_XBC = 1024
COL_SCB = 2048
COL_SCC = 2560
COL_GATES = 3072
COL_SCV = 6144
COL_KV = 6656
P_MAIN = 6912
DT_PAD = 128

ROW_TILE_IN = 1024
COL_TILE_IN = 2304
ROW_TILE = 512
MIX_ROWS = 128
DEC_SEQ = 8
SEQS_PER_STEP = MIX_ROWS // DEC_SEQ
EXPERT_BLOCK = 128
CHUNK_BLOCKS = 8
ROW_CHUNKS = D_MODEL // LANES
ISSUE_UNROLL = 8
ROUTE_EXPERT_ROW = 8
ROUTE_ROWS = 48


def _silu(v):
    return v * jax.nn.sigmoid(v)


def _dot(a, b):
    return jnp.dot(a, b, preferred_element_type=F32)


def _dot_nt(a, b):
    return lax.dot_general(a, b, (((1,), (1,)), ((), ())), preferred_element_type=F32)


def _dot_exact(a, b):
    return jnp.dot(a, b, preferred_element_type=F32, precision=lax.Precision.HIGHEST)


def _params(sem):
    return pltpu.CompilerParams(dimension_semantics=sem, vmem_limit_bytes=VMEM_LIMIT)


def _pick_mod(is_sample, prompt_ref, sample_ref):
    return jnp.where(is_sample, sample_ref[0], prompt_ref[0])


def _layer_norm(v, g, b):
    mu = jnp.mean(v, axis=-1, keepdims=True)
    c = v - mu
    var = jnp.mean(c * c, axis=-1, keepdims=True)
    return c * lax.rsqrt(var + LN_EPS) * g + b


def _mod_kernel(c_ref, w_ref, b_ref, o_ref):
    s = _silu(c_ref[...]).astype(BF16)
    o_ref[0] = _dot(s, w_ref[0].astype(BF16)) + b_ref[0]


def _modulation(c_all, w_ada, b_ada):
    depth, _, width = w_ada.shape
    n = c_all.shape[0]
    tn = 1536
    return pl.pallas_call(
        _mod_kernel,
        grid=(depth, width // tn),
        in_specs=[
            pl.BlockSpec((n, D_MODEL), lambda l, j: (0, 0)),
            pl.BlockSpec((1, D_MODEL, tn), lambda l, j: (l, 0, j)),
            pl.BlockSpec((1, 1, tn), lambda l, j: (l, 0, j)),
        ],
        out_specs=pl.BlockSpec((1, n, tn), lambda l, j: (l, 0, j)),
        out_shape=jax.ShapeDtypeStruct((depth, n, width), F32),
        compiler_params=_params(("parallel", "parallel")),
    )(c_all, w_ada, b_ada.reshape(depth, 1, width))


def _inproj_kernel(n_prompt_tiles, x_ref, shp_ref, shs_ref, scp_ref, scs_ref, w_ref, wdt_ref,
                   p_ref, dt_ref, u_scr):
    i = pl.program_id(0)
    j = pl.program_id(1)

    @pl.when(j == 0)
    def _():
        is_sample = i >= n_prompt_tiles
        sh = _pick_mod(is_sample, shp_ref, shs_ref)
        sc = _pick_mod(is_sample, scp_ref, scs_ref)
        u = (x_ref[...] * (1.0 + sc) + sh).astype(BF16)
        u_scr[...] = u
        dt_ref[...] = _dot(u, wdt_ref[...])

    p_ref[...] = _dot(u_scr[...], w_ref[...]).astype(BF16)


def _mod_specs(layer, col, row_tile, n_prompt_tiles, tiles_per_seq, n_seq):
    def prompt_map(i, *_):
        return (layer, jnp.minimum(i // tiles_per_seq, n_seq - 1), 0, col)

    def sample_map(i, *_):
        return (layer, 0, jnp.maximum(i - n_prompt_tiles, 0), col)

    return (pl.BlockSpec((None, 1, 1, D_MODEL), prompt_map),
            pl.BlockSpec((None, 1, row_tile, D_MODEL), sample_map))


def _layer_spec(layer, shape):
    return pl.BlockSpec((None, *shape), lambda *_: (layer,) + (0,) * len(shape))


def _input_projection(x, mod_p, mod_s, w_main, w_dt, layer, n_prompt, seq_len):
    t_all = x.shape[0]
    tm, tn = ROW_TILE_IN, COL_TILE_IN
    npt = n_prompt // tm
    n_seq = mod_p.shape[1]
    shp, shs = _mod_specs(layer, 0, tm, npt, seq_len // tm, n_seq)
    scp, scs = _mod_specs(layer, 1, tm, npt, seq_len // tm, n_seq)
    return pl.pallas_call(
        functools.partial(_inproj_kernel, npt),
        grid=(t_all // tm, P_MAIN // tn),
        in_specs=[
            pl.BlockSpec((tm, D_MODEL), lambda i, j: (i, 0)),
            shp, shs, scp, scs,
            pl.BlockSpec((None, D_MODEL, tn), lambda i, j: (layer, 0, j)),
            _layer_spec(layer, (D_MODEL, DT_PAD)),
        ],
        out_specs=[
            pl.BlockSpec((tm, tn), lambda i, j: (i, j)),
            pl.BlockSpec((tm, DT_PAD), lambda i, j: (i, 0)),
        ],
        out_shape=[
            jax.ShapeDtypeStruct((t_all, P_MAIN), BF16),
            jax.ShapeDtypeStruct((t_all, DT_PAD), F32),
        ],
        scratch_shapes=[pltpu.VMEM((tm, D_MODEL), BF16)],
        compiler_params=_params(("parallel", "arbitrary")),
    )(x, mod_p, mod_s, mod_p, mod_s, w_main, w_dt)


def _rope(v, cos, sin):
    width = v.shape[-1]
    reps = width // LANES
    if reps > 1:
        cos = jnp.concatenate([cos] * reps, axis=-1)
        sin = jnp.concatenate([sin] * reps, axis=-1)
    lane = lax.broadcasted_iota(jnp.int32, v.shape, 1) % HEAD_DIM
    partner = jnp.where(lane < ROT_HALF,
                        pltpu.roll(v, width - ROT_HALF, 1),
                        pltpu.roll(v, ROT_HALF, 1))
    return v * cos + partner * sin


def _shift_rows(cur, prev, k):
    axis = cur.ndim - 2
    idx = lax.broadcasted_iota(jnp.int32, cur.shape, axis)
    return jnp.where(idx < k, pltpu.roll(prev, k, axis), pltpu.roll(cur, k, axis))


def _causal_conv(cur, prev, w_ref, width):
    out = cur * w_ref[width - 1:width, :]
    for k in range(1, width):
        out = out + _shift_rows(cur, prev, k) * w_ref[width - 1 - k:width - k, :]
    return out


def _head_expand():
    r = lax.broadcasted_iota(jnp.int32, (LANES, SSD_D_INNER), 0)
    c = lax.broadcasted_iota(jnp.int32, (LANES, SSD_D_INNER), 1)
    return (c // SSD_HEAD_DIM == r).astype(F32)


def _head_expand_t():
    r = lax.broadcasted_iota(jnp.int32, (SSD_D_INNER, LANES), 0)
    c = lax.broadcasted_iota(jnp.int32, (SSD_D_INNER, LANES), 1)
    return (r // SSD_HEAD_DIM == c).astype(F32)


def _ssd_tile(act, dt_raw, dtb, alog, seq_rows):
    rows = MIX_ROWS
    xs = act[:, :SSD_D_INNER]
    bm = act[:, SSD_D_INNER:SSD_D_INNER + SSD_GROUPS * SSD_STATE]
    cm = act[:, SSD_D_INNER + SSD_GROUPS * SSD_STATE:]
    v = dt_raw + dtb
    dt = jnp.maximum(v, 0.0) + jnp.log1p(jnp.exp(-jnp.abs(v)))
    a = -jnp.exp(alog)
    dta = dt * a
    ri = lax.broadcasted_iota(jnp.int32, (rows, rows), 0)
    ci = lax.broadcasted_iota(jnp.int32, (rows, rows), 1)
    same = (ri // seq_rows) == (ci // seq_rows)
    causal = same & (ci <= ri)
    cs = _dot_exact(causal.astype(F32), dta)
    expand = _head_expand().astype(BF16)

    def per_head_lanes(v):
        hi = v.astype(BF16)
        lo = (v - hi.astype(F32)).astype(BF16)
        return _dot(hi, expand) + _dot(lo, expand)

    dt_e = per_head_lanes(dt)
    cs_e = per_head_lanes(cs)
    if seq_rows == rows:
        tot = None
        tot_e = cs_e[rows - 1:rows, :]
    else:
        tot = _dot_exact(same.astype(F32), dta)
        tot_e = per_head_lanes(tot)
    cs_t = cs.T
    dtx = xs * dt_e
    xw = dtx * jnp.exp(tot_e - cs_e)
    b_groups, c_groups, y_parts = [], [], []
    for g in range(SSD_GROUPS):
        bg = bm[:, g * SSD_STATE:(g + 1) * SSD_STATE].astype(BF16)
        cg = cm[:, g * SSD_STATE:(g + 1) * SSD_STATE].astype(BF16)
        b_groups.append(bg)
        c_groups.append(cg)
        cb = _dot_nt(cg, bg)
        for hh in range(SSD_HEADS_PER_GROUP):
            h = g * SSD_HEADS_PER_GROUP + hh
            seg = cs[:, h:h + 1] - cs_t[h:h + 1, :]
            decay = jnp.where(causal, jnp.exp(jnp.where(causal, seg, 0.0)), 0.0)
            y_parts.append(_dot((cb * decay).astype(BF16),
                                dtx[:, h * SSD_HEAD_DIM:(h + 1) * SSD_HEAD_DIM].astype(BF16)))
    y_diag = jnp.concatenate(y_parts, axis=-1)
    return xs, y_diag, jnp.exp(cs_e), xw, tot, tot_e, bm, b_groups, c_groups


def _gated_group_norm(y, z, nw):
    y = y * _silu(z)
    half = SSD_D_INNER // SSD_GROUPS
    parts = []
    for g in range(SSD_GROUPS):
        yg = y[:, g * half:(g + 1) * half]
        parts.append(yg * lax.rsqrt(jnp.mean(yg * yg, axis=-1, keepdims=True) + RMS_EPS))
    return jnp.concatenate(parts, axis=-1) * nw


def _attn_prompt_kernel(sink_ref, q_ref, kvc_ref, kvp_ref, cosc_ref, sinc_ref, cosp_ref, sinp_ref, bias_ref,
                        ya_ref, krot_ref):
    i = pl.program_id(1)
    nb = pl.num_programs(1)
    w = WINDOW
    q = _rope(q_ref[...].astype(F32), cosc_ref[...], sinc_ref[...])
    kvc = kvc_ref[...].astype(F32)
    kvp = kvp_ref[...].astype(F32)
    kc = _rope(kvc[:, :KV_WIDTH], cosc_ref[...], sinc_ref[...])
    kp = _rope(kvp[:, :KV_WIDTH], cosp_ref[...], sinp_ref[...])
    vc = kvc[:, KV_WIDTH:]
    vp = kvp[:, KV_WIDTH:]

    @pl.when(i == nb - 1)
    def _():
        krot_ref[0] = kc

    rows = Q_PER_KV * w
    bias = bias_ref[...]
    rcol = lax.broadcasted_iota(jnp.int32, (rows, 1), 0)
    outs = []
    for kh in range(N_KV_HEADS):
        hs = slice(kh * HEAD_DIM, (kh + 1) * HEAD_DIM)
        k2 = jnp.concatenate([kp[:, hs], kc[:, hs]], axis=0).astype(BF16)
        v2 = jnp.concatenate([vp[:, hs], vc[:, hs]], axis=0).astype(BF16)
        qg = jnp.concatenate(
            [q[:, (kh * Q_PER_KV + g) * HEAD_DIM:(kh * Q_PER_KV + g + 1) * HEAD_DIM] for g in range(Q_PER_KV)],
            axis=0).astype(BF16)
        sink = jnp.zeros((rows, 1), F32)
        for g in range(Q_PER_KV):
            sink = jnp.where(rcol // w == g, sink_ref[kh * Q_PER_KV + g], sink)
        logits = _dot_nt(qg, k2) * ATTN_SCALE + bias
        m = jnp.maximum(jnp.max(logits, axis=-1, keepdims=True), sink)
        e = jnp.exp(logits - m)
        den = jnp.sum(e, axis=-1, keepdims=True) + jnp.exp(sink - m)
        o = _dot(e.astype(BF16), v2) / den
        for g in range(Q_PER_KV):
            outs.append(o[g * w:(g + 1) * w, :])
    ya_ref[...] = jnp.concatenate(outs, axis=-1).astype(BF16)


def _band_bias():
    w = WINDOW
    r = np.arange(Q_PER_KV * w)[:, None] % w
    s = np.arange(2 * w)[None, :]
    diff = w + r - s
    band = (diff >= 0) & (diff < w)
    return jnp.asarray(np.where(np.stack([band & (s >= w), band]), 0.0, -np.inf), F32)


def _attention_prompt(p, sink, cos_p, sin_p, n_seq, seq_len):
    nb = seq_len // WINDOW
    n_prompt = n_seq * seq_len
    w = WINDOW

    def cur(b, i):
        return b * nb + i

    def prev(b, i):
        return jnp.maximum(b * nb + i - 1, 0)

    return pl.pallas_call(
        _attn_prompt_kernel,
        grid=(n_seq, nb),
        in_specs=[
            pl.BlockSpec(memory_space=pltpu.SMEM),
            pl.BlockSpec((w, ATTN_WIDTH), lambda b, i: (cur(b, i), COL_Q // ATTN_WIDTH)),
            pl.BlockSpec((w, 2 * KV_WIDTH), lambda b, i: (cur(b, i), COL_KV // (2 * KV_WIDTH))),
            pl.BlockSpec((w, 2 * KV_WIDTH), lambda b, i: (prev(b, i), COL_KV // (2 * KV_WIDTH))),
            pl.BlockSpec((w, LANES), lambda b, i: (i, 0)),
            pl.BlockSpec((w, LANES), lambda b, i: (i, 0)),
            pl.BlockSpec((w, LANES), lambda b, i: (jnp.maximum(i - 1, 0), 0)),
            pl.BlockSpec((w, LANES), lambda b, i: (jnp.maximum(i - 1, 0), 0)),
            pl.BlockSpec((None, Q_PER_KV * w, 2 * w), lambda b, i: (jnp.minimum(i, 1), 0, 0)),
        ],
        out_specs=[
            pl.BlockSpec((w, ATTN_WIDTH), lambda b, i: (cur(b, i), 0)),
            pl.BlockSpec((1, w, KV_WIDTH), lambda b, i: (b, 0, 0)),
        ],
        out_shape=[
            jax.ShapeDtypeStruct((n_prompt, ATTN_WIDTH), BF16),
            jax.ShapeDtypeStruct((n_seq, w, KV_WIDTH), F32),
        ],
        compiler_params=_params(("parallel", "arbitrary")),
    )(sink, p, p, p, cos_p, sin_p, cos_p, sin_p, _band_bias())


def _ssd_prompt_kernel(z_ref, xc_ref, xp_ref, dt_ref, scb_ref, sccc_ref, sccp_ref, scvc_ref, scvp_ref,
                       cw_ref, cb_ref, dtb_ref, alog_ref, de_ref, nw_ref, scw_ref,
                       yb_ref, yc_ref, hout_ref, cvlast_ref, h_scr):
    i = pl.program_id(1)
    nc = pl.num_programs(1)
    first = i == 0

    @pl.when(first)
    def _():
        h_scr[...] = jnp.zeros_like(h_scr)

    xc = xc_ref[...].astype(F32)
    xp = jnp.where(first, 0.0, xp_ref[...].astype(F32))
    act = _silu(_causal_conv(xc, xp, cw_ref, SSD_CONV) + cb_ref[...])
    xs, y_diag, ecs_e, xw, _, tot_e, bm, _, c_groups = _ssd_tile(
        act, dt_ref[...], dtb_ref[...], alog_ref[...], MIX_ROWS)
    gw = SSD_HEADS_PER_GROUP * SSD_HEAD_DIM
    y_off = []
    for g in range(SSD_GROUPS):
        cols = slice(g * gw, (g + 1) * gw)
        hg = h_scr[:, cols]
        y_off.append(_dot(c_groups[g], hg.astype(BF16)))
        b_t = bm[:, g * SSD_STATE:(g + 1) * SSD_STATE].T.astype(BF16)
        h_scr[:, cols] = jnp.exp(tot_e[:, cols]) * hg + _dot(b_t, xw[:, cols].astype(BF16))
    y = y_diag + jnp.concatenate(y_off, axis=-1) * ecs_e + de_ref[...] * xs
    yb_ref[...] = _gated_group_norm(y, z_ref[...].astype(F32), nw_ref[...]).astype(BF16)

    cvc = sccc_ref[...].astype(F32) * scvc_ref[...].astype(F32)
    cvp = jnp.where(first, 0.0, sccp_ref[...].astype(F32) * scvp_ref[...].astype(F32))
    conv_c = _causal_conv(cvc, cvp, scw_ref, SC_WIDTH)
    yc_ref[...] = (scb_ref[...].astype(F32) * conv_c).astype(BF16)

    @pl.when(i == nc - 1)
    def _():
        hout_ref[0] = h_scr[...].T
        cvlast_ref[0] = cvc[MIX_ROWS - SUBLANES:, :]


def _ssd_prompt(p, dt, cw, cb, dtb, alog, de, nw, scw, n_seq, seq_len):
    nc = seq_len // MIX_ROWS
    r = MIX_ROWS
    n_prompt = n_seq * seq_len

    def cur(b, i):
        return b * nc + i

    def prev(b, i):
        return jnp.maximum(b * nc + i - 1, 0)

    def col(width, offset, which):
        return pl.BlockSpec((r, width), lambda b, i: (which(b, i), offset // width))

    def const(shape):
        return pl.BlockSpec(shape, lambda b, i: (0,) * len(shape))

    return pl.pallas_call(
        _ssd_prompt_kernel,
        grid=(n_seq, nc),
        in_specs=[
            col(SSD_D_INNER, COL_Z, cur),
            col(SSD_CONV_DIM, COL_XBC, cur), col(SSD_CONV_DIM, COL_XBC, prev),
            pl.BlockSpec((r, DT_PAD), lambda b, i: (cur(b, i), 0)),
            col(SC_DIM, COL_SCB, cur),
            col(SC_DIM, COL_SCC, cur), col(SC_DIM, COL_SCC, prev),
            col(SC_DIM, COL_SCV, cur), col(SC_DIM, COL_SCV, prev),
            const((SSD_CONV, SSD_CONV_DIM)), const((1, SSD_CONV_DIM)),
            const((1, DT_PAD)), const((1, DT_PAD)), const((1, SSD_D_INNER)), const((1, SSD_D_INNER)),
            const((SC_WIDTH, SC_DIM)),
        ],
        out_specs=[
            pl.BlockSpec((r, SSD_D_INNER), lambda b, i: (cur(b, i), 0)),
            pl.BlockSpec((r, SC_DIM), lambda b, i: (cur(b, i), 0)),
            pl.BlockSpec((1, SSD_D_INNER, SSD_STATE), lambda b, i: (b, 0, 0)),
            pl.BlockSpec((1, SUBLANES, SC_DIM), lambda b, i: (b, 0, 0)),
        ],
        out_shape=[
            jax.ShapeDtypeStruct((n_prompt, SSD_D_INNER), BF16),
            jax.ShapeDtypeStruct((n_prompt, SC_DIM), BF16),
            jax.ShapeDtypeStruct((n_seq, SSD_D_INNER, SSD_STATE), F32),
            jax.ShapeDtypeStruct((n_seq, SUBLANES, SC_DIM), F32),
        ],
        scratch_shapes=[pltpu.VMEM((SSD_STATE, SSD_D_INNER), F32)],
        compiler_params=_params(("parallel", "arbitrary")),
    )(p, p, p, dt, p, p, p, p, p, cw, cb, dtb, alog, de, nw, scw)


def _mix_sample_kernel(sink_ref, q_ref, kv_ref, z_ref, x_ref, dt_ref, scb_ref, scc_ref, scv_ref,
                       cos_ref, sin_ref, ck_ref, cv_ref, h0_ref, xbuf_ref, cbuf_ref,
                       cw_ref, cb_ref, dtb_ref, alog_ref, de_ref, nw_ref, scw_ref,
                       ya_ref, yb_ref, yc_ref, knew_ref, vnew_ref, hnew_ref, cvout_ref):
    ns, t = SEQS_PER_STEP, DEC_SEQ
    wb = ck_ref.shape[1]

    q = _rope(q_ref[...].astype(F32), cos_ref[...], sin_ref[...])
    kv = kv_ref[...].astype(F32)
    kn = _rope(kv[:, :KV_WIDTH], cos_ref[...], sin_ref[...])
    vn = kv[:, KV_WIDTH:]
    q3 = q.reshape(ns, t, ATTN_WIDTH)
    kn3 = kn.reshape(ns, t, KV_WIDTH)
    vn3 = vn.reshape(ns, t, KV_WIDTH)
    knew_ref[:, :wb - t, :] = ck_ref[:, t:, :]
    knew_ref[:, wb - t:, :] = kn3
    vnew_ref[:, :wb - t, :] = cv_ref[:, t:, :]
    vnew_ref[:, wb - t:, :] = vn3
    nq = Q_PER_KV * t
    qi = lax.broadcasted_iota(jnp.int32, (ns, nq, wb + t), 1) % t
    si = lax.broadcasted_iota(jnp.int32, (ns, nq, wb + t), 2)
    valid = ((si < wb) & (si > qi + (wb - WINDOW))) | ((si >= wb) & (si - wb <= qi))
    hrow = lax.broadcasted_iota(jnp.int32, (ns, nq, 1), 1) // t
    heads = [None] * N_HEADS
    for kh in range(N_KV_HEADS):
        hs = slice(kh * HEAD_DIM, (kh + 1) * HEAD_DIM)
        k_all = jnp.concatenate([ck_ref[:, :, hs], kn3[:, :, hs]], axis=1).astype(BF16)
        v_all = jnp.concatenate([cv_ref[:, :, hs], vn3[:, :, hs]], axis=1).astype(BF16)
        qg = jnp.concatenate(
            [q3[:, :, (kh * Q_PER_KV + g) * HEAD_DIM:(kh * Q_PER_KV + g + 1) * HEAD_DIM] for g in range(Q_PER_KV)],
            axis=1).astype(BF16)
        sink = jnp.zeros((ns, nq, 1), F32)
        for g in range(Q_PER_KV):
            sink = jnp.where(hrow == g, sink_ref[kh * Q_PER_KV + g], sink)
        logits = jnp.einsum('bqd,bsd->bqs', qg, k_all, preferred_element_type=F32) * ATTN_SCALE
        logits = jnp.where(valid, logits, -jnp.inf)
        m = jnp.maximum(jnp.max(logits, axis=-1, keepdims=True), sink)
        e = jnp.exp(logits - m)
        den = jnp.sum(e, axis=-1, keepdims=True) + jnp.exp(sink - m)
        o = jnp.einsum('bqs,bsd->bqd', e.astype(BF16), v_all, preferred_element_type=F32) / den
        for g in range(Q_PER_KV):
            heads[kh * Q_PER_KV + g] = o[:, g * t:(g + 1) * t, :]
    ya_ref[...] = jnp.concatenate(heads, axis=-1).reshape(ns * t, ATTN_WIDTH).astype(BF16)

    xc3 = x_ref[...].astype(F32).reshape(ns, t, SSD_CONV_DIM)
    xp3 = xbuf_ref[...].reshape(ns, t, SSD_CONV_DIM)
    conv = _causal_conv(xc3, xp3, cw_ref, SSD_CONV).reshape(ns * t, SSD_CONV_DIM)
    act = _silu(conv + cb_ref[...])
    xs, y_diag, ecs_e, xw, tot, _, _, b_groups, c_groups = _ssd_tile(
        act, dt_ref[...], dtb_ref[...], alog_ref[...], t)
    xw_t = xw.T
    dec_t = jnp.exp(_dot_exact(_head_expand_t(), tot.T))
    gw = SSD_HEADS_PER_GROUP * SSD_HEAD_DIM
    y_off = []
    for g in range(SSD_GROUPS):
        c3 = c_groups[g].reshape(ns, t, SSD_STATE)
        hg = h0_ref[:, g * gw:(g + 1) * gw, :].astype(BF16)
        y_off.append(jnp.einsum('btn,bqn->btq', c3, hg, preferred_element_type=F32).reshape(ns * t, gw))
    y = y_diag + jnp.concatenate(y_off, axis=-1) * ecs_e + de_ref[...] * xs
    yb_ref[...] = _gated_group_norm(y, z_ref[...].astype(F32), nw_ref[...]).astype(BF16)
    col = lax.broadcasted_iota(jnp.int32, xw_t.shape, 1) // t
    for b in range(ns):
        xw_b = jnp.where(col == b, xw_t, jnp.zeros_like(xw_t))
        upd = jnp.concatenate(
            [_dot(xw_b[g * gw:(g + 1) * gw, :].astype(BF16), b_groups[g]) for g in range(SSD_GROUPS)], axis=0)
        hnew_ref[b] = dec_t[:, b * t:b * t + 1] * h0_ref[b] + upd

    cvc = scc_ref[...].astype(F32) * scv_ref[...].astype(F32)
    cvout_ref[...] = cvc
    conv_c = _causal_conv(cvc.reshape(ns, t, SC_DIM), cbuf_ref[...].reshape(ns, t, SC_DIM), scw_ref, SC_WIDTH)
    yc_ref[...] = (scb_ref[...].astype(F32) * conv_c.reshape(ns * t, SC_DIM)).astype(BF16)


def _mix_sample(p, dt, sink, cos_s, sin_s, cache_k, cache_v, h0, xbuf, cbuf,
                cw, cb, dtb, alog, de, nw, scw, layer, n_prompt, n_dec):
    r = MIX_ROWS
    ns = SEQS_PER_STEP
    steps = n_dec // ns
    base = n_prompt // r
    wb = cache_k.shape[2]

    def col(width, offset):
        return pl.BlockSpec((r, width), lambda i: (base + i, offset // width))

    def state(shape):
        return pl.BlockSpec((None, *shape), lambda i: (layer, i) + (0,) * (len(shape) - 1))

    def const(shape):
        return pl.BlockSpec(shape, lambda i: (0,) * len(shape))

    def rows(width):
        return pl.BlockSpec((r, width), lambda i: (i, 0))

    n_rows = n_dec * DEC_SEQ
    return pl.pallas_call(
        _mix_sample_kernel,
        grid=(steps,),
        in_specs=[
            pl.BlockSpec(memory_space=pltpu.SMEM),
            col(ATTN_WIDTH, COL_Q), col(2 * KV_WIDTH, COL_KV), col(SSD_D_INNER, COL_Z),
            col(SSD_CONV_DIM, COL_XBC),
            pl.BlockSpec((r, DT_PAD), lambda i: (base + i, 0)),
            col(SC_DIM, COL_SCB), col(SC_DIM, COL_SCC), col(SC_DIM, COL_SCV),
            const((r, LANES)), const((r, LANES)),
            state((ns, wb, KV_WIDTH)), state((ns, wb, KV_WIDTH)),
            state((ns, SSD_D_INNER, SSD_STATE)),
            state((r, SSD_CONV_DIM)), state((r, SC_DIM)),
            const((SSD_CONV, SSD_CONV_DIM)), const((1, SSD_CONV_DIM)),
            const((1, DT_PAD)), const((1, DT_PAD)), const((1, SSD_D_INNER)), const((1, SSD_D_INNER)),
            const((SC_WIDTH, SC_DIM)),
        ],
        out_specs=[
            rows(ATTN_WIDTH), rows(SSD_D_INNER), rows(SC_DIM),
            pl.BlockSpec((ns, wb, KV_WIDTH), lambda i: (i, 0, 0)),
            pl.BlockSpec((ns, wb, KV_WIDTH), lambda i: (i, 0, 0)),
            pl.BlockSpec((ns, SSD_D_INNER, SSD_STATE), lambda i: (i, 0, 0)),
            rows(SC_DIM),
        ],
        out_shape=[
            jax.ShapeDtypeStruct((n_rows, ATTN_WIDTH), BF16),
            jax.ShapeDtypeStruct((n_rows, SSD_D_INNER), BF16),
            jax.ShapeDtypeStruct((n_rows, SC_DIM), BF16),
            jax.ShapeDtypeStruct((n_dec, wb, KV_WIDTH), F32),
            jax.ShapeDtypeStruct((n_dec, wb, KV_WIDTH), F32),
            jax.ShapeDtypeStruct((n_dec, SSD_D_INNER, SSD_STATE), F32),
            jax.ShapeDtypeStruct((n_rows, SC_DIM), F32),
        ],
        compiler_params=_params(("parallel",)),
    )(sink, p, p, p, p, dt, p, p, p, cos_s, sin_s, cache_k, cache_v, h0, xbuf, cbuf,
      cw, cb, dtb, alog, de, nw, scw)


def _store_token_major(ref, v):
    n = v.shape[0]
    for s in range(ROW_CHUNKS):
        ref[pl.ds(s, n, stride=ROW_CHUNKS), :] = v[:, s * LANES:(s + 1) * LANES]


def _load_token_major(ref, n):
    return jnp.concatenate([ref[pl.ds(s, n, stride=ROW_CHUNKS), :] for s in range(ROW_CHUNKS)], axis=-1)


def _outproj_kernel(n_prompt_tiles, alpha,
                    x_ref, yap_ref, yas_ref, ybp_ref, ybs_ref, ycp_ref, ycs_ref, g_ref,
                    g1p_ref, g1s_ref, sh2p_ref, sh2s_ref, sc2p_ref, sc2s_ref,
                    wpa_ref, wpb_ref, wpc_ref, wout_ref, wr_ref, br_ref, lng_ref, lnb_ref,
                    x1_ref, h_ref, route_ref, cnt_ref, cnt_scr):
    i = pl.program_id(0)
    is_sample = i >= n_prompt_tiles
    tm = x_ref.shape[0]

    @pl.when(i == 0)
    def _():
        cnt_scr[...] = jnp.zeros_like(cnt_scr)

    n = tm
    cnt = cnt_scr[...]
    for r0 in range(0, tm, n):
        rs = slice(r0, r0 + n)

        def pick(prompt_ref, sample_ref):
            return jnp.where(is_sample, sample_ref[0, rs, :], prompt_ref[0])

        ya = jnp.where(is_sample, yas_ref[rs, :], yap_ref[rs, :])
        yb = jnp.where(is_sample, ybs_ref[rs, :], ybp_ref[rs, :])
        yc = jnp.where(is_sample, ycs_ref[rs, :], ycp_ref[rs, :])
        gates = 0.5 * jnp.tanh(0.5 * g_ref[rs, :]) + 0.5
        merged = (gates[:, :D_MODEL] * _dot(ya, wpa_ref[...]).astype(BF16)
                  + gates[:, D_MODEL:2 * D_MODEL] * _dot(yb, wpb_ref[...]).astype(BF16)
                  + gates[:, 2 * D_MODEL:] * _dot(yc, wpc_ref[...]).astype(BF16))
        mix = _dot(merged, wout_ref[...])
        g1 = pick(g1p_ref, g1s_ref)
        x1 = _layer_norm(alpha * x_ref[rs, :] + g1 * mix, lng_ref[...], lnb_ref[...])
        x1_ref[rs, :] = x1
        sh2 = pick(sh2p_ref, sh2s_ref)
        sc2 = pick(sc2p_ref, sc2s_ref)
        h = x1 * (1.0 + sc2) + sh2
        _store_token_major(h_ref.at[pl.ds(r0 * ROW_CHUNKS, n * ROW_CHUNKS)], h)

        lt = (_dot(h.astype(BF16), wr_ref[...]) + br_ref[...]).T[:ROUTE_ROWS]
        row = lax.broadcasted_iota(jnp.int32, lt.shape, 0).astype(F32)
        neg = -jnp.inf
        big = float(ROUTE_ROWS)
        gl = jnp.where(row < MOE_GROUPS, lt, neg)
        gmax = jnp.max(gl, axis=0, keepdims=True)
        g_p = 1.0 / jnp.sum(jnp.exp(gl - gmax), axis=0, keepdims=True)
        gidx = jnp.min(jnp.where(gl == gmax, row, big), axis=0, keepdims=True)
        lo = ROUTE_EXPERT_ROW + EXPERTS_PER_GROUP * gidx
        sel = jnp.where((row >= lo) & (row < lo + EXPERTS_PER_GROUP), lt, neg)
        m1 = jnp.max(sel, axis=0, keepdims=True)
        i1 = jnp.min(jnp.where(sel == m1, row, big), axis=0, keepdims=True)
        sel2 = jnp.where(row == i1, neg, sel)
        m2 = jnp.max(sel2, axis=0, keepdims=True)
        i2 = jnp.min(jnp.where(sel2 == m2, row, big), axis=0, keepdims=True)
        ssum = jnp.sum(jnp.exp(sel - m1), axis=0, keepdims=True)
        p1 = 1.0 / ssum
        p2 = jnp.exp(m2 - m1) / ssum
        w1 = g_p * (p1 / (p1 + p2))
        w2 = g_p * (p2 / (p1 + p2))

        onehot = jnp.where((row == i1) | (row == i2), 1.0, 0.0)
        ri = lax.broadcasted_iota(jnp.int32, (n, n), 0)
        ci = lax.broadcasted_iota(jnp.int32, (n, n), 1)
        earlier = jnp.where(ri < ci, 1.0, 0.0).astype(BF16)
        prefix = _dot(onehot.astype(BF16), earlier) + cnt[:, 0:1]
        rank1 = jnp.sum(jnp.where(row == i1, prefix, 0.0), axis=0, keepdims=True)
        rank2 = jnp.sum(jnp.where(row == i2, prefix, 0.0), axis=0, keepdims=True)
        cnt = cnt + jnp.sum(onehot, axis=1, keepdims=True)
        zero = jnp.zeros_like(w1)
        route_ref[:, rs] = jnp.concatenate(
            [i1 - ROUTE_EXPERT_ROW, i2 - ROUTE_EXPERT_ROW, w1, w2, rank1, rank2, zero, zero], axis=0)

    cnt_scr[...] = cnt
    cnt_ref[...] = cnt


def _output_projection(x, y_prompt, y_sample, p, mod_p, mod_s, wpa, wpb, wpc, wout, wr, br, lng, lnb,
                       layer, n_prompt, seq_len, alpha):
    t_all = x.shape[0]
    tm = ROW_TILE
    npt = n_prompt // tm
    n_seq = mod_p.shape[1]
    mods = []
    for col in (2, 3, 4):
        mods.extend(_mod_specs(layer, col, tm, npt, seq_len // tm, n_seq))

    def rows(width):
        return pl.BlockSpec((tm, width), lambda i: (i, 0))

    def prompt_rows(width):
        return pl.BlockSpec((tm, width), lambda i: (jnp.minimum(i, npt - 1), 0))

    def sample_rows(width):
        return pl.BlockSpec((tm, width), lambda i: (jnp.maximum(i - npt, 0), 0))

    def const(shape):
        return pl.BlockSpec(shape, lambda i: (0,) * len(shape))

    return pl.pallas_call(
        functools.partial(_outproj_kernel, npt, alpha),
        grid=(t_all // tm,),
        in_specs=[
            rows(D_MODEL),
            prompt_rows(ATTN_WIDTH), sample_rows(ATTN_WIDTH),
            prompt_rows(SSD_D_INNER), sample_rows(SSD_D_INNER),
            prompt_rows(SC_DIM), sample_rows(SC_DIM),
            pl.BlockSpec((tm, 3 * D_MODEL), lambda i: (i, COL_GATES // (3 * D_MODEL))),
            *mods,
            _layer_spec(layer, (ATTN_WIDTH, D_MODEL)), _layer_spec(layer, (SSD_D_INNER, D_MODEL)),
            _layer_spec(layer, (SC_DIM, D_MODEL)), _layer_spec(layer, (D_MODEL, D_MODEL)),
            _layer_spec(layer, (D_MODEL, LANES)), _layer_spec(layer, (1, LANES)),
            const((1, D_MODEL)), const((1, D_MODEL)),
        ],
        out_specs=[
            rows(D_MODEL),
            pl.BlockSpec((tm * ROW_CHUNKS, LANES), lambda i: (i, 0)),
            pl.BlockSpec((SUBLANES, tm), lambda i: (0, i)),
            const((ROUTE_ROWS, LANES)),
        ],
        out_shape=[
            jax.ShapeDtypeStruct((t_all, D_MODEL), F32),
            jax.ShapeDtypeStruct((t_all * ROW_CHUNKS, LANES), F32),
            jax.ShapeDtypeStruct((SUBLANES, t_all), F32),
            jax.ShapeDtypeStruct((ROUTE_ROWS, LANES), F32),
        ],
        scratch_shapes=[pltpu.VMEM((ROUTE_ROWS, LANES), F32)],
        compiler_params=_params(("arbitrary",)),
    )(x, y_prompt[0], y_sample[0], y_prompt[1], y_sample[1], y_prompt[2], y_sample[2],
      p, mod_p, mod_s, mod_p, mod_s, mod_p, mod_s,
      wpa, wpb, wpc, wout, wr, br, lng, lnb)


def _slots_kernel(route_ref, start_ref, dest_ref):
    route = route_ref[...]
    tm = route.shape[1]
    expert = lax.broadcasted_iota(jnp.int32, (N_EXPERTS, tm), 0).astype(F32)
    start = start_ref[:, 0:1]
    rows = []
    for e_row, r_row in ((0, 4), (1, 5)):
        first = jnp.sum(jnp.where(expert == route[e_row:e_row + 1], start, 0.0), axis=0, keepdims=True)
        rows.append(first + route[r_row:r_row + 1])
    rows.append(jnp.zeros((SUBLANES - 2, tm), F32))
    dest_ref[...] = jnp.concatenate(rows, axis=0).astype(jnp.int32)


def _slots(route, pad_start):
    t_all = route.shape[1]
    tm = ROW_TILE_IN
    dest = pl.pallas_call(
        _slots_kernel,
        grid=(t_all // tm,),
        in_specs=[pl.BlockSpec((SUBLANES, tm), lambda i: (0, i)),
                  pl.BlockSpec((N_EXPERTS, LANES), lambda i: (0, 0))],
        out_specs=pl.BlockSpec((SUBLANES, tm), lambda i: (0, i)),
        out_shape=jax.ShapeDtypeStruct((SUBLANES, t_all), jnp.int32),
        compiler_params=_params(("parallel",)),
    )(route, jnp.broadcast_to(pad_start.astype(F32)[:, None], (N_EXPERTS, LANES)))
    return dest[0], dest[1]


def _token_copy(src, src_row, dst, dst_row, sem):
    return pltpu.make_async_copy(
        src.at[pl.ds(pl.multiple_of(src_row * ROW_CHUNKS, ROW_CHUNKS), ROW_CHUNKS)],
        dst.at[pl.ds(pl.multiple_of(dst_row * ROW_CHUNKS, ROW_CHUNKS), ROW_CHUNKS)],
        sem)


def _dispatch_kernel(d1_ref, d2_ref, pend_ref, cnt_ref, nb_ref, h_ref, xb_ref, zero_scr, zsem, sem):
    i = pl.program_id(0)
    tm = h_ref.shape[0] // ROW_CHUNKS
    blk_rows = EXPERT_BLOCK * ROW_CHUNKS
    n_blocks = xb_ref.shape[0] // blk_rows

    def zero_block(b):
        start = pl.multiple_of(b * blk_rows, blk_rows)
        return pltpu.make_async_copy(zero_scr, xb_ref.at[pl.ds(start, blk_rows)], zsem)

    def last_block(e):
        return pend_ref[e] // EXPERT_BLOCK - 1

    @pl.when(i == 0)
    def _():
        zero_scr[...] = jnp.zeros_like(zero_scr)
        for e in range(N_EXPERTS):
            @pl.when(cnt_ref[e] > 0)
            def _():
                zero_block(last_block(e)).start()
        lax.fori_loop(nb_ref[0], n_blocks, lambda b, c: (zero_block(b).start(), c)[1], 0)
        for e in range(N_EXPERTS):
            @pl.when(cnt_ref[e] > 0)
            def _():
                zero_block(last_block(e)).wait()
        lax.fori_loop(nb_ref[0], n_blocks, lambda b, c: (zero_block(b).wait(), c)[1], 0)

    def issue(c, carry):
        for u in range(ISSUE_UNROLL):
            t = c * ISSUE_UNROLL + u
            g = i * tm + t
            _token_copy(h_ref, t, xb_ref, d1_ref[g], sem.at[0]).start(priority=0)
            _token_copy(h_ref, t, xb_ref, d2_ref[g], sem.at[1]).start(priority=1)
        return carry

    lax.fori_loop(0, tm // ISSUE_UNROLL, issue, 0)
    for k in range(2):
        pltpu.make_async_copy(h_ref, xb_ref.at[pl.ds(0, tm * ROW_CHUNKS)], sem.at[k]).wait()


def _dispatch(h, dest1, dest2, pad_end, counts, n_used, n_slots):
    t_all = h.shape[0] // ROW_CHUNKS
    tm = ROW_TILE
    grid_spec = pltpu.PrefetchScalarGridSpec(
        num_scalar_prefetch=5,
        grid=(t_all // tm,),
        in_specs=[pl.BlockSpec((tm * ROW_CHUNKS, LANES), lambda i, *_: (i, 0))],
        out_specs=pl.BlockSpec(memory_space=pl.ANY),
        scratch_shapes=[
            pltpu.VMEM((EXPERT_BLOCK * ROW_CHUNKS, LANES), F32),
            pltpu.SemaphoreType.DMA(()),
            pltpu.SemaphoreType.DMA((2,)),
        ],
    )
    return pl.pallas_call(
        _dispatch_kernel,
        grid_spec=grid_spec,
        out_shape=jax.ShapeDtypeStruct((n_slots * ROW_CHUNKS, LANES), F32),
        compiler_params=_params(("arbitrary",)),
    )(dest1, dest2, pad_end, counts, n_used, h)


def _ffn_kernel(layer, cstart_ref, cbig_ref, cexp_ref, next_ref, meta_ref,
                xb_ref, wg_ref, wu_ref, wd_ref, yb_ref,
                x_in, y_out, wg_f, wu_f, wd_f, wg_b, wu_b, wd_b, x_scr, xsem, ysem, wsem):
    blk_rows = EXPERT_BLOCK * ROW_CHUNKS
    n_blocks = yb_ref.shape[0] // blk_rows
    n_chunks, first_expert, n_used = meta_ref[0], meta_ref[1], meta_ref[2]

    def chunk_blocks(big):
        return CHUNK_BLOCKS if big else 1

    def hbm_rows(ref, j, big):
        start = pl.multiple_of(cstart_ref[j] * blk_rows, blk_rows)
        return ref.at[pl.ds(start, chunk_blocks(big) * blk_rows)]

    def x_copy(j, s, big):
        return pltpu.make_async_copy(hbm_rows(xb_ref, j, big),
                                     x_in.at[s, pl.ds(0, chunk_blocks(big) * blk_rows)], xsem.at[s])

    def y_copy(j, s, big):
        return pltpu.make_async_copy(y_out.at[s, pl.ds(0, chunk_blocks(big) * blk_rows)],
                                     hbm_rows(yb_ref, j, big), ysem.at[s])

    def by_size(j, fn):
        for big in (True, False):
            @pl.when(cbig_ref[j] == int(big))
            def _():
                fn(big)

    def w_copies(e, s):
        return (pltpu.make_async_copy(wg_ref.at[layer, e], wg_f.at[s], wsem.at[s, 0]),
                pltpu.make_async_copy(wu_ref.at[layer, e], wu_f.at[s], wsem.at[s, 1]),
                pltpu.make_async_copy(wd_ref.at[layer, e], wd_f.at[s], wsem.at[s, 2]))

    for c in w_copies(first_expert, 0):
        c.start(priority=1)
    by_size(0, lambda big: x_copy(0, 0, big).start())

    def chunk(j, wslot):
        e = cexp_ref[j]
        new_expert = (j == 0) | (cexp_ref[jnp.maximum(j - 1, 0)] != e)
        wslot = jnp.where(new_expert & (j > 0), 1 - wslot, wslot)

        @pl.when(new_expert)
        def _():
            for c in w_copies(e, wslot):
                c.wait()

            @pl.when(next_ref[e] < N_EXPERTS)
            def _():
                for c in w_copies(next_ref[e], 1 - wslot):
                    c.start(priority=1)

            wg_b[...] = wg_f[wslot].astype(BF16)
            wu_b[...] = wu_f[wslot].astype(BF16)
            wd_b[...] = wd_f[wslot].astype(BF16)

        s = j % 2
        by_size(j, lambda big: x_copy(j, s, big).wait())

        @pl.when(j + 1 < n_chunks)
        def _():
            by_size(j + 1, lambda big: x_copy(j + 1, 1 - s, big).start())

        @pl.when(j >= 2)
        def _():
            by_size(j - 2, lambda big: y_copy(j - 2, s, big).wait())

        def compute(big):
            rows = chunk_blocks(big) * EXPERT_BLOCK
            x_tok = x_in.at[s]
            for c in range(ROW_CHUNKS):
                x_scr[0:rows, c * LANES:(c + 1) * LANES] = (
                    x_tok[pl.ds(c, rows, stride=ROW_CHUNKS), :].astype(BF16))
            x = x_scr[0:rows, :]
            act = (_silu(_dot(x, wg_b[...])) * _dot(x, wu_b[...])).astype(BF16)
            _store_token_major(y_out.at[s, pl.ds(0, rows * ROW_CHUNKS)], _dot(act, wd_b[...]))
            y_copy(j, s, big).start()

        by_size(j, compute)
        return wslot

    lax.fori_loop(0, n_chunks, chunk, 0)

    @pl.when(n_chunks >= 2)
    def _():
        by_size(n_chunks - 2, lambda big: y_copy(n_chunks - 2, n_chunks % 2, big).wait())

    by_size(n_chunks - 1, lambda big: y_copy(n_chunks - 1, (n_chunks - 1) % 2, big).wait())

    zero_rows = y_out.at[0, pl.ds(0, blk_rows)]
    zero_rows[...] = jnp.zeros((blk_rows, LANES), F32)

    def tail_copy(b):
        start = pl.multiple_of(b * blk_rows, blk_rows)
        return pltpu.make_async_copy(zero_rows, yb_ref.at[pl.ds(start, blk_rows)], ysem.at[0])

    lax.fori_loop(n_used, n_blocks, lambda b, c: (tail_copy(b).start(), c)[1], 0)
    lax.fori_loop(n_used, n_blocks, lambda b, c: (tail_copy(b).wait(), c)[1], 0)


def _chunk_plan(first_block, n_expert_blocks, n_blocks):
    experts = jnp.arange(N_EXPERTS, dtype=jnp.int32)
    n_big = n_expert_blocks // CHUNK_BLOCKS
    n_chunks_e = n_big + n_expert_blocks % CHUNK_BLOCKS
    chunk_end = jnp.cumsum(n_chunks_e)
    chunk_start = chunk_end - n_chunks_e
    j = jnp.arange(n_blocks, dtype=jnp.int32)
    e_j = jnp.minimum(jnp.sum((chunk_end[None, :] <= j[:, None]).astype(jnp.int32), axis=1), N_EXPERTS - 1)
    pick = (e_j[:, None] == experts[None, :]).astype(jnp.int32)

    def of_expert(v):
        return jnp.sum(pick * v[None, :], axis=1)

    c = j - of_expert(chunk_start)
    big = c < of_expert(n_big)
    start = of_expert(first_block) + jnp.where(big, CHUNK_BLOCKS * c,
                                                (CHUNK_BLOCKS - 1) * of_expert(n_big) + c)
    owner = jnp.where(n_chunks_e > 0, experts, N_EXPERTS)
    later = jnp.flip(lax.cummin(jnp.flip(owner)))
    next_owner = jnp.concatenate([later[1:], jnp.full((1,), N_EXPERTS, jnp.int32)])
    meta = jnp.stack([chunk_end[-1], later[0]]).astype(jnp.int32)
    return start.astype(jnp.int32), big.astype(jnp.int32), e_j, next_owner.astype(jnp.int32), meta


def _expert_ffn(xb, first_block, n_expert_blocks, n_used, wg, wu, wd, layer):
    blk_rows = EXPERT_BLOCK * ROW_CHUNKS
    n_blocks = xb.shape[0] // blk_rows
    cstart, cbig, cexp, next_owner, meta = _chunk_plan(first_block, n_expert_blocks, n_blocks)
    meta = jnp.concatenate([meta, n_used])
    any_spec = pl.BlockSpec(memory_space=pl.ANY)
    grid_spec = pltpu.PrefetchScalarGridSpec(
        num_scalar_prefetch=5,
        grid=(1,),
        in_specs=[any_spec] * 4,
        out_specs=any_spec,
        scratch_shapes=[
            pltpu.VMEM((2, CHUNK_BLOCKS * blk_rows, LANES), F32),
            pltpu.VMEM((2, CHUNK_BLOCKS * blk_rows, LANES), F32),
            pltpu.VMEM((2, D_MODEL, EXPERT_FF), F32),
            pltpu.VMEM((2, D_MODEL, EXPERT_FF), F32),
            pltpu.VMEM((2, EXPERT_FF, D_MODEL), F32),
            pltpu.VMEM((D_MODEL, EXPERT_FF), BF16),
            pltpu.VMEM((D_MODEL, EXPERT_FF), BF16),
            pltpu.VMEM((EXPERT_FF, D_MODEL), BF16),
            pltpu.VMEM((CHUNK_BLOCKS * EXPERT_BLOCK, D_MODEL), BF16),
            pltpu.SemaphoreType.DMA((2,)),
            pltpu.SemaphoreType.DMA((2,)),
            pltpu.SemaphoreType.DMA((2, 3)),
        ],
    )
    return pl.pallas_call(
        functools.partial(_ffn_kernel, layer),
        grid_spec=grid_spec,
        out_shape=jax.ShapeDtypeStruct(xb.shape, F32),
        compiler_params=_params(("arbitrary",)),
    )(cstart, cbig, cexp, next_owner, meta, xb, wg, wu, wd)


def _combine_kernel(n_prompt_tiles, alpha, d1_ref, d2_ref,
                    x1_ref, route_ref, g2p_ref, g2s_ref, lng_ref, lnb_ref, yb_ref, *outs_and_scratch):
    out_refs, (buf_a, buf_b, sem) = outs_and_scratch[:-3], outs_and_scratch[-3:]
    i = pl.program_id(0)
    n_tiles = pl.num_programs(0)
    tm = x1_ref.shape[0]
    slot = i % 2

    def gather_tile(tile, s):
        def issue(c, carry):
            for u in range(ISSUE_UNROLL):
                t = c * ISSUE_UNROLL + u
                g = tile * tm + t
                _token_copy(yb_ref, d1_ref[g], buf_a.at[s], t, sem.at[s, 0]).start(priority=0)
                _token_copy(yb_ref, d2_ref[g], buf_b.at[s], t, sem.at[s, 1]).start(priority=1)
            return carry

        lax.fori_loop(0, tm // ISSUE_UNROLL, issue, 0)

    @pl.when(i == 0)
    def _():
        gather_tile(0, 0)

    @pl.when(i + 1 < n_tiles)
    def _():
        gather_tile(i + 1, 1 - slot)

    whole = yb_ref.at[pl.ds(0, tm * ROW_CHUNKS)]
    pltpu.make_async_copy(whole, buf_a.at[slot], sem.at[slot, 0]).wait()
    pltpu.make_async_copy(whole, buf_b.at[slot], sem.at[slot, 1]).wait()
    route_t = jnp.concatenate(
        [route_ref[...], jnp.zeros((LANES - SUBLANES, tm), F32)], axis=0).T
    ffn = (route_t[:, 2:3] * _load_token_major(buf_a.at[slot], tm)
           + route_t[:, 3:4] * _load_token_major(buf_b.at[slot], tm))
    g2 = _pick_mod(i >= n_prompt_tiles, g2p_ref, g2s_ref)
    out = _layer_norm(alpha * x1_ref[...] + g2 * ffn, lng_ref[...], lnb_ref[...])
    if len(out_refs) == 1:
        out_refs[0][...] = out
    else:
        @pl.when(i < n_prompt_tiles)
        def _():
            out_refs[0][...] = out

        @pl.when(i >= n_prompt_tiles)
        def _():
            out_refs[1][...] = out


def _combine(x1, route, yb, dest1, dest2, mod_p, mod_s, lng, lnb, layer, n_prompt, seq_len, alpha, split):
    t_all = x1.shape[0]
    tm = ROW_TILE
    npt = n_prompt // tm
    if split:
        out_specs = [pl.BlockSpec((tm, D_MODEL), lambda i, *_: (jnp.minimum(i, npt - 1), 0)),
                     pl.BlockSpec((tm, D_MODEL), lambda i, *_: (jnp.maximum(i - npt, 0), 0))]
        out_shape = [jax.ShapeDtypeStruct((n_prompt, D_MODEL), F32),
                     jax.ShapeDtypeStruct((t_all - n_prompt, D_MODEL), F32)]
    else:
        out_specs = pl.BlockSpec((tm, D_MODEL), lambda i, *_: (i, 0))
        out_shape = jax.ShapeDtypeStruct((t_all, D_MODEL), F32)
    g2p, g2s = _mod_specs(layer, 5, tm, npt, seq_len // tm, mod_p.shape[1])
    grid_spec = pltpu.PrefetchScalarGridSpec(
        num_scalar_prefetch=2,
        grid=(t_all // tm,),
        in_specs=[
            pl.BlockSpec((tm, D_MODEL), lambda i, *_: (i, 0)),
            pl.BlockSpec((SUBLANES, tm), lambda i, *_: (0, i)),
            g2p, g2s,
            pl.BlockSpec((1, D_MODEL), lambda i, *_: (0, 0)),
            pl.BlockSpec((1, D_MODEL), lambda i, *_: (0, 0)),
            pl.BlockSpec(memory_space=pl.ANY),
        ],
        out_specs=out_specs,
        scratch_shapes=[
            pltpu.VMEM((2, tm * ROW_CHUNKS, LANES), F32),
            pltpu.VMEM((2, tm * ROW_CHUNKS, LANES), F32),
            pltpu.SemaphoreType.DMA((2, 2)),
        ],
    )
    return pl.pallas_call(
        functools.partial(_combine_kernel, npt, alpha),
        grid_spec=grid_spec,
        out_shape=out_shape,
        compiler_params=_params(("arbitrary",)),
    )(dest1, dest2, x1, route, mod_p, mod_s, lng, lnb, yb)


def _rope_tables(pos):
    inv = jnp.power(ROPE_THETA, -jnp.arange(ROT_HALF, dtype=F32) * (2.0 / ROT_DIM))
    ang = pos.astype(F32)[:, None] * inv[None, :]
    cos, sin = jnp.cos(ang), jnp.sin(ang)
    rest = HEAD_DIM - ROT_DIM
    n = pos.shape[0]
    cos_h = jnp.concatenate([cos, cos, jnp.ones((n, rest), F32)], axis=-1)
    sin_h = jnp.concatenate([-sin, sin, jnp.zeros((n, rest), F32)], axis=-1)
    reps = LANES // HEAD_DIM
    return jnp.tile(cos_h, (1, reps)), jnp.tile(sin_h, (1, reps))


def _permute_w_in(w_in):
    sizes = (ATTN_WIDTH, KV_WIDTH, KV_WIDTH, SSD_D_INNER, SSD_CONV_DIM, SSD_HEADS,
             SC_DIM, SC_DIM, SC_DIM, 3 * D_MODEL)
    offs = np.concatenate([[0], np.cumsum(sizes)])
    q, k, v, z, xbc, dt, scb, scc, scv, gates = (w_in[..., offs[n]:offs[n + 1]] for n in range(len(sizes)))
    main = jnp.concatenate([q, z, xbc, scb, scc, gates, scv, k, v], axis=-1).astype(BF16)
    dt = jnp.pad(dt, ((0, 0), (0, 0), (0, DT_PAD - SSD_HEADS))).astype(BF16)
    return main, dt


def _pad_lanes(v, width):
    return jnp.pad(v, ((0, 0), (0, width - v.shape[-1])))


def kernel(x_prompt, x_sample, c_prompt, c_sample, cache_attn_k, cache_attn_v, state_ssm, state_ssd_conv, state_short_conv, w_ada, b_ada, w_in, attn_sink, ssd_conv_w, ssd_conv_b, ssd_dt_bias, ssd_a_log, ssd_d, ssd_norm_w, sc_conv_w, w_pa, w_pb, w_pc, w_out, ln1_g, ln1_b, ln2_g, ln2_b, router_g_w, router_g_b, router_e_w, router_e_b, moe_w_gate, moe_w_up, moe_w_down):
    depth = w_in.shape[0]
    n_seq, seq_len, _ = x_prompt.shape
    n_dec, dec_len, _ = x_sample.shape
    wb = cache_attn_k.shape[2]
    past_len = 8192
    assert dec_len == DEC_SEQ and wb == WINDOW
    assert seq_len % ROW_TILE_IN == 0 and (n_dec * dec_len) % ROW_TILE_IN == 0
    n_prompt = n_seq * seq_len
    n_sample = n_dec * dec_len
    t_all = n_prompt + n_sample
    alpha = (2 * depth) ** 0.25

    mod = _modulation(jnp.concatenate([c_prompt, c_sample], axis=0), w_ada, b_ada)
    mod_p_all = mod[:, :n_seq].reshape(depth, n_seq, 1, 6 * D_MODEL)
    mod_s_all = jnp.repeat(mod[:, n_seq:], dec_len, axis=1).reshape(depth, 1, n_sample, 6 * D_MODEL)

    w_main_all, w_dt_all = _permute_w_in(w_in)
    cos_p, sin_p = _rope_tables(jnp.arange(seq_len, dtype=jnp.int32))
    cos_s, sin_s = _rope_tables(past_len + (jnp.arange(MIX_ROWS, dtype=jnp.int32) % dec_len))

    a_total = 2 * t_all
    n_blocks = (a_total + N_EXPERTS * (EXPERT_BLOCK - 1)) // EXPERT_BLOCK
    n_slots = n_blocks * EXPERT_BLOCK

    mod_p, mod_s = mod_p_all, mod_s_all
    cache_k = cache_attn_k.reshape(depth, n_dec, wb, KV_WIDTH)
    cache_v = cache_attn_v.reshape(depth, n_dec, wb, KV_WIDTH)
    h0 = state_ssm.reshape(depth, n_dec, SSD_D_INNER, SSD_STATE)
    xbuf = jnp.pad(state_ssd_conv, ((0, 0), (0, 0), (dec_len - (SSD_CONV - 1), 0), (0, 0))
                   ).reshape(depth, n_sample, SSD_CONV_DIM)
    cbuf = jnp.pad(state_short_conv, ((0, 0), (0, 0), (dec_len - (SC_WIDTH - 1), 0), (0, 0))
                   ).reshape(depth, n_sample, SC_DIM)
    wpa, wpb, wpc, wout = (w.astype(BF16) for w in (w_pa, w_pb, w_pc, w_out))
    gap = ROUTE_EXPERT_ROW - MOE_GROUPS
    tail = LANES - ROUTE_EXPERT_ROW - N_EXPERTS
    wr = jnp.concatenate([router_g_w, jnp.zeros((depth, D_MODEL, gap), F32), router_e_w,
                          jnp.zeros((depth, D_MODEL, tail), F32)], axis=-1).astype(BF16)
    br = jnp.concatenate([router_g_b, jnp.zeros((depth, gap), F32), router_e_b,
                          jnp.zeros((depth, tail), F32)], axis=-1)[:, None, :]

    x = jnp.concatenate([x_prompt.reshape(n_prompt, D_MODEL), x_sample.reshape(n_sample, D_MODEL)], axis=0)
    outs_p = [[] for _ in range(5)]
    outs_s = [[] for _ in range(5)]
    for l in range(depth):
        p, dt = _input_projection(x, mod_p, mod_s, w_main_all, w_dt_all, l, n_prompt, seq_len)

        cw, cb = ssd_conv_w[l], ssd_conv_b[l][None]
        dtb = _pad_lanes(ssd_dt_bias[l][None], DT_PAD)
        alog = _pad_lanes(ssd_a_log[l][None], DT_PAD)
        de = jnp.repeat(ssd_d[l], SSD_HEAD_DIM)[None]
        nw = ssd_norm_w[l][None]
        scw = sc_conv_w[l]

        ya_p, krot_p = _attention_prompt(p, attn_sink[l], cos_p, sin_p, n_seq, seq_len)
        yb_p, yc_p, h_p, cv_p = _ssd_prompt(p, dt, cw, cb, dtb, alog, de, nw, scw, n_seq, seq_len)

        ya_s, yb_s, yc_s, k_s, v_s, h_s, cv_s = _mix_sample(
            p, dt, attn_sink[l], cos_s, sin_s, cache_k, cache_v, h0, xbuf, cbuf,
            cw, cb, dtb, alog, de, nw, scw, l, n_prompt, n_dec)

        x1, h, route, counts = _output_projection(
            x, (ya_p, yb_p, yc_p), (ya_s, yb_s, yc_s), p, mod_p, mod_s,
            wpa, wpb, wpc, wout, wr, br, ln1_g[l][None], ln1_b[l][None], l, n_prompt, seq_len, alpha)

        cnt = counts[ROUTE_EXPERT_ROW:ROUTE_EXPERT_ROW + N_EXPERTS, 0].astype(jnp.int32)
        pad_cnt = (cnt + EXPERT_BLOCK - 1) // EXPERT_BLOCK * EXPERT_BLOCK
        pad_end = jnp.cumsum(pad_cnt)
        pad_start = pad_end - pad_cnt
        n_used = (pad_end[-1:] // EXPERT_BLOCK).astype(jnp.int32)
        dest1, dest2 = _slots(route, pad_start)

        xb = _dispatch(h, dest1, dest2, pad_end.astype(jnp.int32), cnt, n_used, n_slots)
        yb_slots = _expert_ffn(xb, (pad_start // EXPERT_BLOCK).astype(jnp.int32),
                               (pad_cnt // EXPERT_BLOCK).astype(jnp.int32), n_used,
                               moe_w_gate, moe_w_up, moe_w_down, l)
        x = _combine(x1, route, yb_slots, dest1, dest2, mod_p, mod_s, ln2_g[l][None], ln2_b[l][None],
                     l, n_prompt, seq_len, alpha, split=(l == depth - 1))

        def prompt_tail(rows, c0, c1):
            return jnp.stack([p[(b + 1) * seq_len - rows:(b + 1) * seq_len, c0:c1] for b in range(n_seq)]
                             ).astype(F32)

        outs_p[0].append(krot_p.reshape(n_seq, wb, N_KV_HEADS, HEAD_DIM))
        outs_p[1].append(prompt_tail(wb, COL_KV + KV_WIDTH, COL_KV + 2 * KV_WIDTH)
                         .reshape(n_seq, wb, N_KV_HEADS, HEAD_DIM))
        outs_p[2].append(h_p.reshape(n_seq, SSD_HEADS, SSD_HEAD_DIM, SSD_STATE))
        outs_p[3].append(prompt_tail(SSD_CONV - 1, COL_XBC, COL_XBC + SSD_CONV_DIM))
        outs_p[4].append(cv_p[:, SUBLANES - (SC_WIDTH - 1):, :])
        outs_s[0].append(k_s.reshape(n_dec, wb, N_KV_HEADS, HEAD_DIM))
        outs_s[1].append(v_s.reshape(n_dec, wb, N_KV_HEADS, HEAD_DIM))
        outs_s[2].append(h_s.reshape(n_dec, SSD_HEADS, SSD_HEAD_DIM, SSD_STATE))
        outs_s[3].append(p[n_prompt:, COL_XBC:COL_XBC + SSD_CONV_DIM].astype(F32)
                         .reshape(n_dec, dec_len, SSD_CONV_DIM)[:, dec_len - (SSD_CONV - 1):, :])
        outs_s[4].append(cv_s.reshape(n_dec, dec_len, SC_DIM)[:, dec_len - (SC_WIDTH - 1):, :])

    y_prompt = x[0].reshape(n_seq, seq_len, D_MODEL)
    y_sample = x[1].reshape(n_dec, dec_len, D_MODEL)
    return (y_prompt, y_sample, *[jnp.stack(o) for o in outs_p], *[jnp.stack(o) for o in outs_s])
```

```python
import functools

import jax
import jax.numpy as jnp
import numpy as np
from jax import lax
from jax.experimental import pallas as pl
from jax.experimental.pallas import tpu as pltpu

F32 = jnp.float32
BF16 = jnp.bfloat16

D_MODEL = 1024
HEAD_DIM = 64
N_HEADS = 8
N_KV_HEADS = 2
Q_PER_KV = N_HEADS // N_KV_HEADS
ATTN_WIDTH = N_HEADS * HEAD_DIM
KV_WIDTH = N_KV_HEADS * HEAD_DIM
WINDOW = 128
ROT_DIM = HEAD_DIM // 4
ROT_HALF = ROT_DIM // 2
ROPE_THETA = 500000.0
ATTN_SCALE = HEAD_DIM ** -0.5
SSD_D_INNER = 512
SSD_HEAD_DIM = 64
SSD_HEADS = 8
SSD_GROUPS = 2
SSD_HEADS_PER_GROUP = SSD_HEADS // SSD_GROUPS
SSD_STATE = 128
SSD_CONV = 4
SSD_CONV_DIM = SSD_D_INNER + 2 * SSD_GROUPS * SSD_STATE
SSD_CHUNK = 128
SC_DIM = 512
SC_WIDTH = 3
MOE_GROUPS = 4
EXPERTS_PER_GROUP = 8
N_EXPERTS = MOE_GROUPS * EXPERTS_PER_GROUP
EXPERT_FF = 512
LN_EPS = 1e-5
RMS_EPS = 1e-5

SUBLANES = 8
LANES = 128
VMEM_LIMIT = 56 * 1024 * 1024

COL_Q = 0
COL_Z = 512
COL_XBC = 1024
COL_SCB = 2048
COL_SCC = 2560
COL_GATES = 3072
COL_SCV = 6144
COL_KV = 6656
P_MAIN = 6912
DT_PAD = 128

ROW_TILE_IN = 1024
COL_TILE_IN = 2304
ROW_TILE = 512
MIX_ROWS = 128
DEC_SEQ = 8
SEQS_PER_STEP = MIX_ROWS // DEC_SEQ
EXPERT_BLOCK = 256
CHUNK_SIZES = (4, 2, 1)
CHUNK_BLOCKS = CHUNK_SIZES[0]
ROW_CHUNKS = D_MODEL // LANES
ISSUE_UNROLL = 8
ROUTE_EXPERT_ROW = 8
ROUTE_ROWS = 48


def _silu(v):
    return v * jax.nn.sigmoid(v)


def _dot(a, b):
    return jnp.dot(a, b, preferred_element_type=F32)


def _dot_nt(a, b):
    return lax.dot_general(a, b, (((1,), (1,)), ((), ())), preferred_element_type=F32)


def _dot_exact(a, b):
    return jnp.dot(a, b, preferred_element_type=F32, precision=lax.Precision.HIGHEST)


def _params(sem):
    return pltpu.CompilerParams(dimension_semantics=sem, vmem_limit_bytes=VMEM_LIMIT)


def _pick_mod(is_sample, prompt_ref, sample_ref):
    return jnp.where(is_sample, sample_ref[0], prompt_ref[0])


def _layer_norm(v, g, b):
    mu = jnp.mean(v, axis=-1, keepdims=True)
    c = v - mu
    var = jnp.mean(c * c, axis=-1, keepdims=True)
    return c * lax.rsqrt(var + LN_EPS) * g + b


def _mod_kernel(c_ref, w_ref, b_ref, o_ref):
    s = _silu(c_ref[...]).astype(BF16)
    o_ref[0] = _dot(s, w_ref[0].astype(BF16)) + b_ref[0]


def _modulation(c_all, w_ada, b_ada):
    depth, _, width = w_ada.shape
    n = c_all.shape[0]
    tn = 1536
    return pl.pallas_call(
        _mod_kernel,
        grid=(depth, width // tn),
        in_specs=[
            pl.BlockSpec((n, D_MODEL), lambda l, j: (0, 0)),
            pl.BlockSpec((1, D_MODEL, tn), lambda l, j: (l, 0, j)),
            pl.BlockSpec((1, 1, tn), lambda l, j: (l, 0, j)),
        ],
        out_specs=pl.BlockSpec((1, n, tn), lambda l, j: (l, 0, j)),
        out_shape=jax.ShapeDtypeStruct((depth, n, width), F32),
        compiler_params=_params(("parallel", "parallel")),
    )(c_all, w_ada, b_ada.reshape(depth, 1, width))


def _inproj_kernel(n_prompt_tiles, x_ref, shp_ref, shs_ref, scp_ref, scs_ref, w_ref, wdt_ref,
                   p_ref, dt_ref, u_scr):
    i = pl.program_id(0)
    j = pl.program_id(1)

    @pl.when(j == 0)
    def _():
        is_sample = i >= n_prompt_tiles
        sh = _pick_mod(is_sample, shp_ref, shs_ref)
        sc = _pick_mod(is_sample, scp_ref, scs_ref)
        u = (x_ref[...] * (1.0 + sc) + sh).astype(BF16)
        u_scr[...] = u
        dt_ref[...] = _dot(u, wdt_ref[...])

    p_ref[...] = _dot(u_scr[...], w_ref[...]).astype(BF16)


def _mod_specs(layer, col, row_tile, n_prompt_tiles, tiles_per_seq, n_seq):
    def prompt_map(i, *_):
        return (layer, jnp.minimum(i // tiles_per_seq, n_seq - 1), 0, col)

    def sample_map(i, *_):
        return (layer, 0, jnp.maximum(i - n_prompt_tiles, 0), col)

    return (pl.BlockSpec((None, 1, 1, D_MODEL), prompt_map),
            pl.BlockSpec((None, 1, row_tile, D_MODEL), sample_map))


def _layer_spec(layer, shape):
    return pl.BlockSpec((None, *shape), lambda *_: (layer,) + (0,) * len(shape))


def _input_projection(x, mod_p, mod_s, w_main, w_dt, layer, n_prompt, seq_len):
    t_all = x.shape[0]
    tm, tn = ROW_TILE_IN, COL_TILE_IN
    npt = n_prompt // tm
    n_seq = mod_p.shape[1]
    shp, shs = _mod_specs(layer, 0, tm, npt, seq_len // tm, n_seq)
    scp, scs = _mod_specs(layer, 1, tm, npt, seq_len // tm, n_seq)
    return pl.pallas_call(
        functools.partial(_inproj_kernel, npt),
        grid=(t_all // tm, P_MAIN // tn),
        in_specs=[
            pl.BlockSpec((tm, D_MODEL), lambda i, j: (i, 0)),
            shp, shs, scp, scs,
            pl.BlockSpec((None, D_MODEL, tn), lambda i, j: (layer, 0, j)),
            _layer_spec(layer, (D_MODEL, DT_PAD)),
        ],
        out_specs=[
            pl.BlockSpec((tm, tn), lambda i, j: (i, j)),
            pl.BlockSpec((tm, DT_PAD), lambda i, j: (i, 0)),
        ],
        out_shape=[
            jax.ShapeDtypeStruct((t_all, P_MAIN), BF16),
            jax.ShapeDtypeStruct((t_all, DT_PAD), F32),
        ],
        scratch_shapes=[pltpu.VMEM((tm, D_MODEL), BF16)],
        compiler_params=_params(("parallel", "arbitrary")),
    )(x, mod_p, mod_s, mod_p, mod_s, w_main, w_dt)


def _rope(v, cos, sin):
    width = v.shape[-1]
    reps = width // LANES
    if reps > 1:
        cos = jnp.concatenate([cos] * reps, axis=-1)
        sin = jnp.concatenate([sin] * reps, axis=-1)
    lane = lax.broadcasted_iota(jnp.int32, v.shape, 1) % HEAD_DIM
    partner = jnp.where(lane < ROT_HALF,
                        pltpu.roll(v, width - ROT_HALF, 1),
                        pltpu.roll(v, ROT_HALF, 1))
    return v * cos + partner * sin


def _shift_rows(cur, prev, k):
    axis = cur.ndim - 2
    idx = lax.broadcasted_iota(jnp.int32, cur.shape, axis)
    return jnp.where(idx < k, pltpu.roll(prev, k, axis), pltpu.roll(cur, k, axis))


def _causal_conv(cur, prev, w_ref, width):
    out = cur * w_ref[width - 1:width, :]
    for k in range(1, width):
        out = out + _shift_rows(cur, prev, k) * w_ref[width - 1 - k:width - k, :]
    return out


def _head_expand():
    r = lax.broadcasted_iota(jnp.int32, (LANES, SSD_D_INNER), 0)
    c = lax.broadcasted_iota(jnp.int32, (LANES, SSD_D_INNER), 1)
    return (c // SSD_HEAD_DIM == r).astype(F32)


def _head_expand_t():
    r = lax.broadcasted_iota(jnp.int32, (SSD_D_INNER, LANES), 0)
    c = lax.broadcasted_iota(jnp.int32, (SSD_D_INNER, LANES), 1)
    return (r // SSD_HEAD_DIM == c).astype(F32)


def _ssd_tile(act, dt_raw, dtb, alog, seq_rows):
    rows = MIX_ROWS
    xs = act[:, :SSD_D_INNER]
    bm = act[:, SSD_D_INNER:SSD_D_INNER + SSD_GROUPS * SSD_STATE]
    cm = act[:, SSD_D_INNER + SSD_GROUPS * SSD_STATE:]
    v = dt_raw + dtb
    dt = jnp.maximum(v, 0.0) + jnp.log1p(jnp.exp(-jnp.abs(v)))
    a = -jnp.exp(alog)
    dta = dt * a
    ri = lax.broadcasted_iota(jnp.int32, (rows, rows), 0)
    ci = lax.broadcasted_iota(jnp.int32, (rows, rows), 1)
    same = (ri // seq_rows) == (ci // seq_rows)
    causal = same & (ci <= ri)
    cs = _dot_exact(causal.astype(F32), dta)
    expand = _head_expand().astype(BF16)

    def per_head_lanes(v):
        hi = v.astype(BF16)
        lo = (v - hi.astype(F32)).astype(BF16)
        return _dot(hi, expand) + _dot(lo, expand)

    dt_e = per_head_lanes(dt)
    cs_e = per_head_lanes(cs)
    if seq_rows == rows:
        tot = None
        tot_e = cs_e[rows - 1:rows, :]
    else:
        tot = _dot_exact(same.astype(F32), dta)
        tot_e = per_head_lanes(tot)
    cs_t = cs.T
    dtx = xs * dt_e
    xw = dtx * jnp.exp(tot_e - cs_e)
    b_groups, c_groups, y_parts = [], [], []
    for g in range(SSD_GROUPS):
        bg = bm[:, g * SSD_STATE:(g + 1) * SSD_STATE].astype(BF16)
        cg = cm[:, g * SSD_STATE:(g + 1) * SSD_STATE].astype(BF16)
        b_groups.append(bg)
        c_groups.append(cg)
        cb = _dot_nt(cg, bg)
        for hh in range(SSD_HEADS_PER_GROUP):
            h = g * SSD_HEADS_PER_GROUP + hh
            seg = cs[:, h:h + 1] - cs_t[h:h + 1, :]
            decay = jnp.where(causal, jnp.exp(jnp.where(causal, seg, 0.0)), 0.0)
            y_parts.append(_dot((cb * decay).astype(BF16),
                                dtx[:, h * SSD_HEAD_DIM:(h + 1) * SSD_HEAD_DIM].astype(BF16)))
    y_diag = jnp.concatenate(y_parts, axis=-1)
    return xs, y_diag, jnp.exp(cs_e), xw, tot, tot_e, bm, b_groups, c_groups


def _gated_group_norm(y, z, nw):
    y = y * _silu(z)
    half = SSD_D_INNER // SSD_GROUPS
    parts = []
    for g in range(SSD_GROUPS):
        yg = y[:, g * half:(g + 1) * half]
        parts.append(yg * lax.rsqrt(jnp.mean(yg * yg, axis=-1, keepdims=True) + RMS_EPS))
    return jnp.concatenate(parts, axis=-1) * nw


def _attn_prompt_kernel(sink_ref, q_ref, kvc_ref, kvp_ref, cosc_ref, sinc_ref, cosp_ref, sinp_ref, bias_ref,
                        ya_ref, krot_ref):
    i = pl.program_id(1)
    nb = pl.num_programs(1)
    w = WINDOW
    q = _rope(q_ref[...].astype(F32), cosc_ref[...], sinc_ref[...])
    kvc = kvc_ref[...].astype(F32)
    kvp = kvp_ref[...].astype(F32)
    kc = _rope(kvc[:, :KV_WIDTH], cosc_ref[...], sinc_ref[...])
    kp = _rope(kvp[:, :KV_WIDTH], cosp_ref[...], sinp_ref[...])
    vc = kvc[:, KV_WIDTH:]
    vp = kvp[:, KV_WIDTH:]

    @pl.when(i == nb - 1)
    def _():
        krot_ref[0] = kc

    rows = Q_PER_KV * w
    bias = bias_ref[...]
    rcol = lax.broadcasted_iota(jnp.int32, (rows, 1), 0)
    outs = []
    for kh in range(N_KV_HEADS):
        hs = slice(kh * HEAD_DIM, (kh + 1) * HEAD_DIM)
        k2 = jnp.concatenate([kp[:, hs], kc[:, hs]], axis=0).astype(BF16)
        v2 = jnp.concatenate([vp[:, hs], vc[:, hs]], axis=0).astype(BF16)
        qg = jnp.concatenate(
            [q[:, (kh * Q_PER_KV + g) * HEAD_DIM:(kh * Q_PER_KV + g + 1) * HEAD_DIM] for g in range(Q_PER_KV)],
            axis=0).astype(BF16)
        sink = jnp.zeros((rows, 1), F32)
        for g in range(Q_PER_KV):
            sink = jnp.where(rcol // w == g, sink_ref[kh * Q_PER_KV + g], sink)
        logits = _dot_nt(qg, k2) * ATTN_SCALE + bias
        m = jnp.maximum(jnp.max(logits, axis=-1, keepdims=True), sink)
        e = jnp.exp(logits - m)
        den = jnp.sum(e, axis=-1, keepdims=True) + jnp.exp(sink - m)
        o = _dot(e.astype(BF16), v2) / den
        for g in range(Q_PER_KV):
            outs.append(o[g * w:(g + 1) * w, :])
    ya_ref[...] = jnp.concatenate(outs, axis=-1).astype(BF16)


def _band_bias():
    w = WINDOW
    r = np.arange(Q_PER_KV * w)[:, None] % w
    s = np.arange(2 * w)[None, :]
    diff = w + r - s
    band = (diff >= 0) & (diff < w)
    return jnp.asarray(np.where(np.stack([band & (s >= w), band]), 0.0, -np.inf), F32)


def _attention_prompt(p, sink, cos_p, sin_p, n_seq, seq_len):
    nb = seq_len // WINDOW
    n_prompt = n_seq * seq_len
    w = WINDOW

    def cur(b, i):
        return b * nb + i

    def prev(b, i):
        return jnp.maximum(b * nb + i - 1, 0)

    return pl.pallas_call(
        _attn_prompt_kernel,
        grid=(n_seq, nb),
        in_specs=[
            pl.BlockSpec(memory_space=pltpu.SMEM),
            pl.BlockSpec((w, ATTN_WIDTH), lambda b, i: (cur(b, i), COL_Q // ATTN_WIDTH)),
            pl.BlockSpec((w, 2 * KV_WIDTH), lambda b, i: (cur(b, i), COL_KV // (2 * KV_WIDTH))),
            pl.BlockSpec((w, 2 * KV_WIDTH), lambda b, i: (prev(b, i), COL_KV // (2 * KV_WIDTH))),
            pl.BlockSpec((w, LANES), lambda b, i: (i, 0)),
            pl.BlockSpec((w, LANES), lambda b, i: (i, 0)),
            pl.BlockSpec((w, LANES), lambda b, i: (jnp.maximum(i - 1, 0), 0)),
            pl.BlockSpec((w, LANES), lambda b, i: (jnp.maximum(i - 1, 0), 0)),
            pl.BlockSpec((None, Q_PER_KV * w, 2 * w), lambda b, i: (jnp.minimum(i, 1), 0, 0)),
        ],
        out_specs=[
            pl.BlockSpec((w, ATTN_WIDTH), lambda b, i: (cur(b, i), 0)),
            pl.BlockSpec((1, w, KV_WIDTH), lambda b, i: (b, 0, 0)),
        ],
        out_shape=[
            jax.ShapeDtypeStruct((n_prompt, ATTN_WIDTH), BF16),
            jax.ShapeDtypeStruct((n_seq, w, KV_WIDTH), F32),
        ],
        compiler_params=_params(("parallel", "arbitrary")),
    )(sink, p, p, p, cos_p, sin_p, cos_p, sin_p, _band_bias())


def _ssd_prompt_kernel(z_ref, xc_ref, xp_ref, dt_ref, scb_ref, sccc_ref, sccp_ref, scvc_ref, scvp_ref,
                       cw_ref, cb_ref, dtb_ref, alog_ref, de_ref, nw_ref, scw_ref,
                       yb_ref, yc_ref, hout_ref, cvlast_ref, h_scr):
    i = pl.program_id(1)
    nc = pl.num_programs(1)
    first = i == 0

    @pl.when(first)
    def _():
        h_scr[...] = jnp.zeros_like(h_scr)

    xc = xc_ref[...].astype(F32)
    xp = jnp.where(first, 0.0, xp_ref[...].astype(F32))
    act = _silu(_causal_conv(xc, xp, cw_ref, SSD_CONV) + cb_ref[...])
    xs, y_diag, ecs_e, xw, _, tot_e, bm, _, c_groups = _ssd_tile(
        act, dt_ref[...], dtb_ref[...], alog_ref[...], MIX_ROWS)
    gw = SSD_HEADS_PER_GROUP * SSD_HEAD_DIM
    y_off = []
    for g in range(SSD_GROUPS):
        cols = slice(g * gw, (g + 1) * gw)
        hg = h_scr[:, cols]
        y_off.append(_dot(c_groups[g], hg.astype(BF16)))
        b_t = bm[:, g * SSD_STATE:(g + 1) * SSD_STATE].T.astype(BF16)
        h_scr[:, cols] = jnp.exp(tot_e[:, cols]) * hg + _dot(b_t, xw[:, cols].astype(BF16))
    y = y_diag + jnp.concatenate(y_off, axis=-1) * ecs_e + de_ref[...] * xs
    yb_ref[...] = _gated_group_norm(y, z_ref[...].astype(F32), nw_ref[...]).astype(BF16)

    cvc = sccc_ref[...].astype(F32) * scvc_ref[...].astype(F32)
    cvp = jnp.where(first, 0.0, sccp_ref[...].astype(F32) * scvp_ref[...].astype(F32))
    conv_c = _causal_conv(cvc, cvp, scw_ref, SC_WIDTH)
    yc_ref[...] = (scb_ref[...].astype(F32) * conv_c).astype(BF16)

    @pl.when(i == nc - 1)
    def _():
        hout_ref[0] = h_scr[...].T
        cvlast_ref[0] = cvc[MIX_ROWS - SUBLANES:, :]


def _ssd_prompt(p, dt, cw, cb, dtb, alog, de, nw, scw, n_seq, seq_len):
    nc = seq_len // MIX_ROWS
    r = MIX_ROWS
    n_prompt = n_seq * seq_len

    def cur(b, i):
        return b * nc + i

    def prev(b, i):
        return jnp.maximum(b * nc + i - 1, 0)

    def col(width, offset, which):
        return pl.BlockSpec((r, width), lambda b, i: (which(b, i), offset // width))

    def const(shape):
        return pl.BlockSpec(shape, lambda b, i: (0,) * len(shape))

    return pl.pallas_call(
        _ssd_prompt_kernel,
        grid=(n_seq, nc),
        in_specs=[
            col(SSD_D_INNER, COL_Z, cur),
            col(SSD_CONV_DIM, COL_XBC, cur), col(SSD_CONV_DIM, COL_XBC, prev),
            pl.BlockSpec((r, DT_PAD), lambda b, i: (cur(b, i), 0)),
            col(SC_DIM, COL_SCB, cur),
            col(SC_DIM, COL_SCC, cur), col(SC_DIM, COL_SCC, prev),
            col(SC_DIM, COL_SCV, cur), col(SC_DIM, COL_SCV, prev),
            const((SSD_CONV, SSD_CONV_DIM)), const((1, SSD_CONV_DIM)),
            const((1, DT_PAD)), const((1, DT_PAD)), const((1, SSD_D_INNER)), const((1, SSD_D_INNER)),
            const((SC_WIDTH, SC_DIM)),
        ],
        out_specs=[
            pl.BlockSpec((r, SSD_D_INNER), lambda b, i: (cur(b, i), 0)),
            pl.BlockSpec((r, SC_DIM), lambda b, i: (cur(b, i), 0)),
            pl.BlockSpec((1, SSD_D_INNER, SSD_STATE), lambda b, i: (b, 0, 0)),
            pl.BlockSpec((1, SUBLANES, SC_DIM), lambda b, i: (b, 0, 0)),
        ],
        out_shape=[
            jax.ShapeDtypeStruct((n_prompt, SSD_D_INNER), BF16),
            jax.ShapeDtypeStruct((n_prompt, SC_DIM), BF16),
            jax.ShapeDtypeStruct((n_seq, SSD_D_INNER, SSD_STATE), F32),
            jax.ShapeDtypeStruct((n_seq, SUBLANES, SC_DIM), F32),
        ],
        scratch_shapes=[pltpu.VMEM((SSD_STATE, SSD_D_INNER), F32)],
        compiler_params=_params(("parallel", "arbitrary")),
    )(p, p, p, dt, p, p, p, p, p, cw, cb, dtb, alog, de, nw, scw)


def _mix_sample_kernel(sink_ref, q_ref, kv_ref, z_ref, x_ref, dt_ref, scb_ref, scc_ref, scv_ref,
                       cos_ref, sin_ref, ck_ref, cv_ref, h0_ref, xbuf_ref, cbuf_ref,
                       cw_ref, cb_ref, dtb_ref, alog_ref, de_ref, nw_ref, scw_ref,
                       ya_ref, yb_ref, yc_ref, knew_ref, vnew_ref, hnew_ref, cvout_ref):
    ns, t = SEQS_PER_STEP, DEC_SEQ
    wb = ck_ref.shape[1]

    q = _rope(q_ref[...].astype(F32), cos_ref[...], sin_ref[...])
    kv = kv_ref[...].astype(F32)
    kn = _rope(kv[:, :KV_WIDTH], cos_ref[...], sin_ref[...])
    vn = kv[:, KV_WIDTH:]
    q3 = q.reshape(ns, t, ATTN_WIDTH)
    kn3 = kn.reshape(ns, t, KV_WIDTH)
    vn3 = vn.reshape(ns, t, KV_WIDTH)
    knew_ref[:, :wb - t, :] = ck_ref[:, t:, :]
    knew_ref[:, wb - t:, :] = kn3
    vnew_ref[:, :wb - t, :] = cv_ref[:, t:, :]
    vnew_ref[:, wb - t:, :] = vn3
    nq = Q_PER_KV * t
    qi = lax.broadcasted_iota(jnp.int32, (ns, nq, wb + t), 1) % t
    si = lax.broadcasted_iota(jnp.int32, (ns, nq, wb + t), 2)
    valid = ((si < wb) & (si > qi + (wb - WINDOW))) | ((si >= wb) & (si - wb <= qi))
    hrow = lax.broadcasted_iota(jnp.int32, (ns, nq, 1), 1) // t
    heads = [None] * N_HEADS
    for kh in range(N_KV_HEADS):
        hs = slice(kh * HEAD_DIM, (kh + 1) * HEAD_DIM)
        k_all = jnp.concatenate([ck_ref[:, :, hs], kn3[:, :, hs]], axis=1).astype(BF16)
        v_all = jnp.concatenate([cv_ref[:, :, hs], vn3[:, :, hs]], axis=1).astype(BF16)
        qg = jnp.concatenate(
            [q3[:, :, (kh * Q_PER_KV + g) * HEAD_DIM:(kh * Q_PER_KV + g + 1) * HEAD_DIM] for g in range(Q_PER_KV)],
            axis=1).astype(BF16)
        sink = jnp.zeros((ns, nq, 1), F32)
        for g in range(Q_PER_KV):
            sink = jnp.where(hrow == g, sink_ref[kh * Q_PER_KV + g], sink)
        logits = jnp.einsum('bqd,bsd->bqs', qg, k_all, preferred_element_type=F32) * ATTN_SCALE
        logits = jnp.where(valid, logits, -jnp.inf)
        m = jnp.maximum(jnp.max(logits, axis=-1, keepdims=True), sink)
        e = jnp.exp(logits - m)
        den = jnp.sum(e, axis=-1, keepdims=True) + jnp.exp(sink - m)
        o = jnp.einsum('bqs,bsd->bqd', e.astype(BF16), v_all, preferred_element_type=F32) / den
        for g in range(Q_PER_KV):
            heads[kh * Q_PER_KV + g] = o[:, g * t:(g + 1) * t, :]
    ya_ref[...] = jnp.concatenate(heads, axis=-1).reshape(ns * t, ATTN_WIDTH).astype(BF16)

    xc3 = x_ref[...].astype(F32).reshape(ns, t, SSD_CONV_DIM)
    xp3 = xbuf_ref[...].reshape(ns, t, SSD_CONV_DIM)
    conv = _causal_conv(xc3, xp3, cw_ref, SSD_CONV).reshape(ns * t, SSD_CONV_DIM)
    act = _silu(conv + cb_ref[...])
    xs, y_diag, ecs_e, xw, tot, _, _, b_groups, c_groups = _ssd_tile(
        act, dt_ref[...], dtb_ref[...], alog_ref[...], t)
    xw_t = xw.T
    dec_t = jnp.exp(_dot_exact(_head_expand_t(), tot.T))
    gw = SSD_HEADS_PER_GROUP * SSD_HEAD_DIM
    y_off = []
    for g in range(SSD_GROUPS):
        c3 = c_groups[g].reshape(ns, t, SSD_STATE)
        hg = h0_ref[:, g * gw:(g + 1) * gw, :].astype(BF16)
        y_off.append(jnp.einsum('btn,bqn->btq', c3, hg, preferred_element_type=F32).reshape(ns * t, gw))
    y = y_diag + jnp.concatenate(y_off, axis=-1) * ecs_e + de_ref[...] * xs
    yb_ref[...] = _gated_group_norm(y, z_ref[...].astype(F32), nw_ref[...]).astype(BF16)
    col = lax.broadcasted_iota(jnp.int32, xw_t.shape, 1) // t
    for b in range(ns):
        xw_b = jnp.where(col == b, xw_t, jnp.zeros_like(xw_t))
        upd = jnp.concatenate(
            [_dot(xw_b[g * gw:(g + 1) * gw, :].astype(BF16), b_groups[g]) for g in range(SSD_GROUPS)], axis=0)
        hnew_ref[b] = dec_t[:, b * t:b * t + 1] * h0_ref[b] + upd

    cvc = scc_ref[...].astype(F32) * scv_ref[...].astype(F32)
    cvout_ref[...] = cvc
    conv_c = _causal_conv(cvc.reshape(ns, t, SC_DIM), cbuf_ref[...].reshape(ns, t, SC_DIM), scw_ref, SC_WIDTH)
    yc_ref[...] = (scb_ref[...].astype(F32) * conv_c.reshape(ns * t, SC_DIM)).astype(BF16)


def _mix_sample(p, dt, sink, cos_s, sin_s, cache_k, cache_v, h0, xbuf, cbuf,
                cw, cb, dtb, alog, de, nw, scw, layer, n_prompt, n_dec):
    r = MIX_ROWS
    ns = SEQS_PER_STEP
    steps = n_dec // ns
    base = n_prompt // r
    wb = cache_k.shape[2]

    def col(width, offset):
        return pl.BlockSpec((r, width), lambda i: (base + i, offset // width))

    def state(shape):
        return pl.BlockSpec((None, *shape), lambda i: (layer, i) + (0,) * (len(shape) - 1))

    def const(shape):
        return pl.BlockSpec(shape, lambda i: (0,) * len(shape))

    def rows(width):
        return pl.BlockSpec((r, width), lambda i: (i, 0))

    n_rows = n_dec * DEC_SEQ
    return pl.pallas_call(
        _mix_sample_kernel,
        grid=(steps,),
        in_specs=[
            pl.BlockSpec(memory_space=pltpu.SMEM),
            col(ATTN_WIDTH, COL_Q), col(2 * KV_WIDTH, COL_KV), col(SSD_D_INNER, COL_Z),
            col(SSD_CONV_DIM, COL_XBC),
            pl.BlockSpec((r, DT_PAD), lambda i: (base + i, 0)),
            col(SC_DIM, COL_SCB), col(SC_DIM, COL_SCC), col(SC_DIM, COL_SCV),
            const((r, LANES)), const((r, LANES)),
            state((ns, wb, KV_WIDTH)), state((ns, wb, KV_WIDTH)),
            state((ns, SSD_D_INNER, SSD_STATE)),
            state((r, SSD_CONV_DIM)), state((r, SC_DIM)),
            const((SSD_CONV, SSD_CONV_DIM)), const((1, SSD_CONV_DIM)),
            const((1, DT_PAD)), const((1, DT_PAD)), const((1, SSD_D_INNER)), const((1, SSD_D_INNER)),
            const((SC_WIDTH, SC_DIM)),
        ],
        out_specs=[
            rows(ATTN_WIDTH), rows(SSD_D_INNER), rows(SC_DIM),
            pl.BlockSpec((ns, wb, KV_WIDTH), lambda i: (i, 0, 0)),
            pl.BlockSpec((ns, wb, KV_WIDTH), lambda i: (i, 0, 0)),
            pl.BlockSpec((ns, SSD_D_INNER, SSD_STATE), lambda i: (i, 0, 0)),
            rows(SC_DIM),
        ],
        out_shape=[
            jax.ShapeDtypeStruct((n_rows, ATTN_WIDTH), BF16),
            jax.ShapeDtypeStruct((n_rows, SSD_D_INNER), BF16),
            jax.ShapeDtypeStruct((n_rows, SC_DIM), BF16),
            jax.ShapeDtypeStruct((n_dec, wb, KV_WIDTH), F32),
            jax.ShapeDtypeStruct((n_dec, wb, KV_WIDTH), F32),
            jax.ShapeDtypeStruct((n_dec, SSD_D_INNER, SSD_STATE), F32),
            jax.ShapeDtypeStruct((n_rows, SC_DIM), F32),
        ],
        compiler_params=_params(("parallel",)),
    )(sink, p, p, p, p, dt, p, p, p, cos_s, sin_s, cache_k, cache_v, h0, xbuf, cbuf,
      cw, cb, dtb, alog, de, nw, scw)


def _store_token_major(ref, v):
    n = v.shape[0]
    for s in range(ROW_CHUNKS):
        ref[pl.ds(s, n, stride=ROW_CHUNKS), :] = v[:, s * LANES:(s + 1) * LANES]


def _load_token_major(ref, n):
    return jnp.concatenate([ref[pl.ds(s, n, stride=ROW_CHUNKS), :] for s in range(ROW_CHUNKS)], axis=-1)


def _outproj_kernel(n_prompt_tiles, alpha,
                    x_ref, yap_ref, yas_ref, ybp_ref, ybs_ref, ycp_ref, ycs_ref, g_ref,
                    g1p_ref, g1s_ref, sh2p_ref, sh2s_ref, sc2p_ref, sc2s_ref,
                    wpa_ref, wpb_ref, wpc_ref, wout_ref, wr_ref, br_ref, lng_ref, lnb_ref,
                    x1_ref, h_ref, route_ref, cnt_ref, cnt_scr):
    i = pl.program_id(0)
    is_sample = i >= n_prompt_tiles
    tm = x_ref.shape[0]

    @pl.when(i == 0)
    def _():
        cnt_scr[...] = jnp.zeros_like(cnt_scr)

    n = tm
    cnt = cnt_scr[...]
    for r0 in range(0, tm, n):
        rs = slice(r0, r0 + n)

        def pick(prompt_ref, sample_ref):
            return jnp.where(is_sample, sample_ref[0, rs, :], prompt_ref[0])

        ya = jnp.where(is_sample, yas_ref[rs, :], yap_ref[rs, :])
        yb = jnp.where(is_sample, ybs_ref[rs, :], ybp_ref[rs, :])
        yc = jnp.where(is_sample, ycs_ref[rs, :], ycp_ref[rs, :])
        gates = 0.5 * jnp.tanh(0.5 * g_ref[rs, :]) + 0.5
        merged = (gates[:, :D_MODEL] * _dot(ya, wpa_ref[...]).astype(BF16)
                  + gates[:, D_MODEL:2 * D_MODEL] * _dot(yb, wpb_ref[...]).astype(BF16)
                  + gates[:, 2 * D_MODEL:] * _dot(yc, wpc_ref[...]).astype(BF16))
        mix = _dot(merged, wout_ref[...])
        g1 = pick(g1p_ref, g1s_ref)
        x1 = _layer_norm(alpha * x_ref[rs, :] + g1 * mix, lng_ref[...], lnb_ref[...])
        x1_ref[rs, :] = x1
        sh2 = pick(sh2p_ref, sh2s_ref)
        sc2 = pick(sc2p_ref, sc2s_ref)
        h = x1 * (1.0 + sc2) + sh2
        _store_token_major(h_ref.at[pl.ds(r0 * ROW_CHUNKS, n * ROW_CHUNKS)], h)

        lt = (_dot(h.astype(BF16), wr_ref[...]) + br_ref[...]).T[:ROUTE_ROWS]
        row = lax.broadcasted_iota(jnp.int32, lt.shape, 0).astype(F32)
        neg = -jnp.inf
        big = float(ROUTE_ROWS)
        gl = jnp.where(row < MOE_GROUPS, lt, neg)
        gmax = jnp.max(gl, axis=0, keepdims=True)
        g_p = 1.0 / jnp.sum(jnp.exp(gl - gmax), axis=0, keepdims=True)
        gidx = jnp.min(jnp.where(gl == gmax, row, big), axis=0, keepdims=True)
        lo = ROUTE_EXPERT_ROW + EXPERTS_PER_GROUP * gidx
        sel = jnp.where((row >= lo) & (row < lo + EXPERTS_PER_GROUP), lt, neg)
        m1 = jnp.max(sel, axis=0, keepdims=True)
        i1 = jnp.min(jnp.where(sel == m1, row, big), axis=0, keepdims=True)
        sel2 = jnp.where(row == i1, neg, sel)
        m2 = jnp.max(sel2, axis=0, keepdims=True)
        i2 = jnp.min(jnp.where(sel2 == m2, row, big), axis=0, keepdims=True)
        ssum = jnp.sum(jnp.exp(sel - m1), axis=0, keepdims=True)
        p1 = 1.0 / ssum
        p2 = jnp.exp(m2 - m1) / ssum
        w1 = g_p * (p1 / (p1 + p2))
        w2 = g_p * (p2 / (p1 + p2))

        onehot = jnp.where((row == i1) | (row == i2), 1.0, 0.0)
        ri = lax.broadcasted_iota(jnp.int32, (n, n), 0)
        ci = lax.broadcasted_iota(jnp.int32, (n, n), 1)
        earlier = jnp.where(ri < ci, 1.0, 0.0).astype(BF16)
        prefix = _dot(onehot.astype(BF16), earlier) + cnt[:, 0:1]
        rank1 = jnp.sum(jnp.where(row == i1, prefix, 0.0), axis=0, keepdims=True)
        rank2 = jnp.sum(jnp.where(row == i2, prefix, 0.0), axis=0, keepdims=True)
        cnt = cnt + jnp.sum(onehot, axis=1, keepdims=True)
        zero = jnp.zeros_like(w1)
        route_ref[:, rs] = jnp.concatenate(
            [i1 - ROUTE_EXPERT_ROW, i2 - ROUTE_EXPERT_ROW, w1, w2, rank1, rank2, zero, zero], axis=0)

    cnt_scr[...] = cnt
    cnt_ref[...] = cnt


def _output_projection(x, y_prompt, y_sample, p, mod_p, mod_s, wpa, wpb, wpc, wout, wr, br, lng, lnb,
                       layer, n_prompt, seq_len, alpha):
    t_all = x.shape[0]
    tm = ROW_TILE
    npt = n_prompt // tm
    n_seq = mod_p.shape[1]
    mods = []
    for col in (2, 3, 4):
        mods.extend(_mod_specs(layer, col, tm, npt, seq_len // tm, n_seq))

    def rows(width):
        return pl.BlockSpec((tm, width), lambda i: (i, 0))

    def prompt_rows(width):
        return pl.BlockSpec((tm, width), lambda i: (jnp.minimum(i, npt - 1), 0))

    def sample_rows(width):
        return pl.BlockSpec((tm, width), lambda i: (jnp.maximum(i - npt, 0), 0))

    def const(shape):
        return pl.BlockSpec(shape, lambda i: (0,) * len(shape))

    return pl.pallas_call(
        functools.partial(_outproj_kernel, npt, alpha),
        grid=(t_all // tm,),
        in_specs=[
            rows(D_MODEL),
            prompt_rows(ATTN_WIDTH), sample_rows(ATTN_WIDTH),
            prompt_rows(SSD_D_INNER), sample_rows(SSD_D_INNER),
            prompt_rows(SC_DIM), sample_rows(SC_DIM),
            pl.BlockSpec((tm, 3 * D_MODEL), lambda i: (i, COL_GATES // (3 * D_MODEL))),
            *mods,
            _layer_spec(layer, (ATTN_WIDTH, D_MODEL)), _layer_spec(layer, (SSD_D_INNER, D_MODEL)),
            _layer_spec(layer, (SC_DIM, D_MODEL)), _layer_spec(layer, (D_MODEL, D_MODEL)),
            _layer_spec(layer, (D_MODEL, LANES)), _layer_spec(layer, (1, LANES)),
            const((1, D_MODEL)), const((1, D_MODEL)),
        ],
        out_specs=[
            rows(D_MODEL),
            pl.BlockSpec((tm * ROW_CHUNKS, LANES), lambda i: (i, 0)),
            pl.BlockSpec((SUBLANES, tm), lambda i: (0, i)),
            const((ROUTE_ROWS, LANES)),
        ],
        out_shape=[
            jax.ShapeDtypeStruct((t_all, D_MODEL), F32),
            jax.ShapeDtypeStruct((t_all * ROW_CHUNKS, LANES), F32),
            jax.ShapeDtypeStruct((SUBLANES, t_all), F32),
            jax.ShapeDtypeStruct((ROUTE_ROWS, LANES), F32),
        ],
        scratch_shapes=[pltpu.VMEM((ROUTE_ROWS, LANES), F32)],
        compiler_params=_params(("arbitrary",)),
    )(x, y_prompt[0], y_sample[0], y_prompt[1], y_sample[1], y_prompt[2], y_sample[2],
      p, mod_p, mod_s, mod_p, mod_s, mod_p, mod_s,
      wpa, wpb, wpc, wout, wr, br, lng, lnb)


def _slots_kernel(route_ref, start_ref, dest_ref):
    route = route_ref[...]
    tm = route.shape[1]
    expert = lax.broadcasted_iota(jnp.int32, (N_EXPERTS, tm), 0).astype(F32)
    start = start_ref[:, 0:1]
    rows = []
    for e_row, r_row in ((0, 4), (1, 5)):
        first = jnp.sum(jnp.where(expert == route[e_row:e_row + 1], start, 0.0), axis=0, keepdims=True)
        rows.append(first + route[r_row:r_row + 1])
    rows.append(jnp.zeros((SUBLANES - 2, tm), F32))
    dest_ref[...] = jnp.concatenate(rows, axis=0).astype(jnp.int32)


def _slots(route, pad_start):
    t_all = route.shape[1]
    tm = ROW_TILE_IN
    dest = pl.pallas_call(
        _slots_kernel,
        grid=(t_all // tm,),
        in_specs=[pl.BlockSpec((SUBLANES, tm), lambda i: (0, i)),
                  pl.BlockSpec((N_EXPERTS, LANES), lambda i: (0, 0))],
        out_specs=pl.BlockSpec((SUBLANES, tm), lambda i: (0, i)),
        out_shape=jax.ShapeDtypeStruct((SUBLANES, t_all), jnp.int32),
        compiler_params=_params(("parallel",)),
    )(route, jnp.broadcast_to(pad_start.astype(F32)[:, None], (N_EXPERTS, LANES)))
    return dest[0], dest[1]


def _token_copy(src, src_row, dst, dst_row, sem):
    return pltpu.make_async_copy(
        src.at[pl.ds(pl.multiple_of(src_row * ROW_CHUNKS, ROW_CHUNKS), ROW_CHUNKS)],
        dst.at[pl.ds(pl.multiple_of(dst_row * ROW_CHUNKS, ROW_CHUNKS), ROW_CHUNKS)],
        sem)


def _dispatch_kernel(d1_ref, d2_ref, pend_ref, cnt_ref, nb_ref, h_ref, xb_ref, zero_scr, zsem, sem):
    i = pl.program_id(0)
    tm = h_ref.shape[0] // ROW_CHUNKS
    blk_rows = EXPERT_BLOCK * ROW_CHUNKS
    n_blocks = xb_ref.shape[0] // blk_rows

    def zero_block(b):
        start = pl.multiple_of(b * blk_rows, blk_rows)
        return pltpu.make_async_copy(zero_scr, xb_ref.at[pl.ds(start, blk_rows)], zsem)

    def last_block(e):
        return pend_ref[e] // EXPERT_BLOCK - 1

    @pl.when(i == 0)
    def _():
        zero_scr[...] = jnp.zeros_like(zero_scr)
        for e in range(N_EXPERTS):
            @pl.when(cnt_ref[e] > 0)
            def _():
                zero_block(last_block(e)).start()
        lax.fori_loop(nb_ref[0], n_blocks, lambda b, c: (zero_block(b).start(), c)[1], 0)
        for e in range(N_EXPERTS):
            @pl.when(cnt_ref[e] > 0)
            def _():
                zero_block(last_block(e)).wait()
        lax.fori_loop(nb_ref[0], n_blocks, lambda b, c: (zero_block(b).wait(), c)[1], 0)

    def issue(c, carry):
        for u in range(ISSUE_UNROLL):
            t = c * ISSUE_UNROLL + u
            g = i * tm + t
            _token_copy(h_ref, t, xb_ref, d1_ref[g], sem.at[0]).start(priority=0)
            _token_copy(h_ref, t, xb_ref, d2_ref[g], sem.at[1]).start(priority=1)
        return carry

    lax.fori_loop(0, tm // ISSUE_UNROLL, issue, 0)
    for k in range(2):
        pltpu.make_async_copy(h_ref, xb_ref.at[pl.ds(0, tm * ROW_CHUNKS)], sem.at[k]).wait()


def _dispatch(h, dest1, dest2, pad_end, counts, n_used, n_slots):
    t_all = h.shape[0] // ROW_CHUNKS
    tm = ROW_TILE
    grid_spec = pltpu.PrefetchScalarGridSpec(
        num_scalar_prefetch=5,
        grid=(t_all // tm,),
        in_specs=[pl.BlockSpec((tm * ROW_CHUNKS, LANES), lambda i, *_: (i, 0))],
        out_specs=pl.BlockSpec(memory_space=pl.ANY),
        scratch_shapes=[
            pltpu.VMEM((EXPERT_BLOCK * ROW_CHUNKS, LANES), F32),
            pltpu.SemaphoreType.DMA(()),
            pltpu.SemaphoreType.DMA((2,)),
        ],
    )
    return pl.pallas_call(
        _dispatch_kernel,
        grid_spec=grid_spec,
        out_shape=jax.ShapeDtypeStruct((n_slots * ROW_CHUNKS, LANES), F32),
        compiler_params=_params(("arbitrary",)),
    )(dest1, dest2, pad_end, counts, n_used, h)


def _ffn_kernel(layer, cstart_ref, cbig_ref, cexp_ref, next_ref, meta_ref,
                xb_ref, wg_ref, wu_ref, wd_ref, yb_ref,
                x_in, y_out, wg_f, wu_f, wd_f, wg_b, wu_b, wd_b, x_scr, xsem, ysem, wsem):
    blk_rows = EXPERT_BLOCK * ROW_CHUNKS
    n_blocks = yb_ref.shape[0] // blk_rows
    n_chunks, first_expert, n_used = meta_ref[0], meta_ref[1], meta_ref[2]

    def chunk_blocks(size_class):
        return CHUNK_SIZES[size_class]

    def hbm_rows(ref, j, big):
        start = pl.multiple_of(cstart_ref[j] * blk_rows, blk_rows)
        return ref.at[pl.ds(start, chunk_blocks(big) * blk_rows)]

    def x_copy(j, s, big):
        return pltpu.make_async_copy(hbm_rows(xb_ref, j, big),
                                     x_in.at[s, pl.ds(0, chunk_blocks(big) * blk_rows)], xsem.at[s])

    def y_copy(j, s, big):
        return pltpu.make_async_copy(y_out.at[s, pl.ds(0, chunk_blocks(big) * blk_rows)],
                                     hbm_rows(yb_ref, j, big), ysem.at[s])

    def by_size(j, fn):
        for big in range(len(CHUNK_SIZES)):
            @pl.when(cbig_ref[j] == big)
            def _():
                fn(big)

    def w_copies(e, s):
        return (pltpu.make_async_copy(wg_ref.at[layer, e], wg_f.at[s], wsem.at[s, 0]),
                pltpu.make_async_copy(wu_ref.at[layer, e], wu_f.at[s], wsem.at[s, 1]),
                pltpu.make_async_copy(wd_ref.at[layer, e], wd_f.at[s], wsem.at[s, 2]))

    for c in w_copies(first_expert, 0):
        c.start(priority=1)
    by_size(0, lambda big: x_copy(0, 0, big).start())

    def chunk(j, wslot):
        e = cexp_ref[j]
        new_expert = (j == 0) | (cexp_ref[jnp.maximum(j - 1, 0)] != e)
        wslot = jnp.where(new_expert & (j > 0), 1 - wslot, wslot)

        @pl.when(new_expert)
        def _():
            for c in w_copies(e, wslot):
                c.wait()

            @pl.when(next_ref[e] < N_EXPERTS)
            def _():
                for c in w_copies(next_ref[e], 1 - wslot):
                    c.start(priority=1)

            wg_b[...] = wg_f[wslot].astype(BF16)
            wu_b[...] = wu_f[wslot].astype(BF16)
            wd_b[...] = wd_f[wslot].astype(BF16)

        s = j % 2
        by_size(j, lambda big: x_copy(j, s, big).wait())

        @pl.when(j + 1 < n_chunks)
        def _():
            by_size(j + 1, lambda big: x_copy(j + 1, 1 - s, big).start())

        @pl.when(j >= 2)
        def _():
            by_size(j - 2, lambda big: y_copy(j - 2, s, big).wait())

        def compute(big):
            rows = chunk_blocks(big) * EXPERT_BLOCK
            x_tok = x_in.at[s]
            for c in range(ROW_CHUNKS):
                x_scr[0:rows, c * LANES:(c + 1) * LANES] = (
                    x_tok[pl.ds(c, rows, stride=ROW_CHUNKS), :].astype(BF16))
            x = x_scr[0:rows, :]
            act = (_silu(_dot(x, wg_b[...])) * _dot(x, wu_b[...])).astype(BF16)
            _store_token_major(y_out.at[s, pl.ds(0, rows * ROW_CHUNKS)], _dot(act, wd_b[...]))
            y_copy(j, s, big).start()

        by_size(j, compute)
        return wslot

    lax.fori_loop(0, n_chunks, chunk, 0)

    @pl.when(n_chunks >= 2)
    def _():
        by_size(n_chunks - 2, lambda big: y_copy(n_chunks - 2, n_chunks % 2, big).wait())

    by_size(n_chunks - 1, lambda big: y_copy(n_chunks - 1, (n_chunks - 1) % 2, big).wait())

    zero_rows = y_out.at[0, pl.ds(0, blk_rows)]
    zero_rows[...] = jnp.zeros((blk_rows, LANES), F32)

    def tail_copy(b):
        start = pl.multiple_of(b * blk_rows, blk_rows)
        return pltpu.make_async_copy(zero_rows, yb_ref.at[pl.ds(start, blk_rows)], ysem.at[0])

    lax.fori_loop(n_used, n_blocks, lambda b, c: (tail_copy(b).start(), c)[1], 0)
    lax.fori_loop(n_used, n_blocks, lambda b, c: (tail_copy(b).wait(), c)[1], 0)


def _chunk_plan(first_block, n_expert_blocks, n_blocks):
    experts = jnp.arange(N_EXPERTS, dtype=jnp.int32)
    per_size, left = [], n_expert_blocks
    for size in CHUNK_SIZES:
        per_size.append(left // size)
        left = left % size
    n_chunks_e = sum(per_size)
    chunk_end = jnp.cumsum(n_chunks_e)
    chunk_start = chunk_end - n_chunks_e
    j = jnp.arange(n_blocks, dtype=jnp.int32)
    e_j = jnp.minimum(jnp.sum((chunk_end[None, :] <= j[:, None]).astype(jnp.int32), axis=1), N_EXPERTS - 1)
    pick = (e_j[:, None] == experts[None, :]).astype(jnp.int32)

    def of_expert(v):
        return jnp.sum(pick * v[None, :], axis=1)

    c = j - of_expert(chunk_start)
    start = of_expert(first_block)
    size_class = jnp.zeros_like(j)
    for k, size in enumerate(CHUNK_SIZES):
        n_k = of_expert(per_size[k])
        inside = (c >= 0) & (c < n_k)
        start = start + jnp.where(inside, size * c, jnp.where(c >= n_k, size * n_k, 0))
        size_class = jnp.where(inside, k, size_class)
        c = jnp.where(inside, -1, c - n_k)
    owner = jnp.where(n_chunks_e > 0, experts, N_EXPERTS)
    later = jnp.flip(lax.cummin(jnp.flip(owner)))
    next_owner = jnp.concatenate([later[1:], jnp.full((1,), N_EXPERTS, jnp.int32)])
    meta = jnp.stack([chunk_end[-1], later[0]]).astype(jnp.int32)
    return start.astype(jnp.int32), size_class.astype(jnp.int32), e_j, next_owner.astype(jnp.int32), meta


def _expert_ffn(xb, first_block, n_expert_blocks, n_used, wg, wu, wd, layer):
    blk_rows = EXPERT_BLOCK * ROW_CHUNKS
    n_blocks = xb.shape[0] // blk_rows
    cstart, cbig, cexp, next_owner, meta = _chunk_plan(first_block, n_expert_blocks, n_blocks)
    meta = jnp.concatenate([meta, n_used])
    any_spec = pl.BlockSpec(memory_space=pl.ANY)
    grid_spec = pltpu.PrefetchScalarGridSpec(
        num_scalar_prefetch=5,
        grid=(1,),
        in_specs=[any_spec] * 4,
        out_specs=any_spec,
        scratch_shapes=[
            pltpu.VMEM((2, CHUNK_BLOCKS * blk_rows, LANES), F32),
            pltpu.VMEM((2, CHUNK_BLOCKS * blk_rows, LANES), F32),
            pltpu.VMEM((2, D_MODEL, EXPERT_FF), F32),
            pltpu.VMEM((2, D_MODEL, EXPERT_FF), F32),
            pltpu.VMEM((2, EXPERT_FF, D_MODEL), F32),
            pltpu.VMEM((D_MODEL, EXPERT_FF), BF16),
            pltpu.VMEM((D_MODEL, EXPERT_FF), BF16),
            pltpu.VMEM((EXPERT_FF, D_MODEL), BF16),
            pltpu.VMEM((CHUNK_BLOCKS * EXPERT_BLOCK, D_MODEL), BF16),
            pltpu.SemaphoreType.DMA((2,)),
            pltpu.SemaphoreType.DMA((2,)),
            pltpu.SemaphoreType.DMA((2, 3)),
        ],
    )
    return pl.pallas_call(
        functools.partial(_ffn_kernel, layer),
        grid_spec=grid_spec,
        out_shape=jax.ShapeDtypeStruct(xb.shape, F32),
        compiler_params=_params(("arbitrary",)),
    )(cstart, cbig, cexp, next_owner, meta, xb, wg, wu, wd)


def _combine_kernel(n_prompt_tiles, alpha, d1_ref, d2_ref,
                    x1_ref, route_ref, g2p_ref, g2s_ref, lng_ref, lnb_ref, yb_ref, *outs_and_scratch):
    out_refs, (buf_a, buf_b, sem) = outs_and_scratch[:-3], outs_and_scratch[-3:]
    i = pl.program_id(0)
    n_tiles = pl.num_programs(0)
    tm = x1_ref.shape[0]
    slot = i % 2

    def gather_tile(tile, s):
        def issue(c, carry):
            for u in range(ISSUE_UNROLL):
                t = c * ISSUE_UNROLL + u
                g = tile * tm + t
                _token_copy(yb_ref, d1_ref[g], buf_a.at[s], t, sem.at[s, 0]).start(priority=0)
                _token_copy(yb_ref, d2_ref[g], buf_b.at[s], t, sem.at[s, 1]).start(priority=1)
            return carry

        lax.fori_loop(0, tm // ISSUE_UNROLL, issue, 0)

    @pl.when(i == 0)
    def _():
        gather_tile(0, 0)

    @pl.when(i + 1 < n_tiles)
    def _():
        gather_tile(i + 1, 1 - slot)

    whole = yb_ref.at[pl.ds(0, tm * ROW_CHUNKS)]
    pltpu.make_async_copy(whole, buf_a.at[slot], sem.at[slot, 0]).wait()
    pltpu.make_async_copy(whole, buf_b.at[slot], sem.at[slot, 1]).wait()
    route_t = jnp.concatenate(
        [route_ref[...], jnp.zeros((LANES - SUBLANES, tm), F32)], axis=0).T
    ffn = (route_t[:, 2:3] * _load_token_major(buf_a.at[slot], tm)
           + route_t[:, 3:4] * _load_token_major(buf_b.at[slot], tm))
    g2 = _pick_mod(i >= n_prompt_tiles, g2p_ref, g2s_ref)
    out = _layer_norm(alpha * x1_ref[...] + g2 * ffn, lng_ref[...], lnb_ref[...])
    if len(out_refs) == 1:
        out_refs[0][...] = out
    else:
        @pl.when(i < n_prompt_tiles)
        def _():
            out_refs[0][...] = out

        @pl.when(i >= n_prompt_tiles)
        def _():
            out_refs[1][...] = out


def _combine(x1, route, yb, dest1, dest2, mod_p, mod_s, lng, lnb, layer, n_prompt, seq_len, alpha, split):
    t_all = x1.shape[0]
    tm = ROW_TILE
    npt = n_prompt // tm
    if split:
        out_specs = [pl.BlockSpec((tm, D_MODEL), lambda i, *_: (jnp.minimum(i, npt - 1), 0)),
                     pl.BlockSpec((tm, D_MODEL), lambda i, *_: (jnp.maximum(i - npt, 0), 0))]
        out_shape = [jax.ShapeDtypeStruct((n_prompt, D_MODEL), F32),
                     jax.ShapeDtypeStruct((t_all - n_prompt, D_MODEL), F32)]
    else:
        out_specs = pl.BlockSpec((tm, D_MODEL), lambda i, *_: (i, 0))
        out_shape = jax.ShapeDtypeStruct((t_all, D_MODEL), F32)
    g2p, g2s = _mod_specs(layer, 5, tm, npt, seq_len // tm, mod_p.shape[1])
    grid_spec = pltpu.PrefetchScalarGridSpec(
        num_scalar_prefetch=2,
        grid=(t_all // tm,),
        in_specs=[
            pl.BlockSpec((tm, D_MODEL), lambda i, *_: (i, 0)),
            pl.BlockSpec((SUBLANES, tm), lambda i, *_: (0, i)),
            g2p, g2s,
            pl.BlockSpec((1, D_MODEL), lambda i, *_: (0, 0)),
            pl.BlockSpec((1, D_MODEL), lambda i, *_: (0, 0)),
            pl.BlockSpec(memory_space=pl.ANY),
        ],
        out_specs=out_specs,
        scratch_shapes=[
            pltpu.VMEM((2, tm * ROW_CHUNKS, LANES), F32),
            pltpu.VMEM((2, tm * ROW_CHUNKS, LANES), F32),
            pltpu.SemaphoreType.DMA((2, 2)),
        ],
    )
    return pl.pallas_call(
        functools.partial(_combine_kernel, npt, alpha),
        grid_spec=grid_spec,
        out_shape=out_shape,
        compiler_params=_params(("arbitrary",)),
    )(dest1, dest2, x1, route, mod_p, mod_s, lng, lnb, yb)


def _rope_tables(pos):
    inv = jnp.power(ROPE_THETA, -jnp.arange(ROT_HALF, dtype=F32) * (2.0 / ROT_DIM))
    ang = pos.astype(F32)[:, None] * inv[None, :]
    cos, sin = jnp.cos(ang), jnp.sin(ang)
    rest = HEAD_DIM - ROT_DIM
    n = pos.shape[0]
    cos_h = jnp.concatenate([cos, cos, jnp.ones((n, rest), F32)], axis=-1)
    sin_h = jnp.concatenate([-sin, sin, jnp.zeros((n, rest), F32)], axis=-1)
    reps = LANES // HEAD_DIM
    return jnp.tile(cos_h, (1, reps)), jnp.tile(sin_h, (1, reps))


def _permute_w_in(w_in):
    sizes = (ATTN_WIDTH, KV_WIDTH, KV_WIDTH, SSD_D_INNER, SSD_CONV_DIM, SSD_HEADS,
             SC_DIM, SC_DIM, SC_DIM, 3 * D_MODEL)
    offs = np.concatenate([[0], np.cumsum(sizes)])
    q, k, v, z, xbc, dt, scb, scc, scv, gates = (w_in[..., offs[n]:offs[n + 1]] for n in range(len(sizes)))
    main = jnp.concatenate([q, z, xbc, scb, scc, gates, scv, k, v], axis=-1).astype(BF16)
    dt = jnp.pad(dt, ((0, 0), (0, 0), (0, DT_PAD - SSD_HEADS))).astype(BF16)
    return main, dt


def _pad_lanes(v, width):
    return jnp.pad(v, ((0, 0), (0, width - v.shape[-1])))


def kernel(x_prompt, x_sample, c_prompt, c_sample, cache_attn_k, cache_attn_v, state_ssm, state_ssd_conv, state_short_conv, w_ada, b_ada, w_in, attn_sink, ssd_conv_w, ssd_conv_b, ssd_dt_bias, ssd_a_log, ssd_d, ssd_norm_w, sc_conv_w, w_pa, w_pb, w_pc, w_out, ln1_g, ln1_b, ln2_g, ln2_b, router_g_w, router_g_b, router_e_w, router_e_b, moe_w_gate, moe_w_up, moe_w_down):
    depth = w_in.shape[0]
    n_seq, seq_len, _ = x_prompt.shape
    n_dec, dec_len, _ = x_sample.shape
    wb = cache_attn_k.shape[2]
    past_len = 8192
    assert dec_len == DEC_SEQ and wb == WINDOW
    assert seq_len % ROW_TILE_IN == 0 and (n_dec * dec_len) % ROW_TILE_IN == 0
    n_prompt = n_seq * seq_len
    n_sample = n_dec * dec_len
    t_all = n_prompt + n_sample
    alpha = (2 * depth) ** 0.25

    mod = _modulation(jnp.concatenate([c_prompt, c_sample], axis=0), w_ada, b_ada)
    mod_p_all = mod[:, :n_seq].reshape(depth, n_seq, 1, 6 * D_MODEL)
    mod_s_all = jnp.repeat(mod[:, n_seq:], dec_len, axis=1).reshape(depth, 1, n_sample, 6 * D_MODEL)

    w_main_all, w_dt_all = _permute_w_in(w_in)
    cos_p, sin_p = _rope_tables(jnp.arange(seq_len, dtype=jnp.int32))
    cos_s, sin_s = _rope_tables(past_len + (jnp.arange(MIX_ROWS, dtype=jnp.int32) % dec_len))

    a_total = 2 * t_all
    n_blocks = (a_total + N_EXPERTS * (EXPERT_BLOCK - 1)) // EXPERT_BLOCK
    n_slots = n_blocks * EXPERT_BLOCK

    mod_p, mod_s = mod_p_all, mod_s_all
    cache_k = cache_attn_k.reshape(depth, n_dec, wb, KV_WIDTH)
    cache_v = cache_attn_v.reshape(depth, n_dec, wb, KV_WIDTH)
    h0 = state_ssm.reshape(depth, n_dec, SSD_D_INNER, SSD_STATE)
    xbuf = jnp.pad(state_ssd_conv, ((0, 0), (0, 0), (dec_len - (SSD_CONV - 1), 0), (0, 0))
                   ).reshape(depth, n_sample, SSD_CONV_DIM)
    cbuf = jnp.pad(state_short_conv, ((0, 0), (0, 0), (dec_len - (SC_WIDTH - 1), 0), (0, 0))
                   ).reshape(depth, n_sample, SC_DIM)
    wpa, wpb, wpc, wout = (w.astype(BF16) for w in (w_pa, w_pb, w_pc, w_out))
    gap = ROUTE_EXPERT_ROW - MOE_GROUPS
    tail = LANES - ROUTE_EXPERT_ROW - N_EXPERTS
    wr = jnp.concatenate([router_g_w, jnp.zeros((depth, D_MODEL, gap), F32), router_e_w,
                          jnp.zeros((depth, D_MODEL, tail), F32)], axis=-1).astype(BF16)
    br = jnp.concatenate([router_g_b, jnp.zeros((depth, gap), F32), router_e_b,
                          jnp.zeros((depth, tail), F32)], axis=-1)[:, None, :]

    x = jnp.concatenate([x_prompt.reshape(n_prompt, D_MODEL), x_sample.reshape(n_sample, D_MODEL)], axis=0)
    outs_p = [[] for _ in range(5)]
    outs_s = [[] for _ in range(5)]
    for l in range(depth):
        p, dt = _input_projection(x, mod_p, mod_s, w_main_all, w_dt_all, l, n_prompt, seq_len)

        cw, cb = ssd_conv_w[l], ssd_conv_b[l][None]
        dtb = _pad_lanes(ssd_dt_bias[l][None], DT_PAD)
        alog = _pad_lanes(ssd_a_log[l][None], DT_PAD)
        de = jnp.repeat(ssd_d[l], SSD_HEAD_DIM)[None]
        nw = ssd_norm_w[l][None]
        scw = sc_conv_w[l]

        ya_p, krot_p = _attention_prompt(p, attn_sink[l], cos_p, sin_p, n_seq, seq_len)
        yb_p, yc_p, h_p, cv_p = _ssd_prompt(p, dt, cw, cb, dtb, alog, de, nw, scw, n_seq, seq_len)

        ya_s, yb_s, yc_s, k_s, v_s, h_s, cv_s = _mix_sample(
            p, dt, attn_sink[l], cos_s, sin_s, cache_k, cache_v, h0, xbuf, cbuf,
            cw, cb, dtb, alog, de, nw, scw, l, n_prompt, n_dec)

        x1, h, route, counts = _output_projection(
            x, (ya_p, yb_p, yc_p), (ya_s, yb_s, yc_s), p, mod_p, mod_s,
            wpa, wpb, wpc, wout, wr, br, ln1_g[l][None], ln1_b[l][None], l, n_prompt, seq_len, alpha)

        cnt = counts[ROUTE_EXPERT_ROW:ROUTE_EXPERT_ROW + N_EXPERTS, 0].astype(jnp.int32)
        pad_cnt = (cnt + EXPERT_BLOCK - 1) // EXPERT_BLOCK * EXPERT_BLOCK
        pad_end = jnp.cumsum(pad_cnt)
        pad_start = pad_end - pad_cnt
        n_used = (pad_end[-1:] // EXPERT_BLOCK).astype(jnp.int32)
        dest1, dest2 = _slots(route, pad_start)

        xb = _dispatch(h, dest1, dest2, pad_end.astype(jnp.int32), cnt, n_used, n_slots)
        yb_slots = _expert_ffn(xb, (pad_start // EXPERT_BLOCK).astype(jnp.int32),
                               (pad_cnt // EXPERT_BLOCK).astype(jnp.int32), n_used,
                               moe_w_gate, moe_w_up, moe_w_down, l)
        x = _combine(x1, route, yb_slots, dest1, dest2, mod_p, mod_s, ln2_g[l][None], ln2_b[l][None],
                     l, n_prompt, seq_len, alpha, split=(l == depth - 1))

        def prompt_tail(rows, c0, c1):
            return jnp.stack([p[(b + 1) * seq_len - rows:(b + 1) * seq_len, c0:c1] for b in range(n_seq)]
                             ).astype(F32)

        outs_p[0].append(krot_p.reshape(n_seq, wb, N_KV_HEADS, HEAD_DIM))
        outs_p[1].append(prompt_tail(wb, COL_KV + KV_WIDTH, COL_KV + 2 * KV_WIDTH)
                         .reshape(n_seq, wb, N_KV_HEADS, HEAD_DIM))
        outs_p[2].append(h_p.reshape(n_seq, SSD_HEADS, SSD_HEAD_DIM, SSD_STATE))
        outs_p[3].append(prompt_tail(SSD_CONV - 1, COL_XBC, COL_XBC + SSD_CONV_DIM))
        outs_p[4].append(cv_p[:, SUBLANES - (SC_WIDTH - 1):, :])
        outs_s[0].append(k_s.reshape(n_dec, wb, N_KV_HEADS, HEAD_DIM))
        outs_s[1].append(v_s.reshape(n_dec, wb, N_KV_HEADS, HEAD_DIM))
        outs_s[2].append(h_s.reshape(n_dec, SSD_HEADS, SSD_HEAD_DIM, SSD_STATE))
        outs_s[3].append(p[n_prompt:, COL_XBC:COL_XBC + SSD_CONV_DIM].astype(F32)
                         .reshape(n_dec, dec_len, SSD_CONV_DIM)[:, dec_len - (SSD_CONV - 1):, :])
        outs_s[4].append(cv_s.reshape(n_dec, dec_len, SC_DIM)[:, dec_len - (SC_WIDTH - 1):, :])

    y_prompt = x[0].reshape(n_seq, seq_len, D_MODEL)
    y_sample = x[1].reshape(n_dec, dec_len, D_MODEL)
    return (y_prompt, y_sample, *[jnp.stack(o) for o in outs_p], *[jnp.stack(o) for o in outs_s])
```

```python
import functools

import jax
import jax.numpy as jnp
import numpy as np
from jax import lax
from jax.experimental import pallas as pl
from jax.experimental.pallas import tpu as pltpu

F32 = jnp.float32
BF16 = jnp.bfloat16

D_MODEL = 1024
HEAD_DIM = 64
N_HEADS = 8
N_KV_HEADS = 2
Q_PER_KV = N_HEADS // N_KV_HEADS
ATTN_WIDTH = N_HEADS * HEAD_DIM
KV_WIDTH = N_KV_HEADS * HEAD_DIM
WINDOW = 128
ROT_DIM = HEAD_DIM // 4
ROT_HALF = ROT_DIM // 2
ROPE_THETA = 500000.0
ATTN_SCALE = HEAD_DIM ** -0.5
SSD_D_INNER = 512
SSD_HEAD_DIM = 64
SSD_HEADS = 8
SSD_GROUPS = 2
SSD_HEADS_PER_GROUP = SSD_HEADS // SSD_GROUPS
SSD_STATE = 128
SSD_CONV = 4
SSD_CONV_DIM = SSD_D_INNER + 2 * SSD_GROUPS * SSD_STATE
SSD_CHUNK = 128
SC_DIM = 512
SC_WIDTH = 3
MOE_GROUPS = 4
EXPERTS_PER_GROUP = 8
N_EXPERTS = MOE_GROUPS * EXPERTS_PER_GROUP
EXPERT_FF = 512
LN_EPS = 1e-5
RMS_EPS = 1e-5

SUBLANES = 8
LANES = 128
VMEM_LIMIT = 56 * 1024 * 1024

COL_Q = 0
COL_Z = 512
COL_XBC = 1024
COL_SCB = 2048
COL_SCC = 2560
COL_GATES = 3072
COL_SCV = 6144
COL_KV = 6656
P_MAIN = 6912
DT_PAD = 128

ROW_TILE_IN = 1024
COL_TILE_IN = 2304
ROW_TILE = 512
MIX_ROWS = 128
DEC_SEQ = 8
SEQS_PER_STEP = MIX_ROWS // DEC_SEQ
EXPERT_BLOCK = 256
CHUNK_SIZES = (4, 2, 1)
CHUNK_BLOCKS = CHUNK_SIZES[0]
ROW_CHUNKS = D_MODEL // LANES
ISSUE_UNROLL = 8
ROUTE_EXPERT_ROW = 8
ROUTE_ROWS = 48


def _silu(v):
    return v * jax.nn.sigmoid(v)


def _dot(a, b):
    return jnp.dot(a, b, preferred_element_type=F32)


def _dot_nt(a, b):
    return lax.dot_general(a, b, (((1,), (1,)), ((), ())), preferred_element_type=F32)


def _dot_exact(a, b):
    return jnp.dot(a, b, preferred_element_type=F32, precision=lax.Precision.HIGHEST)


def _params(sem):
    return pltpu.CompilerParams(dimension_semantics=sem, vmem_limit_bytes=VMEM_LIMIT)


def _pick_mod(is_sample, prompt_ref, sample_ref):
    return jnp.where(is_sample, sample_ref[0], prompt_ref[0])


def _layer_norm(v, g, b):
    mu = jnp.mean(v, axis=-1, keepdims=True)
    c = v - mu
    var = jnp.mean(c * c, axis=-1, keepdims=True)
    return c * lax.rsqrt(var + LN_EPS) * g + b


def _mod_kernel(c_ref, w_ref, b_ref, o_ref):
    s = _silu(c_ref[...]).astype(BF16)
    o_ref[0] = _dot(s, w_ref[0].astype(BF16)) + b_ref[0]


def _modulation(c_all, w_ada, b_ada):
    depth, _, width = w_ada.shape
    n = c_all.shape[0]
    tn = 1536
    return pl.pallas_call(
        _mod_kernel,
        grid=(depth, width // tn),
        in_specs=[
            pl.BlockSpec((n, D_MODEL), lambda l, j: (0, 0)),
            pl.BlockSpec((1, D_MODEL, tn), lambda l, j: (l, 0, j)),
            pl.BlockSpec((1, 1, tn), lambda l, j: (l, 0, j)),
        ],
        out_specs=pl.BlockSpec((1, n, tn), lambda l, j: (l, 0, j)),
        out_shape=jax.ShapeDtypeStruct((depth, n, width), F32),
        compiler_params=_params(("parallel", "parallel")),
    )(c_all, w_ada, b_ada.reshape(depth, 1, width))


def _inproj_kernel(n_prompt_tiles, xp_ref, xs_ref, shp_ref, shs_ref, scp_ref, scs_ref, w_ref, wdt_ref,
                   p_ref, dt_ref, u_scr):
    i = pl.program_id(0)
    j = pl.program_id(1)

    @pl.when(j == 0)
    def _():
        is_sample = i >= n_prompt_tiles
        sh = _pick_mod(is_sample, shp_ref, shs_ref)
        sc = _pick_mod(is_sample, scp_ref, scs_ref)
        x = jnp.where(is_sample, xs_ref[...], xp_ref[...])
        u = (x * (1.0 + sc) + sh).astype(BF16)
        u_scr[...] = u
        dt_ref[...] = _dot(u, wdt_ref[...])

    p_ref[...] = _dot(u_scr[...], w_ref[...]).astype(BF16)


def _mod_specs(layer, col, row_tile, n_prompt_tiles, tiles_per_seq, n_seq):
    def prompt_map(i, *_):
        return (layer, jnp.minimum(i // tiles_per_seq, n_seq - 1), 0, col)

    def sample_map(i, *_):
        return (layer, 0, jnp.maximum(i - n_prompt_tiles, 0), col)

    return (pl.BlockSpec((None, 1, 1, D_MODEL), prompt_map),
            pl.BlockSpec((None, 1, row_tile, D_MODEL), sample_map))


def _layer_spec(layer, shape):
    return pl.BlockSpec((None, *shape), lambda *_: (layer,) + (0,) * len(shape))


def _input_projection(x, mod_p, mod_s, w_main, w_dt, layer, n_prompt, seq_len):
    x_prompt, x_sample = x
    t_all = n_prompt + x_sample.shape[0]
    tm, tn = ROW_TILE_IN, COL_TILE_IN
    npt = n_prompt // tm
    n_seq = mod_p.shape[1]
    shp, shs = _mod_specs(layer, 0, tm, npt, seq_len // tm, n_seq)
    scp, scs = _mod_specs(layer, 1, tm, npt, seq_len // tm, n_seq)
    return pl.pallas_call(
        functools.partial(_inproj_kernel, npt),
        grid=(t_all // tm, P_MAIN // tn),
        in_specs=[
            pl.BlockSpec((tm, D_MODEL), lambda i, j: (jnp.minimum(i, npt - 1), 0)),
            pl.BlockSpec((tm, D_MODEL), lambda i, j: (jnp.maximum(i - npt, 0), 0)),
            shp, shs, scp, scs,
            pl.BlockSpec((None, D_MODEL, tn), lambda i, j: (layer, 0, j)),
            _layer_spec(layer, (D_MODEL, DT_PAD)),
        ],
        out_specs=[
            pl.BlockSpec((tm, tn), lambda i, j: (i, j)),
            pl.BlockSpec((tm, DT_PAD), lambda i, j: (i, 0)),
        ],
        out_shape=[
            jax.ShapeDtypeStruct((t_all, P_MAIN), BF16),
            jax.ShapeDtypeStruct((t_all, DT_PAD), F32),
        ],
        scratch_shapes=[pltpu.VMEM((tm, D_MODEL), BF16)],
        compiler_params=_params(("parallel", "arbitrary")),
    )(x_prompt, x_sample, mod_p, mod_s, mod_p, mod_s, w_main, w_dt)


def _rope(v, cos, sin):
    width = v.shape[-1]
    reps = width // LANES
    if reps > 1:
        cos = jnp.concatenate([cos] * reps, axis=-1)
        sin = jnp.concatenate([sin] * reps, axis=-1)
    lane = lax.broadcasted_iota(jnp.int32, v.shape, 1) % HEAD_DIM
    partner = jnp.where(lane < ROT_HALF,
                        pltpu.roll(v, width - ROT_HALF, 1),
                        pltpu.roll(v, ROT_HALF, 1))
    return v * cos + partner * sin


def _shift_rows(cur, prev, k):
    axis = cur.ndim - 2
    idx = lax.broadcasted_iota(jnp.int32, cur.shape, axis)
    return jnp.where(idx < k, pltpu.roll(prev, k, axis), pltpu.roll(cur, k, axis))


def _causal_conv(cur, prev, w_ref, width):
    out = cur * w_ref[width - 1:width, :]
    for k in range(1, width):
        out = out + _shift_rows(cur, prev, k) * w_ref[width - 1 - k:width - k, :]
    return out


def _head_expand():
    r = lax.broadcasted_iota(jnp.int32, (LANES, SSD_D_INNER), 0)
    c = lax.broadcasted_iota(jnp.int32, (LANES, SSD_D_INNER), 1)
    return (c // SSD_HEAD_DIM == r).astype(F32)


def _head_expand_t():
    r = lax.broadcasted_iota(jnp.int32, (SSD_D_INNER, LANES), 0)
    c = lax.broadcasted_iota(jnp.int32, (SSD_D_INNER, LANES), 1)
    return (r // SSD_HEAD_DIM == c).astype(F32)


def _ssd_tile(act, dt_raw, dtb, alog, seq_rows):
    rows = MIX_ROWS
    xs = act[:, :SSD_D_INNER]
    bm = act[:, SSD_D_INNER:SSD_D_INNER + SSD_GROUPS * SSD_STATE]
    cm = act[:, SSD_D_INNER + SSD_GROUPS * SSD_STATE:]
    v = dt_raw + dtb
    dt = jnp.maximum(v, 0.0) + jnp.log1p(jnp.exp(-jnp.abs(v)))
    a = -jnp.exp(alog)
    dta = dt * a
    ri = lax.broadcasted_iota(jnp.int32, (rows, rows), 0)
    ci = lax.broadcasted_iota(jnp.int32, (rows, rows), 1)
    same = (ri // seq_rows) == (ci // seq_rows)
    causal = same & (ci <= ri)
    cs = _dot_exact(causal.astype(F32), dta)
    expand = _head_expand().astype(BF16)

    def per_head_lanes(v):
        hi = v.astype(BF16)
        lo = (v - hi.astype(F32)).astype(BF16)
        return _dot(hi, expand) + _dot(lo, expand)

    dt_e = per_head_lanes(dt)
    cs_e = per_head_lanes(cs)
    if seq_rows == rows:
        tot = None
        tot_e = cs_e[rows - 1:rows, :]
    else:
        tot = _dot_exact(same.astype(F32), dta)
        tot_e = per_head_lanes(tot)
    cs_t = cs.T
    dtx = xs * dt_e
    xw = dtx * jnp.exp(tot_e - cs_e)
    b_groups, c_groups, y_parts = [], [], []
    for g in range(SSD_GROUPS):
        bg = bm[:, g * SSD_STATE:(g + 1) * SSD_STATE].astype(BF16)
        cg = cm[:, g * SSD_STATE:(g + 1) * SSD_STATE].astype(BF16)
        b_groups.append(bg)
        c_groups.append(cg)
        cb = _dot_nt(cg, bg)
        for hh in range(SSD_HEADS_PER_GROUP):
            h = g * SSD_HEADS_PER_GROUP + hh
            seg = cs[:, h:h + 1] - cs_t[h:h + 1, :]
            decay = jnp.where(causal, jnp.exp(jnp.where(causal, seg, 0.0)), 0.0)
            y_parts.append(_dot((cb * decay).astype(BF16),
                                dtx[:, h * SSD_HEAD_DIM:(h + 1) * SSD_HEAD_DIM].astype(BF16)))
    y_diag = jnp.concatenate(y_parts, axis=-1)
    return xs, y_diag, jnp.exp(cs_e), xw, tot, tot_e, bm, b_groups, c_groups


def _gated_group_norm(y, z, nw):
    y = y * _silu(z)
    half = SSD_D_INNER // SSD_GROUPS
    parts = []
    for g in range(SSD_GROUPS):
        yg = y[:, g * half:(g + 1) * half]
        parts.append(yg * lax.rsqrt(jnp.mean(yg * yg, axis=-1, keepdims=True) + RMS_EPS))
    return jnp.concatenate(parts, axis=-1) * nw


def _attn_prompt_kernel(sink_ref, q_ref, kvc_ref, kvp_ref, cosc_ref, sinc_ref, cosp_ref, sinp_ref, bias_ref,
                        ya_ref, krot_ref):
    i = pl.program_id(1)
    nb = pl.num_programs(1)
    w = WINDOW
    q = _rope(q_ref[...].astype(F32), cosc_ref[...], sinc_ref[...])
    kvc = kvc_ref[...].astype(F32)
    kvp = kvp_ref[...].astype(F32)
    kc = _rope(kvc[:, :KV_WIDTH], cosc_ref[...], sinc_ref[...])
    kp = _rope(kvp[:, :KV_WIDTH], cosp_ref[...], sinp_ref[...])
    vc = kvc[:, KV_WIDTH:]
    vp = kvp[:, KV_WIDTH:]

    @pl.when(i == nb - 1)
    def _():
        krot_ref[0] = kc

    rows = Q_PER_KV * w
    bias = bias_ref[...]
    rcol = lax.broadcasted_iota(jnp.int32, (rows, 1), 0)
    outs = []
    for kh in range(N_KV_HEADS):
        hs = slice(kh * HEAD_DIM, (kh + 1) * HEAD_DIM)
        k2 = jnp.concatenate([kp[:, hs], kc[:, hs]], axis=0).astype(BF16)
        v2 = jnp.concatenate([vp[:, hs], vc[:, hs]], axis=0).astype(BF16)
        qg = jnp.concatenate(
            [q[:, (kh * Q_PER_KV + g) * HEAD_DIM:(kh * Q_PER_KV + g + 1) * HEAD_DIM] for g in range(Q_PER_KV)],
            axis=0).astype(BF16)
        sink = jnp.zeros((rows, 1), F32)
        for g in range(Q_PER_KV):
            sink = jnp.where(rcol // w == g, sink_ref[kh * Q_PER_KV + g], sink)
        logits = _dot_nt(qg, k2) * ATTN_SCALE + bias
        m = jnp.maximum(jnp.max(logits, axis=-1, keepdims=True), sink)
        e = jnp.exp(logits - m)
        den = jnp.sum(e, axis=-1, keepdims=True) + jnp.exp(sink - m)
        o = _dot(e.astype(BF16), v2) / den
        for g in range(Q_PER_KV):
            outs.append(o[g * w:(g + 1) * w, :])
    ya_ref[...] = jnp.concatenate(outs, axis=-1).astype(BF16)


def _band_bias():
    w = WINDOW
    r = np.arange(Q_PER_KV * w)[:, None] % w
    s = np.arange(2 * w)[None, :]
    diff = w + r - s
    band = (diff >= 0) & (diff < w)
    return jnp.asarray(np.where(np.stack([band & (s >= w), band]), 0.0, -np.inf), F32)


def _attention_prompt(p, sink, cos_p, sin_p, n_seq, seq_len):
    nb = seq_len // WINDOW
    n_prompt = n_seq * seq_len
    w = WINDOW

    def cur(b, i):
        return b * nb + i

    def prev(b, i):
        return jnp.maximum(b * nb + i - 1, 0)

    return pl.pallas_call(
        _attn_prompt_kernel,
        grid=(n_seq, nb),
        in_specs=[
            pl.BlockSpec(memory_space=pltpu.SMEM),
            pl.BlockSpec((w, ATTN_WIDTH), lambda b, i: (cur(b, i), COL_Q // ATTN_WIDTH)),
            pl.BlockSpec((w, 2 * KV_WIDTH), lambda b, i: (cur(b, i), COL_KV // (2 * KV_WIDTH))),
            pl.BlockSpec((w, 2 * KV_WIDTH), lambda b, i: (prev(b, i), COL_KV // (2 * KV_WIDTH))),
            pl.BlockSpec((w, LANES), lambda b, i: (i, 0)),
            pl.BlockSpec((w, LANES), lambda b, i: (i, 0)),
            pl.BlockSpec((w, LANES), lambda b, i: (jnp.maximum(i - 1, 0), 0)),
            pl.BlockSpec((w, LANES), lambda b, i: (jnp.maximum(i - 1, 0), 0)),
            pl.BlockSpec((None, Q_PER_KV * w, 2 * w), lambda b, i: (jnp.minimum(i, 1), 0, 0)),
        ],
        out_specs=[
            pl.BlockSpec((w, ATTN_WIDTH), lambda b, i: (cur(b, i), 0)),
            pl.BlockSpec((1, w, KV_WIDTH), lambda b, i: (b, 0, 0)),
        ],
        out_shape=[
            jax.ShapeDtypeStruct((n_prompt, ATTN_WIDTH), BF16),
            jax.ShapeDtypeStruct((n_seq, w, KV_WIDTH), F32),
        ],
        compiler_params=_params(("parallel", "arbitrary")),
    )(sink, p, p, p, cos_p, sin_p, cos_p, sin_p, _band_bias())


def _ssd_prompt_kernel(z_ref, xc_ref, xp_ref, dt_ref, scb_ref, sccc_ref, sccp_ref, scvc_ref, scvp_ref,
                       cw_ref, cb_ref, dtb_ref, alog_ref, de_ref, nw_ref, scw_ref,
                       yb_ref, yc_ref, hout_ref, cvlast_ref, h_scr):
    i = pl.program_id(1)
    nc = pl.num_programs(1)
    first = i == 0

    @pl.when(first)
    def _():
        h_scr[...] = jnp.zeros_like(h_scr)

    xc = xc_ref[...].astype(F32)
    xp = jnp.where(first, 0.0, xp_ref[...].astype(F32))
    act = _silu(_causal_conv(xc, xp, cw_ref, SSD_CONV) + cb_ref[...])
    xs, y_diag, ecs_e, xw, _, tot_e, bm, _, c_groups = _ssd_tile(
        act, dt_ref[...], dtb_ref[...], alog_ref[...], MIX_ROWS)
    gw = SSD_HEADS_PER_GROUP * SSD_HEAD_DIM
    y_off = []
    for g in range(SSD_GROUPS):
        cols = slice(g * gw, (g + 1) * gw)
        hg = h_scr[:, cols]
        y_off.append(_dot(c_groups[g], hg.astype(BF16)))
        b_t = bm[:, g * SSD_STATE:(g + 1) * SSD_STATE].T.astype(BF16)
        h_scr[:, cols] = jnp.exp(tot_e[:, cols]) * hg + _dot(b_t, xw[:, cols].astype(BF16))
    y = y_diag + jnp.concatenate(y_off, axis=-1) * ecs_e + de_ref[...] * xs
    yb_ref[...] = _gated_group_norm(y, z_ref[...].astype(F32), nw_ref[...]).astype(BF16)

    cvc = sccc_ref[...].astype(F32) * scvc_ref[...].astype(F32)
    cvp = jnp.where(first, 0.0, sccp_ref[...].astype(F32) * scvp_ref[...].astype(F32))
    conv_c = _causal_conv(cvc, cvp, scw_ref, SC_WIDTH)
    yc_ref[...] = (scb_ref[...].astype(F32) * conv_c).astype(BF16)

    @pl.when(i == nc - 1)
    def _():
        hout_ref[0] = h_scr[...].T
        cvlast_ref[0] = cvc[MIX_ROWS - SUBLANES:, :]


def _ssd_prompt(p, dt, cw, cb, dtb, alog, de, nw, scw, n_seq, seq_len):
    nc = seq_len // MIX_ROWS
    r = MIX_ROWS
    n_prompt = n_seq * seq_len

    def cur(b, i):
        return b * nc + i

    def prev(b, i):
        return jnp.maximum(b * nc + i - 1, 0)

    def col(width, offset, which):
        return pl.BlockSpec((r, width), lambda b, i: (which(b, i), offset // width))

    def const(shape):
        return pl.BlockSpec(shape, lambda b, i: (0,) * len(shape))

    return pl.pallas_call(
        _ssd_prompt_kernel,
        grid=(n_seq, nc),
        in_specs=[
            col(SSD_D_INNER, COL_Z, cur),
            col(SSD_CONV_DIM, COL_XBC, cur), col(SSD_CONV_DIM, COL_XBC, prev),
            pl.BlockSpec((r, DT_PAD), lambda b, i: (cur(b, i), 0)),
            col(SC_DIM, COL_SCB, cur),
            col(SC_DIM, COL_SCC, cur), col(SC_DIM, COL_SCC, prev),
            col(SC_DIM, COL_SCV, cur), col(SC_DIM, COL_SCV, prev),
            const((SSD_CONV, SSD_CONV_DIM)), const((1, SSD_CONV_DIM)),
            const((1, DT_PAD)), const((1, DT_PAD)), const((1, SSD_D_INNER)), const((1, SSD_D_INNER)),
            const((SC_WIDTH, SC_DIM)),
        ],
        out_specs=[
            pl.BlockSpec((r, SSD_D_INNER), lambda b, i: (cur(b, i), 0)),
            pl.BlockSpec((r, SC_DIM), lambda b, i: (cur(b, i), 0)),
            pl.BlockSpec((1, SSD_D_INNER, SSD_STATE), lambda b, i: (b, 0, 0)),
            pl.BlockSpec((1, SUBLANES, SC_DIM), lambda b, i: (b, 0, 0)),
        ],
        out_shape=[
            jax.ShapeDtypeStruct((n_prompt, SSD_D_INNER), BF16),
            jax.ShapeDtypeStruct((n_prompt, SC_DIM), BF16),
            jax.ShapeDtypeStruct((n_seq, SSD_D_INNER, SSD_STATE), F32),
            jax.ShapeDtypeStruct((n_seq, SUBLANES, SC_DIM), F32),
        ],
        scratch_shapes=[pltpu.VMEM((SSD_STATE, SSD_D_INNER), F32)],
        compiler_params=_params(("parallel", "arbitrary")),
    )(p, p, p, dt, p, p, p, p, p, cw, cb, dtb, alog, de, nw, scw)


def _mix_sample_kernel(sink_ref, q_ref, kv_ref, z_ref, x_ref, dt_ref, scb_ref, scc_ref, scv_ref,
                       cos_ref, sin_ref, ck_ref, cv_ref, h0_ref, xbuf_ref, cbuf_ref,
                       cw_ref, cb_ref, dtb_ref, alog_ref, de_ref, nw_ref, scw_ref,
                       ya_ref, yb_ref, yc_ref, knew_ref, vnew_ref, hnew_ref, cvout_ref):
    ns, t = SEQS_PER_STEP, DEC_SEQ
    wb = ck_ref.shape[1]

    q = _rope(q_ref[...].astype(F32), cos_ref[...], sin_ref[...])
    kv = kv_ref[...].astype(F32)
    kn = _rope(kv[:, :KV_WIDTH], cos_ref[...], sin_ref[...])
    vn = kv[:, KV_WIDTH:]
    q3 = q.reshape(ns, t, ATTN_WIDTH)
    kn3 = kn.reshape(ns, t, KV_WIDTH)
    vn3 = vn.reshape(ns, t, KV_WIDTH)
    knew_ref[:, :wb - t, :] = ck_ref[:, t:, :]
    knew_ref[:, wb - t:, :] = kn3
    vnew_ref[:, :wb - t, :] = cv_ref[:, t:, :]
    vnew_ref[:, wb - t:, :] = vn3
    nq = Q_PER_KV * t
    qi = lax.broadcasted_iota(jnp.int32, (ns, nq, wb + t), 1) % t
    si = lax.broadcasted_iota(jnp.int32, (ns, nq, wb + t), 2)
    valid = ((si < wb) & (si > qi + (wb - WINDOW))) | ((si >= wb) & (si - wb <= qi))
    hrow = lax.broadcasted_iota(jnp.int32, (ns, nq, 1), 1) // t
    heads = [None] * N_HEADS
    for kh in range(N_KV_HEADS):
        hs = slice(kh * HEAD_DIM, (kh + 1) * HEAD_DIM)
        k_all = jnp.concatenate([ck_ref[:, :, hs], kn3[:, :, hs]], axis=1).astype(BF16)
        v_all = jnp.concatenate([cv_ref[:, :, hs], vn3[:, :, hs]], axis=1).astype(BF16)
        qg = jnp.concatenate(
            [q3[:, :, (kh * Q_PER_KV + g) * HEAD_DIM:(kh * Q_PER_KV + g + 1) * HEAD_DIM] for g in range(Q_PER_KV)],
            axis=1).astype(BF16)
        sink = jnp.zeros((ns, nq, 1), F32)
        for g in range(Q_PER_KV):
            sink = jnp.where(hrow == g, sink_ref[kh * Q_PER_KV + g], sink)
        logits = jnp.einsum('bqd,bsd->bqs', qg, k_all, preferred_element_type=F32) * ATTN_SCALE
        logits = jnp.where(valid, logits, -jnp.inf)
        m = jnp.maximum(jnp.max(logits, axis=-1, keepdims=True), sink)
        e = jnp.exp(logits - m)
        den = jnp.sum(e, axis=-1, keepdims=True) + jnp.exp(sink - m)
        o = jnp.einsum('bqs,bsd->bqd', e.astype(BF16), v_all, preferred_element_type=F32) / den
        for g in range(Q_PER_KV):
            heads[kh * Q_PER_KV + g] = o[:, g * t:(g + 1) * t, :]
    ya_ref[...] = jnp.concatenate(heads, axis=-1).reshape(ns * t, ATTN_WIDTH).astype(BF16)

    xc3 = x_ref[...].astype(F32).reshape(ns, t, SSD_CONV_DIM)
    xp3 = xbuf_ref[...].reshape(ns, t, SSD_CONV_DIM)
    conv = _causal_conv(xc3, xp3, cw_ref, SSD_CONV).reshape(ns * t, SSD_CONV_DIM)
    act = _silu(conv + cb_ref[...])
    xs, y_diag, ecs_e, xw, tot, _, _, b_groups, c_groups = _ssd_tile(
        act, dt_ref[...], dtb_ref[...], alog_ref[...], t)
    xw_t = xw.T
    dec_t = jnp.exp(_dot_exact(_head_expand_t(), tot.T))
    gw = SSD_HEADS_PER_GROUP * SSD_HEAD_DIM
    y_off = []
    for g in range(SSD_GROUPS):
        c3 = c_groups[g].reshape(ns, t, SSD_STATE)
        hg = h0_ref[:, g * gw:(g + 1) * gw, :].astype(BF16)
        y_off.append(jnp.einsum('btn,bqn->btq', c3, hg, preferred_element_type=F32).reshape(ns * t, gw))
    y = y_diag + jnp.concatenate(y_off, axis=-1) * ecs_e + de_ref[...] * xs
    yb_ref[...] = _gated_group_norm(y, z_ref[...].astype(F32), nw_ref[...]).astype(BF16)
    col = lax.broadcasted_iota(jnp.int32, xw_t.shape, 1) // t
    for b in range(ns):
        xw_b = jnp.where(col == b, xw_t, jnp.zeros_like(xw_t))
        upd = jnp.concatenate(
            [_dot(xw_b[g * gw:(g + 1) * gw, :].astype(BF16), b_groups[g]) for g in range(SSD_GROUPS)], axis=0)
        hnew_ref[b] = dec_t[:, b * t:b * t + 1] * h0_ref[b] + upd

    cvc = scc_ref[...].astype(F32) * scv_ref[...].astype(F32)
    cvout_ref[...] = cvc
    conv_c = _causal_conv(cvc.reshape(ns, t, SC_DIM), cbuf_ref[...].reshape(ns, t, SC_DIM), scw_ref, SC_WIDTH)
    yc_ref[...] = (scb_ref[...].astype(F32) * conv_c.reshape(ns * t, SC_DIM)).astype(BF16)


def _mix_sample(p, dt, sink, cos_s, sin_s, cache_k, cache_v, h0, xbuf, cbuf,
                cw, cb, dtb, alog, de, nw, scw, layer, n_prompt, n_dec):
    r = MIX_ROWS
    ns = SEQS_PER_STEP
    steps = n_dec // ns
    base = n_prompt // r
    wb = cache_k.shape[2]

    def col(width, offset):
        return pl.BlockSpec((r, width), lambda i: (base + i, offset // width))

    def state(shape):
        return pl.BlockSpec((None, *shape), lambda i: (layer, i) + (0,) * (len(shape) - 1))

    def const(shape):
        return pl.BlockSpec(shape, lambda i: (0,) * len(shape))

    def rows(width):
        return pl.BlockSpec((r, width), lambda i: (i, 0))

    n_rows = n_dec * DEC_SEQ
    return pl.pallas_call(
        _mix_sample_kernel,
        grid=(steps,),
        in_specs=[
            pl.BlockSpec(memory_space=pltpu.SMEM),
            col(ATTN_WIDTH, COL_Q), col(2 * KV_WIDTH, COL_KV), col(SSD_D_INNER, COL_Z),
            col(SSD_CONV_DIM, COL_XBC),
            pl.BlockSpec((r, DT_PAD), lambda i: (base + i, 0)),
            col(SC_DIM, COL_SCB), col(SC_DIM, COL_SCC), col(SC_DIM, COL_SCV),
            const((r, LANES)), const((r, LANES)),
            state((ns, wb, KV_WIDTH)), state((ns, wb, KV_WIDTH)),
            state((ns, SSD_D_INNER, SSD_STATE)),
            state((r, SSD_CONV_DIM)), state((r, SC_DIM)),
            const((SSD_CONV, SSD_CONV_DIM)), const((1, SSD_CONV_DIM)),
            const((1, DT_PAD)), const((1, DT_PAD)), const((1, SSD_D_INNER)), const((1, SSD_D_INNER)),
            const((SC_WIDTH, SC_DIM)),
        ],
        out_specs=[
            rows(ATTN_WIDTH), rows(SSD_D_INNER), rows(SC_DIM),
            pl.BlockSpec((ns, wb, KV_WIDTH), lambda i: (i, 0, 0)),
            pl.BlockSpec((ns, wb, KV_WIDTH), lambda i: (i, 0, 0)),
            pl.BlockSpec((ns, SSD_D_INNER, SSD_STATE), lambda i: (i, 0, 0)),
            rows(SC_DIM),
        ],
        out_shape=[
            jax.ShapeDtypeStruct((n_rows, ATTN_WIDTH), BF16),
            jax.ShapeDtypeStruct((n_rows, SSD_D_INNER), BF16),
            jax.ShapeDtypeStruct((n_rows, SC_DIM), BF16),
            jax.ShapeDtypeStruct((n_dec, wb, KV_WIDTH), F32),
            jax.ShapeDtypeStruct((n_dec, wb, KV_WIDTH), F32),
            jax.ShapeDtypeStruct((n_dec, SSD_D_INNER, SSD_STATE), F32),
            jax.ShapeDtypeStruct((n_rows, SC_DIM), F32),
        ],
        compiler_params=_params(("parallel",)),
    )(sink, p, p, p, p, dt, p, p, p, cos_s, sin_s, cache_k, cache_v, h0, xbuf, cbuf,
      cw, cb, dtb, alog, de, nw, scw)


def _store_token_major(ref, v):
    n = v.shape[0]
    for s in range(ROW_CHUNKS):
        ref[pl.ds(s, n, stride=ROW_CHUNKS), :] = v[:, s * LANES:(s + 1) * LANES]


def _load_token_major(ref, n):
    return jnp.concatenate([ref[pl.ds(s, n, stride=ROW_CHUNKS), :] for s in range(ROW_CHUNKS)], axis=-1)


def _outproj_kernel(n_prompt_tiles, alpha,
                    xp_ref, xs_ref, yap_ref, yas_ref, ybp_ref, ybs_ref, ycp_ref, ycs_ref, g_ref,
                    g1p_ref, g1s_ref, sh2p_ref, sh2s_ref, sc2p_ref, sc2s_ref,
                    wpa_ref, wpb_ref, wpc_ref, wout_ref, wr_ref, br_ref, lng_ref, lnb_ref,
                    x1_ref, h_ref, route_ref, cnt_ref, cnt_scr):
    i = pl.program_id(0)
    is_sample = i >= n_prompt_tiles
    tm = xp_ref.shape[0]

    @pl.when(i == 0)
    def _():
        cnt_scr[...] = jnp.zeros_like(cnt_scr)

    n = tm
    cnt = cnt_scr[...]
    for r0 in range(0, tm, n):
        rs = slice(r0, r0 + n)

        def pick(prompt_ref, sample_ref):
            return jnp.where(is_sample, sample_ref[0, rs, :], prompt_ref[0])

        ya = jnp.where(is_sample, yas_ref[rs, :], yap_ref[rs, :])
        yb = jnp.where(is_sample, ybs_ref[rs, :], ybp_ref[rs, :])
        yc = jnp.where(is_sample, ycs_ref[rs, :], ycp_ref[rs, :])
        gates = 0.5 * jnp.tanh(0.5 * g_ref[rs, :]) + 0.5
        merged = (gates[:, :D_MODEL] * _dot(ya, wpa_ref[...]).astype(BF16)
                  + gates[:, D_MODEL:2 * D_MODEL] * _dot(yb, wpb_ref[...]).astype(BF16)
                  + gates[:, 2 * D_MODEL:] * _dot(yc, wpc_ref[...]).astype(BF16))
        mix = _dot(merged, wout_ref[...])
        g1 = pick(g1p_ref, g1s_ref)
        x = jnp.where(is_sample, xs_ref[rs, :], xp_ref[rs, :])
        x1 = _layer_norm(alpha * x + g1 * mix, lng_ref[...], lnb_ref[...])
        x1_ref[rs, :] = x1
        sh2 = pick(sh2p_ref, sh2s_ref)
        sc2 = pick(sc2p_ref, sc2s_ref)
        h = x1 * (1.0 + sc2) + sh2
        _store_token_major(h_ref.at[pl.ds(r0 * ROW_CHUNKS, n * ROW_CHUNKS)], h)

        lt = (_dot(h.astype(BF16), wr_ref[...]) + br_ref[...]).T[:ROUTE_ROWS]
        row = lax.broadcasted_iota(jnp.int32, lt.shape, 0).astype(F32)
        neg = -jnp.inf
        big = float(ROUTE_ROWS)
        gl = jnp.where(row < MOE_GROUPS, lt, neg)
        gmax = jnp.max(gl, axis=0, keepdims=True)
        g_p = 1.0 / jnp.sum(jnp.exp(gl - gmax), axis=0, keepdims=True)
        gidx = jnp.min(jnp.where(gl == gmax, row, big), axis=0, keepdims=True)
        lo = ROUTE_EXPERT_ROW + EXPERTS_PER_GROUP * gidx
        sel = jnp.where((row >= lo) & (row < lo + EXPERTS_PER_GROUP), lt, neg)
        m1 = jnp.max(sel, axis=0, keepdims=True)
        i1 = jnp.min(jnp.where(sel == m1, row, big), axis=0, keepdims=True)
        sel2 = jnp.where(row == i1, neg, sel)
        m2 = jnp.max(sel2, axis=0, keepdims=True)
        i2 = jnp.min(jnp.where(sel2 == m2, row, big), axis=0, keepdims=True)
        ssum = jnp.sum(jnp.exp(sel - m1), axis=0, keepdims=True)
        p1 = 1.0 / ssum
        p2 = jnp.exp(m2 - m1) / ssum
        w1 = g_p * (p1 / (p1 + p2))
        w2 = g_p * (p2 / (p1 + p2))

        onehot = jnp.where((row == i1) | (row == i2), 1.0, 0.0)
        ri = lax.broadcasted_iota(jnp.int32, (n, n), 0)
        ci = lax.broadcasted_iota(jnp.int32, (n, n), 1)
        earlier = jnp.where(ri < ci, 1.0, 0.0).astype(BF16)
        prefix = _dot(onehot.astype(BF16), earlier) + cnt[:, 0:1]
        rank1 = jnp.sum(jnp.where(row == i1, prefix, 0.0), axis=0, keepdims=True)
        rank2 = jnp.sum(jnp.where(row == i2, prefix, 0.0), axis=0, keepdims=True)
        cnt = cnt + jnp.sum(onehot, axis=1, keepdims=True)
        zero = jnp.zeros_like(w1)
        route_ref[:, rs] = jnp.concatenate(
            [i1 - ROUTE_EXPERT_ROW, i2 - ROUTE_EXPERT_ROW, w1, w2, rank1, rank2, zero, zero], axis=0)

    cnt_scr[...] = cnt
    cnt_ref[...] = cnt


def _output_projection(x, y_prompt, y_sample, p, mod_p, mod_s, wpa, wpb, wpc, wout, wr, br, lng, lnb,
                       layer, n_prompt, seq_len, alpha):
    t_all = n_prompt + x[1].shape[0]
    tm = ROW_TILE
    npt = n_prompt // tm
    n_seq = mod_p.shape[1]
    mods = []
    for col in (2, 3, 4):
        mods.extend(_mod_specs(layer, col, tm, npt, seq_len // tm, n_seq))

    def rows(width):
        return pl.BlockSpec((tm, width), lambda i: (i, 0))

    def prompt_rows(width):
        return pl.BlockSpec((tm, width), lambda i: (jnp.minimum(i, npt - 1), 0))

    def sample_rows(width):
        return pl.BlockSpec((tm, width), lambda i: (jnp.maximum(i - npt, 0), 0))

    def const(shape):
        return pl.BlockSpec(shape, lambda i: (0,) * len(shape))

    return pl.pallas_call(
        functools.partial(_outproj_kernel, npt, alpha),
        grid=(t_all // tm,),
        in_specs=[
            prompt_rows(D_MODEL), sample_rows(D_MODEL),
            prompt_rows(ATTN_WIDTH), sample_rows(ATTN_WIDTH),
            prompt_rows(SSD_D_INNER), sample_rows(SSD_D_INNER),
            prompt_rows(SC_DIM), sample_rows(SC_DIM),
            pl.BlockSpec((tm, 3 * D_MODEL), lambda i: (i, COL_GATES // (3 * D_MODEL))),
            *mods,
            _layer_spec(layer, (ATTN_WIDTH, D_MODEL)), _layer_spec(layer, (SSD_D_INNER, D_MODEL)),
            _layer_spec(layer, (SC_DIM, D_MODEL)), _layer_spec(layer, (D_MODEL, D_MODEL)),
            _layer_spec(layer, (D_MODEL, LANES)), _layer_spec(layer, (1, LANES)),
            const((1, D_MODEL)), const((1, D_MODEL)),
        ],
        out_specs=[
            rows(D_MODEL),
            pl.BlockSpec((tm * ROW_CHUNKS, LANES), lambda i: (i, 0)),
            pl.BlockSpec((SUBLANES, tm), lambda i: (0, i)),
            const((ROUTE_ROWS, LANES)),
        ],
        out_shape=[
            jax.ShapeDtypeStruct((t_all, D_MODEL), F32),
            jax.ShapeDtypeStruct((t_all * ROW_CHUNKS, LANES), F32),
            jax.ShapeDtypeStruct((SUBLANES, t_all), F32),
            jax.ShapeDtypeStruct((ROUTE_ROWS, LANES), F32),
        ],
        scratch_shapes=[pltpu.VMEM((ROUTE_ROWS, LANES), F32)],
        compiler_params=_params(("arbitrary",)),
    )(x[0], x[1], y_prompt[0], y_sample[0], y_prompt[1], y_sample[1], y_prompt[2], y_sample[2],
      p, mod_p, mod_s, mod_p, mod_s, mod_p, mod_s,
      wpa, wpb, wpc, wout, wr, br, lng, lnb)


def _slots_kernel(route_ref, start_ref, dest_ref):
    route = route_ref[...]
    tm = route.shape[1]
    expert = lax.broadcasted_iota(jnp.int32, (N_EXPERTS, tm), 0).astype(F32)
    start = start_ref[:, 0:1]
    rows = []
    for e_row, r_row in ((0, 4), (1, 5)):
        first = jnp.sum(jnp.where(expert == route[e_row:e_row + 1], start, 0.0), axis=0, keepdims=True)
        rows.append(first + route[r_row:r_row + 1])
    rows.append(jnp.zeros((SUBLANES - 2, tm), F32))
    dest_ref[...] = jnp.concatenate(rows, axis=0).astype(jnp.int32)


def _slots(route, pad_start):
    t_all = route.shape[1]
    tm = ROW_TILE_IN
    dest = pl.pallas_call(
        _slots_kernel,
        grid=(t_all // tm,),
        in_specs=[pl.BlockSpec((SUBLANES, tm), lambda i: (0, i)),
                  pl.BlockSpec((N_EXPERTS, LANES), lambda i: (0, 0))],
        out_specs=pl.BlockSpec((SUBLANES, tm), lambda i: (0, i)),
        out_shape=jax.ShapeDtypeStruct((SUBLANES, t_all), jnp.int32),
        compiler_params=_params(("parallel",)),
    )(route, jnp.broadcast_to(pad_start.astype(F32)[:, None], (N_EXPERTS, LANES)))
    return dest[0], dest[1]


def _token_copy(src, src_row, dst, dst_row, sem):
    return pltpu.make_async_copy(
        src.at[pl.ds(pl.multiple_of(src_row * ROW_CHUNKS, ROW_CHUNKS), ROW_CHUNKS)],
        dst.at[pl.ds(pl.multiple_of(dst_row * ROW_CHUNKS, ROW_CHUNKS), ROW_CHUNKS)],
        sem)


def _dispatch_kernel(d1_ref, d2_ref, pend_ref, cnt_ref, nb_ref, h_ref, xb_ref, zero_scr, zsem, sem):
    i = pl.program_id(0)
    tm = h_ref.shape[0] // ROW_CHUNKS
    blk_rows = EXPERT_BLOCK * ROW_CHUNKS
    n_blocks = xb_ref.shape[0] // blk_rows

    def zero_block(b):
        start = pl.multiple_of(b * blk_rows, blk_rows)
        return pltpu.make_async_copy(zero_scr, xb_ref.at[pl.ds(start, blk_rows)], zsem)

    def last_block(e):
        return pend_ref[e] // EXPERT_BLOCK - 1

    @pl.when(i == 0)
    def _():
        zero_scr[...] = jnp.zeros_like(zero_scr)
        for e in range(N_EXPERTS):
            @pl.when(cnt_ref[e] > 0)
            def _():
                zero_block(last_block(e)).start()
        lax.fori_loop(nb_ref[0], n_blocks, lambda b, c: (zero_block(b).start(), c)[1], 0)
        for e in range(N_EXPERTS):
            @pl.when(cnt_ref[e] > 0)
            def _():
                zero_block(last_block(e)).wait()
        lax.fori_loop(nb_ref[0], n_blocks, lambda b, c: (zero_block(b).wait(), c)[1], 0)

    def issue(c, carry):
        for u in range(ISSUE_UNROLL):
            t = c * ISSUE_UNROLL + u
            g = i * tm + t
            _token_copy(h_ref, t, xb_ref, d1_ref[g], sem.at[0]).start(priority=0)
            _token_copy(h_ref, t, xb_ref, d2_ref[g], sem.at[1]).start(priority=1)
        return carry

    lax.fori_loop(0, tm // ISSUE_UNROLL, issue, 0)
    for k in range(2):
        pltpu.make_async_copy(h_ref, xb_ref.at[pl.ds(0, tm * ROW_CHUNKS)], sem.at[k]).wait()


def _dispatch(h, dest1, dest2, pad_end, counts, n_used, n_slots):
    t_all = h.shape[0] // ROW_CHUNKS
    tm = ROW_TILE
    grid_spec = pltpu.PrefetchScalarGridSpec(
        num_scalar_prefetch=5,
        grid=(t_all // tm,),
        in_specs=[pl.BlockSpec((tm * ROW_CHUNKS, LANES), lambda i, *_: (i, 0))],
        out_specs=pl.BlockSpec(memory_space=pl.ANY),
        scratch_shapes=[
            pltpu.VMEM((EXPERT_BLOCK * ROW_CHUNKS, LANES), F32),
            pltpu.SemaphoreType.DMA(()),
            pltpu.SemaphoreType.DMA((2,)),
        ],
    )
    return pl.pallas_call(
        _dispatch_kernel,
        grid_spec=grid_spec,
        out_shape=jax.ShapeDtypeStruct((n_slots * ROW_CHUNKS, LANES), F32),
        compiler_params=_params(("arbitrary",)),
    )(dest1, dest2, pad_end, counts, n_used, h)


def _ffn_kernel(layer, cstart_ref, cbig_ref, cexp_ref, next_ref, meta_ref,
                xb_ref, wg_ref, wu_ref, wd_ref, yb_ref,
                x_in, y_out, wg_f, wu_f, wd_f, wg_b, wu_b, wd_b, x_scr, xsem, ysem, wsem):
    blk_rows = EXPERT_BLOCK * ROW_CHUNKS
    n_blocks = yb_ref.shape[0] // blk_rows
    n_chunks, first_expert, n_used = meta_ref[0], meta_ref[1], meta_ref[2]

    def chunk_blocks(size_class):
        return CHUNK_SIZES[size_class]

    def hbm_rows(ref, j, big):
        start = pl.multiple_of(cstart_ref[j] * blk_rows, blk_rows)
        return ref.at[pl.ds(start, chunk_blocks(big) * blk_rows)]

    def x_copy(j, s, big):
        return pltpu.make_async_copy(hbm_rows(xb_ref, j, big),
                                     x_in.at[s, pl.ds(0, chunk_blocks(big) * blk_rows)], xsem.at[s])

    def y_copy(j, s, big):
        return pltpu.make_async_copy(y_out.at[s, pl.ds(0, chunk_blocks(big) * blk_rows)],
                                     hbm_rows(yb_ref, j, big), ysem.at[s])

    def by_size(j, fn):
        for big in range(len(CHUNK_SIZES)):
            @pl.when(cbig_ref[j] == big)
            def _():
                fn(big)

    def w_copies(e, s):
        return (pltpu.make_async_copy(wg_ref.at[layer, e], wg_f.at[s], wsem.at[s, 0]),
                pltpu.make_async_copy(wu_ref.at[layer, e], wu_f.at[s], wsem.at[s, 1]),
                pltpu.make_async_copy(wd_ref.at[layer, e], wd_f.at[s], wsem.at[s, 2]))

    for c in w_copies(first_expert, 0):
        c.start(priority=1)
    by_size(0, lambda big: x_copy(0, 0, big).start())

    def chunk(j, wslot):
        e = cexp_ref[j]
        new_expert = (j == 0) | (cexp_ref[jnp.maximum(j - 1, 0)] != e)
        wslot = jnp.where(new_expert & (j > 0), 1 - wslot, wslot)

        @pl.when(new_expert)
        def _():
            for c in w_copies(e, wslot):
                c.wait()

            @pl.when(next_ref[e] < N_EXPERTS)
            def _():
                for c in w_copies(next_ref[e], 1 - wslot):
                    c.start(priority=1)

            wg_b[...] = wg_f[wslot].astype(BF16)
            wu_b[...] = wu_f[wslot].astype(BF16)
            wd_b[...] = wd_f[wslot].astype(BF16)

        s = j % 2
        by_size(j, lambda big: x_copy(j, s, big).wait())

        @pl.when(j + 1 < n_chunks)
        def _():
            by_size(j + 1, lambda big: x_copy(j + 1, 1 - s, big).start())

        @pl.when(j >= 2)
        def _():
            by_size(j - 2, lambda big: y_copy(j - 2, s, big).wait())

        def compute(big):
            rows = chunk_blocks(big) * EXPERT_BLOCK
            x_tok = x_in.at[s]
            for c in range(ROW_CHUNKS):
                x_scr[0:rows, c * LANES:(c + 1) * LANES] = (
                    x_tok[pl.ds(c, rows, stride=ROW_CHUNKS), :].astype(BF16))
            x = x_scr[0:rows, :]
            act = (_silu(_dot(x, wg_b[...])) * _dot(x, wu_b[...])).astype(BF16)
            _store_token_major(y_out.at[s, pl.ds(0, rows * ROW_CHUNKS)], _dot(act, wd_b[...]))
            y_copy(j, s, big).start()

        by_size(j, compute)
        return wslot

    lax.fori_loop(0, n_chunks, chunk, 0)

    @pl.when(n_chunks >= 2)
    def _():
        by_size(n_chunks - 2, lambda big: y_copy(n_chunks - 2, n_chunks % 2, big).wait())

    by_size(n_chunks - 1, lambda big: y_copy(n_chunks - 1, (n_chunks - 1) % 2, big).wait())

    zero_rows = y_out.at[0, pl.ds(0, blk_rows)]
    zero_rows[...] = jnp.zeros((blk_rows, LANES), F32)

    def tail_copy(b):
        start = pl.multiple_of(b * blk_rows, blk_rows)
        return pltpu.make_async_copy(zero_rows, yb_ref.at[pl.ds(start, blk_rows)], ysem.at[0])

    lax.fori_loop(n_used, n_blocks, lambda b, c: (tail_copy(b).start(), c)[1], 0)
    lax.fori_loop(n_used, n_blocks, lambda b, c: (tail_copy(b).wait(), c)[1], 0)


def _chunk_plan(first_block, n_expert_blocks, n_blocks):
    experts = jnp.arange(N_EXPERTS, dtype=jnp.int32)
    per_size, left = [], n_expert_blocks
    for size in CHUNK_SIZES:
        per_size.append(left // size)
        left = left % size
    n_chunks_e = sum(per_size)
    chunk_end = jnp.cumsum(n_chunks_e)
    chunk_start = chunk_end - n_chunks_e
    j = jnp.arange(n_blocks, dtype=jnp.int32)
    e_j = jnp.minimum(jnp.sum((chunk_end[None, :] <= j[:, None]).astype(jnp.int32), axis=1), N_EXPERTS - 1)
    pick = (e_j[:, None] == experts[None, :]).astype(jnp.int32)

    def of_expert(v):
        return jnp.sum(pick * v[None, :], axis=1)

    c = j - of_expert(chunk_start)
    start = of_expert(first_block)
    size_class = jnp.zeros_like(j)
    for k, size in enumerate(CHUNK_SIZES):
        n_k = of_expert(per_size[k])
        inside = (c >= 0) & (c < n_k)
        start = start + jnp.where(inside, size * c, jnp.where(c >= n_k, size * n_k, 0))
        size_class = jnp.where(inside, k, size_class)
        c = jnp.where(inside, -1, c - n_k)
    owner = jnp.where(n_chunks_e > 0, experts, N_EXPERTS)
    later = jnp.flip(lax.cummin(jnp.flip(owner)))
    next_owner = jnp.concatenate([later[1:], jnp.full((1,), N_EXPERTS, jnp.int32)])
    meta = jnp.stack([chunk_end[-1], later[0]]).astype(jnp.int32)
    return start.astype(jnp.int32), size_class.astype(jnp.int32), e_j, next_owner.astype(jnp.int32), meta


def _expert_ffn(xb, first_block, n_expert_blocks, n_used, wg, wu, wd, layer):
    blk_rows = EXPERT_BLOCK * ROW_CHUNKS
    n_blocks = xb.shape[0] // blk_rows
    cstart, cbig, cexp, next_owner, meta = _chunk_plan(first_block, n_expert_blocks, n_blocks)
    meta = jnp.concatenate([meta, n_used])
    any_spec = pl.BlockSpec(memory_space=pl.ANY)
    grid_spec = pltpu.PrefetchScalarGridSpec(
        num_scalar_prefetch=5,
        grid=(1,),
        in_specs=[any_spec] * 4,
        out_specs=any_spec,
        scratch_shapes=[
            pltpu.VMEM((2, CHUNK_BLOCKS * blk_rows, LANES), F32),
            pltpu.VMEM((2, CHUNK_BLOCKS * blk_rows, LANES), F32),
            pltpu.VMEM((2, D_MODEL, EXPERT_FF), F32),
            pltpu.VMEM((2, D_MODEL, EXPERT_FF), F32),
            pltpu.VMEM((2, EXPERT_FF, D_MODEL), F32),
            pltpu.VMEM((D_MODEL, EXPERT_FF), BF16),
            pltpu.VMEM((D_MODEL, EXPERT_FF), BF16),
            pltpu.VMEM((EXPERT_FF, D_MODEL), BF16),
            pltpu.VMEM((CHUNK_BLOCKS * EXPERT_BLOCK, D_MODEL), BF16),
            pltpu.SemaphoreType.DMA((2,)),
            pltpu.SemaphoreType.DMA((2,)),
            pltpu.SemaphoreType.DMA((2, 3)),
        ],
    )
    return pl.pallas_call(
        functools.partial(_ffn_kernel, layer),
        grid_spec=grid_spec,
        out_shape=jax.ShapeDtypeStruct(xb.shape, F32),
        compiler_params=_params(("arbitrary",)),
    )(cstart, cbig, cexp, next_owner, meta, xb, wg, wu, wd)


def _combine_kernel(n_prompt_tiles, alpha, d1_ref, d2_ref,
                    x1_ref, route_ref, g2p_ref, g2s_ref, lng_ref, lnb_ref, yb_ref,
                    xp_ref, xs_ref, buf_a, buf_b, sem):
    i = pl.program_id(0)
    n_tiles = pl.num_programs(0)
    tm = x1_ref.shape[0]
    slot = i % 2

    def gather_tile(tile, s):
        def issue(c, carry):
            for u in range(ISSUE_UNROLL):
                t = c * ISSUE_UNROLL + u
                g = tile * tm + t
                _token_copy(yb_ref, d1_ref[g], buf_a.at[s], t, sem.at[s, 0]).start(priority=0)
                _token_copy(yb_ref, d2_ref[g], buf_b.at[s], t, sem.at[s, 1]).start(priority=1)
            return carry

        lax.fori_loop(0, tm // ISSUE_UNROLL, issue, 0)

    @pl.when(i == 0)
    def _():
        gather_tile(0, 0)

    @pl.when(i + 1 < n_tiles)
    def _():
        gather_tile(i + 1, 1 - slot)

    whole = yb_ref.at[pl.ds(0, tm * ROW_CHUNKS)]
    pltpu.make_async_copy(whole, buf_a.at[slot], sem.at[slot, 0]).wait()
    pltpu.make_async_copy(whole, buf_b.at[slot], sem.at[slot, 1]).wait()
    route_t = jnp.concatenate(
        [route_ref[...], jnp.zeros((LANES - SUBLANES, tm), F32)], axis=0).T
    ffn = (route_t[:, 2:3] * _load_token_major(buf_a.at[slot], tm)
           + route_t[:, 3:4] * _load_token_major(buf_b.at[slot], tm))
    g2 = _pick_mod(i >= n_prompt_tiles, g2p_ref, g2s_ref)
    out = _layer_norm(alpha * x1_ref[...] + g2 * ffn, lng_ref[...], lnb_ref[...])
    @pl.when(i < n_prompt_tiles)
    def _():
        xp_ref[...] = out

    @pl.when(i >= n_prompt_tiles)
    def _():
        xs_ref[...] = out


def _combine(x1, route, yb, dest1, dest2, mod_p, mod_s, lng, lnb, layer, n_prompt, seq_len, alpha):
    t_all = x1.shape[0]
    tm = ROW_TILE
    npt = n_prompt // tm
    out_specs = [pl.BlockSpec((tm, D_MODEL), lambda i, *_: (jnp.minimum(i, npt - 1), 0)),
                 pl.BlockSpec((tm, D_MODEL), lambda i, *_: (jnp.maximum(i - npt, 0), 0))]
    out_shape = [jax.ShapeDtypeStruct((n_prompt, D_MODEL), F32),
                 jax.ShapeDtypeStruct((t_all - n_prompt, D_MODEL), F32)]
    g2p, g2s = _mod_specs(layer, 5, tm, npt, seq_len // tm, mod_p.shape[1])
    grid_spec = pltpu.PrefetchScalarGridSpec(
        num_scalar_prefetch=2,
        grid=(t_all // tm,),
        in_specs=[
            pl.BlockSpec((tm, D_MODEL), lambda i, *_: (i, 0)),
            pl.BlockSpec((SUBLANES, tm), lambda i, *_: (0, i)),
            g2p, g2s,
            pl.BlockSpec((1, D_MODEL), lambda i, *_: (0, 0)),
            pl.BlockSpec((1, D_MODEL), lambda i, *_: (0, 0)),
            pl.BlockSpec(memory_space=pl.ANY),
        ],
        out_specs=out_specs,
        scratch_shapes=[
            pltpu.VMEM((2, tm * ROW_CHUNKS, LANES), F32),
            pltpu.VMEM((2, tm * ROW_CHUNKS, LANES), F32),
            pltpu.SemaphoreType.DMA((2, 2)),
        ],
    )
    return pl.pallas_call(
        functools.partial(_combine_kernel, npt, alpha),
        grid_spec=grid_spec,
        out_shape=out_shape,
        compiler_params=_params(("arbitrary",)),
    )(dest1, dest2, x1, route, mod_p, mod_s, lng, lnb, yb)


def _rope_tables(pos):
    inv = jnp.power(ROPE_THETA, -jnp.arange(ROT_HALF, dtype=F32) * (2.0 / ROT_DIM))
    ang = pos.astype(F32)[:, None] * inv[None, :]
    cos, sin = jnp.cos(ang), jnp.sin(ang)
    rest = HEAD_DIM - ROT_DIM
    n = pos.shape[0]
    cos_h = jnp.concatenate([cos, cos, jnp.ones((n, rest), F32)], axis=-1)
    sin_h = jnp.concatenate([-sin, sin, jnp.zeros((n, rest), F32)], axis=-1)
    reps = LANES // HEAD_DIM
    return jnp.tile(cos_h, (1, reps)), jnp.tile(sin_h, (1, reps))


def _permute_w_in(w_in):
    sizes = (ATTN_WIDTH, KV_WIDTH, KV_WIDTH, SSD_D_INNER, SSD_CONV_DIM, SSD_HEADS,
             SC_DIM, SC_DIM, SC_DIM, 3 * D_MODEL)
    offs = np.concatenate([[0], np.cumsum(sizes)])
    q, k, v, z, xbc, dt, scb, scc, scv, gates = (w_in[..., offs[n]:offs[n + 1]] for n in range(len(sizes)))
    main = jnp.concatenate([q, z, xbc, scb, scc, gates, scv, k, v], axis=-1).astype(BF16)
    dt = jnp.pad(dt, ((0, 0), (0, 0), (0, DT_PAD - SSD_HEADS))).astype(BF16)
    return main, dt


def _pad_lanes(v, width):
    return jnp.pad(v, ((0, 0), (0, width - v.shape[-1])))


def kernel(x_prompt, x_sample, c_prompt, c_sample, cache_attn_k, cache_attn_v, state_ssm, state_ssd_conv, state_short_conv, w_ada, b_ada, w_in, attn_sink, ssd_conv_w, ssd_conv_b, ssd_dt_bias, ssd_a_log, ssd_d, ssd_norm_w, sc_conv_w, w_pa, w_pb, w_pc, w_out, ln1_g, ln1_b, ln2_g, ln2_b, router_g_w, router_g_b, router_e_w, router_e_b, moe_w_gate, moe_w_up, moe_w_down):
    depth = w_in.shape[0]
    n_seq, seq_len, _ = x_prompt.shape
    n_dec, dec_len, _ = x_sample.shape
    wb = cache_attn_k.shape[2]
    past_len = 8192
    assert dec_len == DEC_SEQ and wb == WINDOW
    assert seq_len % ROW_TILE_IN == 0 and (n_dec * dec_len) % ROW_TILE_IN == 0
    n_prompt = n_seq * seq_len
    n_sample = n_dec * dec_len
    t_all = n_prompt + n_sample
    alpha = (2 * depth) ** 0.25

    mod = _modulation(jnp.concatenate([c_prompt, c_sample], axis=0), w_ada, b_ada)
    mod_p_all = mod[:, :n_seq].reshape(depth, n_seq, 1, 6 * D_MODEL)
    mod_s_all = jnp.repeat(mod[:, n_seq:], dec_len, axis=1).reshape(depth, 1, n_sample, 6 * D_MODEL)

    w_main_all, w_dt_all = _permute_w_in(w_in)
    cos_p, sin_p = _rope_tables(jnp.arange(seq_len, dtype=jnp.int32))
    cos_s, sin_s = _rope_tables(past_len + (jnp.arange(MIX_ROWS, dtype=jnp.int32) % dec_len))

    a_total = 2 * t_all
    n_blocks = (a_total + N_EXPERTS * (EXPERT_BLOCK - 1)) // EXPERT_BLOCK
    n_slots = n_blocks * EXPERT_BLOCK

    mod_p, mod_s = mod_p_all, mod_s_all
    cache_k = cache_attn_k.reshape(depth, n_dec, wb, KV_WIDTH)
    cache_v = cache_attn_v.reshape(depth, n_dec, wb, KV_WIDTH)
    h0 = state_ssm.reshape(depth, n_dec, SSD_D_INNER, SSD_STATE)
    xbuf = jnp.pad(state_ssd_conv, ((0, 0), (0, 0), (dec_len - (SSD_CONV - 1), 0), (0, 0))
                   ).reshape(depth, n_sample, SSD_CONV_DIM)
    cbuf = jnp.pad(state_short_conv, ((0, 0), (0, 0), (dec_len - (SC_WIDTH - 1), 0), (0, 0))
                   ).reshape(depth, n_sample, SC_DIM)
    wpa, wpb, wpc, wout = (w.astype(BF16) for w in (w_pa, w_pb, w_pc, w_out))
    gap = ROUTE_EXPERT_ROW - MOE_GROUPS
    tail = LANES - ROUTE_EXPERT_ROW - N_EXPERTS
    wr = jnp.concatenate([router_g_w, jnp.zeros((depth, D_MODEL, gap), F32), router_e_w,
                          jnp.zeros((depth, D_MODEL, tail), F32)], axis=-1).astype(BF16)
    br = jnp.concatenate([router_g_b, jnp.zeros((depth, gap), F32), router_e_b,
                          jnp.zeros((depth, tail), F32)], axis=-1)[:, None, :]

    x = (x_prompt.reshape(n_prompt, D_MODEL), x_sample.reshape(n_sample, D_MODEL))
    outs_p = [[] for _ in range(5)]
    outs_s = [[] for _ in range(5)]
    for l in range(depth):
        p, dt = _input_projection(x, mod_p, mod_s, w_main_all, w_dt_all, l, n_prompt, seq_len)

        cw, cb = ssd_conv_w[l], ssd_conv_b[l][None]
        dtb = _pad_lanes(ssd_dt_bias[l][None], DT_PAD)
        alog = _pad_lanes(ssd_a_log[l][None], DT_PAD)
        de = jnp.repeat(ssd_d[l], SSD_HEAD_DIM)[None]
        nw = ssd_norm_w[l][None]
        scw = sc_conv_w[l]

        ya_p, krot_p = _attention_prompt(p, attn_sink[l], cos_p, sin_p, n_seq, seq_len)
        yb_p, yc_p, h_p, cv_p = _ssd_prompt(p, dt, cw, cb, dtb, alog, de, nw, scw, n_seq, seq_len)

        ya_s, yb_s, yc_s, k_s, v_s, h_s, cv_s = _mix_sample(
            p, dt, attn_sink[l], cos_s, sin_s, cache_k, cache_v, h0, xbuf, cbuf,
            cw, cb, dtb, alog, de, nw, scw, l, n_prompt, n_dec)

        x1, h, route, counts = _output_projection(
            x, (ya_p, yb_p, yc_p), (ya_s, yb_s, yc_s), p, mod_p, mod_s,
            wpa, wpb, wpc, wout, wr, br, ln1_g[l][None], ln1_b[l][None], l, n_prompt, seq_len, alpha)

        cnt = counts[ROUTE_EXPERT_ROW:ROUTE_EXPERT_ROW + N_EXPERTS, 0].astype(jnp.int32)
        pad_cnt = (cnt + EXPERT_BLOCK - 1) // EXPERT_BLOCK * EXPERT_BLOCK
        pad_end = jnp.cumsum(pad_cnt)
        pad_start = pad_end - pad_cnt
        n_used = (pad_end[-1:] // EXPERT_BLOCK).astype(jnp.int32)
        dest1, dest2 = _slots(route, pad_start)

        xb = _dispatch(h, dest1, dest2, pad_end.astype(jnp.int32), cnt, n_used, n_slots)
        yb_slots = _expert_ffn(xb, (pad_start // EXPERT_BLOCK).astype(jnp.int32),
                               (pad_cnt // EXPERT_BLOCK).astype(jnp.int32), n_used,
                               moe_w_gate, moe_w_up, moe_w_down, l)
        x = _combine(x1, route, yb_slots, dest1, dest2, mod_p, mod_s, ln2_g[l][None], ln2_b[l][None],
                     l, n_prompt, seq_len, alpha)

        def prompt_tail(rows, c0, c1):
            return jnp.stack([p[(b + 1) * seq_len - rows:(b + 1) * seq_len, c0:c1] for b in range(n_seq)]
                             ).astype(F32)

        outs_p[0].append(krot_p.reshape(n_seq, wb, N_KV_HEADS, HEAD_DIM))
        outs_p[1].append(prompt_tail(wb, COL_KV + KV_WIDTH, COL_KV + 2 * KV_WIDTH)
                         .reshape(n_seq, wb, N_KV_HEADS, HEAD_DIM))
        outs_p[2].append(h_p.reshape(n_seq, SSD_HEADS, SSD_HEAD_DIM, SSD_STATE))
        outs_p[3].append(prompt_tail(SSD_CONV - 1, COL_XBC, COL_XBC + SSD_CONV_DIM))
        outs_p[4].append(cv_p[:, SUBLANES - (SC_WIDTH - 1):, :])
        outs_s[0].append(k_s.reshape(n_dec, wb, N_KV_HEADS, HEAD_DIM))
        outs_s[1].append(v_s.reshape(n_dec, wb, N_KV_HEADS, HEAD_DIM))
        outs_s[2].append(h_s.reshape(n_dec, SSD_HEADS, SSD_HEAD_DIM, SSD_STATE))
        outs_s[3].append(p[n_prompt:, COL_XBC:COL_XBC + SSD_CONV_DIM].astype(F32)
                         .reshape(n_dec, dec_len, SSD_CONV_DIM)[:, dec_len - (SSD_CONV - 1):, :])
        outs_s[4].append(cv_s.reshape(n_dec, dec_len, SC_DIM)[:, dec_len - (SC_WIDTH - 1):, :])

    y_prompt = x[0].reshape(n_seq, seq_len, D_MODEL)
    y_sample = x[1].reshape(n_dec, dec_len, D_MODEL)
    return (y_prompt, y_sample, *[jnp.stack(o) for o in outs_p], *[jnp.stack(o) for o in outs_s])
```

```python
import functools

import jax
import jax.numpy as jnp
import numpy as np
from jax import lax
from jax.experimental import pallas as pl
from jax.experimental.pallas import tpu as pltpu

F32 = jnp.float32
BF16 = jnp.bfloat16

D_MODEL = 1024
HEAD_DIM = 64
N_HEADS = 8
N_KV_HEADS = 2
Q_PER_KV = N_HEADS // N_KV_HEADS
ATTN_WIDTH = N_HEADS * HEAD_DIM
KV_WIDTH = N_KV_HEADS * HEAD_DIM
WINDOW = 128
ROT_DIM = HEAD_DIM // 4
ROT_HALF = ROT_DIM // 2
ROPE_THETA = 500000.0
ATTN_SCALE = HEAD_DIM ** -0.5
SSD_D_INNER = 512
SSD_HEAD_DIM = 64
SSD_HEADS = 8
SSD_GROUPS = 2
SSD_HEADS_PER_GROUP = SSD_HEADS // SSD_GROUPS
SSD_STATE = 128
SSD_CONV = 4
SSD_CONV_DIM = SSD_D_INNER + 2 * SSD_GROUPS * SSD_STATE
SSD_CHUNK = 128
SC_DIM = 512
SC_WIDTH = 3
MOE_GROUPS = 4
EXPERTS_PER_GROUP = 8
N_EXPERTS = MOE_GROUPS * EXPERTS_PER_GROUP
EXPERT_FF = 512
LN_EPS = 1e-5
RMS_EPS = 1e-5

SUBLANES = 8
LANES = 128
VMEM_LIMIT = 56 * 1024 * 1024

COL_Q = 0
COL_Z = 512
COL_XBC = 1024
COL_SCB = 2048
COL_SCC = 2560
COL_GATES = 3072
COL_SCV = 6144
COL_KV = 6656
P_MAIN = 6912
DT_PAD = 128

ROW_TILE_IN = 1024
COL_TILE_IN = 2304
ROW_TILE = 512
ROW_TILE_MOVE = 1024
MIX_ROWS = 128
DEC_SEQ = 8
SEQS_PER_STEP = MIX_ROWS // DEC_SEQ
EXPERT_BLOCK = 256
CHUNK_SIZES = (4, 2, 1)
CHUNK_BLOCKS = CHUNK_SIZES[0]
ROW_CHUNKS = D_MODEL // LANES
ISSUE_UNROLL = 8
ROUTE_EXPERT_ROW = 8
ROUTE_ROWS = 48


def _silu(v):
    return v * jax.nn.sigmoid(v)


def _dot(a, b):
    return jnp.dot(a, b, preferred_element_type=F32)


def _dot_nt(a, b):
    return lax.dot_general(a, b, (((1,), (1,)), ((), ())), preferred_element_type=F32)


def _dot_exact(a, b):
    return jnp.dot(a, b, preferred_element_type=F32, precision=lax.Precision.HIGHEST)


def _params(sem):
    return pltpu.CompilerParams(dimension_semantics=sem, vmem_limit_bytes=VMEM_LIMIT)


def _pick_mod(is_sample, prompt_ref, sample_ref):
    return jnp.where(is_sample, sample_ref[0], prompt_ref[0])


def _layer_norm(v, g, b):
    mu = jnp.mean(v, axis=-1, keepdims=True)
    c = v - mu
    var = jnp.mean(c * c, axis=-1, keepdims=True)
    return c * lax.rsqrt(var + LN_EPS) * g + b


def _mod_kernel(c_ref, w_ref, b_ref, o_ref):
    s = _silu(c_ref[...]).astype(BF16)
    o_ref[0] = _dot(s, w_ref[0].astype(BF16)) + b_ref[0]


def _modulation(c_all, w_ada, b_ada):
    depth, _, width = w_ada.shape
    n = c_all.shape[0]
    tn = 1536
    return pl.pallas_call(
        _mod_kernel,
        grid=(depth, width // tn),
        in_specs=[
            pl.BlockSpec((n, D_MODEL), lambda l, j: (0, 0)),
            pl.BlockSpec((1, D_MODEL, tn), lambda l, j: (l, 0, j)),
            pl.BlockSpec((1, 1, tn), lambda l, j: (l, 0, j)),
        ],
        out_specs=pl.BlockSpec((1, n, tn), lambda l, j: (l, 0, j)),
        out_shape=jax.ShapeDtypeStruct((depth, n, width), F32),
        compiler_params=_params(("parallel", "parallel")),
    )(c_all, w_ada, b_ada.reshape(depth, 1, width))


def _inproj_kernel(n_prompt_tiles, xp_ref, xs_ref, shp_ref, shs_ref, scp_ref, scs_ref, w_ref, wdt_ref,
                   p_ref, dt_ref, u_scr):
    i = pl.program_id(0)
    j = pl.program_id(1)

    @pl.when(j == 0)
    def _():
        is_sample = i >= n_prompt_tiles
        sh = _pick_mod(is_sample, shp_ref, shs_ref)
        sc = _pick_mod(is_sample, scp_ref, scs_ref)
        x = jnp.where(is_sample, xs_ref[...], xp_ref[...])
        u = (x * (1.0 + sc) + sh).astype(BF16)
        u_scr[...] = u
        dt_ref[...] = _dot(u, wdt_ref[...])

    p_ref[...] = _dot(u_scr[...], w_ref[...]).astype(BF16)


def _mod_specs(layer, col, row_tile, n_prompt_tiles, tiles_per_seq, n_seq):
    def prompt_map(i, *_):
        return (layer, jnp.minimum(i // tiles_per_seq, n_seq - 1), 0, col)

    def sample_map(i, *_):
        return (layer, 0, jnp.maximum(i - n_prompt_tiles, 0), col)

    return (pl.BlockSpec((None, 1, 1, D_MODEL), prompt_map),
            pl.BlockSpec((None, 1, row_tile, D_MODEL), sample_map))


def _layer_spec(layer, shape):
    return pl.BlockSpec((None, *shape), lambda *_: (layer,) + (0,) * len(shape))


def _input_projection(x, mod_p, mod_s, w_main, w_dt, layer, n_prompt, seq_len):
    x_prompt, x_sample = x
    t_all = n_prompt + x_sample.shape[0]
    tm, tn = ROW_TILE_IN, COL_TILE_IN
    npt = n_prompt // tm
    n_seq = mod_p.shape[1]
    shp, shs = _mod_specs(layer, 0, tm, npt, seq_len // tm, n_seq)
    scp, scs = _mod_specs(layer, 1, tm, npt, seq_len // tm, n_seq)
    return pl.pallas_call(
        functools.partial(_inproj_kernel, npt),
        grid=(t_all // tm, P_MAIN // tn),
        in_specs=[
            pl.BlockSpec((tm, D_MODEL), lambda i, j: (jnp.minimum(i, npt - 1), 0)),
            pl.BlockSpec((tm, D_MODEL), lambda i, j: (jnp.maximum(i - npt, 0), 0)),
            shp, shs, scp, scs,
            pl.BlockSpec((None, D_MODEL, tn), lambda i, j: (layer, 0, j)),
            _layer_spec(layer, (D_MODEL, DT_PAD)),
        ],
        out_specs=[
            pl.BlockSpec((tm, tn), lambda i, j: (i, j)),
            pl.BlockSpec((tm, DT_PAD), lambda i, j: (i, 0)),
        ],
        out_shape=[
            jax.ShapeDtypeStruct((t_all, P_MAIN), BF16),
            jax.ShapeDtypeStruct((t_all, DT_PAD), F32),
        ],
        scratch_shapes=[pltpu.VMEM((tm, D_MODEL), BF16)],
        compiler_params=_params(("parallel", "arbitrary")),
    )(x_prompt, x_sample, mod_p, mod_s, mod_p, mod_s, w_main, w_dt)


def _rope(v, cos, sin):
    width = v.shape[-1]
    reps = width // LANES
    if reps > 1:
        cos = jnp.concatenate([cos] * reps, axis=-1)
        sin = jnp.concatenate([sin] * reps, axis=-1)
    lane = lax.broadcasted_iota(jnp.int32, v.shape, 1) % HEAD_DIM
    partner = jnp.where(lane < ROT_HALF,
                        pltpu.roll(v, width - ROT_HALF, 1),
                        pltpu.roll(v, ROT_HALF, 1))
    return v * cos + partner * sin


def _shift_rows(cur, prev, k):
    axis = cur.ndim - 2
    idx = lax.broadcasted_iota(jnp.int32, cur.shape, axis)
    return jnp.where(idx < k, pltpu.roll(prev, k, axis), pltpu.roll(cur, k, axis))


def _causal_conv(cur, prev, w_ref, width):
    out = cur * w_ref[width - 1:width, :]
    for k in range(1, width):
        out = out + _shift_rows(cur, prev, k) * w_ref[width - 1 - k:width - k, :]
    return out


def _head_expand():
    r = lax.broadcasted_iota(jnp.int32, (LANES, SSD_D_INNER), 0)
    c = lax.broadcasted_iota(jnp.int32, (LANES, SSD_D_INNER), 1)
    return (c // SSD_HEAD_DIM == r).astype(F32)


def _head_expand_t():
    r = lax.broadcasted_iota(jnp.int32, (SSD_D_INNER, LANES), 0)
    c = lax.broadcasted_iota(jnp.int32, (SSD_D_INNER, LANES), 1)
    return (r // SSD_HEAD_DIM == c).astype(F32)


def _ssd_tile(act, dt_raw, dtb, alog, seq_rows):
    rows = MIX_ROWS
    xs = act[:, :SSD_D_INNER]
    bm = act[:, SSD_D_INNER:SSD_D_INNER + SSD_GROUPS * SSD_STATE]
    cm = act[:, SSD_D_INNER + SSD_GROUPS * SSD_STATE:]
    v = dt_raw + dtb
    dt = jnp.maximum(v, 0.0) + jnp.log1p(jnp.exp(-jnp.abs(v)))
    a = -jnp.exp(alog)
    dta = dt * a
    ri = lax.broadcasted_iota(jnp.int32, (rows, rows), 0)
    ci = lax.broadcasted_iota(jnp.int32, (rows, rows), 1)
    same = (ri // seq_rows) == (ci // seq_rows)
    causal = same & (ci <= ri)
    cs = _dot_exact(causal.astype(F32), dta)
    expand = _head_expand().astype(BF16)

    def per_head_lanes(v):
        hi = v.astype(BF16)
        lo = (v - hi.astype(F32)).astype(BF16)
        return _dot(hi, expand) + _dot(lo, expand)

    dt_e = per_head_lanes(dt)
    cs_e = per_head_lanes(cs)
    if seq_rows == rows:
        tot = None
        tot_e = cs_e[rows - 1:rows, :]
    else:
        tot = _dot_exact(same.astype(F32), dta)
        tot_e = per_head_lanes(tot)
    cs_t = cs.T
    dtx = xs * dt_e
    xw = dtx * jnp.exp(tot_e - cs_e)
    b_groups, c_groups, y_parts = [], [], []
    for g in range(SSD_GROUPS):
        bg = bm[:, g * SSD_STATE:(g + 1) * SSD_STATE].astype(BF16)
        cg = cm[:, g * SSD_STATE:(g + 1) * SSD_STATE].astype(BF16)
        b_groups.append(bg)
        c_groups.append(cg)
        cb = _dot_nt(cg, bg)
        for hh in range(SSD_HEADS_PER_GROUP):
            h = g * SSD_HEADS_PER_GROUP + hh
            seg = cs[:, h:h + 1] - cs_t[h:h + 1, :]
            decay = jnp.where(causal, jnp.exp(jnp.where(causal, seg, 0.0)), 0.0)
            y_parts.append(_dot((cb * decay).astype(BF16),
                                dtx[:, h * SSD_HEAD_DIM:(h + 1) * SSD_HEAD_DIM].astype(BF16)))
    y_diag = jnp.concatenate(y_parts, axis=-1)
    return xs, y_diag, jnp.exp(cs_e), xw, tot, tot_e, bm, b_groups, c_groups


def _gated_group_norm(y, z, nw):
    y = y * _silu(z)
    half = SSD_D_INNER // SSD_GROUPS
    parts = []
    for g in range(SSD_GROUPS):
        yg = y[:, g * half:(g + 1) * half]
        parts.append(yg * lax.rsqrt(jnp.mean(yg * yg, axis=-1, keepdims=True) + RMS_EPS))
    return jnp.concatenate(parts, axis=-1) * nw


def _attn_prompt_kernel(sink_ref, q_ref, kvc_ref, kvp_ref, cosc_ref, sinc_ref, cosp_ref, sinp_ref, bias_ref,
                        ya_ref, krot_ref):
    i = pl.program_id(1)
    nb = pl.num_programs(1)
    w = WINDOW
    q = _rope(q_ref[...].astype(F32), cosc_ref[...], sinc_ref[...])
    kvc = kvc_ref[...].astype(F32)
    kvp = kvp_ref[...].astype(F32)
    kc = _rope(kvc[:, :KV_WIDTH], cosc_ref[...], sinc_ref[...])
    kp = _rope(kvp[:, :KV_WIDTH], cosp_ref[...], sinp_ref[...])
    vc = kvc[:, KV_WIDTH:]
    vp = kvp[:, KV_WIDTH:]

    @pl.when(i == nb - 1)
    def _():
        krot_ref[0] = kc

    rows = Q_PER_KV * w
    bias = bias_ref[...]
    rcol = lax.broadcasted_iota(jnp.int32, (rows, 1), 0)
    outs = []
    for kh in range(N_KV_HEADS):
        hs = slice(kh * HEAD_DIM, (kh + 1) * HEAD_DIM)
        k2 = jnp.concatenate([kp[:, hs], kc[:, hs]], axis=0).astype(BF16)
        v2 = jnp.concatenate([vp[:, hs], vc[:, hs]], axis=0).astype(BF16)
        qg = jnp.concatenate(
            [q[:, (kh * Q_PER_KV + g) * HEAD_DIM:(kh * Q_PER_KV + g + 1) * HEAD_DIM] for g in range(Q_PER_KV)],
            axis=0).astype(BF16)
        sink = jnp.zeros((rows, 1), F32)
        for g in range(Q_PER_KV):
            sink = jnp.where(rcol // w == g, sink_ref[kh * Q_PER_KV + g], sink)
        logits = _dot_nt(qg, k2) * ATTN_SCALE + bias
        m = jnp.maximum(jnp.max(logits, axis=-1, keepdims=True), sink)
        e = jnp.exp(logits - m)
        den = jnp.sum(e, axis=-1, keepdims=True) + jnp.exp(sink - m)
        o = _dot(e.astype(BF16), v2) / den
        for g in range(Q_PER_KV):
            outs.append(o[g * w:(g + 1) * w, :])
    ya_ref[...] = jnp.concatenate(outs, axis=-1).astype(BF16)


def _band_bias():
    w = WINDOW
    r = np.arange(Q_PER_KV * w)[:, None] % w
    s = np.arange(2 * w)[None, :]
    diff = w + r - s
    band = (diff >= 0) & (diff < w)
    return jnp.asarray(np.where(np.stack([band & (s >= w), band]), 0.0, -np.inf), F32)


def _attention_prompt(p, sink, cos_p, sin_p, n_seq, seq_len):
    nb = seq_len // WINDOW
    n_prompt = n_seq * seq_len
    w = WINDOW

    def cur(b, i):
        return b * nb + i

    def prev(b, i):
        return jnp.maximum(b * nb + i - 1, 0)

    return pl.pallas_call(
        _attn_prompt_kernel,
        grid=(n_seq, nb),
        in_specs=[
            pl.BlockSpec(memory_space=pltpu.SMEM),
            pl.BlockSpec((w, ATTN_WIDTH), lambda b, i: (cur(b, i), COL_Q // ATTN_WIDTH)),
            pl.BlockSpec((w, 2 * KV_WIDTH), lambda b, i: (cur(b, i), COL_KV // (2 * KV_WIDTH))),
            pl.BlockSpec((w, 2 * KV_WIDTH), lambda b, i: (prev(b, i), COL_KV // (2 * KV_WIDTH))),
            pl.BlockSpec((w, LANES), lambda b, i: (i, 0)),
            pl.BlockSpec((w, LANES), lambda b, i: (i, 0)),
            pl.BlockSpec((w, LANES), lambda b, i: (jnp.maximum(i - 1, 0), 0)),
            pl.BlockSpec((w, LANES), lambda b, i: (jnp.maximum(i - 1, 0), 0)),
            pl.BlockSpec((None, Q_PER_KV * w, 2 * w), lambda b, i: (jnp.minimum(i, 1), 0, 0)),
        ],
        out_specs=[
            pl.BlockSpec((w, ATTN_WIDTH), lambda b, i: (cur(b, i), 0)),
            pl.BlockSpec((1, w, KV_WIDTH), lambda b, i: (b, 0, 0)),
        ],
        out_shape=[
            jax.ShapeDtypeStruct((n_prompt, ATTN_WIDTH), BF16),
            jax.ShapeDtypeStruct((n_seq, w, KV_WIDTH), F32),
        ],
        compiler_params=_params(("parallel", "arbitrary")),
    )(sink, p, p, p, cos_p, sin_p, cos_p, sin_p, _band_bias())


def _ssd_prompt_kernel(z_ref, xc_ref, xp_ref, dt_ref, scb_ref, sccc_ref, sccp_ref, scvc_ref, scvp_ref,
                       cw_ref, cb_ref, dtb_ref, alog_ref, de_ref, nw_ref, scw_ref,
                       yb_ref, yc_ref, hout_ref, cvlast_ref, h_scr):
    i = pl.program_id(1)
    nc = pl.num_programs(1)
    first = i == 0

    @pl.when(first)
    def _():
        h_scr[...] = jnp.zeros_like(h_scr)

    xc = xc_ref[...].astype(F32)
    xp = jnp.where(first, 0.0, xp_ref[...].astype(F32))
    act = _silu(_causal_conv(xc, xp, cw_ref, SSD_CONV) + cb_ref[...])
    xs, y_diag, ecs_e, xw, _, tot_e, bm, _, c_groups = _ssd_tile(
        act, dt_ref[...], dtb_ref[...], alog_ref[...], MIX_ROWS)
    gw = SSD_HEADS_PER_GROUP * SSD_HEAD_DIM
    y_off = []
    for g in range(SSD_GROUPS):
        cols = slice(g * gw, (g + 1) * gw)
        hg = h_scr[:, cols]
        y_off.append(_dot(c_groups[g], hg.astype(BF16)))
        b_t = bm[:, g * SSD_STATE:(g + 1) * SSD_STATE].T.astype(BF16)
        h_scr[:, cols] = jnp.exp(tot_e[:, cols]) * hg + _dot(b_t, xw[:, cols].astype(BF16))
    y = y_diag + jnp.concatenate(y_off, axis=-1) * ecs_e + de_ref[...] * xs
    yb_ref[...] = _gated_group_norm(y, z_ref[...].astype(F32), nw_ref[...]).astype(BF16)

    cvc = sccc_ref[...].astype(F32) * scvc_ref[...].astype(F32)
    cvp = jnp.where(first, 0.0, sccp_ref[...].astype(F32) * scvp_ref[...].astype(F32))
    conv_c = _causal_conv(cvc, cvp, scw_ref, SC_WIDTH)
    yc_ref[...] = (scb_ref[...].astype(F32) * conv_c).astype(BF16)

    @pl.when(i == nc - 1)
    def _():
        hout_ref[0] = h_scr[...].T
        cvlast_ref[0] = cvc[MIX_ROWS - SUBLANES:, :]


def _ssd_prompt(p, dt, cw, cb, dtb, alog, de, nw, scw, n_seq, seq_len):
    nc = seq_len // MIX_ROWS
    r = MIX_ROWS
    n_prompt = n_seq * seq_len

    def cur(b, i):
        return b * nc + i

    def prev(b, i):
        return jnp.maximum(b * nc + i - 1, 0)

    def col(width, offset, which):
        return pl.BlockSpec((r, width), lambda b, i: (which(b, i), offset // width))

    def const(shape):
        return pl.BlockSpec(shape, lambda b, i: (0,) * len(shape))

    return pl.pallas_call(
        _ssd_prompt_kernel,
        grid=(n_seq, nc),
        in_specs=[
            col(SSD_D_INNER, COL_Z, cur),
            col(SSD_CONV_DIM, COL_XBC, cur), col(SSD_CONV_DIM, COL_XBC, prev),
            pl.BlockSpec((r, DT_PAD), lambda b, i: (cur(b, i), 0)),
            col(SC_DIM, COL_SCB, cur),
            col(SC_DIM, COL_SCC, cur), col(SC_DIM, COL_SCC, prev),
            col(SC_DIM, COL_SCV, cur), col(SC_DIM, COL_SCV, prev),
            const((SSD_CONV, SSD_CONV_DIM)), const((1, SSD_CONV_DIM)),
            const((1, DT_PAD)), const((1, DT_PAD)), const((1, SSD_D_INNER)), const((1, SSD_D_INNER)),
            const((SC_WIDTH, SC_DIM)),
        ],
        out_specs=[
            pl.BlockSpec((r, SSD_D_INNER), lambda b, i: (cur(b, i), 0)),
            pl.BlockSpec((r, SC_DIM), lambda b, i: (cur(b, i), 0)),
            pl.BlockSpec((1, SSD_D_INNER, SSD_STATE), lambda b, i: (b, 0, 0)),
            pl.BlockSpec((1, SUBLANES, SC_DIM), lambda b, i: (b, 0, 0)),
        ],
        out_shape=[
            jax.ShapeDtypeStruct((n_prompt, SSD_D_INNER), BF16),
            jax.ShapeDtypeStruct((n_prompt, SC_DIM), BF16),
            jax.ShapeDtypeStruct((n_seq, SSD_D_INNER, SSD_STATE), F32),
            jax.ShapeDtypeStruct((n_seq, SUBLANES, SC_DIM), F32),
        ],
        scratch_shapes=[pltpu.VMEM((SSD_STATE, SSD_D_INNER), F32)],
        compiler_params=_params(("parallel", "arbitrary")),
    )(p, p, p, dt, p, p, p, p, p, cw, cb, dtb, alog, de, nw, scw)


def _mix_sample_kernel(sink_ref, q_ref, kv_ref, z_ref, x_ref, dt_ref, scb_ref, scc_ref, scv_ref,
                       cos_ref, sin_ref, ck_ref, cv_ref, h0_ref, xbuf_ref, cbuf_ref,
                       cw_ref, cb_ref, dtb_ref, alog_ref, de_ref, nw_ref, scw_ref,
                       ya_ref, yb_ref, yc_ref, knew_ref, vnew_ref, hnew_ref, cvout_ref):
    ns, t = SEQS_PER_STEP, DEC_SEQ
    wb = ck_ref.shape[1]

    q = _rope(q_ref[...].astype(F32), cos_ref[...], sin_ref[...])
    kv = kv_ref[...].astype(F32)
    kn = _rope(kv[:, :KV_WIDTH], cos_ref[...], sin_ref[...])
    vn = kv[:, KV_WIDTH:]
    q3 = q.reshape(ns, t, ATTN_WIDTH)
    kn3 = kn.reshape(ns, t, KV_WIDTH)
    vn3 = vn.reshape(ns, t, KV_WIDTH)
    knew_ref[:, :wb - t, :] = ck_ref[:, t:, :]
    knew_ref[:, wb - t:, :] = kn3
    vnew_ref[:, :wb - t, :] = cv_ref[:, t:, :]
    vnew_ref[:, wb - t:, :] = vn3
    nq = Q_PER_KV * t
    qi = lax.broadcasted_iota(jnp.int32, (ns, nq, wb + t), 1) % t
    si = lax.broadcasted_iota(jnp.int32, (ns, nq, wb + t), 2)
    valid = ((si < wb) & (si > qi + (wb - WINDOW))) | ((si >= wb) & (si - wb <= qi))
    hrow = lax.broadcasted_iota(jnp.int32, (ns, nq, 1), 1) // t
    heads = [None] * N_HEADS
    for kh in range(N_KV_HEADS):
        hs = slice(kh * HEAD_DIM, (kh + 1) * HEAD_DIM)
        k_all = jnp.concatenate([ck_ref[:, :, hs], kn3[:, :, hs]], axis=1).astype(BF16)
        v_all = jnp.concatenate([cv_ref[:, :, hs], vn3[:, :, hs]], axis=1).astype(BF16)
        qg = jnp.concatenate(
            [q3[:, :, (kh * Q_PER_KV + g) * HEAD_DIM:(kh * Q_PER_KV + g + 1) * HEAD_DIM] for g in range(Q_PER_KV)],
            axis=1).astype(BF16)
        sink = jnp.zeros((ns, nq, 1), F32)
        for g in range(Q_PER_KV):
            sink = jnp.where(hrow == g, sink_ref[kh * Q_PER_KV + g], sink)
        logits = jnp.einsum('bqd,bsd->bqs', qg, k_all, preferred_element_type=F32) * ATTN_SCALE
        logits = jnp.where(valid, logits, -jnp.inf)
        m = jnp.maximum(jnp.max(logits, axis=-1, keepdims=True), sink)
        e = jnp.exp(logits - m)
        den = jnp.sum(e, axis=-1, keepdims=True) + jnp.exp(sink - m)
        o = jnp.einsum('bqs,bsd->bqd', e.astype(BF16), v_all, preferred_element_type=F32) / den
        for g in range(Q_PER_KV):
            heads[kh * Q_PER_KV + g] = o[:, g * t:(g + 1) * t, :]
    ya_ref[...] = jnp.concatenate(heads, axis=-1).reshape(ns * t, ATTN_WIDTH).astype(BF16)

    xc3 = x_ref[...].astype(F32).reshape(ns, t, SSD_CONV_DIM)
    xp3 = xbuf_ref[...].reshape(ns, t, SSD_CONV_DIM)
    conv = _causal_conv(xc3, xp3, cw_ref, SSD_CONV).reshape(ns * t, SSD_CONV_DIM)
    act = _silu(conv + cb_ref[...])
    xs, y_diag, ecs_e, xw, tot, _, _, b_groups, c_groups = _ssd_tile(
        act, dt_ref[...], dtb_ref[...], alog_ref[...], t)
    xw_t = xw.T
    dec_t = jnp.exp(_dot_exact(_head_expand_t(), tot.T))
    gw = SSD_HEADS_PER_GROUP * SSD_HEAD_DIM
    y_off = []
    for g in range(SSD_GROUPS):
        c3 = c_groups[g].reshape(ns, t, SSD_STATE)
        hg = h0_ref[:, g * gw:(g + 1) * gw, :].astype(BF16)
        y_off.append(jnp.einsum('btn,bqn->btq', c3, hg, preferred_element_type=F32).reshape(ns * t, gw))
    y = y_diag + jnp.concatenate(y_off, axis=-1) * ecs_e + de_ref[...] * xs
    yb_ref[...] = _gated_group_norm(y, z_ref[...].astype(F32), nw_ref[...]).astype(BF16)
    col = lax.broadcasted_iota(jnp.int32, xw_t.shape, 1) // t
    for b in range(ns):
        xw_b = jnp.where(col == b, xw_t, jnp.zeros_like(xw_t))
        upd = jnp.concatenate(
            [_dot(xw_b[g * gw:(g + 1) * gw, :].astype(BF16), b_groups[g]) for g in range(SSD_GROUPS)], axis=0)
        hnew_ref[b] = dec_t[:, b * t:b * t + 1] * h0_ref[b] + upd

    cvc = scc_ref[...].astype(F32) * scv_ref[...].astype(F32)
    cvout_ref[...] = cvc
    conv_c = _causal_conv(cvc.reshape(ns, t, SC_DIM), cbuf_ref[...].reshape(ns, t, SC_DIM), scw_ref, SC_WIDTH)
    yc_ref[...] = (scb_ref[...].astype(F32) * conv_c.reshape(ns * t, SC_DIM)).astype(BF16)


def _mix_sample(p, dt, sink, cos_s, sin_s, cache_k, cache_v, h0, xbuf, cbuf,
                cw, cb, dtb, alog, de, nw, scw, layer, n_prompt, n_dec):
    r = MIX_ROWS
    ns = SEQS_PER_STEP
    steps = n_dec // ns
    base = n_prompt // r
    wb = cache_k.shape[2]

    def col(width, offset):
        return pl.BlockSpec((r, width), lambda i: (base + i, offset // width))

    def state(shape):
        return pl.BlockSpec((None, *shape), lambda i: (layer, i) + (0,) * (len(shape) - 1))

    def const(shape):
        return pl.BlockSpec(shape, lambda i: (0,) * len(shape))

    def rows(width):
        return pl.BlockSpec((r, width), lambda i: (i, 0))

    n_rows = n_dec * DEC_SEQ
    return pl.pallas_call(
        _mix_sample_kernel,
        grid=(steps,),
        in_specs=[
            pl.BlockSpec(memory_space=pltpu.SMEM),
            col(ATTN_WIDTH, COL_Q), col(2 * KV_WIDTH, COL_KV), col(SSD_D_INNER, COL_Z),
            col(SSD_CONV_DIM, COL_XBC),
            pl.BlockSpec((r, DT_PAD), lambda i: (base + i, 0)),
            col(SC_DIM, COL_SCB), col(SC_DIM, COL_SCC), col(SC_DIM, COL_SCV),
            const((r, LANES)), const((r, LANES)),
            state((ns, wb, KV_WIDTH)), state((ns, wb, KV_WIDTH)),
            state((ns, SSD_D_INNER, SSD_STATE)),
            state((r, SSD_CONV_DIM)), state((r, SC_DIM)),
            const((SSD_CONV, SSD_CONV_DIM)), const((1, SSD_CONV_DIM)),
            const((1, DT_PAD)), const((1, DT_PAD)), const((1, SSD_D_INNER)), const((1, SSD_D_INNER)),
            const((SC_WIDTH, SC_DIM)),
        ],
        out_specs=[
            rows(ATTN_WIDTH), rows(SSD_D_INNER), rows(SC_DIM),
            pl.BlockSpec((ns, wb, KV_WIDTH), lambda i: (i, 0, 0)),
            pl.BlockSpec((ns, wb, KV_WIDTH), lambda i: (i, 0, 0)),
            pl.BlockSpec((ns, SSD_D_INNER, SSD_STATE), lambda i: (i, 0, 0)),
            rows(SC_DIM),
        ],
        out_shape=[
            jax.ShapeDtypeStruct((n_rows, ATTN_WIDTH), BF16),
            jax.ShapeDtypeStruct((n_rows, SSD_D_INNER), BF16),
            jax.ShapeDtypeStruct((n_rows, SC_DIM), BF16),
            jax.ShapeDtypeStruct((n_dec, wb, KV_WIDTH), F32),
            jax.ShapeDtypeStruct((n_dec, wb, KV_WIDTH), F32),
            jax.ShapeDtypeStruct((n_dec, SSD_D_INNER, SSD_STATE), F32),
            jax.ShapeDtypeStruct((n_rows, SC_DIM), F32),
        ],
        compiler_params=_params(("parallel",)),
    )(sink, p, p, p, p, dt, p, p, p, cos_s, sin_s, cache_k, cache_v, h0, xbuf, cbuf,
      cw, cb, dtb, alog, de, nw, scw)


def _store_token_major(ref, v):
    n = v.shape[0]
    for s in range(ROW_CHUNKS):
        ref[pl.ds(s, n, stride=ROW_CHUNKS), :] = v[:, s * LANES:(s + 1) * LANES]


def _load_token_major(ref, n):
    return jnp.concatenate([ref[pl.ds(s, n, stride=ROW_CHUNKS), :] for s in range(ROW_CHUNKS)], axis=-1)


def _outproj_kernel(n_prompt_tiles, alpha,
                    xp_ref, xs_ref, yap_ref, yas_ref, ybp_ref, ybs_ref, ycp_ref, ycs_ref, g_ref,
                    g1p_ref, g1s_ref, sh2p_ref, sh2s_ref, sc2p_ref, sc2s_ref,
                    wpa_ref, wpb_ref, wpc_ref, wout_ref, wr_ref, br_ref, lng_ref, lnb_ref,
                    x1_ref, h_ref, route_ref, cnt_ref, cnt_scr):
    i = pl.program_id(0)
    is_sample = i >= n_prompt_tiles
    tm = xp_ref.shape[0]

    @pl.when(i == 0)
    def _():
        cnt_scr[...] = jnp.zeros_like(cnt_scr)

    n = tm
    cnt = cnt_scr[...]
    for r0 in range(0, tm, n):
        rs = slice(r0, r0 + n)

        def pick(prompt_ref, sample_ref):
            return jnp.where(is_sample, sample_ref[0, rs, :], prompt_ref[0])

        ya = jnp.where(is_sample, yas_ref[rs, :], yap_ref[rs, :])
        yb = jnp.where(is_sample, ybs_ref[rs, :], ybp_ref[rs, :])
        yc = jnp.where(is_sample, ycs_ref[rs, :], ycp_ref[rs, :])
        gates = 0.5 * jnp.tanh(0.5 * g_ref[rs, :]) + 0.5
        merged = (gates[:, :D_MODEL] * _dot(ya, wpa_ref[...]).astype(BF16)
                  + gates[:, D_MODEL:2 * D_MODEL] * _dot(yb, wpb_ref[...]).astype(BF16)
                  + gates[:, 2 * D_MODEL:] * _dot(yc, wpc_ref[...]).astype(BF16))
        mix = _dot(merged, wout_ref[...])
        g1 = pick(g1p_ref, g1s_ref)
        x = jnp.where(is_sample, xs_ref[rs, :], xp_ref[rs, :])
        x1 = _layer_norm(alpha * x + g1 * mix, lng_ref[...], lnb_ref[...])
        x1_ref[rs, :] = x1
        sh2 = pick(sh2p_ref, sh2s_ref)
        sc2 = pick(sc2p_ref, sc2s_ref)
        h = x1 * (1.0 + sc2) + sh2
        _store_token_major(h_ref.at[pl.ds(r0 * ROW_CHUNKS, n * ROW_CHUNKS)], h)

        lt = (_dot(h.astype(BF16), wr_ref[...]) + br_ref[...]).T[:ROUTE_ROWS]
        row = lax.broadcasted_iota(jnp.int32, lt.shape, 0).astype(F32)
        neg = -jnp.inf
        big = float(ROUTE_ROWS)
        gl = jnp.where(row < MOE_GROUPS, lt, neg)
        gmax = jnp.max(gl, axis=0, keepdims=True)
        g_p = 1.0 / jnp.sum(jnp.exp(gl - gmax), axis=0, keepdims=True)
        gidx = jnp.min(jnp.where(gl == gmax, row, big), axis=0, keepdims=True)
        lo = ROUTE_EXPERT_ROW + EXPERTS_PER_GROUP * gidx
        sel = jnp.where((row >= lo) & (row < lo + EXPERTS_PER_GROUP), lt, neg)
        m1 = jnp.max(sel, axis=0, keepdims=True)
        i1 = jnp.min(jnp.where(sel == m1, row, big), axis=0, keepdims=True)
        sel2 = jnp.where(row == i1, neg, sel)
        m2 = jnp.max(sel2, axis=0, keepdims=True)
        i2 = jnp.min(jnp.where(sel2 == m2, row, big), axis=0, keepdims=True)
        ssum = jnp.sum(jnp.exp(sel - m1), axis=0, keepdims=True)
        p1 = 1.0 / ssum
        p2 = jnp.exp(m2 - m1) / ssum
        w1 = g_p * (p1 / (p1 + p2))
        w2 = g_p * (p2 / (p1 + p2))

        onehot = jnp.where((row == i1) | (row == i2), 1.0, 0.0)
        ri = lax.broadcasted_iota(jnp.int32, (n, n), 0)
        ci = lax.broadcasted_iota(jnp.int32, (n, n), 1)
        earlier = jnp.where(ri < ci, 1.0, 0.0).astype(BF16)
        prefix = _dot(onehot.astype(BF16), earlier) + cnt[:, 0:1]
        rank1 = jnp.sum(jnp.where(row == i1, prefix, 0.0), axis=0, keepdims=True)
        rank2 = jnp.sum(jnp.where(row == i2, prefix, 0.0), axis=0, keepdims=True)
        cnt = cnt + jnp.sum(onehot, axis=1, keepdims=True)
        zero = jnp.zeros_like(w1)
        route_ref[:, rs] = jnp.concatenate(
            [i1 - ROUTE_EXPERT_ROW, i2 - ROUTE_EXPERT_ROW, w1, w2, rank1, rank2, zero, zero], axis=0)

    cnt_scr[...] = cnt
    cnt_ref[...] = cnt


def _output_projection(x, y_prompt, y_sample, p, mod_p, mod_s, wpa, wpb, wpc, wout, wr, br, lng, lnb,
                       layer, n_prompt, seq_len, alpha):
    t_all = n_prompt + x[1].shape[0]
    tm = ROW_TILE
    npt = n_prompt // tm
    n_seq = mod_p.shape[1]
    mods = []
    for col in (2, 3, 4):
        mods.extend(_mod_specs(layer, col, tm, npt, seq_len // tm, n_seq))

    def rows(width):
        return pl.BlockSpec((tm, width), lambda i: (i, 0))

    def prompt_rows(width):
        return pl.BlockSpec((tm, width), lambda i: (jnp.minimum(i, npt - 1), 0))

    def sample_rows(width):
        return pl.BlockSpec((tm, width), lambda i: (jnp.maximum(i - npt, 0), 0))

    def const(shape):
        return pl.BlockSpec(shape, lambda i: (0,) * len(shape))

    return pl.pallas_call(
        functools.partial(_outproj_kernel, npt, alpha),
        grid=(t_all // tm,),
        in_specs=[
            prompt_rows(D_MODEL), sample_rows(D_MODEL),
            prompt_rows(ATTN_WIDTH), sample_rows(ATTN_WIDTH),
            prompt_rows(SSD_D_INNER), sample_rows(SSD_D_INNER),
            prompt_rows(SC_DIM), sample_rows(SC_DIM),
            pl.BlockSpec((tm, 3 * D_MODEL), lambda i: (i, COL_GATES // (3 * D_MODEL))),
            *mods,
            _layer_spec(layer, (ATTN_WIDTH, D_MODEL)), _layer_spec(layer, (SSD_D_INNER, D_MODEL)),
            _layer_spec(layer, (SC_DIM, D_MODEL)), _layer_spec(layer, (D_MODEL, D_MODEL)),
            _layer_spec(layer, (D_MODEL, LANES)), _layer_spec(layer, (1, LANES)),
            const((1, D_MODEL)), const((1, D_MODEL)),
        ],
        out_specs=[
            rows(D_MODEL),
            pl.BlockSpec((tm * ROW_CHUNKS, LANES), lambda i: (i, 0)),
            pl.BlockSpec((SUBLANES, tm), lambda i: (0, i)),
            const((ROUTE_ROWS, LANES)),
        ],
        out_shape=[
            jax.ShapeDtypeStruct((t_all, D_MODEL), F32),
            jax.ShapeDtypeStruct((t_all * ROW_CHUNKS, LANES), F32),
            jax.ShapeDtypeStruct((SUBLANES, t_all), F32),
            jax.ShapeDtypeStruct((ROUTE_ROWS, LANES), F32),
        ],
        scratch_shapes=[pltpu.VMEM((ROUTE_ROWS, LANES), F32)],
        compiler_params=_params(("arbitrary",)),
    )(x[0], x[1], y_prompt[0], y_sample[0], y_prompt[1], y_sample[1], y_prompt[2], y_sample[2],
      p, mod_p, mod_s, mod_p, mod_s, mod_p, mod_s,
      wpa, wpb, wpc, wout, wr, br, lng, lnb)


def _slots_kernel(route_ref, start_ref, dest_ref):
    route = route_ref[...]
    tm = route.shape[1]
    expert = lax.broadcasted_iota(jnp.int32, (N_EXPERTS, tm), 0).astype(F32)
    start = start_ref[:, 0:1]
    rows = []
    for e_row, r_row in ((0, 4), (1, 5)):
        first = jnp.sum(jnp.where(expert == route[e_row:e_row + 1], start, 0.0), axis=0, keepdims=True)
        rows.append(first + route[r_row:r_row + 1])
    rows.append(jnp.zeros((SUBLANES - 2, tm), F32))
    dest_ref[...] = jnp.concatenate(rows, axis=0).astype(jnp.int32)


def _slots(route, pad_start):
    t_all = route.shape[1]
    tm = ROW_TILE_IN
    dest = pl.pallas_call(
        _slots_kernel,
        grid=(t_all // tm,),
        in_specs=[pl.BlockSpec((SUBLANES, tm), lambda i: (0, i)),
                  pl.BlockSpec((N_EXPERTS, LANES), lambda i: (0, 0))],
        out_specs=pl.BlockSpec((SUBLANES, tm), lambda i: (0, i)),
        out_shape=jax.ShapeDtypeStruct((SUBLANES, t_all), jnp.int32),
        compiler_params=_params(("parallel",)),
    )(route, jnp.broadcast_to(pad_start.astype(F32)[:, None], (N_EXPERTS, LANES)))
    return dest[0], dest[1]


def _token_copy(src, src_row, dst, dst_row, sem):
    return pltpu.make_async_copy(
        src.at[pl.ds(pl.multiple_of(src_row * ROW_CHUNKS, ROW_CHUNKS), ROW_CHUNKS)],
        dst.at[pl.ds(pl.multiple_of(dst_row * ROW_CHUNKS, ROW_CHUNKS), ROW_CHUNKS)],
        sem)


def _dispatch_kernel(d1_ref, d2_ref, pend_ref, cnt_ref, nb_ref, h_ref, xb_ref, zero_scr, zsem, sem):
    i = pl.program_id(0)
    tm = h_ref.shape[0] // ROW_CHUNKS
    blk_rows = EXPERT_BLOCK * ROW_CHUNKS
    n_blocks = xb_ref.shape[0] // blk_rows

    def zero_block(b):
        start = pl.multiple_of(b * blk_rows, blk_rows)
        return pltpu.make_async_copy(zero_scr, xb_ref.at[pl.ds(start, blk_rows)], zsem)

    def last_block(e):
        return pend_ref[e] // EXPERT_BLOCK - 1

    @pl.when(i == 0)
    def _():
        zero_scr[...] = jnp.zeros_like(zero_scr)
        for e in range(N_EXPERTS):
            @pl.when(cnt_ref[e] > 0)
            def _():
                zero_block(last_block(e)).start()
        lax.fori_loop(nb_ref[0], n_blocks, lambda b, c: (zero_block(b).start(), c)[1], 0)
        for e in range(N_EXPERTS):
            @pl.when(cnt_ref[e] > 0)
            def _():
                zero_block(last_block(e)).wait()
        lax.fori_loop(nb_ref[0], n_blocks, lambda b, c: (zero_block(b).wait(), c)[1], 0)

    def issue(c, carry):
        for u in range(ISSUE_UNROLL):
            t = c * ISSUE_UNROLL + u
            g = i * tm + t
            _token_copy(h_ref, t, xb_ref, d1_ref[g], sem.at[0]).start(priority=0)
            _token_copy(h_ref, t, xb_ref, d2_ref[g], sem.at[1]).start(priority=1)
        return carry

    lax.fori_loop(0, tm // ISSUE_UNROLL, issue, 0)
    for k in range(2):
        pltpu.make_async_copy(h_ref, xb_ref.at[pl.ds(0, tm * ROW_CHUNKS)], sem.at[k]).wait()


def _dispatch(h, dest1, dest2, pad_end, counts, n_used, n_slots):
    t_all = h.shape[0] // ROW_CHUNKS
    tm = ROW_TILE_MOVE
    grid_spec = pltpu.PrefetchScalarGridSpec(
        num_scalar_prefetch=5,
        grid=(t_all // tm,),
        in_specs=[pl.BlockSpec((tm * ROW_CHUNKS, LANES), lambda i, *_: (i, 0))],
        out_specs=pl.BlockSpec(memory_space=pl.ANY),
        scratch_shapes=[
            pltpu.VMEM((EXPERT_BLOCK * ROW_CHUNKS, LANES), F32),
            pltpu.SemaphoreType.DMA(()),
            pltpu.SemaphoreType.DMA((2,)),
        ],
    )
    return pl.pallas_call(
        _dispatch_kernel,
        grid_spec=grid_spec,
        out_shape=jax.ShapeDtypeStruct((n_slots * ROW_CHUNKS, LANES), F32),
        compiler_params=_params(("arbitrary",)),
    )(dest1, dest2, pad_end, counts, n_used, h)


def _ffn_kernel(layer, cstart_ref, cbig_ref, cexp_ref, next_ref, meta_ref,
                xb_ref, wg_ref, wu_ref, wd_ref, yb_ref,
                x_in, y_out, wg_f, wu_f, wd_f, wg_b, wu_b, wd_b, x_scr, xsem, ysem, wsem):
    blk_rows = EXPERT_BLOCK * ROW_CHUNKS
    n_blocks = yb_ref.shape[0] // blk_rows
    n_chunks, first_expert, n_used = meta_ref[0], meta_ref[1], meta_ref[2]

    def chunk_blocks(size_class):
        return CHUNK_SIZES[size_class]

    def hbm_rows(ref, j, big):
        start = pl.multiple_of(cstart_ref[j] * blk_rows, blk_rows)
        return ref.at[pl.ds(start, chunk_blocks(big) * blk_rows)]

    def x_copy(j, s, big):
        return pltpu.make_async_copy(hbm_rows(xb_ref, j, big),
                                     x_in.at[s, pl.ds(0, chunk_blocks(big) * blk_rows)], xsem.at[s])

    def y_copy(j, s, big):
        return pltpu.make_async_copy(y_out.at[s, pl.ds(0, chunk_blocks(big) * blk_rows)],
                                     hbm_rows(yb_ref, j, big), ysem.at[s])

    def by_size(j, fn):
        for big in range(len(CHUNK_SIZES)):
            @pl.when(cbig_ref[j] == big)
            def _():
                fn(big)

    def w_copies(e, s):
        return (pltpu.make_async_copy(wg_ref.at[layer, e], wg_f.at[s], wsem.at[s, 0]),
                pltpu.make_async_copy(wu_ref.at[layer, e], wu_f.at[s], wsem.at[s, 1]),
                pltpu.make_async_copy(wd_ref.at[layer, e], wd_f.at[s], wsem.at[s, 2]))

    for c in w_copies(first_expert, 0):
        c.start(priority=1)
    by_size(0, lambda big: x_copy(0, 0, big).start())

    def chunk(j, wslot):
        e = cexp_ref[j]
        new_expert = (j == 0) | (cexp_ref[jnp.maximum(j - 1, 0)] != e)
        wslot = jnp.where(new_expert & (j > 0), 1 - wslot, wslot)

        @pl.when(new_expert)
        def _():
            for c in w_copies(e, wslot):
                c.wait()

            @pl.when(next_ref[e] < N_EXPERTS)
            def _():
                for c in w_copies(next_ref[e], 1 - wslot):
                    c.start(priority=1)

            wg_b[...] = wg_f[wslot].astype(BF16)
            wu_b[...] = wu_f[wslot].astype(BF16)
            wd_b[...] = wd_f[wslot].astype(BF16)

        s = j % 2
        by_size(j, lambda big: x_copy(j, s, big).wait())

        @pl.when(j + 1 < n_chunks)
        def _():
            by_size(j + 1, lambda big: x_copy(j + 1, 1 - s, big).start())

        @pl.when(j >= 2)
        def _():
            by_size(j - 2, lambda big: y_copy(j - 2, s, big).wait())

        def compute(big):
            rows = chunk_blocks(big) * EXPERT_BLOCK
            x_tok = x_in.at[s]
            for c in range(ROW_CHUNKS):
                x_scr[0:rows, c * LANES:(c + 1) * LANES] = (
                    x_tok[pl.ds(c, rows, stride=ROW_CHUNKS), :].astype(BF16))
            x = x_scr[0:rows, :]
            act = (_silu(_dot(x, wg_b[...])) * _dot(x, wu_b[...])).astype(BF16)
            _store_token_major(y_out.at[s, pl.ds(0, rows * ROW_CHUNKS)], _dot(act, wd_b[...]))
            y_copy(j, s, big).start()

        by_size(j, compute)
        return wslot

    lax.fori_loop(0, n_chunks, chunk, 0)

    @pl.when(n_chunks >= 2)
    def _():
        by_size(n_chunks - 2, lambda big: y_copy(n_chunks - 2, n_chunks % 2, big).wait())

    by_size(n_chunks - 1, lambda big: y_copy(n_chunks - 1, (n_chunks - 1) % 2, big).wait())

    zero_rows = y_out.at[0, pl.ds(0, blk_rows)]
    zero_rows[...] = jnp.zeros((blk_rows, LANES), F32)

    def tail_copy(b):
        start = pl.multiple_of(b * blk_rows, blk_rows)
        return pltpu.make_async_copy(zero_rows, yb_ref.at[pl.ds(start, blk_rows)], ysem.at[0])

    lax.fori_loop(n_used, n_blocks, lambda b, c: (tail_copy(b).start(), c)[1], 0)
    lax.fori_loop(n_used, n_blocks, lambda b, c: (tail_copy(b).wait(), c)[1], 0)


def _chunk_plan(first_block, n_expert_blocks, n_blocks):
    experts = jnp.arange(N_EXPERTS, dtype=jnp.int32)
    per_size, left = [], n_expert_blocks
    for size in CHUNK_SIZES:
        per_size.append(left // size)
        left = left % size
    n_chunks_e = sum(per_size)
    chunk_end = jnp.cumsum(n_chunks_e)
    chunk_start = chunk_end - n_chunks_e
    j = jnp.arange(n_blocks, dtype=jnp.int32)
    e_j = jnp.minimum(jnp.sum((chunk_end[None, :] <= j[:, None]).astype(jnp.int32), axis=1), N_EXPERTS - 1)
    pick = (e_j[:, None] == experts[None, :]).astype(jnp.int32)

    def of_expert(v):
        return jnp.sum(pick * v[None, :], axis=1)

    c = j - of_expert(chunk_start)
    start = of_expert(first_block)
    size_class = jnp.zeros_like(j)
    for k, size in enumerate(CHUNK_SIZES):
        n_k = of_expert(per_size[k])
        inside = (c >= 0) & (c < n_k)
        start = start + jnp.where(inside, size * c, jnp.where(c >= n_k, size * n_k, 0))
        size_class = jnp.where(inside, k, size_class)
        c = jnp.where(inside, -1, c - n_k)
    owner = jnp.where(n_chunks_e > 0, experts, N_EXPERTS)
    later = jnp.flip(lax.cummin(jnp.flip(owner)))
    next_owner = jnp.concatenate([later[1:], jnp.full((1,), N_EXPERTS, jnp.int32)])
    meta = jnp.stack([chunk_end[-1], later[0]]).astype(jnp.int32)
    return start.astype(jnp.int32), size_class.astype(jnp.int32), e_j, next_owner.astype(jnp.int32), meta


def _expert_ffn(xb, first_block, n_expert_blocks, n_used, wg, wu, wd, layer):
    blk_rows = EXPERT_BLOCK * ROW_CHUNKS
    n_blocks = xb.shape[0] // blk_rows
    cstart, cbig, cexp, next_owner, meta = _chunk_plan(first_block, n_expert_blocks, n_blocks)
    meta = jnp.concatenate([meta, n_used])
    any_spec = pl.BlockSpec(memory_space=pl.ANY)
    grid_spec = pltpu.PrefetchScalarGridSpec(
        num_scalar_prefetch=5,
        grid=(1,),
        in_specs=[any_spec] * 4,
        out_specs=any_spec,
        scratch_shapes=[
            pltpu.VMEM((2, CHUNK_BLOCKS * blk_rows, LANES), F32),
            pltpu.VMEM((2, CHUNK_BLOCKS * blk_rows, LANES), F32),
            pltpu.VMEM((2, D_MODEL, EXPERT_FF), F32),
            pltpu.VMEM((2, D_MODEL, EXPERT_FF), F32),
            pltpu.VMEM((2, EXPERT_FF, D_MODEL), F32),
            pltpu.VMEM((D_MODEL, EXPERT_FF), BF16),
            pltpu.VMEM((D_MODEL, EXPERT_FF), BF16),
            pltpu.VMEM((EXPERT_FF, D_MODEL), BF16),
            pltpu.VMEM((CHUNK_BLOCKS * EXPERT_BLOCK, D_MODEL), BF16),
            pltpu.SemaphoreType.DMA((2,)),
            pltpu.SemaphoreType.DMA((2,)),
            pltpu.SemaphoreType.DMA((2, 3)),
        ],
    )
    return pl.pallas_call(
        functools.partial(_ffn_kernel, layer),
        grid_spec=grid_spec,
        out_shape=jax.ShapeDtypeStruct(xb.shape, F32),
        compiler_params=_params(("arbitrary",)),
    )(cstart, cbig, cexp, next_owner, meta, xb, wg, wu, wd)


def _combine_kernel(n_prompt_tiles, alpha, d1_ref, d2_ref,
                    x1_ref, route_ref, g2p_ref, g2s_ref, lng_ref, lnb_ref, yb_ref,
                    xp_ref, xs_ref, buf_a, buf_b, sem):
    i = pl.program_id(0)
    n_tiles = pl.num_programs(0)
    tm = x1_ref.shape[0]
    slot = i % 2

    def gather_tile(tile, s):
        def issue(c, carry):
            for u in range(ISSUE_UNROLL):
                t = c * ISSUE_UNROLL + u
                g = tile * tm + t
                _token_copy(yb_ref, d1_ref[g], buf_a.at[s], t, sem.at[s, 0]).start(priority=0)
                _token_copy(yb_ref, d2_ref[g], buf_b.at[s], t, sem.at[s, 1]).start(priority=1)
            return carry

        lax.fori_loop(0, tm // ISSUE_UNROLL, issue, 0)

    @pl.when(i == 0)
    def _():
        gather_tile(0, 0)

    @pl.when(i + 1 < n_tiles)
    def _():
        gather_tile(i + 1, 1 - slot)

    whole = yb_ref.at[pl.ds(0, tm * ROW_CHUNKS)]
    pltpu.make_async_copy(whole, buf_a.at[slot], sem.at[slot, 0]).wait()
    pltpu.make_async_copy(whole, buf_b.at[slot], sem.at[slot, 1]).wait()
    route_t = jnp.concatenate(
        [route_ref[...], jnp.zeros((LANES - SUBLANES, tm), F32)], axis=0).T
    ffn = (route_t[:, 2:3] * _load_token_major(buf_a.at[slot], tm)
           + route_t[:, 3:4] * _load_token_major(buf_b.at[slot], tm))
    g2 = _pick_mod(i >= n_prompt_tiles, g2p_ref, g2s_ref)
    out = _layer_norm(alpha * x1_ref[...] + g2 * ffn, lng_ref[...], lnb_ref[...])
    @pl.when(i < n_prompt_tiles)
    def _():
        xp_ref[...] = out

    @pl.when(i >= n_prompt_tiles)
    def _():
        xs_ref[...] = out


def _combine(x1, route, yb, dest1, dest2, mod_p, mod_s, lng, lnb, layer, n_prompt, seq_len, alpha):
    t_all = x1.shape[0]
    tm = ROW_TILE_MOVE
    npt = n_prompt // tm
    out_specs = [pl.BlockSpec((tm, D_MODEL), lambda i, *_: (jnp.minimum(i, npt - 1), 0)),
                 pl.BlockSpec((tm, D_MODEL), lambda i, *_: (jnp.maximum(i - npt, 0), 0))]
    out_shape = [jax.ShapeDtypeStruct((n_prompt, D_MODEL), F32),
                 jax.ShapeDtypeStruct((t_all - n_prompt, D_MODEL), F32)]
    g2p, g2s = _mod_specs(layer, 5, tm, npt, seq_len // tm, mod_p.shape[1])
    grid_spec = pltpu.PrefetchScalarGridSpec(
        num_scalar_prefetch=2,
        grid=(t_all // tm,),
        in_specs=[
            pl.BlockSpec((tm, D_MODEL), lambda i, *_: (i, 0)),
            pl.BlockSpec((SUBLANES, tm), lambda i, *_: (0, i)),
            g2p, g2s,
            pl.BlockSpec((1, D_MODEL), lambda i, *_: (0, 0)),
            pl.BlockSpec((1, D_MODEL), lambda i, *_: (0, 0)),
            pl.BlockSpec(memory_space=pl.ANY),
        ],
        out_specs=out_specs,
        scratch_shapes=[
            pltpu.VMEM((2, tm * ROW_CHUNKS, LANES), F32),
            pltpu.VMEM((2, tm * ROW_CHUNKS, LANES), F32),
            pltpu.SemaphoreType.DMA((2, 2)),
        ],
    )
    return pl.pallas_call(
        functools.partial(_combine_kernel, npt, alpha),
        grid_spec=grid_spec,
        out_shape=out_shape,
        compiler_params=_params(("arbitrary",)),
    )(dest1, dest2, x1, route, mod_p, mod_s, lng, lnb, yb)


def _rope_tables(pos):
    inv = jnp.power(ROPE_THETA, -jnp.arange(ROT_HALF, dtype=F32) * (2.0 / ROT_DIM))
    ang = pos.astype(F32)[:, None] * inv[None, :]
    cos, sin = jnp.cos(ang), jnp.sin(ang)
    rest = HEAD_DIM - ROT_DIM
    n = pos.shape[0]
    cos_h = jnp.concatenate([cos, cos, jnp.ones((n, rest), F32)], axis=-1)
    sin_h = jnp.concatenate([-sin, sin, jnp.zeros((n, rest), F32)], axis=-1)
    reps = LANES // HEAD_DIM
    return jnp.tile(cos_h, (1, reps)), jnp.tile(sin_h, (1, reps))


def _permute_w_in(w_in):
    sizes = (ATTN_WIDTH, KV_WIDTH, KV_WIDTH, SSD_D_INNER, SSD_CONV_DIM, SSD_HEADS,
             SC_DIM, SC_DIM, SC_DIM, 3 * D_MODEL)
    offs = np.concatenate([[0], np.cumsum(sizes)])
    q, k, v, z, xbc, dt, scb, scc, scv, gates = (w_in[..., offs[n]:offs[n + 1]] for n in range(len(sizes)))
    main = jnp.concatenate([q, z, xbc, scb, scc, gates, scv, k, v], axis=-1).astype(BF16)
    dt = jnp.pad(dt, ((0, 0), (0, 0), (0, DT_PAD - SSD_HEADS))).astype(BF16)
    return main, dt


def _pad_lanes(v, width):
    return jnp.pad(v, ((0, 0), (0, width - v.shape[-1])))


def kernel(x_prompt, x_sample, c_prompt, c_sample, cache_attn_k, cache_attn_v, state_ssm, state_ssd_conv, state_short_conv, w_ada, b_ada, w_in, attn_sink, ssd_conv_w, ssd_conv_b, ssd_dt_bias, ssd_a_log, ssd_d, ssd_norm_w, sc_conv_w, w_pa, w_pb, w_pc, w_out, ln1_g, ln1_b, ln2_g, ln2_b, router_g_w, router_g_b, router_e_w, router_e_b, moe_w_gate, moe_w_up, moe_w_down):
    depth = w_in.shape[0]
    n_seq, seq_len, _ = x_prompt.shape
    n_dec, dec_len, _ = x_sample.shape
    wb = cache_attn_k.shape[2]
    past_len = 8192
    assert dec_len == DEC_SEQ and wb == WINDOW
    assert seq_len % ROW_TILE_IN == 0 and (n_dec * dec_len) % ROW_TILE_IN == 0
    n_prompt = n_seq * seq_len
    n_sample = n_dec * dec_len
    t_all = n_prompt + n_sample
    alpha = (2 * depth) ** 0.25

    mod = _modulation(jnp.concatenate([c_prompt, c_sample], axis=0), w_ada, b_ada)
    mod_p_all = mod[:, :n_seq].reshape(depth, n_seq, 1, 6 * D_MODEL)
    mod_s_all = jnp.repeat(mod[:, n_seq:], dec_len, axis=1).reshape(depth, 1, n_sample, 6 * D_MODEL)

    w_main_all, w_dt_all = _permute_w_in(w_in)
    cos_p, sin_p = _rope_tables(jnp.arange(seq_len, dtype=jnp.int32))
    cos_s, sin_s = _rope_tables(past_len + (jnp.arange(MIX_ROWS, dtype=jnp.int32) % dec_len))

    a_total = 2 * t_all
    n_blocks = (a_total + N_EXPERTS * (EXPERT_BLOCK - 1)) // EXPERT_BLOCK
    n_slots = n_blocks * EXPERT_BLOCK

    mod_p, mod_s = mod_p_all, mod_s_all
    cache_k = cache_attn_k.reshape(depth, n_dec, wb, KV_WIDTH)
    cache_v = cache_attn_v.reshape(depth, n_dec, wb, KV_WIDTH)
    h0 = state_ssm.reshape(depth, n_dec, SSD_D_INNER, SSD_STATE)
    xbuf = jnp.pad(state_ssd_conv, ((0, 0), (0, 0), (dec_len - (SSD_CONV - 1), 0), (0, 0))
                   ).reshape(depth, n_sample, SSD_CONV_DIM)
    cbuf = jnp.pad(state_short_conv, ((0, 0), (0, 0), (dec_len - (SC_WIDTH - 1), 0), (0, 0))
                   ).reshape(depth, n_sample, SC_DIM)
    wpa, wpb, wpc, wout = (w.astype(BF16) for w in (w_pa, w_pb, w_pc, w_out))
    gap = ROUTE_EXPERT_ROW - MOE_GROUPS
    tail = LANES - ROUTE_EXPERT_ROW - N_EXPERTS
    wr = jnp.concatenate([router_g_w, jnp.zeros((depth, D_MODEL, gap), F32), router_e_w,
                          jnp.zeros((depth, D_MODEL, tail), F32)], axis=-1).astype(BF16)
    br = jnp.concatenate([router_g_b, jnp.zeros((depth, gap), F32), router_e_b,
                          jnp.zeros((depth, tail), F32)], axis=-1)[:, None, :]

    x = (x_prompt.reshape(n_prompt, D_MODEL), x_sample.reshape(n_sample, D_MODEL))
    outs_p = [[] for _ in range(5)]
    outs_s = [[] for _ in range(5)]
    for l in range(depth):
        p, dt = _input_projection(x, mod_p, mod_s, w_main_all, w_dt_all, l, n_prompt, seq_len)

        cw, cb = ssd_conv_w[l], ssd_conv_b[l][None]
        dtb = _pad_lanes(ssd_dt_bias[l][None], DT_PAD)
        alog = _pad_lanes(ssd_a_log[l][None], DT_PAD)
        de = jnp.repeat(ssd_d[l], SSD_HEAD_DIM)[None]
        nw = ssd_norm_w[l][None]
        scw = sc_conv_w[l]

        ya_p, krot_p = _attention_prompt(p, attn_sink[l], cos_p, sin_p, n_seq, seq_len)
        yb_p, yc_p, h_p, cv_p = _ssd_prompt(p, dt, cw, cb, dtb, alog, de, nw, scw, n_seq, seq_len)

        ya_s, yb_s, yc_s, k_s, v_s, h_s, cv_s = _mix_sample(
            p, dt, attn_sink[l], cos_s, sin_s, cache_k, cache_v, h0, xbuf, cbuf,
            cw, cb, dtb, alog, de, nw, scw, l, n_prompt, n_dec)

        x1, h, route, counts = _output_projection(
            x, (ya_p, yb_p, yc_p), (ya_s, yb_s, yc_s), p, mod_p, mod_s,
            wpa, wpb, wpc, wout, wr, br, ln1_g[l][None], ln1_b[l][None], l, n_prompt, seq_len, alpha)

        cnt = counts[ROUTE_EXPERT_ROW:ROUTE_EXPERT_ROW + N_EXPERTS, 0].astype(jnp.int32)
        pad_cnt = (cnt + EXPERT_BLOCK - 1) // EXPERT_BLOCK * EXPERT_BLOCK
        pad_end = jnp.cumsum(pad_cnt)
        pad_start = pad_end - pad_cnt
        n_used = (pad_end[-1:] // EXPERT_BLOCK).astype(jnp.int32)
        dest1, dest2 = _slots(route, pad_start)

        xb = _dispatch(h, dest1, dest2, pad_end.astype(jnp.int32), cnt, n_used, n_slots)
        yb_slots = _expert_ffn(xb, (pad_start // EXPERT_BLOCK).astype(jnp.int32),
                               (pad_cnt // EXPERT_BLOCK).astype(jnp.int32), n_used,
                               moe_w_gate, moe_w_up, moe_w_down, l)
        x = _combine(x1, route, yb_slots, dest1, dest2, mod_p, mod_s, ln2_g[l][None], ln2_b[l][None],
                     l, n_prompt, seq_len, alpha)

        def prompt_tail(rows, c0, c1):
            return jnp.stack([p[(b + 1) * seq_len - rows:(b + 1) * seq_len, c0:c1] for b in range(n_seq)]
                             ).astype(F32)

        outs_p[0].append(krot_p.reshape(n_seq, wb, N_KV_HEADS, HEAD_DIM))
        outs_p[1].append(prompt_tail(wb, COL_KV + KV_WIDTH, COL_KV + 2 * KV_WIDTH)
                         .reshape(n_seq, wb, N_KV_HEADS, HEAD_DIM))
        outs_p[2].append(h_p.reshape(n_seq, SSD_HEADS, SSD_HEAD_DIM, SSD_STATE))
        outs_p[3].append(prompt_tail(SSD_CONV - 1, COL_XBC, COL_XBC + SSD_CONV_DIM))
        outs_p[4].append(cv_p[:, SUBLANES - (SC_WIDTH - 1):, :])
        outs_s[0].append(k_s.reshape(n_dec, wb, N_KV_HEADS, HEAD_DIM))
        outs_s[1].append(v_s.reshape(n_dec, wb, N_KV_HEADS, HEAD_DIM))
        outs_s[2].append(h_s.reshape(n_dec, SSD_HEADS, SSD_HEAD_DIM, SSD_STATE))
        outs_s[3].append(p[n_prompt:, COL_XBC:COL_XBC + SSD_CONV_DIM].astype(F32)
                         .reshape(n_dec, dec_len, SSD_CONV_DIM)[:, dec_len - (SSD_CONV - 1):, :])
        outs_s[4].append(cv_s.reshape(n_dec, dec_len, SC_DIM)[:, dec_len - (SC_WIDTH - 1):, :])

    y_prompt = x[0].reshape(n_seq, seq_len, D_MODEL)
    y_sample = x[1].reshape(n_dec, dec_len, D_MODEL)
    return (y_prompt, y_sample, *[jnp.stack(o) for o in outs_p], *[jnp.stack(o) for o in outs_s])
```

```python
import functools

import jax
import jax.numpy as jnp
import numpy as np
from jax import lax
from jax.experimental import pallas as pl
from jax.experimental.pallas import tpu as pltpu

F32 = jnp.float32
BF16 = jnp.bfloat16

D_MODEL = 1024
HEAD_DIM = 64
N_HEADS = 8
N_KV_HEADS = 2
Q_PER_KV = N_HEADS // N_KV_HEADS
ATTN_WIDTH = N_HEADS * HEAD_DIM
KV_WIDTH = N_KV_HEADS * HEAD_DIM
WINDOW = 128
ROT_DIM = HEAD_DIM // 4
ROT_HALF = ROT_DIM // 2
ROPE_THETA = 500000.0
ATTN_SCALE = HEAD_DIM ** -0.5
SSD_D_INNER = 512
SSD_HEAD_DIM = 64
SSD_HEADS = 8
SSD_GROUPS = 2
SSD_HEADS_PER_GROUP = SSD_HEADS // SSD_GROUPS
SSD_STATE = 128
SSD_CONV = 4
SSD_CONV_DIM = SSD_D_INNER + 2 * SSD_GROUPS * SSD_STATE
SSD_CHUNK = 128
SC_DIM = 512
SC_WIDTH = 3
MOE_GROUPS = 4
EXPERTS_PER_GROUP = 8
N_EXPERTS = MOE_GROUPS * EXPERTS_PER_GROUP
EXPERT_FF = 512
LN_EPS = 1e-5
RMS_EPS = 1e-5

SUBLANES = 8
LANES = 128
VMEM_LIMIT = 56 * 1024 * 1024

COL_Q = 0
COL_Z = 512
COL_XBC = 1024
COL_SCB = 2048
COL_SCC = 2560
COL_GATES = 3072
COL_SCV = 6144
COL_KV = 6656
P_MAIN = 6912
DT_PAD = 128

ROW_TILE_IN = 1024
COL_TILE_IN = 2304
ROW_TILE = 512
MIX_ROWS = 128
DEC_SEQ = 8
SEQS_PER_STEP = MIX_ROWS // DEC_SEQ
EXPERT_BLOCK = 256
CHUNK_SIZES = (4, 2, 1)
CHUNK_BLOCKS = CHUNK_SIZES[0]
ROW_CHUNKS = D_MODEL // LANES
ISSUE_UNROLL = 8
ROUTE_EXPERT_ROW = 8
ROUTE_ROWS = 48


def _silu(v):
    return v * jax.nn.sigmoid(v)


def _dot(a, b):
    return jnp.dot(a, b, preferred_element_type=F32)


def _dot_nt(a, b):
    return lax.dot_general(a, b, (((1,), (1,)), ((), ())), preferred_element_type=F32)


def _dot_exact(a, b):
    return jnp.dot(a, b, preferred_element_type=F32, precision=lax.Precision.HIGHEST)


def _params(sem):
    return pltpu.CompilerParams(dimension_semantics=sem, vmem_limit_bytes=VMEM_LIMIT)


def _pick_mod(is_sample, prompt_ref, sample_ref):
    return jnp.where(is_sample, sample_ref[0], prompt_ref[0])


def _layer_norm(v, g, b):
    mu = jnp.mean(v, axis=-1, keepdims=True)
    c = v - mu
    var = jnp.mean(c * c, axis=-1, keepdims=True)
    return c * lax.rsqrt(var + LN_EPS) * g + b


def _mod_kernel(c_ref, w_ref, b_ref, o_ref):
    s = _silu(c_ref[...]).astype(BF16)
    o_ref[0] = _dot(s, w_ref[0].astype(BF16)) + b_ref[0]


def _modulation(c_all, w_ada, b_ada):
    depth, _, width = w_ada.shape
    n = c_all.shape[0]
    tn = 1536
    return pl.pallas_call(
        _mod_kernel,
        grid=(depth, width // tn),
        in_specs=[
            pl.BlockSpec((n, D_MODEL), lambda l, j: (0, 0)),
            pl.BlockSpec((1, D_MODEL, tn), lambda l, j: (l, 0, j)),
            pl.BlockSpec((1, 1, tn), lambda l, j: (l, 0, j)),
        ],
        out_specs=pl.BlockSpec((1, n, tn), lambda l, j: (l, 0, j)),
        out_shape=jax.ShapeDtypeStruct((depth, n, width), F32),
        compiler_params=_params(("parallel", "parallel")),
    )(c_all, w_ada, b_ada.reshape(depth, 1, width))


def _inproj_kernel(n_prompt_tiles, xp_ref, xs_ref, shp_ref, shs_ref, scp_ref, scs_ref, w_ref, wdt_ref,
                   p_ref, dt_ref, u_scr):
    i = pl.program_id(0)
    j = pl.program_id(1)

    @pl.when(j == 0)
    def _():
        is_sample = i >= n_prompt_tiles
        sh = _pick_mod(is_sample, shp_ref, shs_ref)
        sc = _pick_mod(is_sample, scp_ref, scs_ref)
        x = jnp.where(is_sample, xs_ref[...], xp_ref[...])
        u = (x * (1.0 + sc) + sh).astype(BF16)
        u_scr[...] = u
        dt_ref[...] = _dot(u, wdt_ref[...])

    p_ref[...] = _dot(u_scr[...], w_ref[...]).astype(BF16)


def _mod_specs(layer, col, row_tile, n_prompt_tiles, tiles_per_seq, n_seq):
    def prompt_map(i, *_):
        return (layer, jnp.minimum(i // tiles_per_seq, n_seq - 1), 0, col)

    def sample_map(i, *_):
        return (layer, 0, jnp.maximum(i - n_prompt_tiles, 0), col)

    return (pl.BlockSpec((None, 1, 1, D_MODEL), prompt_map),
            pl.BlockSpec((None, 1, row_tile, D_MODEL), sample_map))


def _layer_spec(layer, shape):
    return pl.BlockSpec((None, *shape), lambda *_: (layer,) + (0,) * len(shape))


def _input_projection(x, mod_p, mod_s, w_main, w_dt, layer, n_prompt, seq_len):
    x_prompt, x_sample = x
    t_all = n_prompt + x_sample.shape[0]
    tm, tn = ROW_TILE_IN, COL_TILE_IN
    npt = n_prompt // tm
    n_seq = mod_p.shape[1]
    shp, shs = _mod_specs(layer, 0, tm, npt, seq_len // tm, n_seq)
    scp, scs = _mod_specs(layer, 1, tm, npt, seq_len // tm, n_seq)
    return pl.pallas_call(
        functools.partial(_inproj_kernel, npt),
        grid=(t_all // tm, P_MAIN // tn),
        in_specs=[
            pl.BlockSpec((tm, D_MODEL), lambda i, j: (jnp.minimum(i, npt - 1), 0)),
            pl.BlockSpec((tm, D_MODEL), lambda i, j: (jnp.maximum(i - npt, 0), 0)),
            shp, shs, scp, scs,
            pl.BlockSpec((None, D_MODEL, tn), lambda i, j: (layer, 0, j)),
            _layer_spec(layer, (D_MODEL, DT_PAD)),
        ],
        out_specs=[
            pl.BlockSpec((tm, tn), lambda i, j: (i, j)),
            pl.BlockSpec((tm, DT_PAD), lambda i, j: (i, 0)),
        ],
        out_shape=[
            jax.ShapeDtypeStruct((t_all, P_MAIN), BF16),
            jax.ShapeDtypeStruct((t_all, DT_PAD), F32),
        ],
        scratch_shapes=[pltpu.VMEM((tm, D_MODEL), BF16)],
        compiler_params=_params(("parallel", "arbitrary")),
    )(x_prompt, x_sample, mod_p, mod_s, mod_p, mod_s, w_main, w_dt)


def _rope(v, cos, sin):
    width = v.shape[-1]
    reps = width // LANES
    if reps > 1:
        cos = jnp.concatenate([cos] * reps, axis=-1)
        sin = jnp.concatenate([sin] * reps, axis=-1)
    lane = lax.broadcasted_iota(jnp.int32, v.shape, 1) % HEAD_DIM
    partner = jnp.where(lane < ROT_HALF,
                        pltpu.roll(v, width - ROT_HALF, 1),
                        pltpu.roll(v, ROT_HALF, 1))
    return v * cos + partner * sin


def _shift_rows(cur, prev, k):
    axis = cur.ndim - 2
    idx = lax.broadcasted_iota(jnp.int32, cur.shape, axis)
    return jnp.where(idx < k, pltpu.roll(prev, k, axis), pltpu.roll(cur, k, axis))


def _causal_conv(cur, prev, w_ref, width):
    out = cur * w_ref[width - 1:width, :]
    for k in range(1, width):
        out = out + _shift_rows(cur, prev, k) * w_ref[width - 1 - k:width - k, :]
    return out


def _head_expand():
    r = lax.broadcasted_iota(jnp.int32, (LANES, SSD_D_INNER), 0)
    c = lax.broadcasted_iota(jnp.int32, (LANES, SSD_D_INNER), 1)
    return (c // SSD_HEAD_DIM == r).astype(F32)


def _head_expand_t():
    r = lax.broadcasted_iota(jnp.int32, (SSD_D_INNER, LANES), 0)
    c = lax.broadcasted_iota(jnp.int32, (SSD_D_INNER, LANES), 1)
    return (r // SSD_HEAD_DIM == c).astype(F32)


def _ssd_tile(act, dt_raw, dtb, alog, seq_rows):
    rows = MIX_ROWS
    xs = act[:, :SSD_D_INNER]
    bm = act[:, SSD_D_INNER:SSD_D_INNER + SSD_GROUPS * SSD_STATE]
    cm = act[:, SSD_D_INNER + SSD_GROUPS * SSD_STATE:]
    v = dt_raw + dtb
    dt = jnp.maximum(v, 0.0) + jnp.log1p(jnp.exp(-jnp.abs(v)))
    a = -jnp.exp(alog)
    dta = dt * a
    ri = lax.broadcasted_iota(jnp.int32, (rows, rows), 0)
    ci = lax.broadcasted_iota(jnp.int32, (rows, rows), 1)
    same = (ri // seq_rows) == (ci // seq_rows)
    causal = same & (ci <= ri)
    cs = _dot_exact(causal.astype(F32), dta)
    expand = _head_expand().astype(BF16)

    def per_head_lanes(v):
        hi = v.astype(BF16)
        lo = (v - hi.astype(F32)).astype(BF16)
        return _dot(hi, expand) + _dot(lo, expand)

    dt_e = per_head_lanes(dt)
    cs_e = per_head_lanes(cs)
    if seq_rows == rows:
        tot = None
        tot_e = cs_e[rows - 1:rows, :]
    else:
        tot = _dot_exact(same.astype(F32), dta)
        tot_e = per_head_lanes(tot)
    cs_t = cs.T
    dtx = xs * dt_e
    xw = dtx * jnp.exp(tot_e - cs_e)
    b_groups, c_groups, y_parts = [], [], []
    for g in range(SSD_GROUPS):
        bg = bm[:, g * SSD_STATE:(g + 1) * SSD_STATE].astype(BF16)
        cg = cm[:, g * SSD_STATE:(g + 1) * SSD_STATE].astype(BF16)
        b_groups.append(bg)
        c_groups.append(cg)
        cb = _dot_nt(cg, bg)
        for hh in range(SSD_HEADS_PER_GROUP):
            h = g * SSD_HEADS_PER_GROUP + hh
            seg = cs[:, h:h + 1] - cs_t[h:h + 1, :]
            decay = jnp.where(causal, jnp.exp(jnp.where(causal, seg, 0.0)), 0.0)
            y_parts.append(_dot((cb * decay).astype(BF16),
                                dtx[:, h * SSD_HEAD_DIM:(h + 1) * SSD_HEAD_DIM].astype(BF16)))
    y_diag = jnp.concatenate(y_parts, axis=-1)
    return xs, y_diag, jnp.exp(cs_e), xw, tot, tot_e, bm, b_groups, c_groups


def _gated_group_norm(y, z, nw):
    y = y * _silu(z)
    half = SSD_D_INNER // SSD_GROUPS
    parts = []
    for g in range(SSD_GROUPS):
        yg = y[:, g * half:(g + 1) * half]
        parts.append(yg * lax.rsqrt(jnp.mean(yg * yg, axis=-1, keepdims=True) + RMS_EPS))
    return jnp.concatenate(parts, axis=-1) * nw


def _attn_prompt_kernel(sink_ref, q_ref, kvc_ref, kvp_ref, cosc_ref, sinc_ref, cosp_ref, sinp_ref, bias_ref,
                        ya_ref, krot_ref):
    i = pl.program_id(1)
    nb = pl.num_programs(1)
    w = WINDOW
    q = _rope(q_ref[...].astype(F32), cosc_ref[...], sinc_ref[...])
    kvc = kvc_ref[...].astype(F32)
    kvp = kvp_ref[...].astype(F32)
    kc = _rope(kvc[:, :KV_WIDTH], cosc_ref[...], sinc_ref[...])
    kp = _rope(kvp[:, :KV_WIDTH], cosp_ref[...], sinp_ref[...])
    vc = kvc[:, KV_WIDTH:]
    vp = kvp[:, KV_WIDTH:]

    @pl.when(i == nb - 1)
    def _():
        krot_ref[0] = kc

    rows = Q_PER_KV * w
    bias = bias_ref[...]
    rcol = lax.broadcasted_iota(jnp.int32, (rows, 1), 0)
    outs = []
    for kh in range(N_KV_HEADS):
        hs = slice(kh * HEAD_DIM, (kh + 1) * HEAD_DIM)
        k2 = jnp.concatenate([kp[:, hs], kc[:, hs]], axis=0).astype(BF16)
        v2 = jnp.concatenate([vp[:, hs], vc[:, hs]], axis=0).astype(BF16)
        qg = jnp.concatenate(
            [q[:, (kh * Q_PER_KV + g) * HEAD_DIM:(kh * Q_PER_KV + g + 1) * HEAD_DIM] for g in range(Q_PER_KV)],
            axis=0).astype(BF16)
        sink = jnp.zeros((rows, 1), F32)
        for g in range(Q_PER_KV):
            sink = jnp.where(rcol // w == g, sink_ref[kh * Q_PER_KV + g], sink)
        logits = _dot_nt(qg, k2) * ATTN_SCALE + bias
        m = jnp.maximum(jnp.max(logits, axis=-1, keepdims=True), sink)
        e = jnp.exp(logits - m)
        den = jnp.sum(e, axis=-1, keepdims=True) + jnp.exp(sink - m)
        o = _dot(e.astype(BF16), v2) / den
        for g in range(Q_PER_KV):
            outs.append(o[g * w:(g + 1) * w, :])
    ya_ref[...] = jnp.concatenate(outs, axis=-1).astype(BF16)


def _band_bias():
    w = WINDOW
    r = np.arange(Q_PER_KV * w)[:, None] % w
    s = np.arange(2 * w)[None, :]
    diff = w + r - s
    band = (diff >= 0) & (diff < w)
    return jnp.asarray(np.where(np.stack([band & (s >= w), band]), 0.0, -np.inf), F32)


def _attention_prompt(p, sink, cos_p, sin_p, n_seq, seq_len):
    nb = seq_len // WINDOW
    n_prompt = n_seq * seq_len
    w = WINDOW

    def cur(b, i):
        return b * nb + i

    def prev(b, i):
        return jnp.maximum(b * nb + i - 1, 0)

    return dict(
        kernel=_attn_prompt_kernel,
        grid=(n_seq, nb),
        scratch_shapes=[],
        in_specs=[
            pl.BlockSpec(memory_space=pltpu.SMEM),
            pl.BlockSpec((w, ATTN_WIDTH), lambda b, i: (cur(b, i), COL_Q // ATTN_WIDTH)),
            pl.BlockSpec((w, 2 * KV_WIDTH), lambda b, i: (cur(b, i), COL_KV // (2 * KV_WIDTH))),
            pl.BlockSpec((w, 2 * KV_WIDTH), lambda b, i: (prev(b, i), COL_KV // (2 * KV_WIDTH))),
            pl.BlockSpec((w, LANES), lambda b, i: (i, 0)),
            pl.BlockSpec((w, LANES), lambda b, i: (i, 0)),
            pl.BlockSpec((w, LANES), lambda b, i: (jnp.maximum(i - 1, 0), 0)),
            pl.BlockSpec((w, LANES), lambda b, i: (jnp.maximum(i - 1, 0), 0)),
            pl.BlockSpec((None, Q_PER_KV * w, 2 * w), lambda b, i: (jnp.minimum(i, 1), 0, 0)),
        ],
        out_specs=[
            pl.BlockSpec((w, ATTN_WIDTH), lambda b, i: (cur(b, i), 0)),
            pl.BlockSpec((1, w, KV_WIDTH), lambda b, i: (b, 0, 0)),
        ],
        out_shape=[
            jax.ShapeDtypeStruct((n_prompt, ATTN_WIDTH), BF16),
            jax.ShapeDtypeStruct((n_seq, w, KV_WIDTH), F32),
        ],
        args=(sink, p, p, p, cos_p, sin_p, cos_p, sin_p, _band_bias()))


def _run_fused(first, second):
    n_in = (len(first["in_specs"]), len(second["in_specs"]))
    n_out = (len(first["out_specs"]), len(second["out_specs"]))
    n_scr = len(first["scratch_shapes"])

    def fused(*refs):
        ins, rest = refs[:sum(n_in)], refs[sum(n_in):]
        outs, scr = rest[:sum(n_out)], rest[sum(n_out):]
        first["kernel"](*ins[:n_in[0]], *outs[:n_out[0]], *scr[:n_scr])
        second["kernel"](*ins[n_in[0]:], *outs[n_out[0]:], *scr[n_scr:])

    assert first["grid"] == second["grid"]
    outs = pl.pallas_call(
        fused,
        grid=first["grid"],
        in_specs=first["in_specs"] + second["in_specs"],
        out_specs=first["out_specs"] + second["out_specs"],
        out_shape=first["out_shape"] + second["out_shape"],
        scratch_shapes=first["scratch_shapes"] + second["scratch_shapes"],
        compiler_params=_params(("parallel", "arbitrary")),
    )(*first["args"], *second["args"])
    return outs[:n_out[0]], outs[n_out[0]:]


def _ssd_prompt_kernel(z_ref, xc_ref, xp_ref, dt_ref, scb_ref, sccc_ref, sccp_ref, scvc_ref, scvp_ref,
                       cw_ref, cb_ref, dtb_ref, alog_ref, de_ref, nw_ref, scw_ref,
                       yb_ref, yc_ref, hout_ref, cvlast_ref, h_scr):
    i = pl.program_id(1)
    nc = pl.num_programs(1)
    first = i == 0

    @pl.when(first)
    def _():
        h_scr[...] = jnp.zeros_like(h_scr)

    xc = xc_ref[...].astype(F32)
    xp = jnp.where(first, 0.0, xp_ref[...].astype(F32))
    act = _silu(_causal_conv(xc, xp, cw_ref, SSD_CONV) + cb_ref[...])
    xs, y_diag, ecs_e, xw, _, tot_e, bm, _, c_groups = _ssd_tile(
        act, dt_ref[...], dtb_ref[...], alog_ref[...], MIX_ROWS)
    gw = SSD_HEADS_PER_GROUP * SSD_HEAD_DIM
    y_off = []
    for g in range(SSD_GROUPS):
        cols = slice(g * gw, (g + 1) * gw)
        hg = h_scr[:, cols]
        y_off.append(_dot(c_groups[g], hg.astype(BF16)))
        b_t = bm[:, g * SSD_STATE:(g + 1) * SSD_STATE].T.astype(BF16)
        h_scr[:, cols] = jnp.exp(tot_e[:, cols]) * hg + _dot(b_t, xw[:, cols].astype(BF16))
    y = y_diag + jnp.concatenate(y_off, axis=-1) * ecs_e + de_ref[...] * xs
    yb_ref[...] = _gated_group_norm(y, z_ref[...].astype(F32), nw_ref[...]).astype(BF16)

    cvc = sccc_ref[...].astype(F32) * scvc_ref[...].astype(F32)
    cvp = jnp.where(first, 0.0, sccp_ref[...].astype(F32) * scvp_ref[...].astype(F32))
    conv_c = _causal_conv(cvc, cvp, scw_ref, SC_WIDTH)
    yc_ref[...] = (scb_ref[...].astype(F32) * conv_c).astype(BF16)

    @pl.when(i == nc - 1)
    def _():
        hout_ref[0] = h_scr[...].T
        cvlast_ref[0] = cvc[MIX_ROWS - SUBLANES:, :]


def _ssd_prompt(p, dt, cw, cb, dtb, alog, de, nw, scw, n_seq, seq_len):
    nc = seq_len // MIX_ROWS
    r = MIX_ROWS
    n_prompt = n_seq * seq_len

    def cur(b, i):
        return b * nc + i

    def prev(b, i):
        return jnp.maximum(b * nc + i - 1, 0)

    def col(width, offset, which):
        return pl.BlockSpec((r, width), lambda b, i: (which(b, i), offset // width))

    def const(shape):
        return pl.BlockSpec(shape, lambda b, i: (0,) * len(shape))

    return dict(
        kernel=_ssd_prompt_kernel,
        grid=(n_seq, nc),
        in_specs=[
            col(SSD_D_INNER, COL_Z, cur),
            col(SSD_CONV_DIM, COL_XBC, cur), col(SSD_CONV_DIM, COL_XBC, prev),
            pl.BlockSpec((r, DT_PAD), lambda b, i: (cur(b, i), 0)),
            col(SC_DIM, COL_SCB, cur),
            col(SC_DIM, COL_SCC, cur), col(SC_DIM, COL_SCC, prev),
            col(SC_DIM, COL_SCV, cur), col(SC_DIM, COL_SCV, prev),
            const((SSD_CONV, SSD_CONV_DIM)), const((1, SSD_CONV_DIM)),
            const((1, DT_PAD)), const((1, DT_PAD)), const((1, SSD_D_INNER)), const((1, SSD_D_INNER)),
            const((SC_WIDTH, SC_DIM)),
        ],
        out_specs=[
            pl.BlockSpec((r, SSD_D_INNER), lambda b, i: (cur(b, i), 0)),
            pl.BlockSpec((r, SC_DIM), lambda b, i: (cur(b, i), 0)),
            pl.BlockSpec((1, SSD_D_INNER, SSD_STATE), lambda b, i: (b, 0, 0)),
            pl.BlockSpec((1, SUBLANES, SC_DIM), lambda b, i: (b, 0, 0)),
        ],
        out_shape=[
            jax.ShapeDtypeStruct((n_prompt, SSD_D_INNER), BF16),
            jax.ShapeDtypeStruct((n_prompt, SC_DIM), BF16),
            jax.ShapeDtypeStruct((n_seq, SSD_D_INNER, SSD_STATE), F32),
            jax.ShapeDtypeStruct((n_seq, SUBLANES, SC_DIM), F32),
        ],
        scratch_shapes=[pltpu.VMEM((SSD_STATE, SSD_D_INNER), F32)],
        args=(p, p, p, dt, p, p, p, p, p, cw, cb, dtb, alog, de, nw, scw))


def _mix_sample_kernel(sink_ref, q_ref, kv_ref, z_ref, x_ref, dt_ref, scb_ref, scc_ref, scv_ref,
                       cos_ref, sin_ref, ck_ref, cv_ref, h0_ref, xbuf_ref, cbuf_ref,
                       cw_ref, cb_ref, dtb_ref, alog_ref, de_ref, nw_ref, scw_ref,
                       ya_ref, yb_ref, yc_ref, knew_ref, vnew_ref, hnew_ref, cvout_ref):
    ns, t = SEQS_PER_STEP, DEC_SEQ
    wb = ck_ref.shape[1]

    q = _rope(q_ref[...].astype(F32), cos_ref[...], sin_ref[...])
    kv = kv_ref[...].astype(F32)
    kn = _rope(kv[:, :KV_WIDTH], cos_ref[...], sin_ref[...])
    vn = kv[:, KV_WIDTH:]
    q3 = q.reshape(ns, t, ATTN_WIDTH)
    kn3 = kn.reshape(ns, t, KV_WIDTH)
    vn3 = vn.reshape(ns, t, KV_WIDTH)
    knew_ref[:, :wb - t, :] = ck_ref[:, t:, :]
    knew_ref[:, wb - t:, :] = kn3
    vnew_ref[:, :wb - t, :] = cv_ref[:, t:, :]
    vnew_ref[:, wb - t:, :] = vn3
    nq = Q_PER_KV * t
    qi = lax.broadcasted_iota(jnp.int32, (ns, nq, wb + t), 1) % t
    si = lax.broadcasted_iota(jnp.int32, (ns, nq, wb + t), 2)
    valid = ((si < wb) & (si > qi + (wb - WINDOW))) | ((si >= wb) & (si - wb <= qi))
    hrow = lax.broadcasted_iota(jnp.int32, (ns, nq, 1), 1) // t
    heads = [None] * N_HEADS
    for kh in range(N_KV_HEADS):
        hs = slice(kh * HEAD_DIM, (kh + 1) * HEAD_DIM)
        k_all = jnp.concatenate([ck_ref[:, :, hs], kn3[:, :, hs]], axis=1).astype(BF16)
        v_all = jnp.concatenate([cv_ref[:, :, hs], vn3[:, :, hs]], axis=1).astype(BF16)
        qg = jnp.concatenate(
            [q3[:, :, (kh * Q_PER_KV + g) * HEAD_DIM:(kh * Q_PER_KV + g + 1) * HEAD_DIM] for g in range(Q_PER_KV)],
            axis=1).astype(BF16)
        sink = jnp.zeros((ns, nq, 1), F32)
        for g in range(Q_PER_KV):
            sink = jnp.where(hrow == g, sink_ref[kh * Q_PER_KV + g], sink)
        logits = jnp.einsum('bqd,bsd->bqs', qg, k_all, preferred_element_type=F32) * ATTN_SCALE
        logits = jnp.where(valid, logits, -jnp.inf)
        m = jnp.maximum(jnp.max(logits, axis=-1, keepdims=True), sink)
        e = jnp.exp(logits - m)
        den = jnp.sum(e, axis=-1, keepdims=True) + jnp.exp(sink - m)
        o = jnp.einsum('bqs,bsd->bqd', e.astype(BF16), v_all, preferred_element_type=F32) / den
        for g in range(Q_PER_KV):
            heads[kh * Q_PER_KV + g] = o[:, g * t:(g + 1) * t, :]
    ya_ref[...] = jnp.concatenate(heads, axis=-1).reshape(ns * t, ATTN_WIDTH).astype(BF16)

    xc3 = x_ref[...].astype(F32).reshape(ns, t, SSD_CONV_DIM)
    xp3 = xbuf_ref[...].reshape(ns, t, SSD_CONV_DIM)
    conv = _causal_conv(xc3, xp3, cw_ref, SSD_CONV).reshape(ns * t, SSD_CONV_DIM)
    act = _silu(conv + cb_ref[...])
    xs, y_diag, ecs_e, xw, tot, _, _, b_groups, c_groups = _ssd_tile(
        act, dt_ref[...], dtb_ref[...], alog_ref[...], t)
    xw_t = xw.T
    dec_t = jnp.exp(_dot_exact(_head_expand_t(), tot.T))
    gw = SSD_HEADS_PER_GROUP * SSD_HEAD_DIM
    y_off = []
    for g in range(SSD_GROUPS):
        c3 = c_groups[g].reshape(ns, t, SSD_STATE)
        hg = h0_ref[:, g * gw:(g + 1) * gw, :].astype(BF16)
        y_off.append(jnp.einsum('btn,bqn->btq', c3, hg, preferred_element_type=F32).reshape(ns * t, gw))
    y = y_diag + jnp.concatenate(y_off, axis=-1) * ecs_e + de_ref[...] * xs
    yb_ref[...] = _gated_group_norm(y, z_ref[...].astype(F32), nw_ref[...]).astype(BF16)
    col = lax.broadcasted_iota(jnp.int32, xw_t.shape, 1) // t
    for b in range(ns):
        xw_b = jnp.where(col == b, xw_t, jnp.zeros_like(xw_t))
        upd = jnp.concatenate(
            [_dot(xw_b[g * gw:(g + 1) * gw, :].astype(BF16), b_groups[g]) for g in range(SSD_GROUPS)], axis=0)
        hnew_ref[b] = dec_t[:, b * t:b * t + 1] * h0_ref[b] + upd

    cvc = scc_ref[...].astype(F32) * scv_ref[...].astype(F32)
    cvout_ref[...] = cvc
    conv_c = _causal_conv(cvc.reshape(ns, t, SC_DIM), cbuf_ref[...].reshape(ns, t, SC_DIM), scw_ref, SC_WIDTH)
    yc_ref[...] = (scb_ref[...].astype(F32) * conv_c.reshape(ns * t, SC_DIM)).astype(BF16)


def _mix_sample(p, dt, sink, cos_s, sin_s, cache_k, cache_v, h0, xbuf, cbuf,
                cw, cb, dtb, alog, de, nw, scw, layer, n_prompt, n_dec):
    r = MIX_ROWS
    ns = SEQS_PER_STEP
    steps = n_dec // ns
    base = n_prompt // r
    wb = cache_k.shape[2]

    def col(width, offset):
        return pl.BlockSpec((r, width), lambda i: (base + i, offset // width))

    def state(shape):
        return pl.BlockSpec((None, *shape), lambda i: (layer, i) + (0,) * (len(shape) - 1))

    def const(shape):
        return pl.BlockSpec(shape, lambda i: (0,) * len(shape))

    def rows(width):
        return pl.BlockSpec((r, width), lambda i: (i, 0))

    n_rows = n_dec * DEC_SEQ
    return pl.pallas_call(
        _mix_sample_kernel,
        grid=(steps,),
        in_specs=[
            pl.BlockSpec(memory_space=pltpu.SMEM),
            col(ATTN_WIDTH, COL_Q), col(2 * KV_WIDTH, COL_KV), col(SSD_D_INNER, COL_Z),
            col(SSD_CONV_DIM, COL_XBC),
            pl.BlockSpec((r, DT_PAD), lambda i: (base + i, 0)),
            col(SC_DIM, COL_SCB), col(SC_DIM, COL_SCC), col(SC_DIM, COL_SCV),
            const((r, LANES)), const((r, LANES)),
            state((ns, wb, KV_WIDTH)), state((ns, wb, KV_WIDTH)),
            state((ns, SSD_D_INNER, SSD_STATE)),
            state((r, SSD_CONV_DIM)), state((r, SC_DIM)),
            const((SSD_CONV, SSD_CONV_DIM)), const((1, SSD_CONV_DIM)),
            const((1, DT_PAD)), const((1, DT_PAD)), const((1, SSD_D_INNER)), const((1, SSD_D_INNER)),
            const((SC_WIDTH, SC_DIM)),
        ],
        out_specs=[
            rows(ATTN_WIDTH), rows(SSD_D_INNER), rows(SC_DIM),
            pl.BlockSpec((ns, wb, KV_WIDTH), lambda i: (i, 0, 0)),
            pl.BlockSpec((ns, wb, KV_WIDTH), lambda i: (i, 0, 0)),
            pl.BlockSpec((ns, SSD_D_INNER, SSD_STATE), lambda i: (i, 0, 0)),
            rows(SC_DIM),
        ],
        out_shape=[
            jax.ShapeDtypeStruct((n_rows, ATTN_WIDTH), BF16),
            jax.ShapeDtypeStruct((n_rows, SSD_D_INNER), BF16),
            jax.ShapeDtypeStruct((n_rows, SC_DIM), BF16),
            jax.ShapeDtypeStruct((n_dec, wb, KV_WIDTH), F32),
            jax.ShapeDtypeStruct((n_dec, wb, KV_WIDTH), F32),
            jax.ShapeDtypeStruct((n_dec, SSD_D_INNER, SSD_STATE), F32),
            jax.ShapeDtypeStruct((n_rows, SC_DIM), F32),
        ],
        compiler_params=_params(("parallel",)),
    )(sink, p, p, p, p, dt, p, p, p, cos_s, sin_s, cache_k, cache_v, h0, xbuf, cbuf,
      cw, cb, dtb, alog, de, nw, scw)


def _store_token_major(ref, v):
    n = v.shape[0]
    for s in range(ROW_CHUNKS):
        ref[pl.ds(s, n, stride=ROW_CHUNKS), :] = v[:, s * LANES:(s + 1) * LANES]


def _load_token_major(ref, n):
    return jnp.concatenate([ref[pl.ds(s, n, stride=ROW_CHUNKS), :] for s in range(ROW_CHUNKS)], axis=-1)


def _outproj_kernel(n_prompt_tiles, alpha,
                    xp_ref, xs_ref, yap_ref, yas_ref, ybp_ref, ybs_ref, ycp_ref, ycs_ref, g_ref,
                    g1p_ref, g1s_ref, sh2p_ref, sh2s_ref, sc2p_ref, sc2s_ref,
                    wpa_ref, wpb_ref, wpc_ref, wout_ref, wr_ref, br_ref, lng_ref, lnb_ref,
                    x1_ref, h_ref, route_ref, cnt_ref, cnt_scr):
    i = pl.program_id(0)
    is_sample = i >= n_prompt_tiles
    tm = xp_ref.shape[0]

    @pl.when(i == 0)
    def _():
        cnt_scr[...] = jnp.zeros_like(cnt_scr)

    n = tm
    cnt = cnt_scr[...]
    for r0 in range(0, tm, n):
        rs = slice(r0, r0 + n)

        def pick(prompt_ref, sample_ref):
            return jnp.where(is_sample, sample_ref[0, rs, :], prompt_ref[0])

        ya = jnp.where(is_sample, yas_ref[rs, :], yap_ref[rs, :])
        yb = jnp.where(is_sample, ybs_ref[rs, :], ybp_ref[rs, :])
        yc = jnp.where(is_sample, ycs_ref[rs, :], ycp_ref[rs, :])
        gates = 0.5 * jnp.tanh(0.5 * g_ref[rs, :]) + 0.5
        merged = (gates[:, :D_MODEL] * _dot(ya, wpa_ref[...]).astype(BF16)
                  + gates[:, D_MODEL:2 * D_MODEL] * _dot(yb, wpb_ref[...]).astype(BF16)
                  + gates[:, 2 * D_MODEL:] * _dot(yc, wpc_ref[...]).astype(BF16))
        mix = _dot(merged, wout_ref[...])
        g1 = pick(g1p_ref, g1s_ref)
        x = jnp.where(is_sample, xs_ref[rs, :], xp_ref[rs, :])
        x1 = _layer_norm(alpha * x + g1 * mix, lng_ref[...], lnb_ref[...])
        x1_ref[rs, :] = x1
        sh2 = pick(sh2p_ref, sh2s_ref)
        sc2 = pick(sc2p_ref, sc2s_ref)
        h = x1 * (1.0 + sc2) + sh2
        _store_token_major(h_ref.at[pl.ds(r0 * ROW_CHUNKS, n * ROW_CHUNKS)], h)

        lt = (_dot(h.astype(BF16), wr_ref[...]) + br_ref[...]).T[:ROUTE_ROWS]
        row = lax.broadcasted_iota(jnp.int32, lt.shape, 0).astype(F32)
        neg = -jnp.inf
        big = float(ROUTE_ROWS)
        gl = jnp.where(row < MOE_GROUPS, lt, neg)
        gmax = jnp.max(gl, axis=0, keepdims=True)
        g_p = 1.0 / jnp.sum(jnp.exp(gl - gmax), axis=0, keepdims=True)
        gidx = jnp.min(jnp.where(gl == gmax, row, big), axis=0, keepdims=True)
        lo = ROUTE_EXPERT_ROW + EXPERTS_PER_GROUP * gidx
        sel = jnp.where((row >= lo) & (row < lo + EXPERTS_PER_GROUP), lt, neg)
        m1 = jnp.max(sel, axis=0, keepdims=True)
        i1 = jnp.min(jnp.where(sel == m1, row, big), axis=0, keepdims=True)
        sel2 = jnp.where(row == i1, neg, sel)
        m2 = jnp.max(sel2, axis=0, keepdims=True)
        i2 = jnp.min(jnp.where(sel2 == m2, row, big), axis=0, keepdims=True)
        ssum = jnp.sum(jnp.exp(sel - m1), axis=0, keepdims=True)
        p1 = 1.0 / ssum
        p2 = jnp.exp(m2 - m1) / ssum
        w1 = g_p * (p1 / (p1 + p2))
        w2 = g_p * (p2 / (p1 + p2))

        onehot = jnp.where((row == i1) | (row == i2), 1.0, 0.0)
        ri = lax.broadcasted_iota(jnp.int32, (n, n), 0)
        ci = lax.broadcasted_iota(jnp.int32, (n, n), 1)
        earlier = jnp.where(ri < ci, 1.0, 0.0).astype(BF16)
        prefix = _dot(onehot.astype(BF16), earlier) + cnt[:, 0:1]
        rank1 = jnp.sum(jnp.where(row == i1, prefix, 0.0), axis=0, keepdims=True)
        rank2 = jnp.sum(jnp.where(row == i2, prefix, 0.0), axis=0, keepdims=True)
        cnt = cnt + jnp.sum(onehot, axis=1, keepdims=True)
        zero = jnp.zeros_like(w1)
        route_ref[:, rs] = jnp.concatenate(
            [i1 - ROUTE_EXPERT_ROW, i2 - ROUTE_EXPERT_ROW, w1, w2, rank1, rank2, zero, zero], axis=0)

    cnt_scr[...] = cnt
    cnt_ref[...] = cnt


def _output_projection(x, y_prompt, y_sample, p, mod_p, mod_s, wpa, wpb, wpc, wout, wr, br, lng, lnb,
                       layer, n_prompt, seq_len, alpha):
    t_all = n_prompt + x[1].shape[0]
    tm = ROW_TILE
    npt = n_prompt // tm
    n_seq = mod_p.shape[1]
    mods = []
    for col in (2, 3, 4):
        mods.extend(_mod_specs(layer, col, tm, npt, seq_len // tm, n_seq))

    def rows(width):
        return pl.BlockSpec((tm, width), lambda i: (i, 0))

    def prompt_rows(width):
        return pl.BlockSpec((tm, width), lambda i: (jnp.minimum(i, npt - 1), 0))

    def sample_rows(width):
        return pl.BlockSpec((tm, width), lambda i: (jnp.maximum(i - npt, 0), 0))

    def const(shape):
        return pl.BlockSpec(shape, lambda i: (0,) * len(shape))

    return pl.pallas_call(
        functools.partial(_outproj_kernel, npt, alpha),
        grid=(t_all // tm,),
        in_specs=[
            prompt_rows(D_MODEL), sample_rows(D_MODEL),
            prompt_rows(ATTN_WIDTH), sample_rows(ATTN_WIDTH),
            prompt_rows(SSD_D_INNER), sample_rows(SSD_D_INNER),
            prompt_rows(SC_DIM), sample_rows(SC_DIM),
            pl.BlockSpec((tm, 3 * D_MODEL), lambda i: (i, COL_GATES // (3 * D_MODEL))),
            *mods,
            _layer_spec(layer, (ATTN_WIDTH, D_MODEL)), _layer_spec(layer, (SSD_D_INNER, D_MODEL)),
            _layer_spec(layer, (SC_DIM, D_MODEL)), _layer_spec(layer, (D_MODEL, D_MODEL)),
            _layer_spec(layer, (D_MODEL, LANES)), _layer_spec(layer, (1, LANES)),
            const((1, D_MODEL)), const((1, D_MODEL)),
        ],
        out_specs=[
            rows(D_MODEL),
            pl.BlockSpec((tm * ROW_CHUNKS, LANES), lambda i: (i, 0)),
            pl.BlockSpec((SUBLANES, tm), lambda i: (0, i)),
            const((ROUTE_ROWS, LANES)),
        ],
        out_shape=[
            jax.ShapeDtypeStruct((t_all, D_MODEL), F32),
            jax.ShapeDtypeStruct((t_all * ROW_CHUNKS, LANES), F32),
            jax.ShapeDtypeStruct((SUBLANES, t_all), F32),
            jax.ShapeDtypeStruct((ROUTE_ROWS, LANES), F32),
        ],
        scratch_shapes=[pltpu.VMEM((ROUTE_ROWS, LANES), F32)],
        compiler_params=_params(("arbitrary",)),
    )(x[0], x[1], y_prompt[0], y_sample[0], y_prompt[1], y_sample[1], y_prompt[2], y_sample[2],
      p, mod_p, mod_s, mod_p, mod_s, mod_p, mod_s,
      wpa, wpb, wpc, wout, wr, br, lng, lnb)


def _slots_kernel(route_ref, start_ref, dest_ref):
    route = route_ref[...]
    tm = route.shape[1]
    expert = lax.broadcasted_iota(jnp.int32, (N_EXPERTS, tm), 0).astype(F32)
    start = start_ref[:, 0:1]
    rows = []
    for e_row, r_row in ((0, 4), (1, 5)):
        first = jnp.sum(jnp.where(expert == route[e_row:e_row + 1], start, 0.0), axis=0, keepdims=True)
        rows.append(first + route[r_row:r_row + 1])
    rows.append(jnp.zeros((SUBLANES - 2, tm), F32))
    dest_ref[...] = jnp.concatenate(rows, axis=0).astype(jnp.int32)


def _slots(route, pad_start):
    t_all = route.shape[1]
    tm = ROW_TILE_IN
    dest = pl.pallas_call(
        _slots_kernel,
        grid=(t_all // tm,),
        in_specs=[pl.BlockSpec((SUBLANES, tm), lambda i: (0, i)),
                  pl.BlockSpec((N_EXPERTS, LANES), lambda i: (0, 0))],
        out_specs=pl.BlockSpec((SUBLANES, tm), lambda i: (0, i)),
        out_shape=jax.ShapeDtypeStruct((SUBLANES, t_all), jnp.int32),
        compiler_params=_params(("parallel",)),
    )(route, jnp.broadcast_to(pad_start.astype(F32)[:, None], (N_EXPERTS, LANES)))
    return dest[0], dest[1]


def _token_copy(src, src_row, dst, dst_row, sem):
    return pltpu.make_async_copy(
        src.at[pl.ds(pl.multiple_of(src_row * ROW_CHUNKS, ROW_CHUNKS), ROW_CHUNKS)],
        dst.at[pl.ds(pl.multiple_of(dst_row * ROW_CHUNKS, ROW_CHUNKS), ROW_CHUNKS)],
        sem)


def _dispatch_kernel(d1_ref, d2_ref, pend_ref, cnt_ref, nb_ref, h_ref, xb_ref, zero_scr, zsem, sem):
    i = pl.program_id(0)
    tm = h_ref.shape[0] // ROW_CHUNKS
    blk_rows = EXPERT_BLOCK * ROW_CHUNKS
    n_blocks = xb_ref.shape[0] // blk_rows

    def zero_block(b):
        start = pl.multiple_of(b * blk_rows, blk_rows)
        return pltpu.make_async_copy(zero_scr, xb_ref.at[pl.ds(start, blk_rows)], zsem)

    def last_block(e):
        return pend_ref[e] // EXPERT_BLOCK - 1

    @pl.when(i == 0)
    def _():
        zero_scr[...] = jnp.zeros_like(zero_scr)
        for e in range(N_EXPERTS):
            @pl.when(cnt_ref[e] > 0)
            def _():
                zero_block(last_block(e)).start()
        lax.fori_loop(nb_ref[0], n_blocks, lambda b, c: (zero_block(b).start(), c)[1], 0)
        for e in range(N_EXPERTS):
            @pl.when(cnt_ref[e] > 0)
            def _():
                zero_block(last_block(e)).wait()
        lax.fori_loop(nb_ref[0], n_blocks, lambda b, c: (zero_block(b).wait(), c)[1], 0)

    def issue(c, carry):
        for u in range(ISSUE_UNROLL):
            t = c * ISSUE_UNROLL + u
            g = i * tm + t
            _token_copy(h_ref, t, xb_ref, d1_ref[g], sem.at[0]).start(priority=0)
            _token_copy(h_ref, t, xb_ref, d2_ref[g], sem.at[1]).start(priority=1)
        return carry

    lax.fori_loop(0, tm // ISSUE_UNROLL, issue, 0)
    for k in range(2):
        pltpu.make_async_copy(h_ref, xb_ref.at[pl.ds(0, tm * ROW_CHUNKS)], sem.at[k]).wait()


def _dispatch(h, dest1, dest2, pad_end, counts, n_used, n_slots):
    t_all = h.shape[0] // ROW_CHUNKS
    tm = ROW_TILE
    grid_spec = pltpu.PrefetchScalarGridSpec(
        num_scalar_prefetch=5,
        grid=(t_all // tm,),
        in_specs=[pl.BlockSpec((tm * ROW_CHUNKS, LANES), lambda i, *_: (i, 0))],
        out_specs=pl.BlockSpec(memory_space=pl.ANY),
        scratch_shapes=[
            pltpu.VMEM((EXPERT_BLOCK * ROW_CHUNKS, LANES), F32),
            pltpu.SemaphoreType.DMA(()),
            pltpu.SemaphoreType.DMA((2,)),
        ],
    )
    return pl.pallas_call(
        _dispatch_kernel,
        grid_spec=grid_spec,
        out_shape=jax.ShapeDtypeStruct((n_slots * ROW_CHUNKS, LANES), F32),
        compiler_params=_params(("arbitrary",)),
    )(dest1, dest2, pad_end, counts, n_used, h)


def _ffn_kernel(layer, cstart_ref, cbig_ref, cexp_ref, next_ref, meta_ref,
                xb_ref, wg_ref, wu_ref, wd_ref, yb_ref,
                x_in, y_out, wg_f, wu_f, wd_f, wg_b, wu_b, wd_b, x_scr, xsem, ysem, wsem):
    blk_rows = EXPERT_BLOCK * ROW_CHUNKS
    n_blocks = yb_ref.shape[0] // blk_rows
    n_chunks, first_expert, n_used = meta_ref[0], meta_ref[1], meta_ref[2]

    def chunk_blocks(size_class):
        return CHUNK_SIZES[size_class]

    def hbm_rows(ref, j, big):
        start = pl.multiple_of(cstart_ref[j] * blk_rows, blk_rows)
        return ref.at[pl.ds(start, chunk_blocks(big) * blk_rows)]

    def x_copy(j, s, big):
        return pltpu.make_async_copy(hbm_rows(xb_ref, j, big),
                                     x_in.at[s, pl.ds(0, chunk_blocks(big) * blk_rows)], xsem.at[s])

    def y_copy(j, s, big):
        return pltpu.make_async_copy(y_out.at[s, pl.ds(0, chunk_blocks(big) * blk_rows)],
                                     hbm_rows(yb_ref, j, big), ysem.at[s])

    def by_size(j, fn):
        for big in range(len(CHUNK_SIZES)):
            @pl.when(cbig_ref[j] == big)
            def _():
                fn(big)

    def w_copies(e, s):
        return (pltpu.make_async_copy(wg_ref.at[layer, e], wg_f.at[s], wsem.at[s, 0]),
                pltpu.make_async_copy(wu_ref.at[layer, e], wu_f.at[s], wsem.at[s, 1]),
                pltpu.make_async_copy(wd_ref.at[layer, e], wd_f.at[s], wsem.at[s, 2]))

    for c in w_copies(first_expert, 0):
        c.start(priority=1)
    by_size(0, lambda big: x_copy(0, 0, big).start())

    def chunk(j, wslot):
        e = cexp_ref[j]
        new_expert = (j == 0) | (cexp_ref[jnp.maximum(j - 1, 0)] != e)
        wslot = jnp.where(new_expert & (j > 0), 1 - wslot, wslot)

        @pl.when(new_expert)
        def _():
            for c in w_copies(e, wslot):
                c.wait()

            @pl.when(next_ref[e] < N_EXPERTS)
            def _():
                for c in w_copies(next_ref[e], 1 - wslot):
                    c.start(priority=1)

            wg_b[...] = wg_f[wslot].astype(BF16)
            wu_b[...] = wu_f[wslot].astype(BF16)
            wd_b[...] = wd_f[wslot].astype(BF16)

        s = j % 2
        by_size(j, lambda big: x_copy(j, s, big).wait())

        @pl.when(j + 1 < n_chunks)
        def _():
            by_size(j + 1, lambda big: x_copy(j + 1, 1 - s, big).start())

        @pl.when(j >= 2)
        def _():
            by_size(j - 2, lambda big: y_copy(j - 2, s, big).wait())

        def compute(big):
            rows = chunk_blocks(big) * EXPERT_BLOCK
            x_tok = x_in.at[s]
            for c in range(ROW_CHUNKS):
                x_scr[0:rows, c * LANES:(c + 1) * LANES] = (
                    x_tok[pl.ds(c, rows, stride=ROW_CHUNKS), :].astype(BF16))
            x = x_scr[0:rows, :]
            act = (_silu(_dot(x, wg_b[...])) * _dot(x, wu_b[...])).astype(BF16)
            _store_token_major(y_out.at[s, pl.ds(0, rows * ROW_CHUNKS)], _dot(act, wd_b[...]))
            y_copy(j, s, big).start()

        by_size(j, compute)
        return wslot

    lax.fori_loop(0, n_chunks, chunk, 0)

    @pl.when(n_chunks >= 2)
    def _():
        by_size(n_chunks - 2, lambda big: y_copy(n_chunks - 2, n_chunks % 2, big).wait())

    by_size(n_chunks - 1, lambda big: y_copy(n_chunks - 1, (n_chunks - 1) % 2, big).wait())

    zero_rows = y_out.at[0, pl.ds(0, blk_rows)]
    zero_rows[...] = jnp.zeros((blk_rows, LANES), F32)

    def tail_copy(b):
        start = pl.multiple_of(b * blk_rows, blk_rows)
        return pltpu.make_async_copy(zero_rows, yb_ref.at[pl.ds(start, blk_rows)], ysem.at[0])

    lax.fori_loop(n_used, n_blocks, lambda b, c: (tail_copy(b).start(), c)[1], 0)
    lax.fori_loop(n_used, n_blocks, lambda b, c: (tail_copy(b).wait(), c)[1], 0)


def _chunk_plan(first_block, n_expert_blocks, n_blocks):
    experts = jnp.arange(N_EXPERTS, dtype=jnp.int32)
    per_size, left = [], n_expert_blocks
    for size in CHUNK_SIZES:
        per_size.append(left // size)
        left = left % size
    n_chunks_e = sum(per_size)
    chunk_end = jnp.cumsum(n_chunks_e)
    chunk_start = chunk_end - n_chunks_e
    j = jnp.arange(n_blocks, dtype=jnp.int32)
    e_j = jnp.minimum(jnp.sum((chunk_end[None, :] <= j[:, None]).astype(jnp.int32), axis=1), N_EXPERTS - 1)
    pick = (e_j[:, None] == experts[None, :]).astype(jnp.int32)

    def of_expert(v):
        return jnp.sum(pick * v[None, :], axis=1)

    c = j - of_expert(chunk_start)
    start = of_expert(first_block)
    size_class = jnp.zeros_like(j)
    for k, size in enumerate(CHUNK_SIZES):
        n_k = of_expert(per_size[k])
        inside = (c >= 0) & (c < n_k)
        start = start + jnp.where(inside, size * c, jnp.where(c >= n_k, size * n_k, 0))
        size_class = jnp.where(inside, k, size_class)
        c = jnp.where(inside, -1, c - n_k)
    owner = jnp.where(n_chunks_e > 0, experts, N_EXPERTS)
    later = jnp.flip(lax.cummin(jnp.flip(owner)))
    next_owner = jnp.concatenate([later[1:], jnp.full((1,), N_EXPERTS, jnp.int32)])
    meta = jnp.stack([chunk_end[-1], later[0]]).astype(jnp.int32)
    return start.astype(jnp.int32), size_class.astype(jnp.int32), e_j, next_owner.astype(jnp.int32), meta


def _expert_ffn(xb, first_block, n_expert_blocks, n_used, wg, wu, wd, layer):
    blk_rows = EXPERT_BLOCK * ROW_CHUNKS
    n_blocks = xb.shape[0] // blk_rows
    cstart, cbig, cexp, next_owner, meta = _chunk_plan(first_block, n_expert_blocks, n_blocks)
    meta = jnp.concatenate([meta, n_used])
    any_spec = pl.BlockSpec(memory_space=pl.ANY)
    grid_spec = pltpu.PrefetchScalarGridSpec(
        num_scalar_prefetch=5,
        grid=(1,),
        in_specs=[any_spec] * 4,
        out_specs=any_spec,
        scratch_shapes=[
            pltpu.VMEM((2, CHUNK_BLOCKS * blk_rows, LANES), F32),
            pltpu.VMEM((2, CHUNK_BLOCKS * blk_rows, LANES), F32),
            pltpu.VMEM((2, D_MODEL, EXPERT_FF), F32),
            pltpu.VMEM((2, D_MODEL, EXPERT_FF), F32),
            pltpu.VMEM((2, EXPERT_FF, D_MODEL), F32),
            pltpu.VMEM((D_MODEL, EXPERT_FF), BF16),
            pltpu.VMEM((D_MODEL, EXPERT_FF), BF16),
            pltpu.VMEM((EXPERT_FF, D_MODEL), BF16),
            pltpu.VMEM((CHUNK_BLOCKS * EXPERT_BLOCK, D_MODEL), BF16),
            pltpu.SemaphoreType.DMA((2,)),
            pltpu.SemaphoreType.DMA((2,)),
            pltpu.SemaphoreType.DMA((2, 3)),
        ],
    )
    return pl.pallas_call(
        functools.partial(_ffn_kernel, layer),
        grid_spec=grid_spec,
        out_shape=jax.ShapeDtypeStruct(xb.shape, F32),
        compiler_params=_params(("arbitrary",)),
    )(cstart, cbig, cexp, next_owner, meta, xb, wg, wu, wd)


def _combine_kernel(n_prompt_tiles, alpha, d1_ref, d2_ref,
                    x1_ref, route_ref, g2p_ref, g2s_ref, lng_ref, lnb_ref, yb_ref,
                    xp_ref, xs_ref, buf_a, buf_b, sem):
    i = pl.program_id(0)
    n_tiles = pl.num_programs(0)
    tm = x1_ref.shape[0]
    slot = i % 2

    def gather_tile(tile, s):
        def issue(c, carry):
            for u in range(ISSUE_UNROLL):
                t = c * ISSUE_UNROLL + u
                g = tile * tm + t
                _token_copy(yb_ref, d1_ref[g], buf_a.at[s], t, sem.at[s, 0]).start(priority=0)
                _token_copy(yb_ref, d2_ref[g], buf_b.at[s], t, sem.at[s, 1]).start(priority=1)
            return carry

        lax.fori_loop(0, tm // ISSUE_UNROLL, issue, 0)

    @pl.when(i == 0)
    def _():
        gather_tile(0, 0)

    @pl.when(i + 1 < n_tiles)
    def _():
        gather_tile(i + 1, 1 - slot)

    whole = yb_ref.at[pl.ds(0, tm * ROW_CHUNKS)]
    pltpu.make_async_copy(whole, buf_a.at[slot], sem.at[slot, 0]).wait()
    pltpu.make_async_copy(whole, buf_b.at[slot], sem.at[slot, 1]).wait()
    route_t = jnp.concatenate(
        [route_ref[...], jnp.zeros((LANES - SUBLANES, tm), F32)], axis=0).T
    ffn = (route_t[:, 2:3] * _load_token_major(buf_a.at[slot], tm)
           + route_t[:, 3:4] * _load_token_major(buf_b.at[slot], tm))
    g2 = _pick_mod(i >= n_prompt_tiles, g2p_ref, g2s_ref)
    out = _layer_norm(alpha * x1_ref[...] + g2 * ffn, lng_ref[...], lnb_ref[...])
    @pl.when(i < n_prompt_tiles)
    def _():
        xp_ref[...] = out

    @pl.when(i >= n_prompt_tiles)
    def _():
        xs_ref[...] = out


def _combine(x1, route, yb, dest1, dest2, mod_p, mod_s, lng, lnb, layer, n_prompt, seq_len, alpha):
    t_all = x1.shape[0]
    tm = ROW_TILE
    npt = n_prompt // tm
    out_specs = [pl.BlockSpec((tm, D_MODEL), lambda i, *_: (jnp.minimum(i, npt - 1), 0)),
                 pl.BlockSpec((tm, D_MODEL), lambda i, *_: (jnp.maximum(i - npt, 0), 0))]
    out_shape = [jax.ShapeDtypeStruct((n_prompt, D_MODEL), F32),
                 jax.ShapeDtypeStruct((t_all - n_prompt, D_MODEL), F32)]
    g2p, g2s = _mod_specs(layer, 5, tm, npt, seq_len // tm, mod_p.shape[1])
    grid_spec = pltpu.PrefetchScalarGridSpec(
        num_scalar_prefetch=2,
        grid=(t_all // tm,),
        in_specs=[
            pl.BlockSpec((tm, D_MODEL), lambda i, *_: (i, 0)),
            pl.BlockSpec((SUBLANES, tm), lambda i, *_: (0, i)),
            g2p, g2s,
            pl.BlockSpec((1, D_MODEL), lambda i, *_: (0, 0)),
            pl.BlockSpec((1, D_MODEL), lambda i, *_: (0, 0)),
            pl.BlockSpec(memory_space=pl.ANY),
        ],
        out_specs=out_specs,
        scratch_shapes=[
            pltpu.VMEM((2, tm * ROW_CHUNKS, LANES), F32),
            pltpu.VMEM((2, tm * ROW_CHUNKS, LANES), F32),
            pltpu.SemaphoreType.DMA((2, 2)),
        ],
    )
    return pl.pallas_call(
        functools.partial(_combine_kernel, npt, alpha),
        grid_spec=grid_spec,
        out_shape=out_shape,
        compiler_params=_params(("arbitrary",)),
    )(dest1, dest2, x1, route, mod_p, mod_s, lng, lnb, yb)


def _rope_tables(pos):
    inv = jnp.power(ROPE_THETA, -jnp.arange(ROT_HALF, dtype=F32) * (2.0 / ROT_DIM))
    ang = pos.astype(F32)[:, None] * inv[None, :]
    cos, sin = jnp.cos(ang), jnp.sin(ang)
    rest = HEAD_DIM - ROT_DIM
    n = pos.shape[0]
    cos_h = jnp.concatenate([cos, cos, jnp.ones((n, rest), F32)], axis=-1)
    sin_h = jnp.concatenate([-sin, sin, jnp.zeros((n, rest), F32)], axis=-1)
    reps = LANES // HEAD_DIM
    return jnp.tile(cos_h, (1, reps)), jnp.tile(sin_h, (1, reps))


def _permute_w_in(w_in):
    sizes = (ATTN_WIDTH, KV_WIDTH, KV_WIDTH, SSD_D_INNER, SSD_CONV_DIM, SSD_HEADS,
             SC_DIM, SC_DIM, SC_DIM, 3 * D_MODEL)
    offs = np.concatenate([[0], np.cumsum(sizes)])
    q, k, v, z, xbc, dt, scb, scc, scv, gates = (w_in[..., offs[n]:offs[n + 1]] for n in range(len(sizes)))
    main = jnp.concatenate([q, z, xbc, scb, scc, gates, scv, k, v], axis=-1).astype(BF16)
    dt = jnp.pad(dt, ((0, 0), (0, 0), (0, DT_PAD - SSD_HEADS))).astype(BF16)
    return main, dt


def _pad_lanes(v, width):
    return jnp.pad(v, ((0, 0), (0, width - v.shape[-1])))


def kernel(x_prompt, x_sample, c_prompt, c_sample, cache_attn_k, cache_attn_v, state_ssm, state_ssd_conv, state_short_conv, w_ada, b_ada, w_in, attn_sink, ssd_conv_w, ssd_conv_b, ssd_dt_bias, ssd_a_log, ssd_d, ssd_norm_w, sc_conv_w, w_pa, w_pb, w_pc, w_out, ln1_g, ln1_b, ln2_g, ln2_b, router_g_w, router_g_b, router_e_w, router_e_b, moe_w_gate, moe_w_up, moe_w_down):
    depth = w_in.shape[0]
    n_seq, seq_len, _ = x_prompt.shape
    n_dec, dec_len, _ = x_sample.shape
    wb = cache_attn_k.shape[2]
    past_len = 8192
    assert dec_len == DEC_SEQ and wb == WINDOW
    assert seq_len % ROW_TILE_IN == 0 and (n_dec * dec_len) % ROW_TILE_IN == 0
    n_prompt = n_seq * seq_len
    n_sample = n_dec * dec_len
    t_all = n_prompt + n_sample
    alpha = (2 * depth) ** 0.25

    mod = _modulation(jnp.concatenate([c_prompt, c_sample], axis=0), w_ada, b_ada)
    mod_p_all = mod[:, :n_seq].reshape(depth, n_seq, 1, 6 * D_MODEL)
    mod_s_all = jnp.repeat(mod[:, n_seq:], dec_len, axis=1).reshape(depth, 1, n_sample, 6 * D_MODEL)

    w_main_all, w_dt_all = _permute_w_in(w_in)
    cos_p, sin_p = _rope_tables(jnp.arange(seq_len, dtype=jnp.int32))
    cos_s, sin_s = _rope_tables(past_len + (jnp.arange(MIX_ROWS, dtype=jnp.int32) % dec_len))

    a_total = 2 * t_all
    n_blocks = (a_total + N_EXPERTS * (EXPERT_BLOCK - 1)) // EXPERT_BLOCK
    n_slots = n_blocks * EXPERT_BLOCK

    mod_p, mod_s = mod_p_all, mod_s_all
    cache_k = cache_attn_k.reshape(depth, n_dec, wb, KV_WIDTH)
    cache_v = cache_attn_v.reshape(depth, n_dec, wb, KV_WIDTH)
    h0 = state_ssm.reshape(depth, n_dec, SSD_D_INNER, SSD_STATE)
    xbuf = jnp.pad(state_ssd_conv, ((0, 0), (0, 0), (dec_len - (SSD_CONV - 1), 0), (0, 0))
                   ).reshape(depth, n_sample, SSD_CONV_DIM)
    cbuf = jnp.pad(state_short_conv, ((0, 0), (0, 0), (dec_len - (SC_WIDTH - 1), 0), (0, 0))
                   ).reshape(depth, n_sample, SC_DIM)
    wpa, wpb, wpc, wout = (w.astype(BF16) for w in (w_pa, w_pb, w_pc, w_out))
    gap = ROUTE_EXPERT_ROW - MOE_GROUPS
    tail = LANES - ROUTE_EXPERT_ROW - N_EXPERTS
    wr = jnp.concatenate([router_g_w, jnp.zeros((depth, D_MODEL, gap), F32), router_e_w,
                          jnp.zeros((depth, D_MODEL, tail), F32)], axis=-1).astype(BF16)
    br = jnp.concatenate([router_g_b, jnp.zeros((depth, gap), F32), router_e_b,
                          jnp.zeros((depth, tail), F32)], axis=-1)[:, None, :]

    x = (x_prompt.reshape(n_prompt, D_MODEL), x_sample.reshape(n_sample, D_MODEL))
    outs_p = [[] for _ in range(5)]
    outs_s = [[] for _ in range(5)]
    for l in range(depth):
        p, dt = _input_projection(x, mod_p, mod_s, w_main_all, w_dt_all, l, n_prompt, seq_len)

        cw, cb = ssd_conv_w[l], ssd_conv_b[l][None]
        dtb = _pad_lanes(ssd_dt_bias[l][None], DT_PAD)
        alog = _pad_lanes(ssd_a_log[l][None], DT_PAD)
        de = jnp.repeat(ssd_d[l], SSD_HEAD_DIM)[None]
        nw = ssd_norm_w[l][None]
        scw = sc_conv_w[l]

        (ya_p, krot_p), (yb_p, yc_p, h_p, cv_p) = _run_fused(
            _attention_prompt(p, attn_sink[l], cos_p, sin_p, n_seq, seq_len),
            _ssd_prompt(p, dt, cw, cb, dtb, alog, de, nw, scw, n_seq, seq_len))

        ya_s, yb_s, yc_s, k_s, v_s, h_s, cv_s = _mix_sample(
            p, dt, attn_sink[l], cos_s, sin_s, cache_k, cache_v, h0, xbuf, cbuf,
            cw, cb, dtb, alog, de, nw, scw, l, n_prompt, n_dec)

        x1, h, route, counts = _output_projection(
            x, (ya_p, yb_p, yc_p), (ya_s, yb_s, yc_s), p, mod_p, mod_s,
            wpa, wpb, wpc, wout, wr, br, ln1_g[l][None], ln1_b[l][None], l, n_prompt, seq_len, alpha)

        cnt = counts[ROUTE_EXPERT_ROW:ROUTE_EXPERT_ROW + N_EXPERTS, 0].astype(jnp.int32)
        pad_cnt = (cnt + EXPERT_BLOCK - 1) // EXPERT_BLOCK * EXPERT_BLOCK
        pad_end = jnp.cumsum(pad_cnt)
        pad_start = pad_end - pad_cnt
        n_used = (pad_end[-1:] // EXPERT_BLOCK).astype(jnp.int32)
        dest1, dest2 = _slots(route, pad_start)

        xb = _dispatch(h, dest1, dest2, pad_end.astype(jnp.int32), cnt, n_used, n_slots)
        yb_slots = _expert_ffn(xb, (pad_start // EXPERT_BLOCK).astype(jnp.int32),
                               (pad_cnt // EXPERT_BLOCK).astype(jnp.int32), n_used,
                               moe_w_gate, moe_w_up, moe_w_down, l)
        x = _combine(x1, route, yb_slots, dest1, dest2, mod_p, mod_s, ln2_g[l][None], ln2_b[l][None],
                     l, n_prompt, seq_len, alpha)

        def prompt_tail(rows, c0, c1):
            return jnp.stack([p[(b + 1) * seq_len - rows:(b + 1) * seq_len, c0:c1] for b in range(n_seq)]
                             ).astype(F32)

        outs_p[0].append(krot_p.reshape(n_seq, wb, N_KV_HEADS, HEAD_DIM))
        outs_p[1].append(prompt_tail(wb, COL_KV + KV_WIDTH, COL_KV + 2 * KV_WIDTH)
                         .reshape(n_seq, wb, N_KV_HEADS, HEAD_DIM))
        outs_p[2].append(h_p.reshape(n_seq, SSD_HEADS, SSD_HEAD_DIM, SSD_STATE))
        outs_p[3].append(prompt_tail(SSD_CONV - 1, COL_XBC, COL_XBC + SSD_CONV_DIM))
        outs_p[4].append(cv_p[:, SUBLANES - (SC_WIDTH - 1):, :])
        outs_s[0].append(k_s.reshape(n_dec, wb, N_KV_HEADS, HEAD_DIM))
        outs_s[1].append(v_s.reshape(n_dec, wb, N_KV_HEADS, HEAD_DIM))
        outs_s[2].append(h_s.reshape(n_dec, SSD_HEADS, SSD_HEAD_DIM, SSD_STATE))
        outs_s[3].append(p[n_prompt:, COL_XBC:COL_XBC + SSD_CONV_DIM].astype(F32)
                         .reshape(n_dec, dec_len, SSD_CONV_DIM)[:, dec_len - (SSD_CONV - 1):, :])
        outs_s[4].append(cv_s.reshape(n_dec, dec_len, SC_DIM)[:, dec_len - (SC_WIDTH - 1):, :])

    y_prompt = x[0].reshape(n_seq, seq_len, D_MODEL)
    y_sample = x[1].reshape(n_dec, dec_len, D_MODEL)
    return (y_prompt, y_sample, *[jnp.stack(o) for o in outs_p], *[jnp.stack(o) for o in outs_s])
```
